```python
import math
import jax
import jax.numpy as jnp
from jax import lax
import numpy as np

D_MODEL = 1024
BATCH = 8
SEQ = 2048
DEPTH = 4

GRID_W = 64
CTX_LEN = 256
N_MIXERS = 4
GROUP_W = D_MODEL // N_MIXERS
MIX_W = N_MIXERS * GROUP_W
HEAD_DIM = 64
N_FOURIER = GROUP_W // HEAD_DIM
N_DIFF = GROUP_W // HEAD_DIM
DIFF_QK = HEAD_DIM // 2
DIFF_V = HEAD_DIM
N_NA = GROUP_W // HEAD_DIM
NA_KH_MAX = 8
NA_KW = 16
N_GMLP = GROUP_W // HEAD_DIM
CHUNK = 128
D_FF = 2816
CONV_W = 3
ROPE_BASE = 10000.0
Q_BLOCK = 128
EPS = 1e-6
IN_W = GROUP_W + 3 * GROUP_W + 3 * GROUP_W + 2 * GROUP_W

kernel_name = "hybrid_fourier_diffattn_natten_gmlp_dit"


def rms_norm(x, g):
    xf = x.astype(jnp.float32)
    y = xf * lax.rsqrt(jnp.mean(xf * xf, axis=-1, keepdims=True) + EPS)
    return (y * g.astype(jnp.float32)).astype(x.dtype)


def axial_rope_tables(n, dim, dtype):
    n_freq = dim // 4
    freqs = ROPE_BASE ** (-jnp.arange(n_freq, dtype=jnp.float32) / n_freq)
    t = jnp.arange(n)
    row = (t // GRID_W).astype(jnp.float32)
    col = (t % GRID_W).astype(jnp.float32)
    ang = jnp.concatenate([row[:, None] * freqs, col[:, None] * freqs], axis=-1)
    return jnp.cos(ang).astype(dtype), jnp.sin(ang).astype(dtype)


def apply_rope(x, cos, sin):
    x1, x2 = jnp.split(x, 2, axis=-1)
    return jnp.concatenate([x1 * cos - x2 * sin, x1 * sin + x2 * cos], axis=-1)


def dense_attention(q, k, v):
    s = jnp.einsum('bhqd,bhkd->bhqk', q, k).astype(jnp.float32) * (q.shape[-1] ** -0.5)
    a = jax.nn.softmax(s, axis=-1).astype(v.dtype)
    return jnp.einsum('bhqk,bhkd->bhqd', a, v)


def fourier_mix(p):
    b, n, _ = p.shape
    g = p.reshape(b, n, N_FOURIER, HEAD_DIM).astype(jnp.float32)
    f = jnp.fft.fft2(g, axes=(1, 3), norm='ortho').real
    return f.reshape(b, n, GROUP_W).astype(p.dtype)


def diff_q(p, qn):
    b, n, _ = p.shape
    q = rms_norm(p[..., :GROUP_W].reshape(b, n, N_DIFF, 2, DIFF_QK), qn)
    return q.transpose(0, 2, 3, 1, 4)


def diff_kv(p, kn):
    b, n, _ = p.shape
    k = rms_norm(p[..., GROUP_W:2 * GROUP_W].reshape(b, n, N_DIFF, 2, DIFF_QK), kn)
    v = p[..., 2 * GROUP_W:].reshape(b, n, N_DIFF, DIFF_V)
    return k.transpose(0, 2, 3, 1, 4), v.transpose(0, 2, 1, 3)


def diff_core(q, k, v, lam):
    s = jnp.einsum('bhcqd,bhckd->bhcqk', q, k).astype(jnp.float32) * (DIFF_QK ** -0.5)
    a = jax.nn.softmax(s, axis=-1)
    w = (a[:, :, 0] - lam * a[:, :, 1]).astype(v.dtype)
    return jnp.einsum('bhqk,bhkd->bhqd', w, v)


def diff_finish(o, subln, lam_init):
    b, _, n, _ = o.shape
    o = rms_norm(o, subln) * (1.0 - lam_init)
    return o.transpose(0, 2, 1, 3).reshape(b, n, GROUP_W)


def diff_attention_latent(p, kc, vc, qn, kn, lam, subln, lam_init, cos, sin):
    b, n, _ = p.shape
    q = apply_rope(diff_q(p, qn), cos, sin)
    k, v = diff_kv(p, kn)
    k = apply_rope(k, cos, sin)
    k_all = jnp.concatenate([k, kc], axis=3)
    v_all = jnp.concatenate([v, vc], axis=2)
    nb = n // Q_BLOCK
    qb = jnp.moveaxis(q.reshape(b, N_DIFF, 2, nb, Q_BLOCK, DIFF_QK), 3, 0)
    o = lax.map(lambda qq: diff_core(qq, k_all, v_all, lam), qb)
    o = jnp.moveaxis(o, 0, 2).reshape(b, N_DIFF, n, DIFF_V)
    return diff_finish(o, subln, lam_init)


def na_q(p, qn):
    b, n, _ = p.shape
    return rms_norm(p[..., :GROUP_W].reshape(b, n, N_NA, HEAD_DIM), qn).transpose(0, 2, 1, 3)


def na_kv(p, kn):
    b, n, _ = p.shape
    k = rms_norm(p[..., GROUP_W:2 * GROUP_W].reshape(b, n, N_NA, HEAD_DIM), kn)
    v = p[..., 2 * GROUP_W:].reshape(b, n, N_NA, HEAD_DIM)
    return k.transpose(0, 2, 1, 3), v.transpose(0, 2, 1, 3)


def na_attention_latent(p, kc, vc, qn, kn, rpb):
    b, n, _ = p.shape
    rows = n // GRID_W
    kh = min(NA_KH_MAX, rows)
    n_loc = kh * NA_KW
    q = na_q(p, qn)
    k, v = na_kv(p, kn)
    r = jnp.arange(rows)
    r0 = jnp.clip(r - kh // 2, 0, rows - kh)
    row_off = r0[:, None] + jnp.arange(kh)[None, :] - r[:, None] + NA_KH_MAX - 1
    j = jnp.arange(GRID_W)
    c0 = jnp.clip(j - NA_KW // 2, 0, GRID_W - NA_KW)
    col_idx = c0[:, None] + jnp.arange(NA_KW)[None, :]
    col_off = col_idx - j[:, None] + NA_KW - 1
    rpb_c = rpb[:, :, col_off]
    kg = k.reshape(b, N_NA, rows, GRID_W, HEAD_DIM)
    vg = v.reshape(b, N_NA, rows, GRID_W, HEAD_DIM)
    scale = HEAD_DIM ** -0.5

    def row_block(args):
        qr, start, roff = args
        kr = lax.dynamic_slice_in_dim(kg, start, kh, axis=2)[:, :, :, col_idx]
        vr = lax.dynamic_slice_in_dim(vg, start, kh, axis=2)[:, :, :, col_idx]
        bias = jnp.transpose(rpb_c[:, roff], (0, 2, 1, 3)).astype(jnp.float32)
        s_loc = jnp.einsum('bhjd,bhajkd->bhjak', qr, kr).astype(jnp.float32) * scale + bias
        s_ctx = jnp.einsum('bhjd,bhcd->bhjc', qr, kc).astype(jnp.float32) * scale
        s = jnp.concatenate([s_loc.reshape(b, N_NA, GRID_W, n_loc), s_ctx], axis=-1)
        a = jax.nn.softmax(s, axis=-1).astype(vr.dtype)
        a_loc = a[..., :n_loc].reshape(b, N_NA, GRID_W, kh, NA_KW)
        return (jnp.einsum('bhjak,bhajkd->bhjd', a_loc, vr)
                + jnp.einsum('bhjc,bhcd->bhjd', a[..., n_loc:], vc))

    qg = jnp.moveaxis(q.reshape(b, N_NA, rows, GRID_W, HEAD_DIM), 2, 0)
    o = lax.map(row_block, (qg, r0, row_off))
    o = jnp.moveaxis(o, 0, 2).reshape(b, N_NA, n, HEAD_DIM)
    return o.transpose(0, 2, 1, 3).reshape(b, n, GROUP_W)


def na_attention_context(p, kc, vc, qn):
    b, n, _ = p.shape
    o = dense_attention(na_q(p, qn), kc, vc)
    return o.transpose(0, 2, 1, 3).reshape(b, n, GROUP_W)


def chunk_gmlp(p, gn, ws, bs):
    b, n, _ = p.shape
    u, v = jnp.split(jax.nn.gelu(p), 2, axis=-1)
    v = rms_norm(v.reshape(b, n // CHUNK, CHUNK, N_GMLP, HEAD_DIM), gn.reshape(N_GMLP, HEAD_DIM))
    s = jnp.einsum('gpq,bnqgc->bnpgc', ws, v) + jnp.swapaxes(bs, 0, 1)[:, :, None]
    return u * s.reshape(b, n, GROUP_W)


def conv_ffn(h, w_up, w_conv, b_conv, w_down):
    z = h @ w_up
    ch = z.shape[-1]
    z = lax.conv_general_dilated(z, w_conv[:, None, :].astype(z.dtype), window_strides=(1,),
                                 padding=((CONV_W // 2, CONV_W // 2),),
                                 dimension_numbers=('NWC', 'WIO', 'NWC'),
                                 feature_group_count=ch) + b_conv
    g, v = jnp.split(z, 2, axis=-1)
    return (jax.nn.silu(g) * v) @ w_down


def setup_inputs(seed: int = 0) -> dict:
    key = jax.random.key(seed)
    ks = jax.random.split(key, 24)
    f32 = jnp.float32

    def nrm(k, shape, s):
        return jax.random.normal(k, shape, f32) * s

    def gain(k, shape):
        return 1.0 + 0.02 * jax.random.normal(k, shape, f32)

    return {
        "x": nrm(ks[0], (BATCH, SEQ, D_MODEL), 1.0),
        "c": nrm(ks[1], (BATCH, D_MODEL), 1.0),
        "ctx": nrm(ks[2], (BATCH, CTX_LEN, D_MODEL), 1.0),
        "c_ctx": nrm(ks[3], (D_MODEL,), 1.0),
        "w_ada": nrm(ks[4], (DEPTH, D_MODEL, 6 * D_MODEL), 0.5 * D_MODEL ** -0.5),
        "b_ada": nrm(ks[5], (DEPTH, 6 * D_MODEL), 0.02),
        "g_mix": gain(ks[6], (DEPTH, D_MODEL)),
        "g_ffn": gain(ks[7], (DEPTH, D_MODEL)),
        "w_in": nrm(ks[8], (DEPTH, D_MODEL, IN_W), D_MODEL ** -0.5),
        "w_out": nrm(ks[9], (DEPTH, MIX_W, D_MODEL), MIX_W ** -0.5),
        "diff_qn": gain(ks[10], (DEPTH, DIFF_QK)),
        "diff_kn": gain(ks[11], (DEPTH, DIFF_QK)),
        "diff_lam": nrm(ks[12], (DEPTH, 4, DIFF_QK), 0.1),
        "diff_subln": gain(ks[13], (DEPTH, DIFF_V)),
        "na_qn": gain(ks[14], (DEPTH, HEAD_DIM)),
        "na_kn": gain(ks[15], (DEPTH, HEAD_DIM)),
        "na_rpb": nrm(ks[16], (DEPTH, N_NA, 2 * NA_KH_MAX - 1, 2 * NA_KW - 1), 0.02),
        "gmlp_norm": gain(ks[17], (DEPTH, GROUP_W)),
        "gmlp_ws": nrm(ks[18], (DEPTH, N_GMLP, CHUNK, CHUNK), CHUNK ** -0.5),
        "gmlp_b": nrm(ks[19], (DEPTH, N_GMLP, CHUNK), 0.02),
        "ffn_up": nrm(ks[20], (DEPTH, D_MODEL, 2 * D_FF), D_MODEL ** -0.5),
        "ffn_conv": nrm(ks[21], (DEPTH, CONV_W, 2 * D_FF), CONV_W ** -0.5),
        "ffn_conv_b": nrm(ks[22], (DEPTH, 2 * D_FF), 0.02),
        "ffn_down": nrm(ks[23], (DEPTH, D_FF, D_MODEL), D_FF ** -0.5),
    }


def reference(x, c, ctx, c_ctx, w_ada, b_ada, g_mix, g_ffn, w_in, w_out, diff_qn, diff_kn, diff_lam,
              diff_subln, na_qn, na_kn, na_rpb, gmlp_norm, gmlp_ws, gmlp_b, ffn_up, ffn_conv, ffn_conv_b,
              ffn_down):
    n = x.shape[1]
    G = GROUP_W
    cos, sin = axial_rope_tables(n, DIFF_QK, x.dtype)
    s_lat = jax.nn.silu(c)[:, None, :]
    s_ctx = jax.nn.silu(c_ctx)
    for li in range(DEPTH):
        ctx_out = li < DEPTH - 1
        lam_init = 0.8 - 0.6 * math.exp(-0.3 * li)
        lf = diff_lam[li].astype(jnp.float32)
        lam = jnp.exp(jnp.sum(lf[0] * lf[1])) - jnp.exp(jnp.sum(lf[2] * lf[3])) + lam_init
        m_lat = jnp.split(s_lat @ w_ada[li] + b_ada[li], 6, axis=-1)
        m_ctx = jnp.split(s_ctx @ w_ada[li] + b_ada[li], 6, axis=-1)

        h = rms_norm(x, g_mix[li]) * (1.0 + m_lat[1]) + m_lat[0]
        hc = rms_norm(ctx, g_mix[li]) * (1.0 + m_ctx[1]) + m_ctx[0]
        p = h @ w_in[li]
        pc = hc @ w_in[li]
        kc_d, vc_d = diff_kv(pc[..., G:4 * G], diff_kn[li])
        kc_n, vc_n = na_kv(pc[..., 4 * G:7 * G], na_kn[li])
        y = jnp.concatenate([
            fourier_mix(p[..., :G]),
            diff_attention_latent(p[..., G:4 * G], kc_d, vc_d, diff_qn[li], diff_kn[li], lam,
                                  diff_subln[li], lam_init, cos, sin),
            na_attention_latent(p[..., 4 * G:7 * G], kc_n, vc_n, na_qn[li], na_kn[li], na_rpb[li]),
            chunk_gmlp(p[..., 7 * G:], gmlp_norm[li], gmlp_ws[li], gmlp_b[li]),
        ], axis=-1)
        x = x + m_lat[2] * (y @ w_out[li])
        if ctx_out:
            yc = jnp.concatenate([
                fourier_mix(pc[..., :G]),
                diff_finish(diff_core(diff_q(pc[..., G:4 * G], diff_qn[li]), kc_d, vc_d, lam),
                            diff_subln[li], lam_init),
                na_attention_context(pc[..., 4 * G:7 * G], kc_n, vc_n, na_qn[li]),
                chunk_gmlp(pc[..., 7 * G:], gmlp_norm[li], gmlp_ws[li], gmlp_b[li]),
            ], axis=-1)
            ctx = ctx + m_ctx[2] * (yc @ w_out[li])

        h = rms_norm(x, g_ffn[li]) * (1.0 + m_lat[4]) + m_lat[3]
        x = x + m_lat[5] * conv_ffn(h, ffn_up[li], ffn_conv[li], ffn_conv_b[li], ffn_down[li])
        if ctx_out:
            hc = rms_norm(ctx, g_ffn[li]) * (1.0 + m_ctx[4]) + m_ctx[3]
            ctx = ctx + m_ctx[5] * conv_ffn(hc, ffn_up[li], ffn_conv[li], ffn_conv_b[li], ffn_down[li])
    return x
```

```python
import functools
import math

import jax
import jax.numpy as jnp
import numpy as np
from jax import lax
from jax.experimental import pallas as pl
from jax.experimental.pallas import tpu as pltpu

F32 = jnp.float32
BF16 = jnp.bfloat16

D_MODEL = 1024
DEPTH = 4
GRID_W = 64
GROUP_W = 256
HEAD_DIM = 64
N_HEADS = 4
DIFF_QK = 32
NA_KH = 8
NA_KW = 16
CHUNK = 128
D_FF = 2816
ROPE_BASE = 10000.0
EPS = 1e-6
IN_W = 9 * GROUP_W

COL_F, COL_DQ, COL_DK, COL_DV, COL_NQ, COL_NK, COL_NV, COL_GU, COL_GV = range(9)

NA_Q_ROWS = 4
NA_WIN_ROWS = 12
NEG = -1e30

ADA_ROWS = 16
VMEM_LIMIT = 56 * 1024 * 1024
HALO = 16
FF_CHUNK = 256


def _dot(a, b):
    return jnp.dot(a, b, preferred_element_type=F32)


def _dot_nt(a, b):
    return lax.dot_general(a, b, (((1,), (1,)), ((), ())), preferred_element_type=F32)


def _resident(shape):
    nd = len(shape)
    return pl.BlockSpec(shape, lambda *_: (0,) * nd, pipeline_mode=pl.Buffered(1))


def _params(n_axes):
    return pltpu.CompilerParams(dimension_semantics=("arbitrary",) * n_axes, vmem_limit_bytes=VMEM_LIMIT)


def _lane(width=GROUP_W):
    return lax.broadcasted_iota(jnp.int32, (1, width), 1)


def _ada_kernel(c_ref, w_ref, b_ref, o_ref):
    c = c_ref[...]
    s = c / (1.0 + jnp.exp(-c))
    w = w_ref[0]
    s_hi = s.astype(BF16)
    s_lo = (s - s_hi.astype(F32)).astype(BF16)
    w_hi = w.astype(BF16)
    w_lo = (w - w_hi.astype(F32)).astype(BF16)
    o_ref[0] = _dot(s_hi, w_hi) + _dot(s_lo, w_hi) + _dot(s_hi, w_lo) + b_ref[0]


def _ada(c_all, w_ada, b_ada):
    tn = 1536
    out_w = 6 * D_MODEL
    return pl.pallas_call(
        _ada_kernel,
        grid=(DEPTH, out_w // tn),
        in_specs=[
            pl.BlockSpec((ADA_ROWS, D_MODEL), lambda l, j: (0, 0)),
            pl.BlockSpec((1, D_MODEL, tn), lambda l, j: (l, 0, j)),
            pl.BlockSpec((1, 1, tn), lambda l, j: (l, 0, j)),
        ],
        out_specs=pl.BlockSpec((1, ADA_ROWS, tn), lambda l, j: (l, 0, j)),
        out_shape=jax.ShapeDtypeStruct((DEPTH, ADA_ROWS, out_w), F32),
        compiler_params=_params(2),
        name="ada",
    )(c_all, w_ada, b_ada.reshape(DEPTH, 1, out_w))


def _gelu(x):
    return 0.5 * x * (1.0 + jnp.tanh(math.sqrt(2.0 / math.pi) * (x + 0.044715 * (x * x * x))))


def _inproj_kernel(*refs, rope):
    if rope:
        x_ref, mod_ref, g_ref, w_ref, gains_ref, g32_ref, g64_ref, cos_ref, sin_ref, o_ref = refs
    else:
        x_ref, mod_ref, g_ref, w_ref, gains_ref, g32_ref, g64_ref, o_ref = refs
    x = x_ref[...]
    ms = jnp.mean(x * x, axis=-1, keepdims=True)
    h = x * lax.rsqrt(ms + EPS) * g_ref[...]
    h = h * (1.0 + mod_ref[0, 1:2, :]) + mod_ref[0, 0:1, :]
    p = _dot(h.astype(BF16), w_ref[...])

    def col(j):
        return p[:, j * GROUP_W:(j + 1) * GROUP_W]

    def put(j, v):
        o_ref[:, j * GROUP_W:(j + 1) * GROUP_W] = v.astype(BF16)

    def group_norm(v, gmat_ref, gain_row):
        gms = _dot((v * v).astype(BF16), gmat_ref[...])
        return v * lax.rsqrt(gms + EPS) * gains_ref[gain_row:gain_row + 1, :]

    def rotary(v):
        if not rope:
            return v
        first_half = (_lane() % DIFF_QK) < (DIFF_QK // 2)
        partner = jnp.where(first_half, pltpu.roll(v, GROUP_W - DIFF_QK // 2, 1), pltpu.roll(v, DIFF_QK // 2, 1))
        return v * cos_ref[...] + partner * sin_ref[...]

    put(COL_F, col(COL_F))
    put(COL_DQ, rotary(group_norm(col(COL_DQ), g32_ref, 0)))
    put(COL_DK, rotary(group_norm(col(COL_DK), g32_ref, 1)))
    put(COL_DV, col(COL_DV))
    put(COL_NQ, group_norm(col(COL_NQ), g64_ref, 2))
    put(COL_NK, group_norm(col(COL_NK), g64_ref, 3))
    put(COL_NV, col(COL_NV))
    put(COL_GU, _gelu(col(COL_GU)))
    put(COL_GV, group_norm(_gelu(col(COL_GV)), g64_ref, 4))


def _inproj(x2d, mod, g_row, w_bf, gains, g32, g64, rope_tabs, n, tm, ctx_stream):
    rows = x2d.shape[0]
    tpb = n // tm
    mod_row = (lambda i: (ADA_ROWS // 2, 0, 0)) if ctx_stream else (lambda i: (i // tpb, 0, 0))
    in_specs = [
        pl.BlockSpec((tm, D_MODEL), lambda i: (i, 0)),
        pl.BlockSpec((1, 6, D_MODEL), mod_row),
        _resident((1, D_MODEL)),
        _resident((D_MODEL, IN_W)),
        _resident((8, GROUP_W)),
        _resident((GROUP_W, GROUP_W)),
        _resident((GROUP_W, GROUP_W)),
    ]
    args = [x2d, mod, g_row, w_bf, gains, g32, g64]
    if rope_tabs is not None:
        in_specs += [pl.BlockSpec((tm, GROUP_W), lambda i: (i % tpb, 0))] * 2
        args += list(rope_tabs)
    return pl.pallas_call(
        functools.partial(_inproj_kernel, rope=rope_tabs is not None),
        grid=(rows // tm,),
        in_specs=in_specs,
        out_specs=pl.BlockSpec((tm, IN_W), lambda i: (i, 0)),
        out_shape=jax.ShapeDtypeStruct((rows, IN_W), BF16),
        compiler_params=_params(1),
        name="inproj_ctx" if ctx_stream else "inproj",
    )(*args)


def _fourier_kernel(p_ref, bdc_ref, bds_ref, cn_ref, sn_ref, o_ref):
    pf = p_ref[0]
    a = _dot(pf, bdc_ref[...]).astype(BF16)
    b = _dot(pf, bds_ref[...]).astype(BF16)
    o_ref[0] = (_dot(cn_ref[...], a) - _dot(sn_ref[...], b)).astype(BF16)


def _fourier(p3, bdc, bds, cn, sn, name):
    b, n, _ = p3.shape
    return pl.pallas_call(
        _fourier_kernel,
        grid=(b,),
        in_specs=[
            pl.BlockSpec((1, n, GROUP_W), lambda i: (i, 0, COL_F)),
            _resident((GROUP_W, GROUP_W)),
            _resident((GROUP_W, GROUP_W)),
            _resident((n, n)),
            _resident((n, n)),
        ],
        out_specs=pl.BlockSpec((1, n, GROUP_W), lambda i: (i, 0, 0)),
        out_shape=jax.ShapeDtypeStruct((b, n, GROUP_W), BF16),
        compiler_params=_params(1),
        name=name,
    )(p3, bdc, bds, cn, sn)


def _softmax_parts(parts):
    mx = parts[0].max(axis=-1, keepdims=True)
    for s in parts[1:]:
        mx = jnp.maximum(mx, s.max(axis=-1, keepdims=True))
    es = [jnp.exp(s - mx) for s in parts]
    tot = es[0].sum(axis=-1, keepdims=True)
    for e in es[1:]:
        tot = tot + e.sum(axis=-1, keepdims=True)
    return es, tot


def _diff_kernel(*refs, has_lat):
    if has_lat:
        lam_ref, q_ref, kl_ref, vl_ref, kc_ref, vc_ref, g64_ref, sub_ref, o_ref = refs
        keys = [kl_ref[0], kc_ref[0]]
        vals = [vl_ref[0], vc_ref[0]]
    else:
        lam_ref, q_ref, kc_ref, vc_ref, g64_ref, sub_ref, o_ref = refs
        keys = [kc_ref[0]]
        vals = [vc_ref[0]]
    q = q_ref[0]
    lam = lam_ref[0]
    lane = _lane()
    acc = jnp.zeros(q.shape, F32)
    for h in range(N_HEADS):
        sm = []
        for c in range(2):
            lo = h * HEAD_DIM + c * DIFF_QK
            qm = jnp.where((lane >= lo) & (lane < lo + DIFF_QK), q, jnp.zeros_like(q))
            sm.append(_softmax_parts([_dot_nt(qm, k) for k in keys]))
        (e1, l1), (e2, l2) = sm
        r1 = 1.0 / l1
        r2 = lam / l2
        o = None
        for ea, eb, v in zip(e1, e2, vals):
            t = _dot((ea * r1 - eb * r2).astype(BF16), v)
            o = t if o is None else o + t
        acc = jnp.where((lane >= h * HEAD_DIM) & (lane < (h + 1) * HEAD_DIM), o, acc)
    gms = _dot((acc * acc).astype(BF16), g64_ref[...])
    o_ref[0] = (acc * lax.rsqrt(gms + EPS) * sub_ref[...]).astype(BF16)


def _diff(lam, p3, pc3, g64, sub_row, tq, has_lat):
    q_src = p3 if has_lat else pc3
    b, n, _ = q_src.shape
    nc = pc3.shape[1]
    smem = pl.BlockSpec(memory_space=pltpu.SMEM)
    in_specs = [smem, pl.BlockSpec((1, tq, GROUP_W), lambda i, j: (i, j, COL_DQ))]
    args = [lam, q_src]
    if has_lat:
        in_specs += [pl.BlockSpec((1, n, GROUP_W), lambda i, j: (i, 0, COL_DK)),
                     pl.BlockSpec((1, n, GROUP_W), lambda i, j: (i, 0, COL_DV))]
        args += [p3, p3]
    in_specs += [pl.BlockSpec((1, nc, GROUP_W), lambda i, j: (i, 0, COL_DK)),
                 pl.BlockSpec((1, nc, GROUP_W), lambda i, j: (i, 0, COL_DV)),
                 _resident((GROUP_W, GROUP_W)),
                 _resident((1, GROUP_W))]
    args += [pc3, pc3, g64, sub_row]
    return pl.pallas_call(
        functools.partial(_diff_kernel, has_lat=has_lat),
        grid=(b, n // tq),
        in_specs=in_specs,
        out_specs=pl.BlockSpec((1, tq, GROUP_W), lambda i, j: (i, j, 0)),
        out_shape=jax.ShapeDtypeStruct((b, n, GROUP_W), BF16),
        compiler_params=_params(2),
        name="diff" if has_lat else "diff_ctx",
    )(*args)


def _na_kernel(*refs, local):
    if local:
        q_ref, k0_ref, k1_ref, k2_ref, v0_ref, v1_ref, v2_ref, kc_ref, vc_ref, bias_ref, o_ref = refs
        keys = [k0_ref[0], k1_ref[0], k2_ref[0], kc_ref[0]]
        vals = [v0_ref[0], v1_ref[0], v2_ref[0], vc_ref[0]]
    else:
        q_ref, kc_ref, vc_ref, o_ref = refs
        keys = [kc_ref[0]]
        vals = [vc_ref[0]]
    q = q_ref[0]
    lane = _lane()
    blk = q.shape[0]
    acc = jnp.zeros(q.shape, F32)
    for h in range(N_HEADS):
        head = (lane >= h * HEAD_DIM) & (lane < (h + 1) * HEAD_DIM)
        qm = jnp.where(head, q, jnp.zeros_like(q))
        parts = [_dot_nt(qm, k) for k in keys]
        if local:
            for j in range(3):
                parts[j] = parts[j] + bias_ref[0, h, :, j * blk:(j + 1) * blk]
        es, tot = _softmax_parts(parts)
        r = 1.0 / tot
        o = None
        for e, v in zip(es, vals):
            t = _dot((e * r).astype(BF16), v)
            o = t if o is None else o + t
        acc = jnp.where(head, o, acc)
    o_ref[0] = acc.astype(BF16)


def _na(p3, pc3, bias):
    b, n, _ = p3.shape
    nc = pc3.shape[1]
    blk = NA_Q_ROWS * GRID_W
    steps = n // blk
    last_win = n // blk - NA_WIN_ROWS // NA_Q_ROWS

    def win(j, colblk):
        return lambda g, i: (i, jnp.clip(g - 1, 0, last_win) + j, colblk)

    def cls(g, i):
        return ((g > 0).astype(jnp.int32) + (g == steps - 1).astype(jnp.int32), 0, 0, 0)

    in_specs = [pl.BlockSpec((1, blk, GROUP_W), lambda g, i: (i, g, COL_NQ))]
    in_specs += [pl.BlockSpec((1, blk, GROUP_W), win(j, COL_NK)) for j in range(3)]
    in_specs += [pl.BlockSpec((1, blk, GROUP_W), win(j, COL_NV)) for j in range(3)]
    in_specs += [pl.BlockSpec((1, nc, GROUP_W), lambda g, i: (i, 0, COL_NK)),
                 pl.BlockSpec((1, nc, GROUP_W), lambda g, i: (i, 0, COL_NV)),
                 pl.BlockSpec((1, N_HEADS, blk, 3 * blk), cls)]
    return pl.pallas_call(
        functools.partial(_na_kernel, local=True),
        grid=(steps, b),
        in_specs=in_specs,
        out_specs=pl.BlockSpec((1, blk, GROUP_W), lambda g, i: (i, g, 0)),
        out_shape=jax.ShapeDtypeStruct((b, n, GROUP_W), BF16),
        compiler_params=_params(2),
        name="na",
    )(p3, p3, p3, p3, p3, p3, p3, pc3, pc3, bias)


def _na_ctx(pc3):
    b, nc, _ = pc3.shape
    return pl.pallas_call(
        functools.partial(_na_kernel, local=False),
        grid=(b,),
        in_specs=[pl.BlockSpec((1, nc, GROUP_W), lambda i: (i, 0, COL_NQ)),
                  pl.BlockSpec((1, nc, GROUP_W), lambda i: (i, 0, COL_NK)),
                  pl.BlockSpec((1, nc, GROUP_W), lambda i: (i, 0, COL_NV))],
        out_specs=pl.BlockSpec((1, nc, GROUP_W), lambda i: (i, 0, 0)),
        out_shape=jax.ShapeDtypeStruct((b, nc, GROUP_W), BF16),
        compiler_params=_params(1),
        name="na_ctx",
    )(pc3, pc3, pc3)


def _na_bias_tables(rpb):
    i = np.arange(NA_Q_ROWS)[:, None]
    a = np.arange(NA_WIN_ROWS)[None, :]
    first_row = NA_WIN_ROWS - NA_KH
    row_ok = np.stack([a + 0 * i < NA_KH, (a >= i) & (a < i + NA_KH), a + 0 * i >= first_row])
    row_off = np.stack([a - i + NA_KH - 1, a - i + NA_KH // 2 - 1, a - i - 1])
    row_off = np.where(row_ok, row_off, 0)
    j = np.arange(GRID_W)[:, None]
    kc = np.arange(GRID_W)[None, :]
    c0 = np.clip(j - NA_KW // 2, 0, GRID_W - NA_KW)
    col_ok = (kc >= c0) & (kc < c0 + NA_KW)
    col_off = np.where(col_ok, kc - j + NA_KW - 1, 0)
    vals = rpb[:, row_off[:, :, :, None, None], col_off[None, None, None, :, :]]
    ok = row_ok[:, :, :, None, None] & col_ok[None, None, None, :, :]
    tab = jnp.where(ok[None], vals.astype(F32), NEG)
    tab = tab.transpose(1, 0, 2, 4, 3, 5)
    return tab.reshape(3, N_HEADS, NA_Q_ROWS * GRID_W, NA_WIN_ROWS * GRID_W)


def _gmlp_kernel(u_ref, v_ref, ws_ref, bias_ref, o_ref, *, chunks_per_step):
    n = u_ref.shape[1]
    lane = _lane()
    step_rows = chunks_per_step * CHUNK
    for s in range(n // step_rows):
        base = s * step_rows
        vcat = jnp.concatenate(
            [v_ref[0, base + c * CHUNK: base + (c + 1) * CHUNK, :] for c in range(chunks_per_step)], axis=1)
        r = _dot(ws_ref[...], vcat)
        for c in range(chunks_per_step):
            mix = r[0:CHUNK, c * GROUP_W:(c + 1) * GROUP_W]
            for g in range(1, N_HEADS):
                mix = jnp.where(lane >= g * HEAD_DIM, r[g * CHUNK:(g + 1) * CHUNK, c * GROUP_W:(c + 1) * GROUP_W], mix)
            rows = slice(base + c * CHUNK, base + (c + 1) * CHUNK)
            o_ref[0, rows, :] = (u_ref[0, rows, :].astype(F32) * (mix + bias_ref[...])).astype(BF16)


def _gmlp(p3, ws_stack, bias_full, name):
    b, n, _ = p3.shape
    cps = min(4, n // CHUNK)
    return pl.pallas_call(
        functools.partial(_gmlp_kernel, chunks_per_step=cps),
        grid=(b,),
        in_specs=[pl.BlockSpec((1, n, GROUP_W), lambda i: (i, 0, COL_GU)),
                  pl.BlockSpec((1, n, GROUP_W), lambda i: (i, 0, COL_GV)),
                  _resident((N_HEADS * CHUNK, CHUNK)),
                  _resident((CHUNK, GROUP_W))],
        out_specs=pl.BlockSpec((1, n, GROUP_W), lambda i: (i, 0, 0)),
        out_shape=jax.ShapeDtypeStruct((b, n, GROUP_W), BF16),
        compiler_params=_params(1),
        name=name,
    )(p3, p3, ws_stack, bias_full)


def _outproj_kernel(yf_ref, yd_ref, yn_ref, yg_ref, x_ref, mod_ref, g_ref, w_ref, xo_ref, h_ref):
    acc = None
    for j, y_ref in enumerate((yf_ref, yd_ref, yn_ref, yg_ref)):
        t = _dot(y_ref[...], w_ref[j * GROUP_W:(j + 1) * GROUP_W, :])
        acc = t if acc is None else acc + t
    x = x_ref[...] + mod_ref[0, 2:3, :] * acc
    xo_ref[...] = x
    ms = jnp.mean(x * x, axis=-1, keepdims=True)
    h = x * lax.rsqrt(ms + EPS) * g_ref[...]
    h_ref[...] = (h * (1.0 + mod_ref[0, 4:5, :]) + mod_ref[0, 3:4, :]).astype(BF16)


def _outproj(ys, x2d, mod, g_row, w_bf, n, tm, ctx_stream):
    rows = x2d.shape[0]
    tpb = n // tm
    mod_row = (lambda i: (ADA_ROWS // 2, 0, 0)) if ctx_stream else (lambda i: (i // tpb, 0, 0))
    y_spec = pl.BlockSpec((tm, GROUP_W), lambda i: (i, 0))
    x_spec = pl.BlockSpec((tm, D_MODEL), lambda i: (i, 0))
    return pl.pallas_call(
        _outproj_kernel,
        grid=(rows // tm,),
        in_specs=[y_spec] * 4 + [x_spec, pl.BlockSpec((1, 6, D_MODEL), mod_row),
                                 _resident((1, D_MODEL)), _resident((D_MODEL, D_MODEL))],
        out_specs=[x_spec, x_spec],
        out_shape=[jax.ShapeDtypeStruct((rows, D_MODEL), F32), jax.ShapeDtypeStruct((rows, D_MODEL), BF16)],
        input_output_aliases={4: 0},
        compiler_params=_params(1),
        name="outproj_ctx" if ctx_stream else "outproj",
    )(*[y.reshape(rows, GROUP_W) for y in ys], x2d, mod, g_row, w_bf)


def _ffn_kernel(hp_ref, hm_ref, hn_ref, x_ref, mod_ref, wu_ref, cw_ref, cb_ref, wd_ref, o_ref, *, tiles_per_seq):
    tm = hm_ref.shape[0]
    pos = pl.program_id(0) % tiles_per_seq
    hp = hp_ref[...]
    hn = hn_ref[...]
    hp = jnp.where(pos != 0, hp, jnp.zeros_like(hp))
    hn = jnp.where(pos != tiles_per_seq - 1, hn, jnp.zeros_like(hn))
    hcat = jnp.concatenate([hp, hm_ref[...], hn], axis=0)

    def conv(z, lo):
        cols = slice(lo, lo + FF_CHUNK)
        return (z[HALO - 1:HALO - 1 + tm] * cw_ref[0:1, cols] + z[HALO:HALO + tm] * cw_ref[1:2, cols]
                + z[HALO + 1:HALO + 1 + tm] * cw_ref[2:3, cols] + cb_ref[:, cols])

    acc = jnp.zeros((tm, D_MODEL), F32)
    for c in range(D_FF // FF_CHUNK):
        lo_g = c * FF_CHUNK
        lo_v = D_FF + c * FF_CHUNK
        g = conv(_dot(hcat, wu_ref[:, lo_g:lo_g + FF_CHUNK]), lo_g)
        v = conv(_dot(hcat, wu_ref[:, lo_v:lo_v + FF_CHUNK]), lo_v)
        act = g / (1.0 + jnp.exp(-g)) * v
        acc = acc + _dot(act.astype(BF16), wd_ref[lo_g:lo_g + FF_CHUNK, :])
    o_ref[...] = x_ref[...] + mod_ref[0, 5:6, :] * acc


def _ffn(h2d, x2d, mod, wu_bf, conv_w, conv_b, wd_bf, n, tm, ctx_stream):
    rows = x2d.shape[0]
    tpb = n // tm
    per_tile = tm // HALO
    last_halo = rows // HALO - 1
    mod_row = (lambda i: (ADA_ROWS // 2, 0, 0)) if ctx_stream else (lambda i: (i // tpb, 0, 0))
    x_spec = pl.BlockSpec((tm, D_MODEL), lambda i: (i, 0))
    return pl.pallas_call(
        functools.partial(_ffn_kernel, tiles_per_seq=tpb),
        grid=(rows // tm,),
        in_specs=[
            pl.BlockSpec((HALO, D_MODEL), lambda i: (jnp.maximum(i * per_tile - 1, 0), 0)),
            x_spec,
            pl.BlockSpec((HALO, D_MODEL), lambda i: (jnp.minimum((i + 1) * per_tile, last_halo), 0)),
            x_spec,
            pl.BlockSpec((1, 6, D_MODEL), mod_row),
            _resident((D_MODEL, 2 * D_FF)),
            _resident((3, 2 * D_FF)),
            _resident((1, 2 * D_FF)),
            _resident((D_FF, D_MODEL)),
        ],
        out_specs=x_spec,
        out_shape=jax.ShapeDtypeStruct((rows, D_MODEL), F32),
        input_output_aliases={3: 0},
        compiler_params=_params(1),
        name="ffn_ctx" if ctx_stream else "ffn",
    )(h2d, h2d, h2d, x2d, mod, wu_bf, conv_w, conv_b.reshape(1, 2 * D_FF), wd_bf)


def _dft_tables(n):
    t = jnp.arange(n, dtype=jnp.int32)
    ang = ((t[:, None] * t[None, :]) % n).astype(F32) * (2.0 * math.pi / n)
    cn, sn = jnp.cos(ang).astype(BF16), jnp.sin(ang).astype(BF16)
    c = np.arange(HEAD_DIM)
    angc = 2.0 * np.pi * ((c[:, None] * c[None, :]) % HEAD_DIM) / HEAD_DIM
    norm = 1.0 / math.sqrt(n * HEAD_DIM)
    eye = np.eye(N_HEADS)
    bdc = jnp.asarray(np.kron(eye, np.cos(angc) * norm), F32).astype(BF16)
    bds = jnp.asarray(np.kron(eye, np.sin(angc) * norm), F32).astype(BF16)
    return bdc, bds, cn, sn


def _rope_tables(n):
    n_freq = DIFF_QK // 4
    freqs = ROPE_BASE ** (-jnp.arange(n_freq, dtype=F32) / n_freq)
    t = jnp.arange(n)
    row = (t // GRID_W).astype(F32)
    col = (t % GRID_W).astype(F32)
    ang = jnp.concatenate([row[:, None] * freqs, col[:, None] * freqs], axis=-1)
    cos, sin = jnp.cos(ang), jnp.sin(ang)
    reps = GROUP_W // DIFF_QK
    return jnp.tile(jnp.concatenate([cos, cos], axis=-1), (1, reps)), jnp.tile(jnp.concatenate([-sin, sin], axis=-1), (1, reps))


def _group_mean_matrix(group):
    return jnp.asarray(np.kron(np.eye(GROUP_W // group), np.full((group, group), 1.0 / group)), F32).astype(BF16)


def kernel(x, c, ctx, c_ctx, w_ada, b_ada, g_mix, g_ffn, w_in, w_out, diff_qn, diff_kn, diff_lam, diff_subln, na_qn,
           na_kn, na_rpb, gmlp_norm, gmlp_ws, gmlp_b, ffn_up, ffn_conv, ffn_conv_b, ffn_down):
    b, n, d = x.shape
    nc = ctx.shape[1]
    assert (d, n % (NA_Q_ROWS * GRID_W), nc % CHUNK, b < ADA_ROWS // 2 + 1) == (D_MODEL, 0, 0, True)
    tm = 512
    tq = 256

    c_all = jnp.zeros((ADA_ROWS, d), F32).at[:b].set(c).at[ADA_ROWS // 2].set(c_ctx)
    mods = _ada(c_all, w_ada, b_ada).reshape(DEPTH, ADA_ROWS, 6, d)

    g32 = _group_mean_matrix(DIFF_QK)
    g64 = _group_mean_matrix(HEAD_DIM)
    rope_tabs = _rope_tables(n)
    dft_lat = _dft_tables(n)
    dft_ctx = _dft_tables(nc)

    x2d = x.reshape(b * n, d)
    c2d = ctx.reshape(b * nc, d)
    for li in range(DEPTH):
        ctx_out = li < DEPTH - 1
        lam_init = 0.8 - 0.6 * math.exp(-0.3 * li)
        lf = diff_lam[li].astype(F32)
        lam = (jnp.exp(jnp.sum(lf[0] * lf[1])) - jnp.exp(jnp.sum(lf[2] * lf[3])) + lam_init).reshape(1)
        mod = mods[li]
        gains = jnp.zeros((8, GROUP_W), F32)
        gains = gains.at[0].set(jnp.tile(diff_qn[li], GROUP_W // DIFF_QK) * DIFF_QK ** -0.5)
        gains = gains.at[1].set(jnp.tile(diff_kn[li], GROUP_W // DIFF_QK))
        gains = gains.at[2].set(jnp.tile(na_qn[li], N_HEADS) * HEAD_DIM ** -0.5)
        gains = gains.at[3].set(jnp.tile(na_kn[li], N_HEADS))
        gains = gains.at[4].set(gmlp_norm[li])
        sub_row = (jnp.tile(diff_subln[li], N_HEADS) * (1.0 - lam_init)).reshape(1, GROUP_W)
        w_in_bf = w_in[li].astype(BF16)
        w_out_bf = w_out[li].astype(BF16)
        g_mix_row = g_mix[li].reshape(1, d)
        g_ffn_row = g_ffn[li].reshape(1, d)
        ws_stack = gmlp_ws[li].reshape(N_HEADS * CHUNK, CHUNK).astype(BF16)
        gbias = jnp.repeat(gmlp_b[li].T, HEAD_DIM, axis=1)
        bias_tab = _na_bias_tables(na_rpb[li])

        p3 = _inproj(x2d, mod, g_mix_row, w_in_bf, gains, g32, g64, rope_tabs, n, tm, False).reshape(b, n, IN_W)
        pc3 = _inproj(c2d, mod, g_mix_row, w_in_bf, gains, g32, g64, None, nc, nc, True).reshape(b, nc, IN_W)

        ys = [_fourier(p3, *dft_lat, "fourier"),
              _diff(lam, p3, pc3, g64, sub_row, tq, True),
              _na(p3, pc3, bias_tab),
              _gmlp(p3, ws_stack, gbias, "gmlp")]
        x2d, h2d = _outproj(ys, x2d, mod, g_ffn_row, w_out_bf, n, tm, False)

        wu_bf = ffn_up[li].astype(BF16)
        wd_bf = ffn_down[li].astype(BF16)
        x2d = _ffn(h2d, x2d, mod, wu_bf, ffn_conv[li], ffn_conv_b[li], wd_bf, n, tm, False)

        if ctx_out:
            ycs = [_fourier(pc3, *dft_ctx, "fourier_ctx"),
                   _diff(lam, None, pc3, g64, sub_row, nc, False),
                   _na_ctx(pc3),
                   _gmlp(pc3, ws_stack, gbias, "gmlp_ctx")]
            c2d, hc2d = _outproj(ycs, c2d, mod, g_ffn_row, w_out_bf, nc, nc, True)
            c2d = _ffn(hc2d, c2d, mod, wu_bf, ffn_conv[li], ffn_conv_b[li], wd_bf, nc, nc, True)
    return x2d.reshape(b, n, d)
```

```python
import functools
import math

import jax
import jax.numpy as jnp
import numpy as np
from jax import lax
from jax.experimental import pallas as pl
from jax.experimental.pallas import tpu as pltpu

F32 = jnp.float32
BF16 = jnp.bfloat16

D_MODEL = 1024
DEPTH = 4
GRID_W = 64
GROUP_W = 256
HEAD_DIM = 64
N_HEADS = 4
DIFF_QK = 32
NA_KH = 8
NA_KW = 16
CHUNK = 128
D_FF = 2816
ROPE_BASE = 10000.0
EPS = 1e-6
IN_W = 9 * GROUP_W

COL_F, COL_DQ, COL_DK, COL_DV, COL_NQ, COL_NK, COL_NV, COL_GU, COL_GV = range(9)

NA_Q_ROWS = 4
NA_WIN_ROWS = 12
NEG = -1e30
LOG2E = math.log2(math.e)

ADA_ROWS = 16
VMEM_LIMIT = 56 * 1024 * 1024
HALO = 16
FF_CHUNK = 256


def _dot(a, b):
    return jnp.dot(a, b, preferred_element_type=F32)


def _dot_nt(a, b):
    return lax.dot_general(a, b, (((1,), (1,)), ((), ())), preferred_element_type=F32)


def _resident(shape):
    nd = len(shape)
    return pl.BlockSpec(shape, lambda *_: (0,) * nd, pipeline_mode=pl.Buffered(1))


def _params(n_axes):
    return pltpu.CompilerParams(dimension_semantics=("arbitrary",) * n_axes, vmem_limit_bytes=VMEM_LIMIT)


def _lane(width=GROUP_W):
    return lax.broadcasted_iota(jnp.int32, (1, width), 1)


def _ada_kernel(c_ref, w_ref, b_ref, o_ref):
    c = c_ref[...]
    s = c / (1.0 + jnp.exp(-c))
    w = w_ref[0]
    s_hi = s.astype(BF16)
    s_lo = (s - s_hi.astype(F32)).astype(BF16)
    w_hi = w.astype(BF16)
    w_lo = (w - w_hi.astype(F32)).astype(BF16)
    o_ref[0] = _dot(s_hi, w_hi) + _dot(s_lo, w_hi) + _dot(s_hi, w_lo) + b_ref[0]


def _ada(c_all, w_ada, b_ada):
    tn = 1536
    out_w = 6 * D_MODEL
    return pl.pallas_call(
        _ada_kernel,
        grid=(DEPTH, out_w // tn),
        in_specs=[
            pl.BlockSpec((ADA_ROWS, D_MODEL), lambda l, j: (0, 0)),
            pl.BlockSpec((1, D_MODEL, tn), lambda l, j: (l, 0, j)),
            pl.BlockSpec((1, 1, tn), lambda l, j: (l, 0, j)),
        ],
        out_specs=pl.BlockSpec((1, ADA_ROWS, tn), lambda l, j: (l, 0, j)),
        out_shape=jax.ShapeDtypeStruct((DEPTH, ADA_ROWS, out_w), F32),
        compiler_params=_params(2),
        name="ada",
    )(c_all, w_ada, b_ada.reshape(DEPTH, 1, out_w))


def _gelu(x):
    return 0.5 * x * (1.0 + jnp.tanh(math.sqrt(2.0 / math.pi) * (x + 0.044715 * (x * x * x))))


def _inproj_kernel(*refs, rope):
    if rope:
        x_ref, mod_ref, g_ref, w_ref, gains_ref, g32_ref, g64_ref, cos_ref, sin_ref, o_ref = refs
    else:
        x_ref, mod_ref, g_ref, w_ref, gains_ref, g32_ref, g64_ref, o_ref = refs
    x = x_ref[...]
    ms = jnp.mean(x * x, axis=-1, keepdims=True)
    h = x * lax.rsqrt(ms + EPS) * g_ref[...]
    h = h * (1.0 + mod_ref[0, 1:2, :]) + mod_ref[0, 0:1, :]
    p = _dot(h.astype(BF16), w_ref[...])

    def col(j):
        return p[:, j * GROUP_W:(j + 1) * GROUP_W]

    def put(j, v):
        o_ref[:, j * GROUP_W:(j + 1) * GROUP_W] = v.astype(BF16)

    def group_norm(v, gmat_ref, gain_row):
        gms = _dot((v * v).astype(BF16), gmat_ref[...])
        return v * lax.rsqrt(gms + EPS) * gains_ref[gain_row:gain_row + 1, :]

    def rotary(v):
        if not rope:
            return v
        first_half = (_lane() % DIFF_QK) < (DIFF_QK // 2)
        partner = jnp.where(first_half, pltpu.roll(v, GROUP_W - DIFF_QK // 2, 1), pltpu.roll(v, DIFF_QK // 2, 1))
        return v * cos_ref[...] + partner * sin_ref[...]

    put(COL_F, col(COL_F))
    put(COL_DQ, rotary(group_norm(col(COL_DQ), g32_ref, 0)))
    put(COL_DK, rotary(group_norm(col(COL_DK), g32_ref, 1)))
    put(COL_DV, col(COL_DV))
    put(COL_NQ, group_norm(col(COL_NQ), g64_ref, 2))
    put(COL_NK, group_norm(col(COL_NK), g64_ref, 3))
    put(COL_NV, col(COL_NV))
    put(COL_GU, _gelu(col(COL_GU)))
    put(COL_GV, group_norm(_gelu(col(COL_GV)), g64_ref, 4))


def _inproj(x2d, mod, g_row, w_bf, gains, g32, g64, rope_tabs, n, tm, ctx_stream):
    rows = x2d.shape[0]
    tpb = n // tm
    mod_row = (lambda i: (ADA_ROWS // 2, 0, 0)) if ctx_stream else (lambda i: (i // tpb, 0, 0))
    in_specs = [
        pl.BlockSpec((tm, D_MODEL), lambda i: (i, 0)),
        pl.BlockSpec((1, 6, D_MODEL), mod_row),
        _resident((1, D_MODEL)),
        _resident((D_MODEL, IN_W)),
        _resident((8, GROUP_W)),
        _resident((GROUP_W, GROUP_W)),
        _resident((GROUP_W, GROUP_W)),
    ]
    args = [x2d, mod, g_row, w_bf, gains, g32, g64]
    if rope_tabs is not None:
        in_specs += [pl.BlockSpec((tm, GROUP_W), lambda i: (i % tpb, 0))] * 2
        args += list(rope_tabs)
    return pl.pallas_call(
        functools.partial(_inproj_kernel, rope=rope_tabs is not None),
        grid=(rows // tm,),
        in_specs=in_specs,
        out_specs=pl.BlockSpec((tm, IN_W), lambda i: (i, 0)),
        out_shape=jax.ShapeDtypeStruct((rows, IN_W), BF16),
        compiler_params=_params(1),
        name="inproj_ctx" if ctx_stream else "inproj",
    )(*args)


def _fourier_kernel(p_ref, bdc_ref, bds_ref, cn_ref, sn_ref, o_ref):
    pf = p_ref[0]
    a = _dot(pf, bdc_ref[...]).astype(BF16)
    b = _dot(pf, bds_ref[...]).astype(BF16)
    o_ref[0] = (_dot(cn_ref[...], a) - _dot(sn_ref[...], b)).astype(BF16)


def _fourier(p3, bdc, bds, cn, sn, name):
    b, n, _ = p3.shape
    return pl.pallas_call(
        _fourier_kernel,
        grid=(b,),
        in_specs=[
            pl.BlockSpec((1, n, GROUP_W), lambda i: (i, 0, COL_F)),
            _resident((GROUP_W, GROUP_W)),
            _resident((GROUP_W, GROUP_W)),
            _resident((n, n)),
            _resident((n, n)),
        ],
        out_specs=pl.BlockSpec((1, n, GROUP_W), lambda i: (i, 0, 0)),
        out_shape=jax.ShapeDtypeStruct((b, n, GROUP_W), BF16),
        compiler_params=_params(1),
        name=name,
    )(p3, bdc, bds, cn, sn)


def _exp2_parts(parts):
    mx = parts[0].max(axis=-1, keepdims=True)
    for s in parts[1:]:
        mx = jnp.maximum(mx, s.max(axis=-1, keepdims=True))
    return [jnp.exp2(s - mx).astype(BF16) for s in parts]


def _ones_lane(h):
    return ((h + 1) % N_HEADS) * HEAD_DIM


def _with_ones_lane(v, h):
    return jnp.where(_lane() == _ones_lane(h), jnp.ones_like(v), v)


def _row_sum_lane(o, h):
    return jnp.sum(jnp.where(_lane() == _ones_lane(h), o, 0.0), axis=-1, keepdims=True)


def _head_mask(h):
    lane = _lane()
    return (lane >= h * HEAD_DIM) & (lane < (h + 1) * HEAD_DIM)


def _diff_kernel(*refs, has_lat):
    if has_lat:
        lam_ref, q_ref, kl_ref, vl_ref, kc_ref, vc_ref, g64_ref, sub_ref, o_ref, vx_ref = refs
        key_refs, val_refs = [kl_ref, kc_ref], [vl_ref, vc_ref]
    else:
        lam_ref, q_ref, kc_ref, vc_ref, g64_ref, sub_ref, o_ref, vx_ref = refs
        key_refs, val_refs = [kc_ref], [vc_ref]
    starts = [0]
    for r in val_refs[:-1]:
        starts.append(starts[-1] + r.shape[1])

    @pl.when(pl.program_id(1) == 0)
    def _():
        for h in range(N_HEADS):
            for r, lo in zip(val_refs, starts):
                vx_ref[h, lo:lo + r.shape[1], :] = _with_ones_lane(r[0], h)

    q = q_ref[0]
    lam = lam_ref[0]
    lane = _lane()
    acc = jnp.zeros(q.shape, F32)
    for h in range(N_HEADS):
        outs = []
        for c in range(2):
            lo = h * HEAD_DIM + c * DIFF_QK
            qm = jnp.where((lane >= lo) & (lane < lo + DIFF_QK), q, jnp.zeros_like(q))
            es = _exp2_parts([_dot_nt(qm, r[0]) for r in key_refs])
            o = None
            for e, r, klo in zip(es, val_refs, starts):
                t = _dot(e, vx_ref[h, klo:klo + r.shape[1], :])
                o = t if o is None else o + t
            outs.append(o)
        o1, o2 = outs
        o = o1 * (1.0 / _row_sum_lane(o1, h)) - o2 * (lam / _row_sum_lane(o2, h))
        acc = jnp.where(_head_mask(h), o, acc)
    gms = _dot((acc * acc).astype(BF16), g64_ref[...])
    o_ref[0] = (acc * lax.rsqrt(gms + EPS) * sub_ref[...]).astype(BF16)


def _diff(lam, p3, pc3, g64, sub_row, tq, has_lat):
    q_src = p3 if has_lat else pc3
    b, n, _ = q_src.shape
    nc = pc3.shape[1]
    smem = pl.BlockSpec(memory_space=pltpu.SMEM)
    in_specs = [smem, pl.BlockSpec((1, tq, GROUP_W), lambda i, j: (i, j, COL_DQ))]
    args = [lam, q_src]
    if has_lat:
        in_specs += [pl.BlockSpec((1, n, GROUP_W), lambda i, j: (i, 0, COL_DK)),
                     pl.BlockSpec((1, n, GROUP_W), lambda i, j: (i, 0, COL_DV))]
        args += [p3, p3]
    in_specs += [pl.BlockSpec((1, nc, GROUP_W), lambda i, j: (i, 0, COL_DK)),
                 pl.BlockSpec((1, nc, GROUP_W), lambda i, j: (i, 0, COL_DV)),
                 _resident((GROUP_W, GROUP_W)),
                 _resident((1, GROUP_W))]
    args += [pc3, pc3, g64, sub_row]
    return pl.pallas_call(
        functools.partial(_diff_kernel, has_lat=has_lat),
        grid=(b, n // tq),
        in_specs=in_specs,
        out_specs=pl.BlockSpec((1, tq, GROUP_W), lambda i, j: (i, j, 0)),
        out_shape=jax.ShapeDtypeStruct((b, n, GROUP_W), BF16),
        scratch_shapes=[pltpu.VMEM((N_HEADS, (n if has_lat else 0) + nc, GROUP_W), BF16)],
        compiler_params=_params(2),
        name="diff" if has_lat else "diff_ctx",
    )(*args)


def _na_kernel(*refs, local):
    if local:
        q_ref, k0_ref, k1_ref, k2_ref, v0_ref, v1_ref, v2_ref, kc_ref, vc_ref, bias_ref, o_ref = refs
        keys = [k0_ref[0], k1_ref[0], k2_ref[0], kc_ref[0]]
        vals = [v0_ref[0], v1_ref[0], v2_ref[0], vc_ref[0]]
    else:
        q_ref, kc_ref, vc_ref, o_ref = refs
        keys = [kc_ref[0]]
        vals = [vc_ref[0]]
    q = q_ref[0]
    blk = q.shape[0]
    acc = jnp.zeros(q.shape, F32)
    for h in range(N_HEADS):
        head = _head_mask(h)
        qm = jnp.where(head, q, jnp.zeros_like(q))
        parts = [_dot_nt(qm, k) for k in keys]
        if local:
            for j in range(3):
                parts[j] = parts[j] + bias_ref[0, h, :, j * blk:(j + 1) * blk]
        o = None
        for e, v in zip(_exp2_parts(parts), vals):
            t = _dot(e, _with_ones_lane(v, h))
            o = t if o is None else o + t
        acc = jnp.where(head, o * (1.0 / _row_sum_lane(o, h)), acc)
    o_ref[0] = acc.astype(BF16)


def _na(p3, pc3, bias):
    b, n, _ = p3.shape
    nc = pc3.shape[1]
    blk = NA_Q_ROWS * GRID_W
    steps = n // blk
    last_win = n // blk - NA_WIN_ROWS // NA_Q_ROWS

    def win(j, colblk):
        return lambda g, i: (i, jnp.clip(g - 1, 0, last_win) + j, colblk)

    def cls(g, i):
        return ((g > 0).astype(jnp.int32) + (g == steps - 1).astype(jnp.int32), 0, 0, 0)

    in_specs = [pl.BlockSpec((1, blk, GROUP_W), lambda g, i: (i, g, COL_NQ))]
    in_specs += [pl.BlockSpec((1, blk, GROUP_W), win(j, COL_NK)) for j in range(3)]
    in_specs += [pl.BlockSpec((1, blk, GROUP_W), win(j, COL_NV)) for j in range(3)]
    in_specs += [pl.BlockSpec((1, nc, GROUP_W), lambda g, i: (i, 0, COL_NK)),
                 pl.BlockSpec((1, nc, GROUP_W), lambda g, i: (i, 0, COL_NV)),
                 pl.BlockSpec((1, N_HEADS, blk, 3 * blk), cls)]
    return pl.pallas_call(
        functools.partial(_na_kernel, local=True),
        grid=(steps, b),
        in_specs=in_specs,
        out_specs=pl.BlockSpec((1, blk, GROUP_W), lambda g, i: (i, g, 0)),
        out_shape=jax.ShapeDtypeStruct((b, n, GROUP_W), BF16),
        compiler_params=_params(2),
        name="na",
    )(p3, p3, p3, p3, p3, p3, p3, pc3, pc3, bias)


def _na_ctx(pc3):
    b, nc, _ = pc3.shape
    return pl.pallas_call(
        functools.partial(_na_kernel, local=False),
        grid=(b,),
        in_specs=[pl.BlockSpec((1, nc, GROUP_W), lambda i: (i, 0, COL_NQ)),
                  pl.BlockSpec((1, nc, GROUP_W), lambda i: (i, 0, COL_NK)),
                  pl.BlockSpec((1, nc, GROUP_W), lambda i: (i, 0, COL_NV))],
        out_specs=pl.BlockSpec((1, nc, GROUP_W), lambda i: (i, 0, 0)),
        out_shape=jax.ShapeDtypeStruct((b, nc, GROUP_W), BF16),
        compiler_params=_params(1),
        name="na_ctx",
    )(pc3, pc3, pc3)


def _na_bias_tables(rpb):
    i = np.arange(NA_Q_ROWS)[:, None]
    a = np.arange(NA_WIN_ROWS)[None, :]
    first_row = NA_WIN_ROWS - NA_KH
    row_ok = np.stack([a + 0 * i < NA_KH, (a >= i) & (a < i + NA_KH), a + 0 * i >= first_row])
    row_off = np.stack([a - i + NA_KH - 1, a - i + NA_KH // 2 - 1, a - i - 1])
    row_off = np.where(row_ok, row_off, 0)
    j = np.arange(GRID_W)[:, None]
    kc = np.arange(GRID_W)[None, :]
    c0 = np.clip(j - NA_KW // 2, 0, GRID_W - NA_KW)
    col_ok = (kc >= c0) & (kc < c0 + NA_KW)
    onehot = (np.arange(2 * NA_KW - 1)[:, None, None] == (kc - j + NA_KW - 1)[None]) & col_ok[None]
    band = jnp.einsum("hdo,ojk->hdjk", rpb.astype(F32), jnp.asarray(onehot, F32), precision=lax.Precision.HIGHEST)
    pick = (np.arange(2 * NA_KH - 1) == row_off[..., None]) & row_ok[..., None]
    tab = jnp.einsum("kiad,hdjc->khijac", jnp.asarray(pick, F32), band, precision=lax.Precision.HIGHEST)
    ok = row_ok[:, None, :, None, :, None] & col_ok[None, None, None, :, None, :]
    tab = jnp.where(ok, tab * LOG2E, NEG)
    return tab.reshape(3, N_HEADS, NA_Q_ROWS * GRID_W, NA_WIN_ROWS * GRID_W)


def _gmlp_kernel(u_ref, v_ref, ws_ref, bias_ref, o_ref, *, chunks_per_step):
    n = u_ref.shape[1]
    lane = _lane()
    step_rows = chunks_per_step * CHUNK
    for s in range(n // step_rows):
        base = s * step_rows
        vcat = jnp.concatenate(
            [v_ref[0, base + c * CHUNK: base + (c + 1) * CHUNK, :] for c in range(chunks_per_step)], axis=1)
        r = _dot(ws_ref[...], vcat)
        for c in range(chunks_per_step):
            mix = r[0:CHUNK, c * GROUP_W:(c + 1) * GROUP_W]
            for g in range(1, N_HEADS):
                mix = jnp.where(lane >= g * HEAD_DIM, r[g * CHUNK:(g + 1) * CHUNK, c * GROUP_W:(c + 1) * GROUP_W], mix)
            rows = slice(base + c * CHUNK, base + (c + 1) * CHUNK)
            o_ref[0, rows, :] = (u_ref[0, rows, :].astype(F32) * (mix + bias_ref[...])).astype(BF16)


def _gmlp(p3, ws_stack, bias_full, name):
    b, n, _ = p3.shape
    cps = min(4, n // CHUNK)
    return pl.pallas_call(
        functools.partial(_gmlp_kernel, chunks_per_step=cps),
        grid=(b,),
        in_specs=[pl.BlockSpec((1, n, GROUP_W), lambda i: (i, 0, COL_GU)),
                  pl.BlockSpec((1, n, GROUP_W), lambda i: (i, 0, COL_GV)),
                  _resident((N_HEADS * CHUNK, CHUNK)),
                  _resident((CHUNK, GROUP_W))],
        out_specs=pl.BlockSpec((1, n, GROUP_W), lambda i: (i, 0, 0)),
        out_shape=jax.ShapeDtypeStruct((b, n, GROUP_W), BF16),
        compiler_params=_params(1),
        name=name,
    )(p3, p3, ws_stack, bias_full)


def _outproj_kernel(yf_ref, yd_ref, yn_ref, yg_ref, x_ref, mod_ref, g_ref, w_ref, xo_ref, h_ref):
    acc = None
    for j, y_ref in enumerate((yf_ref, yd_ref, yn_ref, yg_ref)):
        t = _dot(y_ref[...], w_ref[j * GROUP_W:(j + 1) * GROUP_W, :])
        acc = t if acc is None else acc + t
    x = x_ref[...] + mod_ref[0, 2:3, :] * acc
    xo_ref[...] = x
    ms = jnp.mean(x * x, axis=-1, keepdims=True)
    h = x * lax.rsqrt(ms + EPS) * g_ref[...]
    h_ref[...] = (h * (1.0 + mod_ref[0, 4:5, :]) + mod_ref[0, 3:4, :]).astype(BF16)


def _outproj(ys, x2d, mod, g_row, w_bf, n, tm, ctx_stream):
    rows = x2d.shape[0]
    tpb = n // tm
    mod_row = (lambda i: (ADA_ROWS // 2, 0, 0)) if ctx_stream else (lambda i: (i // tpb, 0, 0))
    y_spec = pl.BlockSpec((tm, GROUP_W), lambda i: (i, 0))
    x_spec = pl.BlockSpec((tm, D_MODEL), lambda i: (i, 0))
    return pl.pallas_call(
        _outproj_kernel,
        grid=(rows // tm,),
        in_specs=[y_spec] * 4 + [x_spec, pl.BlockSpec((1, 6, D_MODEL), mod_row),
                                 _resident((1, D_MODEL)), _resident((D_MODEL, D_MODEL))],
        out_specs=[x_spec, x_spec],
        out_shape=[jax.ShapeDtypeStruct((rows, D_MODEL), F32), jax.ShapeDtypeStruct((rows, D_MODEL), BF16)],
        input_output_aliases={4: 0},
        compiler_params=_params(1),
        name="outproj_ctx" if ctx_stream else "outproj",
    )(*[y.reshape(rows, GROUP_W) for y in ys], x2d, mod, g_row, w_bf)


def _ffn_kernel(hp_ref, hm_ref, hn_ref, x_ref, mod_ref, wu_ref, cw_ref, cb_ref, wd_ref, o_ref, z_ref, act_ref, *,
                tiles_per_seq):
    tm = hm_ref.shape[0]
    pos = pl.program_id(0) % tiles_per_seq
    hp = hp_ref[...]
    hn = hn_ref[...]
    hp = jnp.where(pos != 0, hp, jnp.zeros_like(hp))
    hn = jnp.where(pos != tiles_per_seq - 1, hn, jnp.zeros_like(hn))
    hcat = jnp.concatenate([hp, hm_ref[...], hn], axis=0)

    def conv(z, lo, slot):
        z_ref[slot] = z
        cols = slice(lo, lo + FF_CHUNK)
        return (z_ref[slot, HALO - 1:HALO - 1 + tm, :] * cw_ref[0:1, cols]
                + z_ref[slot, HALO:HALO + tm, :] * cw_ref[1:2, cols]
                + z_ref[slot, HALO + 1:HALO + 1 + tm, :] * cw_ref[2:3, cols] + cb_ref[:, cols])

    for c in range(D_FF // FF_CHUNK):
        lo_g = c * FF_CHUNK
        lo_v = D_FF + c * FF_CHUNK
        slot = 2 * (c % 2)
        g = conv(_dot(hcat, wu_ref[:, lo_g:lo_g + FF_CHUNK]), lo_g, slot)
        v = conv(_dot(hcat, wu_ref[:, lo_v:lo_v + FF_CHUNK]), lo_v, slot + 1)
        act_ref[:, lo_g:lo_g + FF_CHUNK] = (g / (1.0 + jnp.exp(-g)) * v).astype(BF16)
    o_ref[...] = x_ref[...] + mod_ref[0, 5:6, :] * _dot(act_ref[...], wd_ref[...])


def _ffn(h2d, x2d, mod, wu_bf, conv_w, conv_b, wd_bf, n, tm, ctx_stream):
    rows = x2d.shape[0]
    tpb = n // tm
    per_tile = tm // HALO
    last_halo = rows // HALO - 1
    mod_row = (lambda i: (ADA_ROWS // 2, 0, 0)) if ctx_stream else (lambda i: (i // tpb, 0, 0))
    x_spec = pl.BlockSpec((tm, D_MODEL), lambda i: (i, 0))
    return pl.pallas_call(
        functools.partial(_ffn_kernel, tiles_per_seq=tpb),
        grid=(rows // tm,),
        in_specs=[
            pl.BlockSpec((HALO, D_MODEL), lambda i: (jnp.maximum(i * per_tile - 1, 0), 0)),
            x_spec,
            pl.BlockSpec((HALO, D_MODEL), lambda i: (jnp.minimum((i + 1) * per_tile, last_halo), 0)),
            x_spec,
            pl.BlockSpec((1, 6, D_MODEL), mod_row),
            _resident((D_MODEL, 2 * D_FF)),
            _resident((3, 2 * D_FF)),
            _resident((1, 2 * D_FF)),
            _resident((D_FF, D_MODEL)),
        ],
        out_specs=x_spec,
        out_shape=jax.ShapeDtypeStruct((rows, D_MODEL), F32),
        scratch_shapes=[pltpu.VMEM((4, tm + 2 * HALO, FF_CHUNK), F32), pltpu.VMEM((tm, D_FF), BF16)],
        input_output_aliases={3: 0},
        compiler_params=_params(1),
        name="ffn_ctx" if ctx_stream else "ffn",
    )(h2d, h2d, h2d, x2d, mod, wu_bf, conv_w, conv_b.reshape(1, 2 * D_FF), wd_bf)


def _dft_tables(n):
    t = jnp.arange(n, dtype=jnp.int32)
    ang = ((t[:, None] * t[None, :]) % n).astype(F32) * (2.0 * math.pi / n)
    cn, sn = jnp.cos(ang).astype(BF16), jnp.sin(ang).astype(BF16)
    c = np.arange(HEAD_DIM)
    angc = 2.0 * np.pi * ((c[:, None] * c[None, :]) % HEAD_DIM) / HEAD_DIM
    norm = 1.0 / math.sqrt(n * HEAD_DIM)
    eye = np.eye(N_HEADS)
    bdc = jnp.asarray(np.kron(eye, np.cos(angc) * norm), F32).astype(BF16)
    bds = jnp.asarray(np.kron(eye, np.sin(angc) * norm), F32).astype(BF16)
    return bdc, bds, cn, sn


def _rope_tables(n):
    n_freq = DIFF_QK // 4
    freqs = ROPE_BASE ** (-jnp.arange(n_freq, dtype=F32) / n_freq)
    t = jnp.arange(n)
    row = (t // GRID_W).astype(F32)
    col = (t % GRID_W).astype(F32)
    ang = jnp.concatenate([row[:, None] * freqs, col[:, None] * freqs], axis=-1)
    cos, sin = jnp.cos(ang), jnp.sin(ang)
    reps = GROUP_W // DIFF_QK
    return jnp.tile(jnp.concatenate([cos, cos], axis=-1), (1, reps)), jnp.tile(jnp.concatenate([-sin, sin], axis=-1), (1, reps))


def _group_mean_matrix(group):
    return jnp.asarray(np.kron(np.eye(GROUP_W // group), np.full((group, group), 1.0 / group)), F32).astype(BF16)


def kernel(x, c, ctx, c_ctx, w_ada, b_ada, g_mix, g_ffn, w_in, w_out, diff_qn, diff_kn, diff_lam, diff_subln, na_qn,
           na_kn, na_rpb, gmlp_norm, gmlp_ws, gmlp_b, ffn_up, ffn_conv, ffn_conv_b, ffn_down):
    b, n, d = x.shape
    nc = ctx.shape[1]
    assert (d, n % (NA_Q_ROWS * GRID_W), nc % CHUNK, b < ADA_ROWS // 2 + 1) == (D_MODEL, 0, 0, True)
    tm = 512
    tq = 256

    c_all = jnp.zeros((ADA_ROWS, d), F32).at[:b].set(c).at[ADA_ROWS // 2].set(c_ctx)
    mods = _ada(c_all, w_ada, b_ada).reshape(DEPTH, ADA_ROWS, 6, d)

    g32 = _group_mean_matrix(DIFF_QK)
    g64 = _group_mean_matrix(HEAD_DIM)
    rope_tabs = _rope_tables(n)
    dft_lat = _dft_tables(n)
    dft_ctx = _dft_tables(nc)

    x2d = x.reshape(b * n, d)
    c2d = ctx.reshape(b * nc, d)
    for li in range(DEPTH):
        ctx_out = li < DEPTH - 1
        lam_init = 0.8 - 0.6 * math.exp(-0.3 * li)
        lf = diff_lam[li].astype(F32)
        lam = (jnp.exp(jnp.sum(lf[0] * lf[1])) - jnp.exp(jnp.sum(lf[2] * lf[3])) + lam_init).reshape(1)
        mod = mods[li]
        gains = jnp.zeros((8, GROUP_W), F32)
        gains = gains.at[0].set(jnp.tile(diff_qn[li], GROUP_W // DIFF_QK) * (DIFF_QK ** -0.5 * LOG2E))
        gains = gains.at[1].set(jnp.tile(diff_kn[li], GROUP_W // DIFF_QK))
        gains = gains.at[2].set(jnp.tile(na_qn[li], N_HEADS) * (HEAD_DIM ** -0.5 * LOG2E))
        gains = gains.at[3].set(jnp.tile(na_kn[li], N_HEADS))
        gains = gains.at[4].set(gmlp_norm[li])
        sub_row = (jnp.tile(diff_subln[li], N_HEADS) * (1.0 - lam_init)).reshape(1, GROUP_W)
        w_in_bf = w_in[li].astype(BF16)
        w_out_bf = w_out[li].astype(BF16)
        g_mix_row = g_mix[li].reshape(1, d)
        g_ffn_row = g_ffn[li].reshape(1, d)
        ws_stack = gmlp_ws[li].reshape(N_HEADS * CHUNK, CHUNK).astype(BF16)
        gbias = jnp.repeat(gmlp_b[li].T, HEAD_DIM, axis=1)
        bias_tab = _na_bias_tables(na_rpb[li])

        p3 = _inproj(x2d, mod, g_mix_row, w_in_bf, gains, g32, g64, rope_tabs, n, tm, False).reshape(b, n, IN_W)
        pc3 = _inproj(c2d, mod, g_mix_row, w_in_bf, gains, g32, g64, None, nc, nc, True).reshape(b, nc, IN_W)

        ys = [_fourier(p3, *dft_lat, "fourier"),
              _diff(lam, p3, pc3, g64, sub_row, tq, True),
              _na(p3, pc3, bias_tab),
              _gmlp(p3, ws_stack, gbias, "gmlp")]
        x2d, h2d = _outproj(ys, x2d, mod, g_ffn_row, w_out_bf, n, tm, False)

        wu_bf = ffn_up[li].astype(BF16)
        wd_bf = ffn_down[li].astype(BF16)
        x2d = _ffn(h2d, x2d, mod, wu_bf, ffn_conv[li], ffn_conv_b[li], wd_bf, n, tm, False)

        if ctx_out:
            ycs = [_fourier(pc3, *dft_ctx, "fourier_ctx"),
                   _diff(lam, None, pc3, g64, sub_row, nc, False),
                   _na_ctx(pc3),
                   _gmlp(pc3, ws_stack, gbias, "gmlp_ctx")]
            c2d, hc2d = _outproj(ycs, c2d, mod, g_ffn_row, w_out_bf, nc, nc, True)
            c2d = _ffn(hc2d, c2d, mod, wu_bf, ffn_conv[li], ffn_conv_b[li], wd_bf, nc, nc, True)
    return x2d.reshape(b, n, d)
```

```python
import functools
import math

import jax
import jax.numpy as jnp
import numpy as np
from jax import lax
from jax.experimental import pallas as pl
from jax.experimental.pallas import tpu as pltpu

F32 = jnp.float32
BF16 = jnp.bfloat16

D_MODEL = 1024
DEPTH = 4
GRID_W = 64
GROUP_W = 256
HEAD_DIM = 64
N_HEADS = 4
DIFF_QK = 32
NA_KH = 8
NA_KW = 16
CHUNK = 128
D_FF = 2816
ROPE_BASE = 10000.0
EPS = 1e-6
IN_W = 9 * GROUP_W

COL_F, COL_DQ, COL_DK, COL_DV, COL_NQ, COL_NK, COL_NV, COL_GU, COL_GV = range(9)

VT_ROWS = HEAD_DIM + 16
PAIR_ROWS = 16
SUM_FLOOR = 2.0 ** -80
NA_Q_ROWS = 4
NA_WIN_ROWS = 12
NEG = -1e30
LOG2E = math.log2(math.e)

ADA_ROWS = 16
VMEM_LIMIT = 56 * 1024 * 1024
HALO = 16
FF_CHUNK = 256


def _dot(a, b):
    return jnp.dot(a, b, preferred_element_type=F32)


def _dot_nt(a, b):
    return lax.dot_general(a, b, (((1,), (1,)), ((), ())), preferred_element_type=F32)


def _resident(shape):
    nd = len(shape)
    return pl.BlockSpec(shape, lambda *_: (0,) * nd, pipeline_mode=pl.Buffered(1))


def _params(n_axes):
    return pltpu.CompilerParams(dimension_semantics=("arbitrary",) * n_axes, vmem_limit_bytes=VMEM_LIMIT)


def _lane(width=GROUP_W):
    return lax.broadcasted_iota(jnp.int32, (1, width), 1)


def _ada_kernel(c_ref, w_ref, b_ref, o_ref):
    c = c_ref[...]
    s = c / (1.0 + jnp.exp(-c))
    w = w_ref[0]
    s_hi = s.astype(BF16)
    s_lo = (s - s_hi.astype(F32)).astype(BF16)
    w_hi = w.astype(BF16)
    w_lo = (w - w_hi.astype(F32)).astype(BF16)
    o_ref[0] = _dot(s_hi, w_hi) + _dot(s_lo, w_hi) + _dot(s_hi, w_lo) + b_ref[0]


def _ada(c_all, w_ada, b_ada):
    tn = 1536
    out_w = 6 * D_MODEL
    return pl.pallas_call(
        _ada_kernel,
        grid=(DEPTH, out_w // tn),
        in_specs=[
            pl.BlockSpec((ADA_ROWS, D_MODEL), lambda l, j: (0, 0)),
            pl.BlockSpec((1, D_MODEL, tn), lambda l, j: (l, 0, j)),
            pl.BlockSpec((1, 1, tn), lambda l, j: (l, 0, j)),
        ],
        out_specs=pl.BlockSpec((1, ADA_ROWS, tn), lambda l, j: (l, 0, j)),
        out_shape=jax.ShapeDtypeStruct((DEPTH, ADA_ROWS, out_w), F32),
        compiler_params=_params(2),
        name="ada",
    )(c_all, w_ada, b_ada.reshape(DEPTH, 1, out_w))


def _gelu(x):
    return 0.5 * x * (1.0 + jnp.tanh(math.sqrt(2.0 / math.pi) * (x + 0.044715 * (x * x * x))))


def _inproj_kernel(*refs, rope):
    if rope:
        x_ref, mod_ref, g_ref, w_ref, gains_ref, g32_ref, g64_ref, cos_ref, sin_ref, o_ref = refs
    else:
        x_ref, mod_ref, g_ref, w_ref, gains_ref, g32_ref, g64_ref, o_ref = refs
    x = x_ref[...]
    ms = jnp.mean(x * x, axis=-1, keepdims=True)
    h = x * lax.rsqrt(ms + EPS) * g_ref[...]
    h = h * (1.0 + mod_ref[0, 1:2, :]) + mod_ref[0, 0:1, :]
    p = _dot(h.astype(BF16), w_ref[...])

    def col(j):
        return p[:, j * GROUP_W:(j + 1) * GROUP_W]

    def put(j, v):
        o_ref[:, j * GROUP_W:(j + 1) * GROUP_W] = v.astype(BF16)

    def group_norm(v, gmat_ref, gain_row):
        gms = _dot((v * v).astype(BF16), gmat_ref[...])
        return v * lax.rsqrt(gms + EPS) * gains_ref[gain_row:gain_row + 1, :]

    def rotary(v):
        if not rope:
            return v
        first_half = (_lane() % DIFF_QK) < (DIFF_QK // 2)
        partner = jnp.where(first_half, pltpu.roll(v, GROUP_W - DIFF_QK // 2, 1), pltpu.roll(v, DIFF_QK // 2, 1))
        return v * cos_ref[...] + partner * sin_ref[...]

    put(COL_F, col(COL_F))
    put(COL_DQ, rotary(group_norm(col(COL_DQ), g32_ref, 0)))
    put(COL_DK, rotary(group_norm(col(COL_DK), g32_ref, 1)))
    put(COL_DV, col(COL_DV))
    put(COL_NQ, group_norm(col(COL_NQ), g64_ref, 2))
    put(COL_NK, group_norm(col(COL_NK), g64_ref, 3))
    put(COL_NV, col(COL_NV))
    put(COL_GU, _gelu(col(COL_GU)))
    put(COL_GV, group_norm(_gelu(col(COL_GV)), g64_ref, 4))


def _inproj(x2d, mod, g_row, w_bf, gains, g32, g64, rope_tabs, n, tm, ctx_stream):
    rows = x2d.shape[0]
    tpb = n // tm
    mod_row = (lambda i: (ADA_ROWS // 2, 0, 0)) if ctx_stream else (lambda i: (i // tpb, 0, 0))
    in_specs = [
        pl.BlockSpec((tm, D_MODEL), lambda i: (i, 0)),
        pl.BlockSpec((1, 6, D_MODEL), mod_row),
        _resident((1, D_MODEL)),
        _resident((D_MODEL, IN_W)),
        _resident((8, GROUP_W)),
        _resident((GROUP_W, GROUP_W)),
        _resident((GROUP_W, GROUP_W)),
    ]
    args = [x2d, mod, g_row, w_bf, gains, g32, g64]
    if rope_tabs is not None:
        in_specs += [pl.BlockSpec((tm, GROUP_W), lambda i: (i % tpb, 0))] * 2
        args += list(rope_tabs)
    return pl.pallas_call(
        functools.partial(_inproj_kernel, rope=rope_tabs is not None),
        grid=(rows // tm,),
        in_specs=in_specs,
        out_specs=pl.BlockSpec((tm, IN_W), lambda i: (i, 0)),
        out_shape=jax.ShapeDtypeStruct((rows, IN_W), BF16),
        compiler_params=_params(1),
        name="inproj_ctx" if ctx_stream else "inproj",
    )(*args)


def _fourier_kernel(p_ref, bdc_ref, bds_ref, cn_ref, sn_ref, o_ref):
    pf = p_ref[0]
    a = _dot(pf, bdc_ref[...]).astype(BF16)
    b = _dot(pf, bds_ref[...]).astype(BF16)
    o_ref[0] = (_dot(cn_ref[...], a) - _dot(sn_ref[...], b)).astype(BF16)


def _fourier(p3, bdc, bds, cn, sn, name):
    b, n, _ = p3.shape
    return pl.pallas_call(
        _fourier_kernel,
        grid=(b,),
        in_specs=[
            pl.BlockSpec((1, n, GROUP_W), lambda i: (i, 0, COL_F)),
            _resident((GROUP_W, GROUP_W)),
            _resident((GROUP_W, GROUP_W)),
            _resident((n, n)),
            _resident((n, n)),
        ],
        out_specs=pl.BlockSpec((1, n, GROUP_W), lambda i: (i, 0, 0)),
        out_shape=jax.ShapeDtypeStruct((b, n, GROUP_W), BF16),
        compiler_params=_params(1),
        name=name,
    )(p3, bdc, bds, cn, sn)


def _exp2_parts(parts):
    mx = parts[0].max(axis=-1, keepdims=True)
    for s in parts[1:]:
        mx = jnp.maximum(mx, s.max(axis=-1, keepdims=True))
    return [jnp.exp2(s - mx).astype(BF16) for s in parts]


def _ones_lane(h):
    return ((h + 1) % N_HEADS) * HEAD_DIM


def _with_ones_lane(v, h):
    return jnp.where(_lane() == _ones_lane(h), jnp.ones_like(v), v)


def _row_sum_lane(o, h):
    return jnp.sum(jnp.where(_lane() == _ones_lane(h), o, 0.0), axis=-1, keepdims=True)


def _head_mask(h):
    lane = _lane()
    return (lane >= h * HEAD_DIM) & (lane < (h + 1) * HEAD_DIM)


def _diff_kernel(*refs, has_lat):
    if has_lat:
        lam_ref, q_ref, kl_ref, vl_ref, kc_ref, vc_ref, sub_ref, sel_ref, o_ref, vt_ref, kmax_ref = refs
        key_refs, val_refs = [kl_ref, kc_ref], [vl_ref, vc_ref]
    else:
        lam_ref, q_ref, kc_ref, vc_ref, sub_ref, sel_ref, o_ref, vt_ref, kmax_ref = refs
        key_refs, val_refs = [kc_ref], [vc_ref]
    starts = [0]
    for r in val_refs[:-1]:
        starts.append(starts[-1] + r.shape[1])

    def pair_sq_norms(x):
        xf = x.astype(F32)
        return _dot_nt(sel_ref[...], (xf * xf).astype(BF16))

    @pl.when(pl.program_id(1) == 0)
    def _():
        for r, lo in zip(val_refs, starts):
            rows = r.shape[1]
            vt = r[0].astype(F32).T
            for h in range(N_HEADS):
                vt_ref[h, 0:HEAD_DIM, lo:lo + rows] = vt[h * HEAD_DIM:(h + 1) * HEAD_DIM].astype(BF16)
                vt_ref[h, HEAD_DIM:VT_ROWS, lo:lo + rows] = jnp.ones((VT_ROWS - HEAD_DIM, rows), BF16)
        kmax = None
        for r in key_refs:
            m = pair_sq_norms(r[0]).max(axis=1, keepdims=True)
            kmax = m if kmax is None else jnp.maximum(kmax, m)
        kmax_ref[...] = jnp.broadcast_to(kmax, kmax_ref.shape)

    q = q_ref[0]
    lam = lam_ref[0]
    lane = _lane()
    bound = jnp.sqrt(pair_sq_norms(q) * kmax_ref[:, 0:1])

    def exact_max(parts, pair):
        mx = parts[0].max(axis=0, keepdims=True)
        for s in parts[1:]:
            mx = jnp.maximum(mx, s.max(axis=0, keepdims=True))
        return mx

    def attend(shift_fn):
        outs = []
        for pair in range(2 * N_HEADS):
            h = pair // 2
            lo = pair * DIFF_QK
            qm = jnp.where((lane >= lo) & (lane < lo + DIFF_QK), q, jnp.zeros_like(q))
            parts = [_dot_nt(r[0], qm) for r in key_refs]
            shift = shift_fn(parts, pair)
            o = None
            for s, r, klo in zip(parts, val_refs, starts):
                t = _dot(vt_ref[h, :, klo:klo + r.shape[1]], jnp.exp2(s - shift).astype(BF16))
                o = t if o is None else o + t
            outs.append(o)
        return outs

    def finish(outs):
        heads = []
        for h in range(N_HEADS):
            o1, o2 = outs[2 * h], outs[2 * h + 1]
            o = (o1[0:HEAD_DIM] * (1.0 / o1[HEAD_DIM:HEAD_DIM + 1])
                 - o2[0:HEAD_DIM] * (lam / o2[HEAD_DIM:HEAD_DIM + 1]))
            heads.append(o * lax.rsqrt(jnp.mean(o * o, axis=0, keepdims=True) + EPS))
        o_ref[0] = (jnp.concatenate(heads, axis=0) * sub_ref[...]).T.astype(BF16)

    outs = attend(lambda parts, pair: bound[pair:pair + 1])
    finish(outs)
    smallest = outs[0][HEAD_DIM:HEAD_DIM + 1]
    for o in outs[1:]:
        smallest = jnp.minimum(smallest, o[HEAD_DIM:HEAD_DIM + 1])

    @pl.when(jnp.min(smallest) < SUM_FLOOR)
    def _():
        finish(attend(exact_max))


def _diff(lam, p3, pc3, sub_row, tq, has_lat):
    q_src = p3 if has_lat else pc3
    sub_t = jnp.broadcast_to(sub_row.reshape(GROUP_W, 1), (GROUP_W, tq))
    b, n, _ = q_src.shape
    nc = pc3.shape[1]
    smem = pl.BlockSpec(memory_space=pltpu.SMEM)
    in_specs = [smem, pl.BlockSpec((1, tq, GROUP_W), lambda i, j: (i, j, COL_DQ))]
    args = [lam, q_src]
    if has_lat:
        in_specs += [pl.BlockSpec((1, n, GROUP_W), lambda i, j: (i, 0, COL_DK)),
                     pl.BlockSpec((1, n, GROUP_W), lambda i, j: (i, 0, COL_DV))]
        args += [p3, p3]
    in_specs += [pl.BlockSpec((1, nc, GROUP_W), lambda i, j: (i, 0, COL_DK)),
                 pl.BlockSpec((1, nc, GROUP_W), lambda i, j: (i, 0, COL_DV)),
                 _resident((GROUP_W, tq)),
                 _resident((PAIR_ROWS, GROUP_W))]
    pair_sel = np.arange(PAIR_ROWS)[:, None] == np.arange(GROUP_W)[None, :] // DIFF_QK
    args += [pc3, pc3, sub_t, jnp.asarray(pair_sel, F32).astype(BF16)]
    return pl.pallas_call(
        functools.partial(_diff_kernel, has_lat=has_lat),
        grid=(b, n // tq),
        in_specs=in_specs,
        out_specs=pl.BlockSpec((1, tq, GROUP_W), lambda i, j: (i, j, 0)),
        out_shape=jax.ShapeDtypeStruct((b, n, GROUP_W), BF16),
        scratch_shapes=[pltpu.VMEM((N_HEADS, VT_ROWS, (n if has_lat else 0) + nc), BF16),
                        pltpu.VMEM((PAIR_ROWS, 128), F32)],
        compiler_params=_params(2),
        name="diff" if has_lat else "diff_ctx",
    )(*args)


def _na_kernel(*refs, local):
    if local:
        q_ref, k0_ref, k1_ref, k2_ref, v0_ref, v1_ref, v2_ref, kc_ref, vc_ref, bias_ref, o_ref = refs
        keys = [k0_ref[0], k1_ref[0], k2_ref[0], kc_ref[0]]
        vals = [v0_ref[0], v1_ref[0], v2_ref[0], vc_ref[0]]
    else:
        q_ref, kc_ref, vc_ref, o_ref = refs
        keys = [kc_ref[0]]
        vals = [vc_ref[0]]
    q = q_ref[0]
    blk = q.shape[0]
    acc = jnp.zeros(q.shape, F32)
    for h in range(N_HEADS):
        head = _head_mask(h)
        qm = jnp.where(head, q, jnp.zeros_like(q))
        parts = [_dot_nt(qm, k) for k in keys]
        if local:
            for j in range(3):
                parts[j] = parts[j] + bias_ref[0, h, :, j * blk:(j + 1) * blk]
        o = None
        for e, v in zip(_exp2_parts(parts), vals):
            t = _dot(e, _with_ones_lane(v, h))
            o = t if o is None else o + t
        acc = jnp.where(head, o * (1.0 / _row_sum_lane(o, h)), acc)
    o_ref[0] = acc.astype(BF16)


def _na(p3, pc3, bias):
    b, n, _ = p3.shape
    nc = pc3.shape[1]
    blk = NA_Q_ROWS * GRID_W
    steps = n // blk
    last_win = n // blk - NA_WIN_ROWS // NA_Q_ROWS

    def win(j, colblk):
        return lambda g, i: (i, jnp.clip(g - 1, 0, last_win) + j, colblk)

    def cls(g, i):
        return ((g > 0).astype(jnp.int32) + (g == steps - 1).astype(jnp.int32), 0, 0, 0)

    in_specs = [pl.BlockSpec((1, blk, GROUP_W), lambda g, i: (i, g, COL_NQ))]
    in_specs += [pl.BlockSpec((1, blk, GROUP_W), win(j, COL_NK)) for j in range(3)]
    in_specs += [pl.BlockSpec((1, blk, GROUP_W), win(j, COL_NV)) for j in range(3)]
    in_specs += [pl.BlockSpec((1, nc, GROUP_W), lambda g, i: (i, 0, COL_NK)),
                 pl.BlockSpec((1, nc, GROUP_W), lambda g, i: (i, 0, COL_NV)),
                 pl.BlockSpec((1, N_HEADS, blk, 3 * blk), cls)]
    return pl.pallas_call(
        functools.partial(_na_kernel, local=True),
        grid=(steps, b),
        in_specs=in_specs,
        out_specs=pl.BlockSpec((1, blk, GROUP_W), lambda g, i: (i, g, 0)),
        out_shape=jax.ShapeDtypeStruct((b, n, GROUP_W), BF16),
        compiler_params=_params(2),
        name="na",
    )(p3, p3, p3, p3, p3, p3, p3, pc3, pc3, bias)


def _na_ctx(pc3):
    b, nc, _ = pc3.shape
    return pl.pallas_call(
        functools.partial(_na_kernel, local=False),
        grid=(b,),
        in_specs=[pl.BlockSpec((1, nc, GROUP_W), lambda i: (i, 0, COL_NQ)),
                  pl.BlockSpec((1, nc, GROUP_W), lambda i: (i, 0, COL_NK)),
                  pl.BlockSpec((1, nc, GROUP_W), lambda i: (i, 0, COL_NV))],
        out_specs=pl.BlockSpec((1, nc, GROUP_W), lambda i: (i, 0, 0)),
        out_shape=jax.ShapeDtypeStruct((b, nc, GROUP_W), BF16),
        compiler_params=_params(1),
        name="na_ctx",
    )(pc3, pc3, pc3)


def _na_bias_tables(rpb):
    i = np.arange(NA_Q_ROWS)[:, None]
    a = np.arange(NA_WIN_ROWS)[None, :]
    first_row = NA_WIN_ROWS - NA_KH
    row_ok = np.stack([a + 0 * i < NA_KH, (a >= i) & (a < i + NA_KH), a + 0 * i >= first_row])
    row_off = np.stack([a - i + NA_KH - 1, a - i + NA_KH // 2 - 1, a - i - 1])
    row_off = np.where(row_ok, row_off, 0)
    j = np.arange(GRID_W)[:, None]
    kc = np.arange(GRID_W)[None, :]
    c0 = np.clip(j - NA_KW // 2, 0, GRID_W - NA_KW)
    col_ok = (kc >= c0) & (kc < c0 + NA_KW)
    onehot = (np.arange(2 * NA_KW - 1)[:, None, None] == (kc - j + NA_KW - 1)[None]) & col_ok[None]
    band = jnp.einsum("hdo,ojk->hdjk", rpb.astype(F32), jnp.asarray(onehot, F32), precision=lax.Precision.HIGHEST)
    pick = (np.arange(2 * NA_KH - 1) == row_off[..., None]) & row_ok[..., None]
    tab = jnp.einsum("kiad,hdjc->khijac", jnp.asarray(pick, F32), band, precision=lax.Precision.HIGHEST)
    ok = row_ok[:, None, :, None, :, None] & col_ok[None, None, None, :, None, :]
    tab = jnp.where(ok, tab * LOG2E, NEG)
    return tab.reshape(3, N_HEADS, NA_Q_ROWS * GRID_W, NA_WIN_ROWS * GRID_W)


def _gmlp_kernel(u_ref, v_ref, ws_ref, bias_ref, o_ref, *, chunks_per_step):
    n = u_ref.shape[1]
    lane = _lane()
    step_rows = chunks_per_step * CHUNK
    for s in range(n // step_rows):
        base = s * step_rows
        vcat = jnp.concatenate(
            [v_ref[0, base + c * CHUNK: base + (c + 1) * CHUNK, :] for c in range(chunks_per_step)], axis=1)
        r = _dot(ws_ref[...], vcat)
        for c in range(chunks_per_step):
            mix = r[0:CHUNK, c * GROUP_W:(c + 1) * GROUP_W]
            for g in range(1, N_HEADS):
                mix = jnp.where(lane >= g * HEAD_DIM, r[g * CHUNK:(g + 1) * CHUNK, c * GROUP_W:(c + 1) * GROUP_W], mix)
            rows = slice(base + c * CHUNK, base + (c + 1) * CHUNK)
            o_ref[0, rows, :] = (u_ref[0, rows, :].astype(F32) * (mix + bias_ref[...])).astype(BF16)


def _gmlp(p3, ws_stack, bias_full, name):
    b, n, _ = p3.shape
    cps = min(4, n // CHUNK)
    return pl.pallas_call(
        functools.partial(_gmlp_kernel, chunks_per_step=cps),
        grid=(b,),
        in_specs=[pl.BlockSpec((1, n, GROUP_W), lambda i: (i, 0, COL_GU)),
                  pl.BlockSpec((1, n, GROUP_W), lambda i: (i, 0, COL_GV)),
                  _resident((N_HEADS * CHUNK, CHUNK)),
                  _resident((CHUNK, GROUP_W))],
        out_specs=pl.BlockSpec((1, n, GROUP_W), lambda i: (i, 0, 0)),
        out_shape=jax.ShapeDtypeStruct((b, n, GROUP_W), BF16),
        compiler_params=_params(1),
        name=name,
    )(p3, p3, ws_stack, bias_full)


def _outproj_kernel(yf_ref, yd_ref, yn_ref, yg_ref, x_ref, mod_ref, g_ref, w_ref, xo_ref, h_ref):
    acc = None
    for j, y_ref in enumerate((yf_ref, yd_ref, yn_ref, yg_ref)):
        t = _dot(y_ref[...], w_ref[j * GROUP_W:(j + 1) * GROUP_W, :])
        acc = t if acc is None else acc + t
    x = x_ref[...] + mod_ref[0, 2:3, :] * acc
    xo_ref[...] = x
    ms = jnp.mean(x * x, axis=-1, keepdims=True)
    h = x * lax.rsqrt(ms + EPS) * g_ref[...]
    h_ref[...] = (h * (1.0 + mod_ref[0, 4:5, :]) + mod_ref[0, 3:4, :]).astype(BF16)


def _outproj(ys, x2d, mod, g_row, w_bf, n, tm, ctx_stream):
    rows = x2d.shape[0]
    tpb = n // tm
    mod_row = (lambda i: (ADA_ROWS // 2, 0, 0)) if ctx_stream else (lambda i: (i // tpb, 0, 0))
    y_spec = pl.BlockSpec((tm, GROUP_W), lambda i: (i, 0))
    x_spec = pl.BlockSpec((tm, D_MODEL), lambda i: (i, 0))
    return pl.pallas_call(
        _outproj_kernel,
        grid=(rows // tm,),
        in_specs=[y_spec] * 4 + [x_spec, pl.BlockSpec((1, 6, D_MODEL), mod_row),
                                 _resident((1, D_MODEL)), _resident((D_MODEL, D_MODEL))],
        out_specs=[x_spec, x_spec],
        out_shape=[jax.ShapeDtypeStruct((rows, D_MODEL), F32), jax.ShapeDtypeStruct((rows, D_MODEL), BF16)],
        compiler_params=_params(1),
        name="outproj_ctx" if ctx_stream else "outproj",
    )(*[y.reshape(rows, GROUP_W) for y in ys], x2d, mod, g_row, w_bf)


def _ffn_kernel(hp_ref, hm_ref, hn_ref, x_ref, mod_ref, wu_ref, cw_ref, cb_ref, wd_ref, o_ref, act_ref, *,
                tiles_per_seq):
    tm = hm_ref.shape[0]
    pos = pl.program_id(0) % tiles_per_seq
    hp = hp_ref[...]
    hn = hn_ref[...]
    hp = jnp.where(pos != 0, hp, jnp.zeros_like(hp))
    hn = jnp.where(pos != tiles_per_seq - 1, hn, jnp.zeros_like(hn))
    hcat = jnp.concatenate([hp, hm_ref[...], hn], axis=0)

    def conv(z, lo):
        rows = z.shape[0]
        cols = slice(lo, lo + FF_CHUNK)
        return (pltpu.roll(z, 1, 0)[HALO:HALO + tm] * cw_ref[0:1, cols]
                + z[HALO:HALO + tm] * cw_ref[1:2, cols]
                + pltpu.roll(z, rows - 1, 0)[HALO:HALO + tm] * cw_ref[2:3, cols] + cb_ref[:, cols])

    for c in range(D_FF // FF_CHUNK):
        lo_g = c * FF_CHUNK
        lo_v = D_FF + c * FF_CHUNK
        g = conv(_dot(hcat, wu_ref[:, lo_g:lo_g + FF_CHUNK]), lo_g)
        v = conv(_dot(hcat, wu_ref[:, lo_v:lo_v + FF_CHUNK]), lo_v)
        act_ref[:, lo_g:lo_g + FF_CHUNK] = (g / (1.0 + jnp.exp(-g)) * v).astype(BF16)
    o_ref[...] = x_ref[...] + mod_ref[0, 5:6, :] * _dot(act_ref[...], wd_ref[...])


def _ffn(h2d, x2d, mod, wu_bf, conv_w, conv_b, wd_bf, n, tm, ctx_stream):
    rows = x2d.shape[0]
    tpb = n // tm
    per_tile = tm // HALO
    last_halo = rows // HALO - 1
    mod_row = (lambda i: (ADA_ROWS // 2, 0, 0)) if ctx_stream else (lambda i: (i // tpb, 0, 0))
    x_spec = pl.BlockSpec((tm, D_MODEL), lambda i: (i, 0))
    return pl.pallas_call(
        functools.partial(_ffn_kernel, tiles_per_seq=tpb),
        grid=(rows // tm,),
        in_specs=[
            pl.BlockSpec((HALO, D_MODEL), lambda i: (jnp.maximum(i * per_tile - 1, 0), 0)),
            x_spec,
            pl.BlockSpec((HALO, D_MODEL), lambda i: (jnp.minimum((i + 1) * per_tile, last_halo), 0)),
            x_spec,
            pl.BlockSpec((1, 6, D_MODEL), mod_row),
            _resident((D_MODEL, 2 * D_FF)),
            _resident((3, 2 * D_FF)),
            _resident((1, 2 * D_FF)),
            _resident((D_FF, D_MODEL)),
        ],
        out_specs=x_spec,
        out_shape=jax.ShapeDtypeStruct((rows, D_MODEL), F32),
        scratch_shapes=[pltpu.VMEM((tm, D_FF), BF16)],
        compiler_params=_params(1),
        name="ffn_ctx" if ctx_stream else "ffn",
    )(h2d, h2d, h2d, x2d, mod, wu_bf, conv_w, conv_b.reshape(1, 2 * D_FF), wd_bf)


def _dft_tables(n):
    lo_n = 64
    t = jnp.arange(n, dtype=jnp.int32)

    def table(k):
        ang = ((k[:, None] * t[None, :]) % n).astype(F32) * (2.0 * math.pi / n)
        return jnp.cos(ang), jnp.sin(ang)

    (ch, sh), (cl, sl) = table(jnp.arange(n // lo_n, dtype=jnp.int32) * lo_n), table(jnp.arange(lo_n, dtype=jnp.int32))
    cn = (ch[:, None] * cl[None] - sh[:, None] * sl[None]).reshape(n, n).astype(BF16)
    sn = (sh[:, None] * cl[None] + ch[:, None] * sl[None]).reshape(n, n).astype(BF16)
    c = np.arange(HEAD_DIM)
    angc = 2.0 * np.pi * ((c[:, None] * c[None, :]) % HEAD_DIM) / HEAD_DIM
    norm = 1.0 / math.sqrt(n * HEAD_DIM)
    eye = np.eye(N_HEADS)
    bdc = jnp.asarray(np.kron(eye, np.cos(angc) * norm), F32).astype(BF16)
    bds = jnp.asarray(np.kron(eye, np.sin(angc) * norm), F32).astype(BF16)
    return bdc, bds, cn, sn


def _rope_tables(n):
    n_freq = DIFF_QK // 4
    freqs = ROPE_BASE ** (-jnp.arange(n_freq, dtype=F32) / n_freq)
    t = jnp.arange(n)
    row = (t // GRID_W).astype(F32)
    col = (t % GRID_W).astype(F32)
    ang = jnp.concatenate([row[:, None] * freqs, col[:, None] * freqs], axis=-1)
    cos, sin = jnp.cos(ang), jnp.sin(ang)
    reps = GROUP_W // DIFF_QK
    return jnp.tile(jnp.concatenate([cos, cos], axis=-1), (1, reps)), jnp.tile(jnp.concatenate([-sin, sin], axis=-1), (1, reps))


def _group_mean_matrix(group):
    return jnp.asarray(np.kron(np.eye(GROUP_W // group), np.full((group, group), 1.0 / group)), F32).astype(BF16)


def kernel(x, c, ctx, c_ctx, w_ada, b_ada, g_mix, g_ffn, w_in, w_out, diff_qn, diff_kn, diff_lam, diff_subln, na_qn,
           na_kn, na_rpb, gmlp_norm, gmlp_ws, gmlp_b, ffn_up, ffn_conv, ffn_conv_b, ffn_down):
    b, n, d = x.shape
    nc = ctx.shape[1]
    assert (d, n % (NA_Q_ROWS * GRID_W), nc % CHUNK, b < ADA_ROWS // 2 + 1) == (D_MODEL, 0, 0, True)
    tm = 512
    tq = 512

    c_all = jnp.zeros((ADA_ROWS, d), F32).at[:b].set(c).at[ADA_ROWS // 2].set(c_ctx)
    mods = _ada(c_all, w_ada, b_ada).reshape(DEPTH, ADA_ROWS, 6, d)

    g32 = _group_mean_matrix(DIFF_QK)
    g64 = _group_mean_matrix(HEAD_DIM)
    rope_tabs = _rope_tables(n)
    dft_lat = _dft_tables(n)
    dft_ctx = _dft_tables(nc)

    x2d = x.reshape(b * n, d)
    c2d = ctx.reshape(b * nc, d)
    for li in range(DEPTH):
        ctx_out = li < DEPTH - 1
        lam_init = 0.8 - 0.6 * math.exp(-0.3 * li)
        lf = diff_lam[li].astype(F32)
        lam = (jnp.exp(jnp.sum(lf[0] * lf[1])) - jnp.exp(jnp.sum(lf[2] * lf[3])) + lam_init).reshape(1)
        mod = mods[li]
        gains = jnp.zeros((8, GROUP_W), F32)
        gains = gains.at[0].set(jnp.tile(diff_qn[li], GROUP_W // DIFF_QK) * (DIFF_QK ** -0.5 * LOG2E))
        gains = gains.at[1].set(jnp.tile(diff_kn[li], GROUP_W // DIFF_QK))
        gains = gains.at[2].set(jnp.tile(na_qn[li], N_HEADS) * (HEAD_DIM ** -0.5 * LOG2E))
        gains = gains.at[3].set(jnp.tile(na_kn[li], N_HEADS))
        gains = gains.at[4].set(gmlp_norm[li])
        sub_row = (jnp.tile(diff_subln[li], N_HEADS) * (1.0 - lam_init)).reshape(1, GROUP_W)
        w_in_bf = w_in[li].astype(BF16)
        w_out_bf = w_out[li].astype(BF16)
        g_mix_row = g_mix[li].reshape(1, d)
        g_ffn_row = g_ffn[li].reshape(1, d)
        ws_stack = gmlp_ws[li].reshape(N_HEADS * CHUNK, CHUNK).astype(BF16)
        gbias = jnp.repeat(gmlp_b[li].T, HEAD_DIM, axis=1)
        bias_tab = _na_bias_tables(na_rpb[li])

        p3 = _inproj(x2d, mod, g_mix_row, w_in_bf, gains, g32, g64, rope_tabs, n, tm, False).reshape(b, n, IN_W)
        pc3 = _inproj(c2d, mod, g_mix_row, w_in_bf, gains, g32, g64, None, nc, nc, True).reshape(b, nc, IN_W)

        ys = [_fourier(p3, *dft_lat, "fourier"),
              _diff(lam, p3, pc3, sub_row, tq, True),
              _na(p3, pc3, bias_tab),
              _gmlp(p3, ws_stack, gbias, "gmlp")]
        x2d, h2d = _outproj(ys, x2d, mod, g_ffn_row, w_out_bf, n, tm, False)

        wu_bf = ffn_up[li].astype(BF16)
        wd_bf = ffn_down[li].astype(BF16)
        x2d = _ffn(h2d, x2d, mod, wu_bf, ffn_conv[li], ffn_conv_b[li], wd_bf, n, tm, False)

        if ctx_out:
            ycs = [_fourier(pc3, *dft_ctx, "fourier_ctx"),
                   _diff(lam, None, pc3, sub_row, nc, False),
                   _na_ctx(pc3),
                   _gmlp(pc3, ws_stack, gbias, "gmlp_ctx")]
            c2d, hc2d = _outproj(ycs, c2d, mod, g_ffn_row, w_out_bf, nc, nc, True)
            c2d = _ffn(hc2d, c2d, mod, wu_bf, ffn_conv[li], ffn_conv_b[li], wd_bf, nc, nc, True)
    return x2d.reshape(b, n, d)
```

```python
import functools
import math

import jax
import jax.numpy as jnp
import numpy as np
from jax import lax
from jax.experimental import pallas as pl
from jax.experimental.pallas import tpu as pltpu

F32 = jnp.float32
BF16 = jnp.bfloat16

D_MODEL = 1024
DEPTH = 4
GRID_W = 64
GROUP_W = 256
HEAD_DIM = 64
N_HEADS = 4
DIFF_QK = 32
NA_KH = 8
NA_KW = 16
CHUNK = 128
D_FF = 2816
ROPE_BASE = 10000.0
EPS = 1e-6
IN_W = 9 * GROUP_W

COL_F, COL_DQ, COL_DK, COL_DV, COL_NQ, COL_NK, COL_NV, COL_GU, COL_GV = range(9)

VT_ROWS = HEAD_DIM + 16
PAIR_ROWS = 16
SUM_FLOOR = 2.0 ** -80
NA_Q_ROWS = 4
NA_WIN_ROWS = 12
NEG = -1e30
LOG2E = math.log2(math.e)

ADA_ROWS = 16
VMEM_LIMIT = 56 * 1024 * 1024
HALO = 16
FF_CHUNK = 256


def _dot(a, b):
    return jnp.dot(a, b, preferred_element_type=F32)


def _dot_nt(a, b):
    return lax.dot_general(a, b, (((1,), (1,)), ((), ())), preferred_element_type=F32)


def _resident(shape):
    nd = len(shape)
    return pl.BlockSpec(shape, lambda *_: (0,) * nd, pipeline_mode=pl.Buffered(1))


def _params(n_axes):
    return pltpu.CompilerParams(dimension_semantics=("arbitrary",) * n_axes, vmem_limit_bytes=VMEM_LIMIT)


def _lane(width=GROUP_W):
    return lax.broadcasted_iota(jnp.int32, (1, width), 1)


def _ada_kernel(c_ref, w_ref, b_ref, o_ref):
    c = c_ref[...]
    s = c / (1.0 + jnp.exp(-c))
    w = w_ref[0]
    s_hi = s.astype(BF16)
    s_lo = (s - s_hi.astype(F32)).astype(BF16)
    w_hi = w.astype(BF16)
    w_lo = (w - w_hi.astype(F32)).astype(BF16)
    o_ref[0] = _dot(s_hi, w_hi) + _dot(s_lo, w_hi) + _dot(s_hi, w_lo) + b_ref[0]


def _ada(c_all, w_ada, b_ada):
    tn = 1536
    out_w = 6 * D_MODEL
    return pl.pallas_call(
        _ada_kernel,
        grid=(DEPTH, out_w // tn),
        in_specs=[
            pl.BlockSpec((ADA_ROWS, D_MODEL), lambda l, j: (0, 0)),
            pl.BlockSpec((1, D_MODEL, tn), lambda l, j: (l, 0, j)),
            pl.BlockSpec((1, 1, tn), lambda l, j: (l, 0, j)),
        ],
        out_specs=pl.BlockSpec((1, ADA_ROWS, tn), lambda l, j: (l, 0, j)),
        out_shape=jax.ShapeDtypeStruct((DEPTH, ADA_ROWS, out_w), F32),
        compiler_params=_params(2),
        name="ada",
    )(c_all, w_ada, b_ada.reshape(DEPTH, 1, out_w))


def _gelu(x):
    return 0.5 * x * (1.0 + jnp.tanh(math.sqrt(2.0 / math.pi) * (x + 0.044715 * (x * x * x))))


def _inproj_kernel(*refs, rope):
    if rope:
        x_ref, mod_ref, g_ref, w_ref, gains_ref, g32_ref, g64_ref, cos_ref, sin_ref, o_ref = refs
    else:
        x_ref, mod_ref, g_ref, w_ref, gains_ref, g32_ref, g64_ref, o_ref = refs
    x = x_ref[...]
    ms = jnp.mean(x * x, axis=-1, keepdims=True)
    h = x * lax.rsqrt(ms + EPS) * g_ref[...]
    h = h * (1.0 + mod_ref[0, 1:2, :]) + mod_ref[0, 0:1, :]
    p = _dot(h.astype(BF16), w_ref[...])

    def col(j):
        return p[:, j * GROUP_W:(j + 1) * GROUP_W]

    def put(j, v):
        o_ref[:, j * GROUP_W:(j + 1) * GROUP_W] = v.astype(BF16)

    def group_norm(v, gmat_ref, gain_row):
        gms = _dot((v * v).astype(BF16), gmat_ref[...])
        return v * lax.rsqrt(gms + EPS) * gains_ref[gain_row:gain_row + 1, :]

    def rotary(v):
        if not rope:
            return v
        first_half = (_lane() % DIFF_QK) < (DIFF_QK // 2)
        partner = jnp.where(first_half, pltpu.roll(v, GROUP_W - DIFF_QK // 2, 1), pltpu.roll(v, DIFF_QK // 2, 1))
        return v * cos_ref[...] + partner * sin_ref[...]

    put(COL_F, col(COL_F))
    put(COL_DQ, rotary(group_norm(col(COL_DQ), g32_ref, 0)))
    put(COL_DK, rotary(group_norm(col(COL_DK), g32_ref, 1)))
    put(COL_DV, col(COL_DV))
    put(COL_NQ, group_norm(col(COL_NQ), g64_ref, 2))
    put(COL_NK, group_norm(col(COL_NK), g64_ref, 3))
    put(COL_NV, col(COL_NV))
    put(COL_GU, _gelu(col(COL_GU)))
    put(COL_GV, group_norm(_gelu(col(COL_GV)), g64_ref, 4))


def _inproj(x2d, mod, g_row, w_bf, gains, g32, g64, rope_tabs, n, tm, ctx_stream):
    rows = x2d.shape[0]
    tpb = n // tm
    mod_row = (lambda i: (ADA_ROWS // 2, 0, 0)) if ctx_stream else (lambda i: (i // tpb, 0, 0))
    in_specs = [
        pl.BlockSpec((tm, D_MODEL), lambda i: (i, 0)),
        pl.BlockSpec((1, 6, D_MODEL), mod_row),
        _resident((1, D_MODEL)),
        _resident((D_MODEL, IN_W)),
        _resident((8, GROUP_W)),
        _resident((GROUP_W, GROUP_W)),
        _resident((GROUP_W, GROUP_W)),
    ]
    args = [x2d, mod, g_row, w_bf, gains, g32, g64]
    if rope_tabs is not None:
        in_specs += [pl.BlockSpec((tm, GROUP_W), lambda i: (i % tpb, 0))] * 2
        args += list(rope_tabs)
    return pl.pallas_call(
        functools.partial(_inproj_kernel, rope=rope_tabs is not None),
        grid=(rows // tm,),
        in_specs=in_specs,
        out_specs=pl.BlockSpec((tm, IN_W), lambda i: (i, 0)),
        out_shape=jax.ShapeDtypeStruct((rows, IN_W), BF16),
        compiler_params=_params(1),
        name="inproj_ctx" if ctx_stream else "inproj",
    )(*args)


def _fourier_kernel(p_ref, bdc_ref, bds_ref, cn_ref, sn_ref, o_ref):
    pf = p_ref[0]
    a = _dot(pf, bdc_ref[...]).astype(BF16)
    b = _dot(pf, bds_ref[...]).astype(BF16)
    o_ref[0] = (_dot(cn_ref[...], a) - _dot(sn_ref[...], b)).astype(BF16)


def _fourier(p3, bdc, bds, cn, sn, name):
    b, n, _ = p3.shape
    return pl.pallas_call(
        _fourier_kernel,
        grid=(b,),
        in_specs=[
            pl.BlockSpec((1, n, GROUP_W), lambda i: (i, 0, COL_F)),
            _resident((GROUP_W, GROUP_W)),
            _resident((GROUP_W, GROUP_W)),
            _resident((n, n)),
            _resident((n, n)),
        ],
        out_specs=pl.BlockSpec((1, n, GROUP_W), lambda i: (i, 0, 0)),
        out_shape=jax.ShapeDtypeStruct((b, n, GROUP_W), BF16),
        compiler_params=_params(1),
        name=name,
    )(p3, bdc, bds, cn, sn)


def _exp2_parts(parts):
    mx = parts[0].max(axis=-1, keepdims=True)
    for s in parts[1:]:
        mx = jnp.maximum(mx, s.max(axis=-1, keepdims=True))
    return [jnp.exp2(s - mx).astype(BF16) for s in parts]


def _ones_lane(h):
    return ((h + 1) % N_HEADS) * HEAD_DIM


def _with_ones_lane(v, h):
    return jnp.where(_lane() == _ones_lane(h), jnp.ones_like(v), v)


def _row_sum_lane(o, h):
    return jnp.sum(jnp.where(_lane() == _ones_lane(h), o, 0.0), axis=-1, keepdims=True)


def _head_mask(h):
    lane = _lane()
    return (lane >= h * HEAD_DIM) & (lane < (h + 1) * HEAD_DIM)


def _diff_kernel(*refs, has_lat):
    if has_lat:
        lam_ref, q_ref, kl_ref, vl_ref, kc_ref, vc_ref, sub_ref, sel_ref, o_ref, vt_ref, kmax_ref = refs
        key_refs, val_refs = [kl_ref, kc_ref], [vl_ref, vc_ref]
    else:
        lam_ref, q_ref, kc_ref, vc_ref, sub_ref, sel_ref, o_ref, vt_ref, kmax_ref = refs
        key_refs, val_refs = [kc_ref], [vc_ref]
    starts = [0]
    for r in val_refs[:-1]:
        starts.append(starts[-1] + r.shape[1])

    def pair_sq_norms(x):
        xf = x.astype(F32)
        return _dot_nt(sel_ref[...], (xf * xf).astype(BF16))

    @pl.when(pl.program_id(1) == 0)
    def _():
        for r, lo in zip(val_refs, starts):
            rows = r.shape[1]
            vt = r[0].astype(F32).T
            for h in range(N_HEADS):
                vt_ref[h, 0:HEAD_DIM, lo:lo + rows] = vt[h * HEAD_DIM:(h + 1) * HEAD_DIM].astype(BF16)
                vt_ref[h, HEAD_DIM:VT_ROWS, lo:lo + rows] = jnp.ones((VT_ROWS - HEAD_DIM, rows), BF16)
        kmax = None
        for r in key_refs:
            m = pair_sq_norms(r[0]).max(axis=1, keepdims=True)
            kmax = m if kmax is None else jnp.maximum(kmax, m)
        kmax_ref[...] = jnp.broadcast_to(kmax, kmax_ref.shape)

    q = q_ref[0]
    lam = lam_ref[0]
    lane = _lane()
    bound = jnp.sqrt(pair_sq_norms(q) * kmax_ref[:, 0:1])

    def exact_max(parts, pair):
        mx = parts[0].max(axis=0, keepdims=True)
        for s in parts[1:]:
            mx = jnp.maximum(mx, s.max(axis=0, keepdims=True))
        return mx

    def attend(shift_fn):
        outs = []
        for pair in range(2 * N_HEADS):
            h = pair // 2
            lo = pair * DIFF_QK
            qm = jnp.where((lane >= lo) & (lane < lo + DIFF_QK), q, jnp.zeros_like(q))
            parts = [_dot_nt(r[0], qm) for r in key_refs]
            shift = shift_fn(parts, pair)
            o = None
            for s, r, klo in zip(parts, val_refs, starts):
                t = _dot(vt_ref[h, :, klo:klo + r.shape[1]], jnp.exp2(s - shift).astype(BF16))
                o = t if o is None else o + t
            outs.append(o)
        return outs

    def finish(outs):
        heads = []
        for h in range(N_HEADS):
            o1, o2 = outs[2 * h], outs[2 * h + 1]
            o = (o1[0:HEAD_DIM] * (1.0 / o1[HEAD_DIM:HEAD_DIM + 1])
                 - o2[0:HEAD_DIM] * (lam / o2[HEAD_DIM:HEAD_DIM + 1]))
            heads.append(o * lax.rsqrt(jnp.mean(o * o, axis=0, keepdims=True) + EPS))
        o_ref[0] = (jnp.concatenate(heads, axis=0) * sub_ref[...]).T.astype(BF16)

    outs = attend(lambda parts, pair: bound[pair:pair + 1])
    finish(outs)
    smallest = outs[0][HEAD_DIM:HEAD_DIM + 1]
    for o in outs[1:]:
        smallest = jnp.minimum(smallest, o[HEAD_DIM:HEAD_DIM + 1])

    @pl.when(jnp.min(smallest) < SUM_FLOOR)
    def _():
        finish(attend(exact_max))


def _diff(lam, p3, pc3, sub_row, tq, has_lat):
    q_src = p3 if has_lat else pc3
    sub_t = jnp.broadcast_to(sub_row.reshape(GROUP_W, 1), (GROUP_W, tq))
    b, n, _ = q_src.shape
    nc = pc3.shape[1]
    smem = pl.BlockSpec(memory_space=pltpu.SMEM)
    in_specs = [smem, pl.BlockSpec((1, tq, GROUP_W), lambda i, j: (i, j, COL_DQ))]
    args = [lam, q_src]
    if has_lat:
        in_specs += [pl.BlockSpec((1, n, GROUP_W), lambda i, j: (i, 0, COL_DK)),
                     pl.BlockSpec((1, n, GROUP_W), lambda i, j: (i, 0, COL_DV))]
        args += [p3, p3]
    in_specs += [pl.BlockSpec((1, nc, GROUP_W), lambda i, j: (i, 0, COL_DK)),
                 pl.BlockSpec((1, nc, GROUP_W), lambda i, j: (i, 0, COL_DV)),
                 _resident((GROUP_W, tq)),
                 _resident((PAIR_ROWS, GROUP_W))]
    pair_sel = np.arange(PAIR_ROWS)[:, None] == np.arange(GROUP_W)[None, :] // DIFF_QK
    args += [pc3, pc3, sub_t, jnp.asarray(pair_sel, F32).astype(BF16)]
    return pl.pallas_call(
        functools.partial(_diff_kernel, has_lat=has_lat),
        grid=(b, n // tq),
        in_specs=in_specs,
        out_specs=pl.BlockSpec((1, tq, GROUP_W), lambda i, j: (i, j, 0)),
        out_shape=jax.ShapeDtypeStruct((b, n, GROUP_W), BF16),
        scratch_shapes=[pltpu.VMEM((N_HEADS, VT_ROWS, (n if has_lat else 0) + nc), BF16),
                        pltpu.VMEM((PAIR_ROWS, 128), F32)],
        compiler_params=_params(2),
        name="diff" if has_lat else "diff_ctx",
    )(*args)


def _na_kernel(*refs, local):
    if local:
        q_ref, k0_ref, k1_ref, k2_ref, v0_ref, v1_ref, v2_ref, kc_ref, vc_ref, bias_ref, o_ref = refs
        keys = [k0_ref[0], k1_ref[0], k2_ref[0], kc_ref[0]]
        vals = [v0_ref[0], v1_ref[0], v2_ref[0], vc_ref[0]]
    else:
        q_ref, kc_ref, vc_ref, o_ref = refs
        keys = [kc_ref[0]]
        vals = [vc_ref[0]]
    q = q_ref[0]
    blk = q.shape[0]
    acc = jnp.zeros(q.shape, F32)
    for h in range(N_HEADS):
        head = _head_mask(h)
        qm = jnp.where(head, q, jnp.zeros_like(q))
        parts = [_dot_nt(qm, k) for k in keys]
        if local:
            for j in range(3):
                parts[j] = parts[j] + bias_ref[0, h, :, j * blk:(j + 1) * blk]
        o = None
        for e, v in zip(_exp2_parts(parts), vals):
            t = _dot(e, _with_ones_lane(v, h))
            o = t if o is None else o + t
        acc = jnp.where(head, o * (1.0 / _row_sum_lane(o, h)), acc)
    o_ref[0] = acc.astype(BF16)


def _na(p3, pc3, bias):
    b, n, _ = p3.shape
    nc = pc3.shape[1]
    blk = NA_Q_ROWS * GRID_W
    steps = n // blk
    last_win = n // blk - NA_WIN_ROWS // NA_Q_ROWS

    def win(j, colblk):
        return lambda g, i: (i, jnp.clip(g - 1, 0, last_win) + j, colblk)

    def cls(g, i):
        return ((g > 0).astype(jnp.int32) + (g == steps - 1).astype(jnp.int32), 0, 0, 0)

    in_specs = [pl.BlockSpec((1, blk, GROUP_W), lambda g, i: (i, g, COL_NQ))]
    in_specs += [pl.BlockSpec((1, blk, GROUP_W), win(j, COL_NK)) for j in range(3)]
    in_specs += [pl.BlockSpec((1, blk, GROUP_W), win(j, COL_NV)) for j in range(3)]
    in_specs += [pl.BlockSpec((1, nc, GROUP_W), lambda g, i: (i, 0, COL_NK)),
                 pl.BlockSpec((1, nc, GROUP_W), lambda g, i: (i, 0, COL_NV)),
                 pl.BlockSpec((1, N_HEADS, blk, 3 * blk), cls)]
    return pl.pallas_call(
        functools.partial(_na_kernel, local=True),
        grid=(steps, b),
        in_specs=in_specs,
        out_specs=pl.BlockSpec((1, blk, GROUP_W), lambda g, i: (i, g, 0)),
        out_shape=jax.ShapeDtypeStruct((b, n, GROUP_W), BF16),
        compiler_params=_params(2),
        name="na",
    )(p3, p3, p3, p3, p3, p3, p3, pc3, pc3, bias)


def _na_ctx(pc3):
    b, nc, _ = pc3.shape
    return pl.pallas_call(
        functools.partial(_na_kernel, local=False),
        grid=(b,),
        in_specs=[pl.BlockSpec((1, nc, GROUP_W), lambda i: (i, 0, COL_NQ)),
                  pl.BlockSpec((1, nc, GROUP_W), lambda i: (i, 0, COL_NK)),
                  pl.BlockSpec((1, nc, GROUP_W), lambda i: (i, 0, COL_NV))],
        out_specs=pl.BlockSpec((1, nc, GROUP_W), lambda i: (i, 0, 0)),
        out_shape=jax.ShapeDtypeStruct((b, nc, GROUP_W), BF16),
        compiler_params=_params(1),
        name="na_ctx",
    )(pc3, pc3, pc3)


def _na_row_offsets():
    first_row = NA_WIN_ROWS - NA_KH
    rules = [lambda i, a: a - i + NA_KH - 1 if a < NA_KH else None,
             lambda i, a: a - i + NA_KH // 2 - 1 if i <= a < i + NA_KH else None,
             lambda i, a: a - i - 1 if a >= first_row else None]
    return [[[rule(i, a) for a in range(NA_WIN_ROWS)] for i in range(NA_Q_ROWS)] for rule in rules]


def _na_bias_kernel(band_ref, o_ref):
    masked = jnp.full((GRID_W, GRID_W), NEG, F32)
    for kind, per_query_row in enumerate(_na_row_offsets()):
        for i, offs in enumerate(per_query_row):
            blocks = [masked if d is None else band_ref[0, d] for d in offs]
            o_ref[kind, 0, i * GRID_W:(i + 1) * GRID_W, :] = jnp.concatenate(blocks, axis=1)


def _na_bias_tables(rpb):
    j = np.arange(GRID_W)[:, None]
    kc = np.arange(GRID_W)[None, :]
    c0 = np.clip(j - NA_KW // 2, 0, GRID_W - NA_KW)
    col_ok = (kc >= c0) & (kc < c0 + NA_KW)
    onehot = (np.arange(2 * NA_KW - 1)[:, None, None] == (kc - j + NA_KW - 1)[None]) & col_ok[None]
    band = jnp.einsum("hdo,ojk->hdjk", rpb.astype(F32), jnp.asarray(onehot, F32), precision=lax.Precision.HIGHEST)
    band = jnp.where(col_ok, band * LOG2E, NEG)
    n_off = 2 * NA_KH - 1
    q, k = NA_Q_ROWS * GRID_W, NA_WIN_ROWS * GRID_W
    return pl.pallas_call(
        _na_bias_kernel,
        grid=(N_HEADS,),
        in_specs=[pl.BlockSpec((1, n_off, GRID_W, GRID_W), lambda h: (h, 0, 0, 0))],
        out_specs=pl.BlockSpec((3, 1, q, k), lambda h: (0, h, 0, 0)),
        out_shape=jax.ShapeDtypeStruct((3, N_HEADS, q, k), F32),
        compiler_params=_params(1),
        name="na_bias",
    )(band)


def _gmlp_kernel(u_ref, v_ref, ws_ref, bias_ref, o_ref, *, chunks_per_step):
    n = u_ref.shape[1]
    lane = _lane()
    step_rows = chunks_per_step * CHUNK
    for s in range(n // step_rows):
        base = s * step_rows
        vcat = jnp.concatenate(
            [v_ref[0, base + c * CHUNK: base + (c + 1) * CHUNK, :] for c in range(chunks_per_step)], axis=1)
        r = _dot(ws_ref[...], vcat)
        for c in range(chunks_per_step):
            mix = r[0:CHUNK, c * GROUP_W:(c + 1) * GROUP_W]
            for g in range(1, N_HEADS):
                mix = jnp.where(lane >= g * HEAD_DIM, r[g * CHUNK:(g + 1) * CHUNK, c * GROUP_W:(c + 1) * GROUP_W], mix)
            rows = slice(base + c * CHUNK, base + (c + 1) * CHUNK)
            o_ref[0, rows, :] = (u_ref[0, rows, :].astype(F32) * (mix + bias_ref[...])).astype(BF16)


def _gmlp(p3, ws_stack, bias_full, name):
    b, n, _ = p3.shape
    cps = min(4, n // CHUNK)
    return pl.pallas_call(
        functools.partial(_gmlp_kernel, chunks_per_step=cps),
        grid=(b,),
        in_specs=[pl.BlockSpec((1, n, GROUP_W), lambda i: (i, 0, COL_GU)),
                  pl.BlockSpec((1, n, GROUP_W), lambda i: (i, 0, COL_GV)),
                  _resident((N_HEADS * CHUNK, CHUNK)),
                  _resident((CHUNK, GROUP_W))],
        out_specs=pl.BlockSpec((1, n, GROUP_W), lambda i: (i, 0, 0)),
        out_shape=jax.ShapeDtypeStruct((b, n, GROUP_W), BF16),
        compiler_params=_params(1),
        name=name,
    )(p3, p3, ws_stack, bias_full)


def _mix_ffn_kernel(*refs, tiles_per_seq):
    y_refs, (xp_ref, xm_ref, xn_ref) = refs[:12], refs[12:15]
    mod_ref, g_ref, wo_ref, wu_ref, cw_ref, cb_ref, wd_ref, o_ref, xs_ref, act_ref = refs[15:]
    tm = xm_ref.shape[0]
    pos = pl.program_id(0) % tiles_per_seq

    def with_halo(p_ref, m_ref, n_ref):
        return jnp.concatenate([p_ref[...], m_ref[...], n_ref[...]], axis=0)

    acc = None
    for j in range(4):
        t = _dot(with_halo(*y_refs[3 * j:3 * j + 3]), wo_ref[j * GROUP_W:(j + 1) * GROUP_W, :])
        acc = t if acc is None else acc + t
    x = with_halo(xp_ref, xm_ref, xn_ref) + mod_ref[0, 2:3, :] * acc
    xs_ref[...] = x[HALO:HALO + tm]
    ms = jnp.mean(x * x, axis=-1, keepdims=True)
    h = x * lax.rsqrt(ms + EPS) * g_ref[...]
    h = (h * (1.0 + mod_ref[0, 4:5, :]) + mod_ref[0, 3:4, :]).astype(BF16)
    hp, hn = h[0:HALO], h[HALO + tm:]
    hp = jnp.where(pos != 0, hp, jnp.zeros_like(hp))
    hn = jnp.where(pos != tiles_per_seq - 1, hn, jnp.zeros_like(hn))
    hcat = jnp.concatenate([hp, h[HALO:HALO + tm], hn], axis=0)

    def conv(z, lo):
        rows = z.shape[0]
        cols = slice(lo, lo + FF_CHUNK)
        return (pltpu.roll(z, 1, 0)[HALO:HALO + tm] * cw_ref[0:1, cols]
                + z[HALO:HALO + tm] * cw_ref[1:2, cols]
                + pltpu.roll(z, rows - 1, 0)[HALO:HALO + tm] * cw_ref[2:3, cols] + cb_ref[:, cols])

    for c in range(D_FF // FF_CHUNK):
        lo_g = c * FF_CHUNK
        lo_v = D_FF + c * FF_CHUNK
        g = conv(_dot(hcat, wu_ref[:, lo_g:lo_g + FF_CHUNK]), lo_g)
        v = conv(_dot(hcat, wu_ref[:, lo_v:lo_v + FF_CHUNK]), lo_v)
        act_ref[:, lo_g:lo_g + FF_CHUNK] = (g / (1.0 + jnp.exp(-g)) * v).astype(BF16)
    o_ref[...] = xs_ref[...] + mod_ref[0, 5:6, :] * _dot(act_ref[...], wd_ref[...])


def _mix_ffn(ys, x2d, mod, g_row, wo_bf, wu_bf, conv_w, conv_b, wd_bf, n, tm, ctx_stream):
    rows = x2d.shape[0]
    tpb = n // tm
    per_tile = tm // HALO
    last_halo = rows // HALO - 1
    mod_row = (lambda i: (ADA_ROWS // 2, 0, 0)) if ctx_stream else (lambda i: (i // tpb, 0, 0))

    def halo_specs(width):
        return [pl.BlockSpec((HALO, width), lambda i: (jnp.maximum(i * per_tile - 1, 0), 0)),
                pl.BlockSpec((tm, width), lambda i: (i, 0)),
                pl.BlockSpec((HALO, width), lambda i: (jnp.minimum((i + 1) * per_tile, last_halo), 0))]

    y2d = [y.reshape(rows, GROUP_W) for y in ys]
    return pl.pallas_call(
        functools.partial(_mix_ffn_kernel, tiles_per_seq=tpb),
        grid=(rows // tm,),
        in_specs=halo_specs(GROUP_W) * 4 + halo_specs(D_MODEL) + [
            pl.BlockSpec((1, 6, D_MODEL), mod_row),
            _resident((1, D_MODEL)),
            _resident((D_MODEL, D_MODEL)),
            _resident((D_MODEL, 2 * D_FF)),
            _resident((3, 2 * D_FF)),
            _resident((1, 2 * D_FF)),
            _resident((D_FF, D_MODEL)),
        ],
        out_specs=pl.BlockSpec((tm, D_MODEL), lambda i: (i, 0)),
        out_shape=jax.ShapeDtypeStruct((rows, D_MODEL), F32),
        scratch_shapes=[pltpu.VMEM((tm, D_MODEL), F32), pltpu.VMEM((tm, D_FF), BF16)],
        compiler_params=_params(1),
        name="mix_ffn_ctx" if ctx_stream else "mix_ffn",
    )(*[a for y in y2d for a in (y, y, y)], x2d, x2d, x2d, mod, g_row, wo_bf, wu_bf, conv_w,
      conv_b.reshape(1, 2 * D_FF), wd_bf)


def _dft_tables(n):
    lo_n = 64
    t = jnp.arange(n, dtype=jnp.int32)

    def table(k):
        ang = ((k[:, None] * t[None, :]) % n).astype(F32) * (2.0 * math.pi / n)
        return jnp.cos(ang), jnp.sin(ang)

    (ch, sh), (cl, sl) = table(jnp.arange(n // lo_n, dtype=jnp.int32) * lo_n), table(jnp.arange(lo_n, dtype=jnp.int32))
    cn = (ch[:, None] * cl[None] - sh[:, None] * sl[None]).reshape(n, n).astype(BF16)
    sn = (sh[:, None] * cl[None] + ch[:, None] * sl[None]).reshape(n, n).astype(BF16)
    c = np.arange(HEAD_DIM)
    angc = 2.0 * np.pi * ((c[:, None] * c[None, :]) % HEAD_DIM) / HEAD_DIM
    norm = 1.0 / math.sqrt(n * HEAD_DIM)
    eye = np.eye(N_HEADS)
    bdc = jnp.asarray(np.kron(eye, np.cos(angc) * norm), F32).astype(BF16)
    bds = jnp.asarray(np.kron(eye, np.sin(angc) * norm), F32).astype(BF16)
    return bdc, bds, cn, sn


def _rope_tables(n):
    n_freq = DIFF_QK // 4
    freqs = ROPE_BASE ** (-jnp.arange(n_freq, dtype=F32) / n_freq)
    t = jnp.arange(n)
    row = (t // GRID_W).astype(F32)
    col = (t % GRID_W).astype(F32)
    ang = jnp.concatenate([row[:, None] * freqs, col[:, None] * freqs], axis=-1)
    cos, sin = jnp.cos(ang), jnp.sin(ang)
    reps = GROUP_W // DIFF_QK
    return jnp.tile(jnp.concatenate([cos, cos], axis=-1), (1, reps)), jnp.tile(jnp.concatenate([-sin, sin], axis=-1), (1, reps))


def _group_mean_matrix(group):
    return jnp.asarray(np.kron(np.eye(GROUP_W // group), np.full((group, group), 1.0 / group)), F32).astype(BF16)


def kernel(x, c, ctx, c_ctx, w_ada, b_ada, g_mix, g_ffn, w_in, w_out, diff_qn, diff_kn, diff_lam, diff_subln, na_qn,
           na_kn, na_rpb, gmlp_norm, gmlp_ws, gmlp_b, ffn_up, ffn_conv, ffn_conv_b, ffn_down):
    b, n, d = x.shape
    nc = ctx.shape[1]
    assert (d, n % (NA_Q_ROWS * GRID_W), nc % CHUNK, b < ADA_ROWS // 2 + 1) == (D_MODEL, 0, 0, True)
    tm = 512
    tq = 512

    c_all = jnp.zeros((ADA_ROWS, d), F32).at[:b].set(c).at[ADA_ROWS // 2].set(c_ctx)
    mods = _ada(c_all, w_ada, b_ada).reshape(DEPTH, ADA_ROWS, 6, d)

    g32 = _group_mean_matrix(DIFF_QK)
    g64 = _group_mean_matrix(HEAD_DIM)
    rope_tabs = _rope_tables(n)
    dft_lat = _dft_tables(n)
    dft_ctx = _dft_tables(nc)

    x2d = x.reshape(b * n, d)
    c2d = ctx.reshape(b * nc, d)
    for li in range(DEPTH):
        ctx_out = li < DEPTH - 1
        lam_init = 0.8 - 0.6 * math.exp(-0.3 * li)
        lf = diff_lam[li].astype(F32)
        lam = (jnp.exp(jnp.sum(lf[0] * lf[1])) - jnp.exp(jnp.sum(lf[2] * lf[3])) + lam_init).reshape(1)
        mod = mods[li]
        gains = jnp.zeros((8, GROUP_W), F32)
        gains = gains.at[0].set(jnp.tile(diff_qn[li], GROUP_W // DIFF_QK) * (DIFF_QK ** -0.5 * LOG2E))
        gains = gains.at[1].set(jnp.tile(diff_kn[li], GROUP_W // DIFF_QK))
        gains = gains.at[2].set(jnp.tile(na_qn[li], N_HEADS) * (HEAD_DIM ** -0.5 * LOG2E))
        gains = gains.at[3].set(jnp.tile(na_kn[li], N_HEADS))
        gains = gains.at[4].set(gmlp_norm[li])
        sub_row = (jnp.tile(diff_subln[li], N_HEADS) * (1.0 - lam_init)).reshape(1, GROUP_W)
        w_in_bf = w_in[li].astype(BF16)
        w_out_bf = w_out[li].astype(BF16)
        g_mix_row = g_mix[li].reshape(1, d)
        g_ffn_row = g_ffn[li].reshape(1, d)
        ws_stack = gmlp_ws[li].reshape(N_HEADS * CHUNK, CHUNK).astype(BF16)
        gbias = jnp.repeat(gmlp_b[li].T, HEAD_DIM, axis=1)
        bias_tab = _na_bias_tables(na_rpb[li])

        p3 = _inproj(x2d, mod, g_mix_row, w_in_bf, gains, g32, g64, rope_tabs, n, tm, False).reshape(b, n, IN_W)
        pc3 = _inproj(c2d, mod, g_mix_row, w_in_bf, gains, g32, g64, None, nc, nc, True).reshape(b, nc, IN_W)

        ys = [_fourier(p3, *dft_lat, "fourier"),
              _diff(lam, p3, pc3, sub_row, tq, True),
              _na(p3, pc3, bias_tab),
              _gmlp(p3, ws_stack, gbias, "gmlp")]
        wu_bf = ffn_up[li].astype(BF16)
        wd_bf = ffn_down[li].astype(BF16)
        ffn_w = (g_ffn_row, w_out_bf, wu_bf, ffn_conv[li], ffn_conv_b[li], wd_bf)
        x2d = _mix_ffn(ys, x2d, mod, *ffn_w, n, tm, False)

        if ctx_out:
            ycs = [_fourier(pc3, *dft_ctx, "fourier_ctx"),
                   _diff(lam, None, pc3, sub_row, nc, False),
                   _na_ctx(pc3),
                   _gmlp(pc3, ws_stack, gbias, "gmlp_ctx")]
            c2d = _mix_ffn(ycs, c2d, mod, *ffn_w, nc, nc, True)
    return x2d.reshape(b, n, d)
```

```python
import functools
import math

import jax
import jax.numpy as jnp
import numpy as np
from jax import lax
from jax.experimental import pallas as pl
from jax.experimental.pallas import tpu as pltpu

F32 = jnp.float32
BF16 = jnp.bfloat16

D_MODEL = 1024
DEPTH = 4
GRID_W = 64
GROUP_W = 256
HEAD_DIM = 64
N_HEADS = 4
DIFF_QK = 32
NA_KH = 8
NA_KW = 16
CHUNK = 128
D_FF = 2816
ROPE_BASE = 10000.0
EPS = 1e-6
IN_W = 9 * GROUP_W

W_F, W_DQ, W_DK, W_DV, W_NQ, W_NK, W_NV, W_GU, W_GV = range(9)
COL_F, COL_DQ, COL_DK, COL_NQ, COL_NK, COL_GU, COL_GV = range(7)
P_W = 7 * GROUP_W

VT_ROWS = HEAD_DIM + 16
PAIR_ROWS = 16
SUM_FLOOR = 2.0 ** -80
NA_Q_ROWS = 4
NA_ELEMS = 2
NA_WIN_ROWS = 12
NEG = -1e30
LOG2E = math.log2(math.e)

ADA_ROWS = 16
VMEM_LIMIT = 56 * 1024 * 1024
HALO = 16
FF_CHUNK = 256


def _dot(a, b):
    return jnp.dot(a, b, preferred_element_type=F32)


def _dot_nt(a, b):
    return lax.dot_general(a, b, (((1,), (1,)), ((), ())), preferred_element_type=F32)


def _resident(shape):
    nd = len(shape)
    return pl.BlockSpec(shape, lambda *_: (0,) * nd, pipeline_mode=pl.Buffered(1))


def _params(n_axes):
    return pltpu.CompilerParams(dimension_semantics=("arbitrary",) * n_axes, vmem_limit_bytes=VMEM_LIMIT)


def _lane(width=GROUP_W):
    return lax.broadcasted_iota(jnp.int32, (1, width), 1)


def _ada_kernel(c_ref, w_ref, b_ref, o_ref):
    c = c_ref[...]
    s = c / (1.0 + jnp.exp(-c))
    w = w_ref[0]
    s_hi = s.astype(BF16)
    s_lo = (s - s_hi.astype(F32)).astype(BF16)
    w_hi = w.astype(BF16)
    w_lo = (w - w_hi.astype(F32)).astype(BF16)
    o_ref[0] = _dot(s_hi, w_hi) + _dot(s_lo, w_hi) + _dot(s_hi, w_lo) + b_ref[0]


def _ada(c_all, w_ada, b_ada):
    tn = 1536
    out_w = 6 * D_MODEL
    return pl.pallas_call(
        _ada_kernel,
        grid=(DEPTH, out_w // tn),
        in_specs=[
            pl.BlockSpec((ADA_ROWS, D_MODEL), lambda l, j: (0, 0)),
            pl.BlockSpec((1, D_MODEL, tn), lambda l, j: (l, 0, j)),
            pl.BlockSpec((1, 1, tn), lambda l, j: (l, 0, j)),
        ],
        out_specs=pl.BlockSpec((1, ADA_ROWS, tn), lambda l, j: (l, 0, j)),
        out_shape=jax.ShapeDtypeStruct((DEPTH, ADA_ROWS, out_w), F32),
        compiler_params=_params(2),
        name="ada",
    )(c_all, w_ada, b_ada.reshape(DEPTH, 1, out_w))


def _gelu(x):
    return 0.5 * x * (1.0 + jnp.tanh(math.sqrt(2.0 / math.pi) * (x + 0.044715 * (x * x * x))))


def _inproj_kernel(*refs, rope):
    if rope:
        x_ref, mod_ref, g_ref, w_ref, gains_ref, g32_ref, g64_ref, cos_ref, sin_ref, o_ref, dvt_ref, nvt_ref = refs
    else:
        x_ref, mod_ref, g_ref, w_ref, gains_ref, g32_ref, g64_ref, o_ref, dvt_ref, nvt_ref = refs
    x = x_ref[...]
    ms = jnp.mean(x * x, axis=-1, keepdims=True)
    h = x * lax.rsqrt(ms + EPS) * g_ref[...]
    h = h * (1.0 + mod_ref[0, 1:2, :]) + mod_ref[0, 0:1, :]
    p = _dot(h.astype(BF16), w_ref[...])

    def col(j):
        return p[:, j * GROUP_W:(j + 1) * GROUP_W]

    def put(j, v):
        o_ref[:, j * GROUP_W:(j + 1) * GROUP_W] = v.astype(BF16)

    def group_norm(v, gmat_ref, gain_row):
        gms = _dot((v * v).astype(BF16), gmat_ref[...])
        return v * lax.rsqrt(gms + EPS) * gains_ref[gain_row:gain_row + 1, :]

    def rotary(v):
        if not rope:
            return v
        first_half = (_lane() % DIFF_QK) < (DIFF_QK // 2)
        partner = jnp.where(first_half, pltpu.roll(v, GROUP_W - DIFF_QK // 2, 1), pltpu.roll(v, DIFF_QK // 2, 1))
        return v * cos_ref[...] + partner * sin_ref[...]

    put(COL_F, col(W_F))
    put(COL_DQ, rotary(group_norm(col(W_DQ), g32_ref, 0)))
    put(COL_DK, rotary(group_norm(col(W_DK), g32_ref, 1)))
    dvt_ref[0] = col(W_DV).T.astype(BF16)
    put(COL_NQ, group_norm(col(W_NQ), g64_ref, 2))
    put(COL_NK, group_norm(col(W_NK), g64_ref, 3))
    nvt_ref[0] = col(W_NV).T.astype(BF16)
    put(COL_GU, _gelu(col(W_GU)))
    put(COL_GV, group_norm(_gelu(col(W_GV)), g64_ref, 4))


def _inproj(x2d, mod, g_row, w_bf, gains, g32, g64, rope_tabs, n, tm, ctx_stream):
    rows = x2d.shape[0]
    tpb = n // tm
    mod_row = (lambda i: (ADA_ROWS // 2, 0, 0)) if ctx_stream else (lambda i: (i // tpb, 0, 0))
    in_specs = [
        pl.BlockSpec((tm, D_MODEL), lambda i: (i, 0)),
        pl.BlockSpec((1, 6, D_MODEL), mod_row),
        _resident((1, D_MODEL)),
        _resident((D_MODEL, IN_W)),
        _resident((8, GROUP_W)),
        _resident((GROUP_W, GROUP_W)),
        _resident((GROUP_W, GROUP_W)),
    ]
    args = [x2d, mod, g_row, w_bf, gains, g32, g64]
    if rope_tabs is not None:
        in_specs += [pl.BlockSpec((tm, GROUP_W), lambda i: (i % tpb, 0))] * 2
        args += list(rope_tabs)
    return pl.pallas_call(
        functools.partial(_inproj_kernel, rope=rope_tabs is not None),
        grid=(rows // tm,),
        in_specs=in_specs,
        out_specs=[pl.BlockSpec((tm, P_W), lambda i: (i, 0))]
        + [pl.BlockSpec((1, GROUP_W, tm), lambda i: (i // tpb, 0, i % tpb))] * 2,
        out_shape=[jax.ShapeDtypeStruct((rows, P_W), BF16)]
        + [jax.ShapeDtypeStruct((rows // n, GROUP_W, n), BF16)] * 2,
        compiler_params=_params(1),
        name="inproj_ctx" if ctx_stream else "inproj",
    )(*args)


def _fourier_kernel(p_ref, bdc_ref, bds_ref, cn_ref, sn_ref, o_ref):
    pf = p_ref[0]
    a = _dot(pf, bdc_ref[...]).astype(BF16)
    b = _dot(pf, bds_ref[...]).astype(BF16)
    o_ref[0] = (_dot(cn_ref[...], a) - _dot(sn_ref[...], b)).astype(BF16)


def _fourier(p3, bdc, bds, cn, sn, name):
    b, n, _ = p3.shape
    return pl.pallas_call(
        _fourier_kernel,
        grid=(b,),
        in_specs=[
            pl.BlockSpec((1, n, GROUP_W), lambda i: (i, 0, COL_F)),
            _resident((GROUP_W, GROUP_W)),
            _resident((GROUP_W, GROUP_W)),
            _resident((n, n)),
            _resident((n, n)),
        ],
        out_specs=pl.BlockSpec((1, n, GROUP_W), lambda i: (i, 0, 0)),
        out_shape=jax.ShapeDtypeStruct((b, n, GROUP_W), BF16),
        compiler_params=_params(1),
        name=name,
    )(p3, bdc, bds, cn, sn)


def _column_max(parts):
    mx = parts[0].max(axis=0, keepdims=True)
    for s in parts[1:]:
        mx = jnp.maximum(mx, s.max(axis=0, keepdims=True))
    return mx


def _diff_kernel(*refs, has_lat):
    if has_lat:
        lam_ref, q_ref, kl_ref, vl_ref, kc_ref, vc_ref, sub_ref, sel_ref, o_ref, vt_ref, kmax_ref = refs
        key_refs, val_refs = [kl_ref, kc_ref], [vl_ref, vc_ref]
    else:
        lam_ref, q_ref, kc_ref, vc_ref, sub_ref, sel_ref, o_ref, vt_ref, kmax_ref = refs
        key_refs, val_refs = [kc_ref], [vc_ref]
    counts = [r.shape[1] for r in key_refs]
    starts = [sum(counts[:i]) for i in range(len(counts))]

    def pair_sq_norms(x):
        xf = x.astype(F32)
        return _dot_nt(sel_ref[...], (xf * xf).astype(BF16))

    @pl.when(pl.program_id(1) == 0)
    def _():
        for r, lo, cnt in zip(val_refs, starts, counts):
            for h in range(N_HEADS):
                vt_ref[h, 0:HEAD_DIM, lo:lo + cnt] = r[0, h * HEAD_DIM:(h + 1) * HEAD_DIM, :]
                vt_ref[h, HEAD_DIM:VT_ROWS, lo:lo + cnt] = jnp.ones((VT_ROWS - HEAD_DIM, cnt), BF16)
        kmax = None
        for r in key_refs:
            m = pair_sq_norms(r[0]).max(axis=1, keepdims=True)
            kmax = m if kmax is None else jnp.maximum(kmax, m)
        kmax_ref[...] = jnp.broadcast_to(kmax, kmax_ref.shape)

    q = q_ref[0]
    lam = lam_ref[0]
    lane = _lane()
    bound = jnp.sqrt(pair_sq_norms(q) * kmax_ref[:, 0:1])

    def attend(shift_fn):
        outs = []
        for pair in range(2 * N_HEADS):
            h = pair // 2
            lo = pair * DIFF_QK
            qm = jnp.where((lane >= lo) & (lane < lo + DIFF_QK), q, jnp.zeros_like(q))
            parts = [_dot_nt(r[0], qm) for r in key_refs]
            shift = shift_fn(parts, pair)
            o = None
            for s, klo, cnt in zip(parts, starts, counts):
                t = _dot(vt_ref[h, :, klo:klo + cnt], jnp.exp2(s - shift).astype(BF16))
                o = t if o is None else o + t
            outs.append(o)
        return outs

    def finish(outs):
        heads = []
        for h in range(N_HEADS):
            o1, o2 = outs[2 * h], outs[2 * h + 1]
            o = (o1[0:HEAD_DIM] * (1.0 / o1[HEAD_DIM:HEAD_DIM + 1])
                 - o2[0:HEAD_DIM] * (lam / o2[HEAD_DIM:HEAD_DIM + 1]))
            heads.append(o * lax.rsqrt(jnp.mean(o * o, axis=0, keepdims=True) + EPS))
        o_ref[0] = (jnp.concatenate(heads, axis=0) * sub_ref[...]).T.astype(BF16)

    outs = attend(lambda parts, pair: bound[pair:pair + 1])
    finish(outs)
    smallest = outs[0][HEAD_DIM:HEAD_DIM + 1]
    for o in outs[1:]:
        smallest = jnp.minimum(smallest, o[HEAD_DIM:HEAD_DIM + 1])

    @pl.when(jnp.min(smallest) < SUM_FLOOR)
    def _():
        finish(attend(lambda parts, pair: _column_max(parts)))


def _diff(lam, p3, vt3, pc3, vtc3, sub_row, tq, has_lat):
    q_src = p3 if has_lat else pc3
    sub_t = jnp.broadcast_to(sub_row.reshape(GROUP_W, 1), (GROUP_W, tq))
    b, n, _ = q_src.shape
    nc = pc3.shape[1]
    smem = pl.BlockSpec(memory_space=pltpu.SMEM)
    in_specs = [smem, pl.BlockSpec((1, tq, GROUP_W), lambda i, j: (i, j, COL_DQ))]
    args = [lam, q_src]
    if has_lat:
        in_specs += [pl.BlockSpec((1, n, GROUP_W), lambda i, j: (i, 0, COL_DK)),
                     pl.BlockSpec((1, GROUP_W, n), lambda i, j: (i, 0, 0))]
        args += [p3, vt3]
    in_specs += [pl.BlockSpec((1, nc, GROUP_W), lambda i, j: (i, 0, COL_DK)),
                 pl.BlockSpec((1, GROUP_W, nc), lambda i, j: (i, 0, 0)),
                 _resident((GROUP_W, tq)),
                 _resident((PAIR_ROWS, GROUP_W))]
    pair_sel = np.arange(PAIR_ROWS)[:, None] == np.arange(GROUP_W)[None, :] // DIFF_QK
    args += [pc3, vtc3, sub_t, jnp.asarray(pair_sel, F32).astype(BF16)]
    return pl.pallas_call(
        functools.partial(_diff_kernel, has_lat=has_lat),
        grid=(b, n // tq),
        in_specs=in_specs,
        out_specs=pl.BlockSpec((1, tq, GROUP_W), lambda i, j: (i, j, 0)),
        out_shape=jax.ShapeDtypeStruct((b, n, GROUP_W), BF16),
        scratch_shapes=[pltpu.VMEM((N_HEADS, VT_ROWS, (n if has_lat else 0) + nc), BF16),
                        pltpu.VMEM((PAIR_ROWS, 128), F32)],
        compiler_params=_params(2),
        name="diff" if has_lat else "diff_ctx",
    )(*args)


def _na_kernel(*refs, local):
    if local:
        q_ref, k0_ref, k1_ref, k2_ref, v0_ref, v1_ref, v2_ref, kc_ref, vc_ref, bias_ref, o_ref = refs
        key_refs = [k0_ref, k1_ref, k2_ref, kc_ref]
        vt_refs = [v0_ref, v1_ref, v2_ref, vc_ref]
    else:
        q_ref, kc_ref, vc_ref, o_ref = refs
        key_refs = [kc_ref]
        vt_refs = [vc_ref]
    n_elems, blk = q_ref.shape[0], q_ref.shape[1]
    lane = _lane()

    def scores(el):
        q = q_ref[el]
        q_heads = jnp.concatenate(
            [jnp.where((lane >= h * HEAD_DIM) & (lane < (h + 1) * HEAD_DIM), q, jnp.zeros_like(q))
             for h in range(N_HEADS)], axis=0)
        k_all = jnp.concatenate([r[el] for r in key_refs], axis=0)
        return _dot_nt(k_all, q_heads)

    def weights(s):
        if local:
            s = s + bias_ref[0]
        return jnp.exp2(s - s.max(axis=0, keepdims=True)).astype(BF16)

    def attend(el, e):
        vt_all = jnp.concatenate([r[el] for r in vt_refs], axis=1)
        ones = jnp.ones((VT_ROWS - HEAD_DIM, vt_all.shape[1]), BF16)
        heads = []
        for h in range(N_HEADS):
            vt_h = jnp.concatenate([vt_all[h * HEAD_DIM:(h + 1) * HEAD_DIM], ones], axis=0)
            o = _dot(vt_h, e[:, h * blk:(h + 1) * blk])
            heads.append(o[0:HEAD_DIM] * (1.0 / o[HEAD_DIM:HEAD_DIM + 1]))
        o_ref[el] = jnp.concatenate(heads, axis=0).T.astype(BF16)

    es = [weights(s) for s in [scores(el) for el in range(n_elems)]]
    for el in range(n_elems):
        attend(el, es[el])


def _na(p3, vt3, pc3, vtc3, bias):
    b, n, _ = p3.shape
    nc = pc3.shape[1]
    ne = NA_ELEMS
    blk = NA_Q_ROWS * GRID_W
    steps = n // blk
    last_win = n // blk - NA_WIN_ROWS // NA_Q_ROWS

    def first(g):
        return jnp.clip(g - 1, 0, last_win)

    def cls(g, i):
        return ((g > 0).astype(jnp.int32) + (g == steps - 1).astype(jnp.int32), 0, 0)

    in_specs = [pl.BlockSpec((ne, blk, GROUP_W), lambda g, i: (i, g, COL_NQ))]
    in_specs += [pl.BlockSpec((ne, blk, GROUP_W), lambda g, i, j=j: (i, first(g) + j, COL_NK)) for j in range(3)]
    in_specs += [pl.BlockSpec((ne, GROUP_W, blk), lambda g, i, j=j: (i, 0, first(g) + j)) for j in range(3)]
    in_specs += [pl.BlockSpec((ne, nc, GROUP_W), lambda g, i: (i, 0, COL_NK)),
                 pl.BlockSpec((ne, GROUP_W, nc), lambda g, i: (i, 0, 0)),
                 pl.BlockSpec((1, 3 * blk + nc, N_HEADS * blk), cls)]
    return pl.pallas_call(
        functools.partial(_na_kernel, local=True),
        grid=(steps, b // ne),
        in_specs=in_specs,
        out_specs=pl.BlockSpec((ne, blk, GROUP_W), lambda g, i: (i, g, 0)),
        out_shape=jax.ShapeDtypeStruct((b, n, GROUP_W), BF16),
        compiler_params=_params(2),
        name="na",
    )(p3, p3, p3, p3, vt3, vt3, vt3, pc3, vtc3, bias)


def _na_ctx(pc3, vtc3):
    b, nc, _ = pc3.shape
    ne = NA_ELEMS
    return pl.pallas_call(
        functools.partial(_na_kernel, local=False),
        grid=(b // ne,),
        in_specs=[pl.BlockSpec((ne, nc, GROUP_W), lambda i: (i, 0, COL_NQ)),
                  pl.BlockSpec((ne, nc, GROUP_W), lambda i: (i, 0, COL_NK)),
                  pl.BlockSpec((ne, GROUP_W, nc), lambda i: (i, 0, 0))],
        out_specs=pl.BlockSpec((ne, nc, GROUP_W), lambda i: (i, 0, 0)),
        out_shape=jax.ShapeDtypeStruct((b, nc, GROUP_W), BF16),
        compiler_params=_params(1),
        name="na_ctx",
    )(pc3, pc3, vtc3)


def _na_row_offsets():
    first_row = NA_WIN_ROWS - NA_KH
    rules = [lambda i, a: a - i + NA_KH - 1 if a < NA_KH else None,
             lambda i, a: a - i + NA_KH // 2 - 1 if i <= a < i + NA_KH else None,
             lambda i, a: a - i - 1 if a >= first_row else None]
    return [[[rule(i, a) for a in range(NA_WIN_ROWS)] for i in range(NA_Q_ROWS)] for rule in rules]


def _na_bias_kernel(band_ref, o_ref):
    masked = jnp.full((GRID_W, GRID_W), NEG, F32)
    for kind, per_query_row in enumerate(_na_row_offsets()):
        for a in range(NA_WIN_ROWS):
            blocks = [masked if offs[a] is None else band_ref[0, offs[a]] for offs in per_query_row]
            o_ref[kind, a * GRID_W:(a + 1) * GRID_W, :] = jnp.concatenate(blocks, axis=1)
        o_ref[kind, NA_WIN_ROWS * GRID_W:, :] = jnp.zeros((o_ref.shape[1] - NA_WIN_ROWS * GRID_W, o_ref.shape[2]), F32)


def _na_bias_tables(rpb, nc):
    kc = np.arange(GRID_W)[:, None]
    j = np.arange(GRID_W)[None, :]
    c0 = np.clip(j - NA_KW // 2, 0, GRID_W - NA_KW)
    col_ok = (kc >= c0) & (kc < c0 + NA_KW)
    onehot = (np.arange(2 * NA_KW - 1)[:, None, None] == (kc - j + NA_KW - 1)[None]) & col_ok[None]
    band = jnp.einsum("hdo,okj->hdkj", rpb.astype(F32), jnp.asarray(onehot, F32), precision=lax.Precision.HIGHEST)
    band = jnp.where(col_ok, band * LOG2E, NEG)
    n_off = 2 * NA_KH - 1
    q, k = NA_Q_ROWS * GRID_W, NA_WIN_ROWS * GRID_W
    return pl.pallas_call(
        _na_bias_kernel,
        grid=(N_HEADS,),
        in_specs=[pl.BlockSpec((1, n_off, GRID_W, GRID_W), lambda h: (h, 0, 0, 0))],
        out_specs=pl.BlockSpec((3, k + nc, q), lambda h: (0, 0, h)),
        out_shape=jax.ShapeDtypeStruct((3, k + nc, N_HEADS * q), F32),
        compiler_params=_params(1),
        name="na_bias",
    )(band)


def _gmlp_kernel(u_ref, v_ref, ws_ref, bias_ref, o_ref, *, chunks_per_step):
    n = u_ref.shape[1]
    lane = _lane()
    step_rows = chunks_per_step * CHUNK
    for s in range(n // step_rows):
        base = s * step_rows
        vcat = jnp.concatenate(
            [v_ref[0, base + c * CHUNK: base + (c + 1) * CHUNK, :] for c in range(chunks_per_step)], axis=1)
        r = _dot(ws_ref[...], vcat)
        for c in range(chunks_per_step):
            mix = r[0:CHUNK, c * GROUP_W:(c + 1) * GROUP_W]
            for g in range(1, N_HEADS):
                mix = jnp.where(lane >= g * HEAD_DIM, r[g * CHUNK:(g + 1) * CHUNK, c * GROUP_W:(c + 1) * GROUP_W], mix)
            rows = slice(base + c * CHUNK, base + (c + 1) * CHUNK)
            o_ref[0, rows, :] = (u_ref[0, rows, :].astype(F32) * (mix + bias_ref[...])).astype(BF16)


def _gmlp(p3, ws_stack, bias_full, name):
    b, n, _ = p3.shape
    cps = min(4, n // CHUNK)
    return pl.pallas_call(
        functools.partial(_gmlp_kernel, chunks_per_step=cps),
        grid=(b,),
        in_specs=[pl.BlockSpec((1, n, GROUP_W), lambda i: (i, 0, COL_GU)),
                  pl.BlockSpec((1, n, GROUP_W), lambda i: (i, 0, COL_GV)),
                  _resident((N_HEADS * CHUNK, CHUNK)),
                  _resident((CHUNK, GROUP_W))],
        out_specs=pl.BlockSpec((1, n, GROUP_W), lambda i: (i, 0, 0)),
        out_shape=jax.ShapeDtypeStruct((b, n, GROUP_W), BF16),
        compiler_params=_params(1),
        name=name,
    )(p3, p3, ws_stack, bias_full)


def _mix_ffn_kernel(*refs, tiles_per_seq):
    y_refs, (xp_ref, xm_ref, xn_ref) = refs[:12], refs[12:15]
    mod_ref, g_ref, wo_ref, wu_ref, cw_ref, cb_ref, wd_ref, o_ref, xs_ref, act_ref = refs[15:]
    tm = xm_ref.shape[0]
    pos = pl.program_id(0) % tiles_per_seq

    def with_halo(p_ref, m_ref, n_ref):
        return jnp.concatenate([p_ref[...], m_ref[...], n_ref[...]], axis=0)

    acc = None
    for j in range(4):
        t = _dot(with_halo(*y_refs[3 * j:3 * j + 3]), wo_ref[j * GROUP_W:(j + 1) * GROUP_W, :])
        acc = t if acc is None else acc + t
    x = with_halo(xp_ref, xm_ref, xn_ref) + mod_ref[0, 2:3, :] * acc
    xs_ref[...] = x[HALO:HALO + tm]
    ms = jnp.mean(x * x, axis=-1, keepdims=True)
    h = x * lax.rsqrt(ms + EPS) * g_ref[...]
    h = (h * (1.0 + mod_ref[0, 4:5, :]) + mod_ref[0, 3:4, :]).astype(BF16)
    hp, hn = h[0:HALO], h[HALO + tm:]
    hp = jnp.where(pos != 0, hp, jnp.zeros_like(hp))
    hn = jnp.where(pos != tiles_per_seq - 1, hn, jnp.zeros_like(hn))
    hcat = jnp.concatenate([hp, h[HALO:HALO + tm], hn], axis=0)

    def conv(z, lo):
        rows = z.shape[0]
        cols = slice(lo, lo + FF_CHUNK)
        return (pltpu.roll(z, 1, 0)[HALO:HALO + tm] * cw_ref[0:1, cols]
                + z[HALO:HALO + tm] * cw_ref[1:2, cols]
                + pltpu.roll(z, rows - 1, 0)[HALO:HALO + tm] * cw_ref[2:3, cols] + cb_ref[:, cols])

    for c in range(D_FF // FF_CHUNK):
        lo_g = c * FF_CHUNK
        lo_v = D_FF + c * FF_CHUNK
        g = conv(_dot(hcat, wu_ref[:, lo_g:lo_g + FF_CHUNK]), lo_g)
        v = conv(_dot(hcat, wu_ref[:, lo_v:lo_v + FF_CHUNK]), lo_v)
        act_ref[:, lo_g:lo_g + FF_CHUNK] = (g / (1.0 + jnp.exp(-g)) * v).astype(BF16)
    o_ref[...] = xs_ref[...] + mod_ref[0, 5:6, :] * _dot(act_ref[...], wd_ref[...])


def _mix_ffn(ys, x2d, mod, g_row, wo_bf, wu_bf, conv_w, conv_b, wd_bf, n, tm, ctx_stream):
    rows = x2d.shape[0]
    tpb = n // tm
    per_tile = tm // HALO
    last_halo = rows // HALO - 1
    mod_row = (lambda i: (ADA_ROWS // 2, 0, 0)) if ctx_stream else (lambda i: (i // tpb, 0, 0))

    def halo_specs(width):
        return [pl.BlockSpec((HALO, width), lambda i: (jnp.maximum(i * per_tile - 1, 0), 0)),
                pl.BlockSpec((tm, width), lambda i: (i, 0)),
                pl.BlockSpec((HALO, width), lambda i: (jnp.minimum((i + 1) * per_tile, last_halo), 0))]

    y2d = [y.reshape(rows, GROUP_W) for y in ys]
    return pl.pallas_call(
        functools.partial(_mix_ffn_kernel, tiles_per_seq=tpb),
        grid=(rows // tm,),
        in_specs=halo_specs(GROUP_W) * 4 + halo_specs(D_MODEL) + [
            pl.BlockSpec((1, 6, D_MODEL), mod_row),
            _resident((1, D_MODEL)),
            _resident((D_MODEL, D_MODEL)),
            _resident((D_MODEL, 2 * D_FF)),
            _resident((3, 2 * D_FF)),
            _resident((1, 2 * D_FF)),
            _resident((D_FF, D_MODEL)),
        ],
        out_specs=pl.BlockSpec((tm, D_MODEL), lambda i: (i, 0)),
        out_shape=jax.ShapeDtypeStruct((rows, D_MODEL), F32),
        scratch_shapes=[pltpu.VMEM((tm, D_MODEL), F32), pltpu.VMEM((tm, D_FF), BF16)],
        compiler_params=_params(1),
        name="mix_ffn_ctx" if ctx_stream else "mix_ffn",
    )(*[a for y in y2d for a in (y, y, y)], x2d, x2d, x2d, mod, g_row, wo_bf, wu_bf, conv_w,
      conv_b.reshape(1, 2 * D_FF), wd_bf)


def _dft_tables(n):
    lo_n = 64
    t = jnp.arange(n, dtype=jnp.int32)

    def table(k):
        ang = ((k[:, None] * t[None, :]) % n).astype(F32) * (2.0 * math.pi / n)
        return jnp.cos(ang), jnp.sin(ang)

    (ch, sh), (cl, sl) = table(jnp.arange(n // lo_n, dtype=jnp.int32) * lo_n), table(jnp.arange(lo_n, dtype=jnp.int32))
    cn = (ch[:, None] * cl[None] - sh[:, None] * sl[None]).reshape(n, n).astype(BF16)
    sn = (sh[:, None] * cl[None] + ch[:, None] * sl[None]).reshape(n, n).astype(BF16)
    c = np.arange(HEAD_DIM)
    angc = 2.0 * np.pi * ((c[:, None] * c[None, :]) % HEAD_DIM) / HEAD_DIM
    norm = 1.0 / math.sqrt(n * HEAD_DIM)
    eye = np.eye(N_HEADS)
    bdc = jnp.asarray(np.kron(eye, np.cos(angc) * norm), F32).astype(BF16)
    bds = jnp.asarray(np.kron(eye, np.sin(angc) * norm), F32).astype(BF16)
    return bdc, bds, cn, sn


def _rope_tables(n):
    n_freq = DIFF_QK // 4
    freqs = ROPE_BASE ** (-jnp.arange(n_freq, dtype=F32) / n_freq)
    t = jnp.arange(n)
    row = (t // GRID_W).astype(F32)
    col = (t % GRID_W).astype(F32)
    ang = jnp.concatenate([row[:, None] * freqs, col[:, None] * freqs], axis=-1)
    cos, sin = jnp.cos(ang), jnp.sin(ang)
    reps = GROUP_W // DIFF_QK
    return jnp.tile(jnp.concatenate([cos, cos], axis=-1), (1, reps)), jnp.tile(jnp.concatenate([-sin, sin], axis=-1), (1, reps))


def _group_mean_matrix(group):
    return jnp.asarray(np.kron(np.eye(GROUP_W // group), np.full((group, group), 1.0 / group)), F32).astype(BF16)


def kernel(x, c, ctx, c_ctx, w_ada, b_ada, g_mix, g_ffn, w_in, w_out, diff_qn, diff_kn, diff_lam, diff_subln, na_qn,
           na_kn, na_rpb, gmlp_norm, gmlp_ws, gmlp_b, ffn_up, ffn_conv, ffn_conv_b, ffn_down):
    b, n, d = x.shape
    nc = ctx.shape[1]
    assert (d, n % (NA_Q_ROWS * GRID_W), nc % CHUNK, b < ADA_ROWS // 2 + 1) == (D_MODEL, 0, 0, True)
    tm = 512
    tq = 512

    c_all = jnp.zeros((ADA_ROWS, d), F32).at[:b].set(c).at[ADA_ROWS // 2].set(c_ctx)
    mods = _ada(c_all, w_ada, b_ada).reshape(DEPTH, ADA_ROWS, 6, d)

    g32 = _group_mean_matrix(DIFF_QK)
    g64 = _group_mean_matrix(HEAD_DIM)
    rope_tabs = _rope_tables(n)
    dft_lat = _dft_tables(n)
    dft_ctx = _dft_tables(nc)

    x2d = x.reshape(b * n, d)
    c2d = ctx.reshape(b * nc, d)
    for li in range(DEPTH):
        ctx_out = li < DEPTH - 1
        lam_init = 0.8 - 0.6 * math.exp(-0.3 * li)
        lf = diff_lam[li].astype(F32)
        lam = (jnp.exp(jnp.sum(lf[0] * lf[1])) - jnp.exp(jnp.sum(lf[2] * lf[3])) + lam_init).reshape(1)
        mod = mods[li]
        gains = jnp.zeros((8, GROUP_W), F32)
        gains = gains.at[0].set(jnp.tile(diff_qn[li], GROUP_W // DIFF_QK) * (DIFF_QK ** -0.5 * LOG2E))
        gains = gains.at[1].set(jnp.tile(diff_kn[li], GROUP_W // DIFF_QK))
        gains = gains.at[2].set(jnp.tile(na_qn[li], N_HEADS) * (HEAD_DIM ** -0.5 * LOG2E))
        gains = gains.at[3].set(jnp.tile(na_kn[li], N_HEADS))
        gains = gains.at[4].set(gmlp_norm[li])
        sub_row = (jnp.tile(diff_subln[li], N_HEADS) * (1.0 - lam_init)).reshape(1, GROUP_W)
        w_in_bf = w_in[li].astype(BF16)
        w_out_bf = w_out[li].astype(BF16)
        g_mix_row = g_mix[li].reshape(1, d)
        g_ffn_row = g_ffn[li].reshape(1, d)
        ws_stack = gmlp_ws[li].reshape(N_HEADS * CHUNK, CHUNK).astype(BF16)
        gbias = jnp.repeat(gmlp_b[li].T, HEAD_DIM, axis=1)
        bias_tab = _na_bias_tables(na_rpb[li], nc)

        p2d, dvt, nvt = _inproj(x2d, mod, g_mix_row, w_in_bf, gains, g32, g64, rope_tabs, n, tm, False)
        pc2d, dvtc, nvtc = _inproj(c2d, mod, g_mix_row, w_in_bf, gains, g32, g64, None, nc, nc, True)
        p3 = p2d.reshape(b, n, P_W)
        pc3 = pc2d.reshape(b, nc, P_W)

        ys = [_fourier(p3, *dft_lat, "fourier"),
              _diff(lam, p3, dvt, pc3, dvtc, sub_row, tq, True),
              _na(p3, nvt, pc3, nvtc, bias_tab),
              _gmlp(p3, ws_stack, gbias, "gmlp")]
        wu_bf = ffn_up[li].astype(BF16)
        wd_bf = ffn_down[li].astype(BF16)
        ffn_w = (g_ffn_row, w_out_bf, wu_bf, ffn_conv[li], ffn_conv_b[li], wd_bf)
        x2d = _mix_ffn(ys, x2d, mod, *ffn_w, n, tm, False)

        if ctx_out:
            ycs = [_fourier(pc3, *dft_ctx, "fourier_ctx"),
                   _diff(lam, None, None, pc3, dvtc, sub_row, nc, False),
                   _na_ctx(pc3, nvtc),
                   _gmlp(pc3, ws_stack, gbias, "gmlp_ctx")]
            c2d = _mix_ffn(ycs, c2d, mod, *ffn_w, nc, nc, True)
    return x2d.reshape(b, n, d)
```

```python
import functools
import math

import jax
import jax.numpy as jnp
import numpy as np
from jax import lax
from jax.experimental import pallas as pl
from jax.experimental.pallas import tpu as pltpu

F32 = jnp.float32
BF16 = jnp.bfloat16

D_MODEL = 1024
DEPTH = 4
GRID_W = 64
GROUP_W = 256
HEAD_DIM = 64
N_HEADS = 4
DIFF_QK = 32
NA_KH = 8
NA_KW = 16
CHUNK = 128
D_FF = 2816
ROPE_BASE = 10000.0
EPS = 1e-6
IN_W = 9 * GROUP_W

W_F, W_DQ, W_DK, W_DV, W_NQ, W_NK, W_NV, W_GU, W_GV = range(9)
COL_F, COL_DQ, COL_DK, COL_NQ, COL_NK, COL_GU, COL_GV = range(7)
P_W = 7 * GROUP_W

VT_ROWS = HEAD_DIM + 16
PAIR_ROWS = 16
SUM_FLOOR = 2.0 ** -80
NA_Q_ROWS = 4
NA_ELEMS = 2
NA_WIN_ROWS = 12
NEG = -1e30
LOG2E = math.log2(math.e)

ADA_ROWS = 16
VMEM_LIMIT = 56 * 1024 * 1024
HALO = 16
FF_CHUNK = 256


def _dot(a, b):
    return jnp.dot(a, b, preferred_element_type=F32)


def _dot_nt(a, b):
    return lax.dot_general(a, b, (((1,), (1,)), ((), ())), preferred_element_type=F32)


def _resident(shape):
    nd = len(shape)
    return pl.BlockSpec(shape, lambda *_: (0,) * nd, pipeline_mode=pl.Buffered(1))


def _resident_layer(shape, li):
    nd = len(shape)
    return pl.BlockSpec((None,) + tuple(shape), lambda *_: (li,) + (0,) * nd, pipeline_mode=pl.Buffered(1))


def _params(n_axes):
    return pltpu.CompilerParams(dimension_semantics=("arbitrary",) * n_axes, vmem_limit_bytes=VMEM_LIMIT)


def _lane(width=GROUP_W):
    return lax.broadcasted_iota(jnp.int32, (1, width), 1)


def _ada_kernel(c_ref, w_ref, b_ref, o_ref):
    c = c_ref[...]
    s = c / (1.0 + jnp.exp(-c))
    w = w_ref[0]
    s_hi = s.astype(BF16)
    s_lo = (s - s_hi.astype(F32)).astype(BF16)
    w_hi = w.astype(BF16)
    w_lo = (w - w_hi.astype(F32)).astype(BF16)
    o_ref[0] = _dot(s_hi, w_hi) + _dot(s_lo, w_hi) + _dot(s_hi, w_lo) + b_ref[0]


def _ada(c_all, w_ada, b_ada):
    tn = 1536
    out_w = 6 * D_MODEL
    return pl.pallas_call(
        _ada_kernel,
        grid=(DEPTH, out_w // tn),
        in_specs=[
            pl.BlockSpec((ADA_ROWS, D_MODEL), lambda l, j: (0, 0)),
            pl.BlockSpec((1, D_MODEL, tn), lambda l, j: (l, 0, j)),
            pl.BlockSpec((1, 1, tn), lambda l, j: (l, 0, j)),
        ],
        out_specs=pl.BlockSpec((1, ADA_ROWS, tn), lambda l, j: (l, 0, j)),
        out_shape=jax.ShapeDtypeStruct((DEPTH, ADA_ROWS, out_w), F32),
        compiler_params=_params(2),
        name="ada",
    )(c_all, w_ada, b_ada.reshape(DEPTH, 1, out_w))


def _gelu(x):
    return 0.5 * x * (1.0 + jnp.tanh(math.sqrt(2.0 / math.pi) * (x + 0.044715 * (x * x * x))))


def _inproj_kernel(*refs, rope):
    if rope:
        x_ref, mod_ref, g_ref, w_ref, gains_ref, g32_ref, g64_ref, cos_ref, sin_ref, o_ref, dvt_ref, nvt_ref = refs
    else:
        x_ref, mod_ref, g_ref, w_ref, gains_ref, g32_ref, g64_ref, o_ref, dvt_ref, nvt_ref = refs
    x = x_ref[...]
    ms = jnp.mean(x * x, axis=-1, keepdims=True)
    h = x * lax.rsqrt(ms + EPS) * g_ref[...]
    h = h * (1.0 + mod_ref[0, 1:2, :]) + mod_ref[0, 0:1, :]
    p = _dot(h.astype(BF16), w_ref[...])

    def col(j):
        return p[:, j * GROUP_W:(j + 1) * GROUP_W]

    def put(j, v):
        o_ref[:, j * GROUP_W:(j + 1) * GROUP_W] = v.astype(BF16)

    def group_norm(v, gmat_ref, gain_row):
        gms = _dot((v * v).astype(BF16), gmat_ref[...])
        return v * lax.rsqrt(gms + EPS) * gains_ref[gain_row:gain_row + 1, :]

    def rotary(v):
        if not rope:
            return v
        first_half = (_lane() % DIFF_QK) < (DIFF_QK // 2)
        partner = jnp.where(first_half, pltpu.roll(v, GROUP_W - DIFF_QK // 2, 1), pltpu.roll(v, DIFF_QK // 2, 1))
        return v * cos_ref[...] + partner * sin_ref[...]

    put(COL_F, col(W_F))
    put(COL_DQ, rotary(group_norm(col(W_DQ), g32_ref, 0)))
    put(COL_DK, rotary(group_norm(col(W_DK), g32_ref, 1)))
    dvt_ref[0] = col(W_DV).T.astype(BF16)
    put(COL_NQ, group_norm(col(W_NQ), g64_ref, 2))
    put(COL_NK, group_norm(col(W_NK), g64_ref, 3))
    nvt_ref[0] = col(W_NV).T.astype(BF16)
    put(COL_GU, _gelu(col(W_GU)))
    put(COL_GV, group_norm(_gelu(col(W_GV)), g64_ref, 4))


def _inproj(x2d, mod, g_row, w_bf, li, gains, g32, g64, rope_tabs, n, tm, ctx_stream):
    rows = x2d.shape[0]
    tpb = n // tm
    mod_row = (lambda i: (ADA_ROWS // 2, 0, 0)) if ctx_stream else (lambda i: (i // tpb, 0, 0))
    in_specs = [
        pl.BlockSpec((tm, D_MODEL), lambda i: (i, 0)),
        pl.BlockSpec((1, 6, D_MODEL), mod_row),
        _resident((1, D_MODEL)),
        _resident_layer((D_MODEL, IN_W), li),
        _resident((8, GROUP_W)),
        _resident((GROUP_W, GROUP_W)),
        _resident((GROUP_W, GROUP_W)),
    ]
    args = [x2d, mod, g_row, w_bf, gains, g32, g64]
    if rope_tabs is not None:
        in_specs += [pl.BlockSpec((tm, GROUP_W), lambda i: (i % tpb, 0))] * 2
        args += list(rope_tabs)
    return pl.pallas_call(
        functools.partial(_inproj_kernel, rope=rope_tabs is not None),
        grid=(rows // tm,),
        in_specs=in_specs,
        out_specs=[pl.BlockSpec((tm, P_W), lambda i: (i, 0))]
        + [pl.BlockSpec((1, GROUP_W, tm), lambda i: (i // tpb, 0, i % tpb))] * 2,
        out_shape=[jax.ShapeDtypeStruct((rows, P_W), BF16)]
        + [jax.ShapeDtypeStruct((rows // n, GROUP_W, n), BF16)] * 2,
        compiler_params=_params(1),
        name="inproj_ctx" if ctx_stream else "inproj",
    )(*args)


def _fourier_kernel(p_ref, bdc_ref, bds_ref, cn_ref, sn_ref, rev_ref, o_ref):
    n = p_ref.shape[1]
    half = n // 2
    rb = rev_ref.shape[0]
    nb = n // rb
    mirrored = []
    for blk in range(half // rb):
        src = nb - blk - 1
        nxt = (src + 1) % nb
        pair = jnp.concatenate([p_ref[0, src * rb:(src + 1) * rb, :], p_ref[0, nxt * rb:(nxt + 1) * rb, :]], axis=0)
        mirrored.append(_dot(rev_ref[...], pair))
    mirrored = jnp.concatenate(mirrored, axis=0)
    x = p_ref[0, 0:half, :].astype(F32)
    first = lax.broadcasted_iota(jnp.int32, x.shape, 0) == 0
    fold_c = jnp.where(first, x, x + mirrored).astype(BF16)
    fold_s = (x - mirrored).astype(BF16)
    a = _dot(fold_c, bdc_ref[...]).astype(BF16)
    b = _dot(fold_s, bds_ref[...]).astype(BF16)
    mid = _dot(p_ref[0, half:half + 16, :], bdc_ref[...])[0:1]
    out = _dot(cn_ref[...], a) - _dot(sn_ref[...], b)
    odd = (lax.broadcasted_iota(jnp.int32, out.shape, 0) & 1) == 1
    o_ref[0] = (out + jnp.where(odd, -mid, mid)).astype(BF16)


def _fourier(p3, bdc, bds, cn, sn, rev, name):
    b, n, _ = p3.shape
    return pl.pallas_call(
        _fourier_kernel,
        grid=(b,),
        in_specs=[
            pl.BlockSpec((1, n, GROUP_W), lambda i: (i, 0, COL_F)),
            _resident((GROUP_W, GROUP_W)),
            _resident((GROUP_W, GROUP_W)),
            _resident((n, n // 2)),
            _resident((n, n // 2)),
            _resident(rev.shape),
        ],
        out_specs=pl.BlockSpec((1, n, GROUP_W), lambda i: (i, 0, 0)),
        out_shape=jax.ShapeDtypeStruct((b, n, GROUP_W), BF16),
        compiler_params=_params(1),
        name=name,
    )(p3, bdc, bds, cn, sn, rev)


def _column_max(parts):
    mx = parts[0].max(axis=0, keepdims=True)
    for s in parts[1:]:
        mx = jnp.maximum(mx, s.max(axis=0, keepdims=True))
    return mx


def _diff_kernel(*refs, has_lat):
    if has_lat:
        lam_ref, q_ref, kl_ref, vl_ref, kc_ref, vc_ref, sub_ref, sel_ref, o_ref, vt_ref, kmax_ref = refs
        key_refs, val_refs = [kl_ref, kc_ref], [vl_ref, vc_ref]
    else:
        lam_ref, q_ref, kc_ref, vc_ref, sub_ref, sel_ref, o_ref, vt_ref, kmax_ref = refs
        key_refs, val_refs = [kc_ref], [vc_ref]
    counts = [r.shape[1] for r in key_refs]
    starts = [sum(counts[:i]) for i in range(len(counts))]

    def pair_sq_norms(x):
        xf = x.astype(F32)
        return _dot_nt(sel_ref[...], (xf * xf).astype(BF16))

    @pl.when(pl.program_id(1) == 0)
    def _():
        for r, lo, cnt in zip(val_refs, starts, counts):
            for h in range(N_HEADS):
                vt_ref[h, 0:HEAD_DIM, lo:lo + cnt] = r[0, h * HEAD_DIM:(h + 1) * HEAD_DIM, :]
                vt_ref[h, HEAD_DIM:VT_ROWS, lo:lo + cnt] = jnp.ones((VT_ROWS - HEAD_DIM, cnt), BF16)
        kmax = None
        for r in key_refs:
            m = pair_sq_norms(r[0]).max(axis=1, keepdims=True)
            kmax = m if kmax is None else jnp.maximum(kmax, m)
        kmax_ref[...] = jnp.broadcast_to(kmax, kmax_ref.shape)

    q = q_ref[0]
    lam = lam_ref[0]
    lane = _lane()
    bound = jnp.sqrt(pair_sq_norms(q) * kmax_ref[:, 0:1])

    def attend(shift_fn):
        outs = []
        for pair in range(2 * N_HEADS):
            h = pair // 2
            lo = pair * DIFF_QK
            qm = jnp.where((lane >= lo) & (lane < lo + DIFF_QK), q, jnp.zeros_like(q))
            parts = [_dot_nt(r[0], qm) for r in key_refs]
            shift = shift_fn(parts, pair)
            o = None
            for s, klo, cnt in zip(parts, starts, counts):
                t = _dot(vt_ref[h, :, klo:klo + cnt], jnp.exp2(s - shift).astype(BF16))
                o = t if o is None else o + t
            outs.append(o)
        return outs

    def finish(outs):
        heads = []
        for h in range(N_HEADS):
            o1, o2 = outs[2 * h], outs[2 * h + 1]
            o = (o1[0:HEAD_DIM] * (1.0 / o1[HEAD_DIM:HEAD_DIM + 1])
                 - o2[0:HEAD_DIM] * (lam / o2[HEAD_DIM:HEAD_DIM + 1]))
            heads.append(o * lax.rsqrt(jnp.mean(o * o, axis=0, keepdims=True) + EPS))
        o_ref[0] = (jnp.concatenate(heads, axis=0) * sub_ref[...]).T.astype(BF16)

    outs = attend(lambda parts, pair: bound[pair:pair + 1])
    finish(outs)
    smallest = outs[0][HEAD_DIM:HEAD_DIM + 1]
    for o in outs[1:]:
        smallest = jnp.minimum(smallest, o[HEAD_DIM:HEAD_DIM + 1])

    @pl.when(jnp.min(smallest) < SUM_FLOOR)
    def _():
        finish(attend(lambda parts, pair: _column_max(parts)))


def _diff(lam, p3, vt3, pc3, vtc3, sub_row, tq, has_lat):
    q_src = p3 if has_lat else pc3
    sub_t = jnp.broadcast_to(sub_row.reshape(GROUP_W, 1), (GROUP_W, tq))
    b, n, _ = q_src.shape
    nc = pc3.shape[1]
    smem = pl.BlockSpec(memory_space=pltpu.SMEM)
    in_specs = [smem, pl.BlockSpec((1, tq, GROUP_W), lambda i, j: (i, j, COL_DQ))]
    args = [lam, q_src]
    if has_lat:
        in_specs += [pl.BlockSpec((1, n, GROUP_W), lambda i, j: (i, 0, COL_DK)),
                     pl.BlockSpec((1, GROUP_W, n), lambda i, j: (i, 0, 0))]
        args += [p3, vt3]
    in_specs += [pl.BlockSpec((1, nc, GROUP_W), lambda i, j: (i, 0, COL_DK)),
                 pl.BlockSpec((1, GROUP_W, nc), lambda i, j: (i, 0, 0)),
                 _resident((GROUP_W, tq)),
                 _resident((PAIR_ROWS, GROUP_W))]
    pair_sel = np.arange(PAIR_ROWS)[:, None] == np.arange(GROUP_W)[None, :] // DIFF_QK
    args += [pc3, vtc3, sub_t, jnp.asarray(pair_sel, F32).astype(BF16)]
    return pl.pallas_call(
        functools.partial(_diff_kernel, has_lat=has_lat),
        grid=(b, n // tq),
        in_specs=in_specs,
        out_specs=pl.BlockSpec((1, tq, GROUP_W), lambda i, j: (i, j, 0)),
        out_shape=jax.ShapeDtypeStruct((b, n, GROUP_W), BF16),
        scratch_shapes=[pltpu.VMEM((N_HEADS, VT_ROWS, (n if has_lat else 0) + nc), BF16),
                        pltpu.VMEM((PAIR_ROWS, 128), F32)],
        compiler_params=_params(2),
        name="diff" if has_lat else "diff_ctx",
    )(*args)


def _na_kernel(*refs, local):
    if local:
        q_ref, k0_ref, k1_ref, k2_ref, v0_ref, v1_ref, v2_ref, kc_ref, vc_ref, bias_ref, o_ref = refs
        key_refs = [k0_ref, k1_ref, k2_ref, kc_ref]
        vt_refs = [v0_ref, v1_ref, v2_ref, vc_ref]
    else:
        q_ref, kc_ref, vc_ref, o_ref = refs
        key_refs = [kc_ref]
        vt_refs = [vc_ref]
    n_elems, blk = q_ref.shape[0], q_ref.shape[1]
    lane = _lane()

    def scores(el):
        q = q_ref[el]
        q_heads = jnp.concatenate(
            [jnp.where((lane >= h * HEAD_DIM) & (lane < (h + 1) * HEAD_DIM), q, jnp.zeros_like(q))
             for h in range(N_HEADS)], axis=0)
        k_all = jnp.concatenate([r[el] for r in key_refs], axis=0)
        return _dot_nt(k_all, q_heads)

    def weights(s):
        if local:
            s = s + bias_ref[0]
        return jnp.exp2(s - s.max(axis=0, keepdims=True)).astype(BF16)

    def attend(el, e):
        vt_all = jnp.concatenate([r[el] for r in vt_refs], axis=1)
        ones = jnp.ones((VT_ROWS - HEAD_DIM, vt_all.shape[1]), BF16)
        heads = []
        for h in range(N_HEADS):
            vt_h = jnp.concatenate([vt_all[h * HEAD_DIM:(h + 1) * HEAD_DIM], ones], axis=0)
            o = _dot(vt_h, e[:, h * blk:(h + 1) * blk])
            heads.append(o[0:HEAD_DIM] * (1.0 / o[HEAD_DIM:HEAD_DIM + 1]))
        o_ref[el] = jnp.concatenate(heads, axis=0).T.astype(BF16)

    es = [weights(s) for s in [scores(el) for el in range(n_elems)]]
    for el in range(n_elems):
        attend(el, es[el])


def _na(p3, vt3, pc3, vtc3, bias):
    b, n, _ = p3.shape
    nc = pc3.shape[1]
    ne = NA_ELEMS
    blk = NA_Q_ROWS * GRID_W
    steps = n // blk
    last_win = n // blk - NA_WIN_ROWS // NA_Q_ROWS

    def first(g):
        return jnp.clip(g - 1, 0, last_win)

    def cls(g, i):
        return ((g > 0).astype(jnp.int32) + (g == steps - 1).astype(jnp.int32), 0, 0)

    in_specs = [pl.BlockSpec((ne, blk, GROUP_W), lambda g, i: (i, g, COL_NQ))]
    in_specs += [pl.BlockSpec((ne, blk, GROUP_W), lambda g, i, j=j: (i, first(g) + j, COL_NK)) for j in range(3)]
    in_specs += [pl.BlockSpec((ne, GROUP_W, blk), lambda g, i, j=j: (i, 0, first(g) + j)) for j in range(3)]
    in_specs += [pl.BlockSpec((ne, nc, GROUP_W), lambda g, i: (i, 0, COL_NK)),
                 pl.BlockSpec((ne, GROUP_W, nc), lambda g, i: (i, 0, 0)),
                 pl.BlockSpec((1, 3 * blk + nc, N_HEADS * blk), cls)]
    return pl.pallas_call(
        functools.partial(_na_kernel, local=True),
        grid=(steps, b // ne),
        in_specs=in_specs,
        out_specs=pl.BlockSpec((ne, blk, GROUP_W), lambda g, i: (i, g, 0)),
        out_shape=jax.ShapeDtypeStruct((b, n, GROUP_W), BF16),
        compiler_params=_params(2),
        name="na",
    )(p3, p3, p3, p3, vt3, vt3, vt3, pc3, vtc3, bias)


def _na_ctx(pc3, vtc3):
    b, nc, _ = pc3.shape
    ne = NA_ELEMS
    return pl.pallas_call(
        functools.partial(_na_kernel, local=False),
        grid=(b // ne,),
        in_specs=[pl.BlockSpec((ne, nc, GROUP_W), lambda i: (i, 0, COL_NQ)),
                  pl.BlockSpec((ne, nc, GROUP_W), lambda i: (i, 0, COL_NK)),
                  pl.BlockSpec((ne, GROUP_W, nc), lambda i: (i, 0, 0))],
        out_specs=pl.BlockSpec((ne, nc, GROUP_W), lambda i: (i, 0, 0)),
        out_shape=jax.ShapeDtypeStruct((b, nc, GROUP_W), BF16),
        compiler_params=_params(1),
        name="na_ctx",
    )(pc3, pc3, vtc3)


def _na_row_offsets():
    first_row = NA_WIN_ROWS - NA_KH
    rules = [lambda i, a: a - i + NA_KH - 1 if a < NA_KH else None,
             lambda i, a: a - i + NA_KH // 2 - 1 if i <= a < i + NA_KH else None,
             lambda i, a: a - i - 1 if a >= first_row else None]
    return [[[rule(i, a) for a in range(NA_WIN_ROWS)] for i in range(NA_Q_ROWS)] for rule in rules]


def _na_bias_kernel(band_ref, o_ref):
    masked = jnp.full((GRID_W, GRID_W), NEG, F32)
    for kind, per_query_row in enumerate(_na_row_offsets()):
        for a in range(NA_WIN_ROWS):
            blocks = [masked if offs[a] is None else band_ref[0, offs[a]] for offs in per_query_row]
            o_ref[kind, a * GRID_W:(a + 1) * GRID_W, :] = jnp.concatenate(blocks, axis=1)
        o_ref[kind, NA_WIN_ROWS * GRID_W:, :] = jnp.zeros((o_ref.shape[1] - NA_WIN_ROWS * GRID_W, o_ref.shape[2]), F32)


def _na_bias_tables(rpb, nc):
    kc = np.arange(GRID_W)[:, None]
    j = np.arange(GRID_W)[None, :]
    c0 = np.clip(j - NA_KW // 2, 0, GRID_W - NA_KW)
    col_ok = (kc >= c0) & (kc < c0 + NA_KW)
    onehot = (np.arange(2 * NA_KW - 1)[:, None, None] == (kc - j + NA_KW - 1)[None]) & col_ok[None]
    band = jnp.einsum("hdo,okj->hdkj", rpb.astype(F32), jnp.asarray(onehot, F32), precision=lax.Precision.HIGHEST)
    band = jnp.where(col_ok, band * LOG2E, NEG)
    n_off = 2 * NA_KH - 1
    q, k = NA_Q_ROWS * GRID_W, NA_WIN_ROWS * GRID_W
    return pl.pallas_call(
        _na_bias_kernel,
        grid=(N_HEADS,),
        in_specs=[pl.BlockSpec((1, n_off, GRID_W, GRID_W), lambda h: (h, 0, 0, 0))],
        out_specs=pl.BlockSpec((3, k + nc, q), lambda h: (0, 0, h)),
        out_shape=jax.ShapeDtypeStruct((3, k + nc, N_HEADS * q), F32),
        compiler_params=_params(1),
        name="na_bias",
    )(band)


def _gmlp_kernel(u_ref, v_ref, ws_ref, bias_ref, o_ref, *, chunks_per_step):
    n = u_ref.shape[1]
    lane = _lane()
    step_rows = chunks_per_step * CHUNK
    for s in range(n // step_rows):
        base = s * step_rows
        vcat = jnp.concatenate(
            [v_ref[0, base + c * CHUNK: base + (c + 1) * CHUNK, :] for c in range(chunks_per_step)], axis=1)
        r = _dot(ws_ref[...], vcat)
        for c in range(chunks_per_step):
            mix = r[0:CHUNK, c * GROUP_W:(c + 1) * GROUP_W]
            for g in range(1, N_HEADS):
                mix = jnp.where(lane >= g * HEAD_DIM, r[g * CHUNK:(g + 1) * CHUNK, c * GROUP_W:(c + 1) * GROUP_W], mix)
            rows = slice(base + c * CHUNK, base + (c + 1) * CHUNK)
            o_ref[0, rows, :] = (u_ref[0, rows, :].astype(F32) * (mix + bias_ref[...])).astype(BF16)


def _gmlp(p3, ws_stack, bias_full, name):
    b, n, _ = p3.shape
    cps = min(4, n // CHUNK)
    return pl.pallas_call(
        functools.partial(_gmlp_kernel, chunks_per_step=cps),
        grid=(b,),
        in_specs=[pl.BlockSpec((1, n, GROUP_W), lambda i: (i, 0, COL_GU)),
                  pl.BlockSpec((1, n, GROUP_W), lambda i: (i, 0, COL_GV)),
                  _resident((N_HEADS * CHUNK, CHUNK)),
                  _resident((CHUNK, GROUP_W))],
        out_specs=pl.BlockSpec((1, n, GROUP_W), lambda i: (i, 0, 0)),
        out_shape=jax.ShapeDtypeStruct((b, n, GROUP_W), BF16),
        compiler_params=_params(1),
        name=name,
    )(p3, p3, ws_stack, bias_full)


def _mix_ffn_kernel(*refs, tiles_per_seq):
    y_refs, (xp_ref, xm_ref, xn_ref) = refs[:12], refs[12:15]
    mod_ref, g_ref, wo_ref, wu_ref, cw_ref, cb_ref, wd_ref, o_ref, xs_ref, act_ref = refs[15:]
    tm = xm_ref.shape[0]
    pos = pl.program_id(0) % tiles_per_seq

    def with_halo(p_ref, m_ref, n_ref):
        return jnp.concatenate([p_ref[...], m_ref[...], n_ref[...]], axis=0)

    acc = None
    for j in range(4):
        t = _dot(with_halo(*y_refs[3 * j:3 * j + 3]), wo_ref[j * GROUP_W:(j + 1) * GROUP_W, :])
        acc = t if acc is None else acc + t
    x = with_halo(xp_ref, xm_ref, xn_ref) + mod_ref[0, 2:3, :] * acc
    xs_ref[...] = x[HALO:HALO + tm]
    ms = jnp.mean(x * x, axis=-1, keepdims=True)
    h = x * lax.rsqrt(ms + EPS) * g_ref[...]
    h = (h * (1.0 + mod_ref[0, 4:5, :]) + mod_ref[0, 3:4, :]).astype(BF16)
    hp, hn = h[0:HALO], h[HALO + tm:]
    hp = jnp.where(pos != 0, hp, jnp.zeros_like(hp))
    hn = jnp.where(pos != tiles_per_seq - 1, hn, jnp.zeros_like(hn))
    hcat = jnp.concatenate([hp, h[HALO:HALO + tm], hn], axis=0)

    def conv(z, lo):
        rows = z.shape[0]
        cols = slice(lo, lo + FF_CHUNK)
        return (pltpu.roll(z, 1, 0)[HALO:HALO + tm] * cw_ref[0:1, cols]
                + z[HALO:HALO + tm] * cw_ref[1:2, cols]
                + pltpu.roll(z, rows - 1, 0)[HALO:HALO + tm] * cw_ref[2:3, cols] + cb_ref[:, cols])

    for c in range(D_FF // FF_CHUNK):
        lo_g = c * FF_CHUNK
        lo_v = D_FF + c * FF_CHUNK
        g = conv(_dot(hcat, wu_ref[:, lo_g:lo_g + FF_CHUNK]), lo_g)
        v = conv(_dot(hcat, wu_ref[:, lo_v:lo_v + FF_CHUNK]), lo_v)
        act_ref[:, lo_g:lo_g + FF_CHUNK] = (g / (1.0 + jnp.exp(-g)) * v).astype(BF16)
    o_ref[...] = xs_ref[...] + mod_ref[0, 5:6, :] * _dot(act_ref[...], wd_ref[...])


def _mix_ffn(ys, x2d, mod, g_row, wo_bf, wu_bf, conv_w, conv_b, wd_bf, li, n, tm, ctx_stream):
    rows = x2d.shape[0]
    tpb = n // tm
    per_tile = tm // HALO
    last_halo = rows // HALO - 1
    mod_row = (lambda i: (ADA_ROWS // 2, 0, 0)) if ctx_stream else (lambda i: (i // tpb, 0, 0))

    def halo_specs(width):
        return [pl.BlockSpec((HALO, width), lambda i: (jnp.maximum(i * per_tile - 1, 0), 0)),
                pl.BlockSpec((tm, width), lambda i: (i, 0)),
                pl.BlockSpec((HALO, width), lambda i: (jnp.minimum((i + 1) * per_tile, last_halo), 0))]

    y2d = [y.reshape(rows, GROUP_W) for y in ys]
    return pl.pallas_call(
        functools.partial(_mix_ffn_kernel, tiles_per_seq=tpb),
        grid=(rows // tm,),
        in_specs=halo_specs(GROUP_W) * 4 + halo_specs(D_MODEL) + [
            pl.BlockSpec((1, 6, D_MODEL), mod_row),
            _resident((1, D_MODEL)),
            _resident_layer((D_MODEL, D_MODEL), li),
            _resident_layer((D_MODEL, 2 * D_FF), li),
            _resident((3, 2 * D_FF)),
            _resident((1, 2 * D_FF)),
            _resident_layer((D_FF, D_MODEL), li),
        ],
        out_specs=pl.BlockSpec((tm, D_MODEL), lambda i: (i, 0)),
        out_shape=jax.ShapeDtypeStruct((rows, D_MODEL), F32),
        scratch_shapes=[pltpu.VMEM((tm, D_MODEL), F32), pltpu.VMEM((tm, D_FF), BF16)],
        compiler_params=_params(1),
        name="mix_ffn_ctx" if ctx_stream else "mix_ffn",
    )(*[a for y in y2d for a in (y, y, y)], x2d, x2d, x2d, mod, g_row, wo_bf, wu_bf, conv_w,
      conv_b.reshape(1, 2 * D_FF), wd_bf)


def _dft_tables(n):
    lo_n = 64
    half = n // 2
    t = jnp.arange(half, dtype=jnp.int32)

    def table(k):
        ang = ((k[:, None] * t[None, :]) % n).astype(F32) * (2.0 * math.pi / n)
        return jnp.cos(ang), jnp.sin(ang)

    (ch, sh), (cl, sl) = table(jnp.arange(n // lo_n, dtype=jnp.int32) * lo_n), table(jnp.arange(lo_n, dtype=jnp.int32))
    cn = (ch[:, None] * cl[None] - sh[:, None] * sl[None]).reshape(n, half).astype(BF16)
    sn = (sh[:, None] * cl[None] + ch[:, None] * sl[None]).reshape(n, half).astype(BF16)
    rb = min(GROUP_W, half)
    r = np.arange(rb)
    rev = np.zeros((rb, 2 * rb), np.float32)
    rev[r[1:], rb - r[1:]] = 1.0
    rev[0, rb] = 1.0
    c = np.arange(HEAD_DIM)
    angc = 2.0 * np.pi * ((c[:, None] * c[None, :]) % HEAD_DIM) / HEAD_DIM
    norm = 1.0 / math.sqrt(n * HEAD_DIM)
    eye = np.eye(N_HEADS)
    bdc = jnp.asarray(np.kron(eye, np.cos(angc) * norm), F32).astype(BF16)
    bds = jnp.asarray(np.kron(eye, np.sin(angc) * norm), F32).astype(BF16)
    return bdc, bds, cn, sn, jnp.asarray(rev, F32).astype(BF16)


def _rope_tables(n):
    n_freq = DIFF_QK // 4
    freqs = ROPE_BASE ** (-jnp.arange(n_freq, dtype=F32) / n_freq)
    t = jnp.arange(n)
    row = (t // GRID_W).astype(F32)
    col = (t % GRID_W).astype(F32)
    ang = jnp.concatenate([row[:, None] * freqs, col[:, None] * freqs], axis=-1)
    cos, sin = jnp.cos(ang), jnp.sin(ang)
    reps = GROUP_W // DIFF_QK
    return jnp.tile(jnp.concatenate([cos, cos], axis=-1), (1, reps)), jnp.tile(jnp.concatenate([-sin, sin], axis=-1), (1, reps))


def _group_mean_matrix(group):
    return jnp.asarray(np.kron(np.eye(GROUP_W // group), np.full((group, group), 1.0 / group)), F32).astype(BF16)


def kernel(x, c, ctx, c_ctx, w_ada, b_ada, g_mix, g_ffn, w_in, w_out, diff_qn, diff_kn, diff_lam, diff_subln, na_qn,
           na_kn, na_rpb, gmlp_norm, gmlp_ws, gmlp_b, ffn_up, ffn_conv, ffn_conv_b, ffn_down):
    b, n, d = x.shape
    nc = ctx.shape[1]
    assert (d, n % (NA_Q_ROWS * GRID_W), nc % CHUNK, b < ADA_ROWS // 2 + 1) == (D_MODEL, 0, 0, True)
    tm = 512
    tq = 512

    c_all = jnp.zeros((ADA_ROWS, d), F32).at[:b].set(c).at[ADA_ROWS // 2].set(c_ctx)
    mods = _ada(c_all, w_ada, b_ada).reshape(DEPTH, ADA_ROWS, 6, d)

    g32 = _group_mean_matrix(DIFF_QK)
    g64 = _group_mean_matrix(HEAD_DIM)
    rope_tabs = _rope_tables(n)
    dft_lat = _dft_tables(n)
    dft_ctx = _dft_tables(nc)

    w_in_bf, w_out_bf, wu_bf, wd_bf = (w.astype(BF16) for w in (w_in, w_out, ffn_up, ffn_down))

    x2d = x.reshape(b * n, d)
    c2d = ctx.reshape(b * nc, d)
    for li in range(DEPTH):
        ctx_out = li < DEPTH - 1
        lam_init = 0.8 - 0.6 * math.exp(-0.3 * li)
        lf = diff_lam[li].astype(F32)
        lam = (jnp.exp(jnp.sum(lf[0] * lf[1])) - jnp.exp(jnp.sum(lf[2] * lf[3])) + lam_init).reshape(1)
        mod = mods[li]
        gains = jnp.zeros((8, GROUP_W), F32)
        gains = gains.at[0].set(jnp.tile(diff_qn[li], GROUP_W // DIFF_QK) * (DIFF_QK ** -0.5 * LOG2E))
        gains = gains.at[1].set(jnp.tile(diff_kn[li], GROUP_W // DIFF_QK))
        gains = gains.at[2].set(jnp.tile(na_qn[li], N_HEADS) * (HEAD_DIM ** -0.5 * LOG2E))
        gains = gains.at[3].set(jnp.tile(na_kn[li], N_HEADS))
        gains = gains.at[4].set(gmlp_norm[li])
        sub_row = (jnp.tile(diff_subln[li], N_HEADS) * (1.0 - lam_init)).reshape(1, GROUP_W)
        g_mix_row = g_mix[li].reshape(1, d)
        g_ffn_row = g_ffn[li].reshape(1, d)
        ws_stack = gmlp_ws[li].reshape(N_HEADS * CHUNK, CHUNK).astype(BF16)
        gbias = jnp.repeat(gmlp_b[li].T, HEAD_DIM, axis=1)
        bias_tab = _na_bias_tables(na_rpb[li], nc)

        p2d, dvt, nvt = _inproj(x2d, mod, g_mix_row, w_in_bf, li, gains, g32, g64, rope_tabs, n, tm, False)
        pc2d, dvtc, nvtc = _inproj(c2d, mod, g_mix_row, w_in_bf, li, gains, g32, g64, None, nc, nc, True)
        p3 = p2d.reshape(b, n, P_W)
        pc3 = pc2d.reshape(b, nc, P_W)

        ys = [_fourier(p3, *dft_lat, "fourier"),
              _diff(lam, p3, dvt, pc3, dvtc, sub_row, tq, True),
              _na(p3, nvt, pc3, nvtc, bias_tab),
              _gmlp(p3, ws_stack, gbias, "gmlp")]
        ffn_w = (g_ffn_row, w_out_bf, wu_bf, ffn_conv[li], ffn_conv_b[li], wd_bf, li)
        x2d = _mix_ffn(ys, x2d, mod, *ffn_w, n, tm, False)

        if ctx_out:
            ycs = [_fourier(pc3, *dft_ctx, "fourier_ctx"),
                   _diff(lam, None, None, pc3, dvtc, sub_row, nc, False),
                   _na_ctx(pc3, nvtc),
                   _gmlp(pc3, ws_stack, gbias, "gmlp_ctx")]
            c2d = _mix_ffn(ycs, c2d, mod, *ffn_w, nc, nc, True)
    return x2d.reshape(b, n, d)
```

```python
import functools
import math

import jax
import jax.numpy as jnp
import numpy as np
from jax import lax
from jax.experimental import pallas as pl
from jax.experimental.pallas import tpu as pltpu

F32 = jnp.float32
BF16 = jnp.bfloat16

D_MODEL = 1024
DEPTH = 4
GRID_W = 64
GROUP_W = 256
HEAD_DIM = 64
N_HEADS = 4
DIFF_QK = 32
NA_KH = 8
NA_KW = 16
CHUNK = 128
D_FF = 2816
ROPE_BASE = 10000.0
EPS = 1e-6
IN_W = 9 * GROUP_W

REF_F, REF_DQ, REF_DK, REF_DV, REF_NQ, REF_NK, REF_NV, REF_GU, REF_GV = range(9)
W_ORDER = (REF_GV, REF_DQ, REF_DK, REF_NQ, REF_NK, REF_GU, REF_DV, REF_NV, REF_F)
W_GV, W_DQ, W_DK, W_NQ, W_NK, W_GU, W_DV, W_NV, W_F = range(9)
COL_F, COL_DQ, COL_DK, COL_NQ, COL_NK, COL_GU, COL_GV = range(7)
P_W = 7 * GROUP_W

VT_ROWS = HEAD_DIM + 16
DIFF_HEADS_PER_DOT = 2
PAIR_ROWS = 16
SUM_FLOOR = 2.0 ** -80
NA_Q_ROWS = 4
NA_ELEMS = 2
NA_WIN_ROWS = 12
NEG = -1e30
LOG2E = math.log2(math.e)

ADA_ROWS = 16
VMEM_LIMIT = 56 * 1024 * 1024
HALO = 16
FF_CHUNK = 256


def _dot(a, b):
    return jnp.dot(a, b, preferred_element_type=F32)


def _dot_nt(a, b):
    return lax.dot_general(a, b, (((1,), (1,)), ((), ())), preferred_element_type=F32)


def _resident(shape):
    nd = len(shape)
    return pl.BlockSpec(shape, lambda *_: (0,) * nd, pipeline_mode=pl.Buffered(1))


def _resident_layer(shape, li):
    nd = len(shape)
    return pl.BlockSpec((None,) + tuple(shape), lambda *_: (li,) + (0,) * nd, pipeline_mode=pl.Buffered(1))


def _params(n_axes):
    return pltpu.CompilerParams(dimension_semantics=("arbitrary",) * n_axes, vmem_limit_bytes=VMEM_LIMIT)


def _lane(width=GROUP_W):
    return lax.broadcasted_iota(jnp.int32, (1, width), 1)


def _ada_kernel(c_ref, w_ref, b_ref, o_ref):
    c = c_ref[...]
    s = c / (1.0 + jnp.exp(-c))
    w = w_ref[0]
    s_hi = s.astype(BF16)
    s_lo = (s - s_hi.astype(F32)).astype(BF16)
    w_hi = w.astype(BF16)
    w_lo = (w - w_hi.astype(F32)).astype(BF16)
    o_ref[0] = _dot(s_hi, w_hi) + _dot(s_lo, w_hi) + _dot(s_hi, w_lo) + b_ref[0]


def _ada(c_all, w_ada, b_ada):
    tn = 1536
    out_w = 6 * D_MODEL
    return pl.pallas_call(
        _ada_kernel,
        grid=(DEPTH, out_w // tn),
        in_specs=[
            pl.BlockSpec((ADA_ROWS, D_MODEL), lambda l, j: (0, 0)),
            pl.BlockSpec((1, D_MODEL, tn), lambda l, j: (l, 0, j)),
            pl.BlockSpec((1, 1, tn), lambda l, j: (l, 0, j)),
        ],
        out_specs=pl.BlockSpec((1, ADA_ROWS, tn), lambda l, j: (l, 0, j)),
        out_shape=jax.ShapeDtypeStruct((DEPTH, ADA_ROWS, out_w), F32),
        compiler_params=_params(2),
        name="ada",
    )(c_all, w_ada, b_ada.reshape(DEPTH, 1, out_w))


def _gelu(x):
    return 0.5 * x * (1.0 + jnp.tanh(math.sqrt(2.0 / math.pi) * (x + 0.044715 * (x * x * x))))


def _inproj_kernel(*refs, rope):
    if rope:
        x_ref, mod_ref, g_ref, w_ref, gains_ref, g32_ref, g64_ref, cos_ref, sin_ref, o_ref, dvt_ref, nvt_ref = refs
    else:
        x_ref, mod_ref, g_ref, w_ref, gains_ref, g32_ref, g64_ref, o_ref, dvt_ref, nvt_ref = refs
    x = x_ref[...]
    ms = jnp.mean(x * x, axis=-1, keepdims=True)
    h = x * lax.rsqrt(ms + EPS) * g_ref[...]
    h = h * (1.0 + mod_ref[0, 1:2, :]) + mod_ref[0, 0:1, :]
    p = _dot(h.astype(BF16), w_ref[...])

    def col(j):
        return p[:, j * GROUP_W:(j + 1) * GROUP_W]

    def put(j, v):
        o_ref[:, j * GROUP_W:(j + 1) * GROUP_W] = v.astype(BF16)

    def group_norm(v, gmat_ref, gain_row):
        gms = _dot((v * v).astype(BF16), gmat_ref[...])
        return v * lax.rsqrt(gms + EPS) * gains_ref[gain_row:gain_row + 1, :]

    def rotary(v):
        if not rope:
            return v
        first_half = (_lane() % DIFF_QK) < (DIFF_QK // 2)
        partner = jnp.where(first_half, pltpu.roll(v, GROUP_W - DIFF_QK // 2, 1), pltpu.roll(v, DIFF_QK // 2, 1))
        return v * cos_ref[...] + partner * sin_ref[...]

    put(COL_GV, group_norm(_gelu(col(W_GV)), g64_ref, 4))
    put(COL_DQ, rotary(group_norm(col(W_DQ), g32_ref, 0)))
    put(COL_DK, rotary(group_norm(col(W_DK), g32_ref, 1)))
    put(COL_NQ, group_norm(col(W_NQ), g64_ref, 2))
    put(COL_NK, group_norm(col(W_NK), g64_ref, 3))
    put(COL_GU, _gelu(col(W_GU)))
    dvt_ref[0] = col(W_DV).T.astype(BF16)
    nvt_ref[0] = col(W_NV).T.astype(BF16)
    put(COL_F, col(W_F))


def _inproj(x2d, mod, g_row, w_bf, li, gains, g32, g64, rope_tabs, n, tm, ctx_stream):
    rows = x2d.shape[0]
    tpb = n // tm
    mod_row = (lambda i: (ADA_ROWS // 2, 0, 0)) if ctx_stream else (lambda i: (i // tpb, 0, 0))
    in_specs = [
        pl.BlockSpec((tm, D_MODEL), lambda i: (i, 0)),
        pl.BlockSpec((1, 6, D_MODEL), mod_row),
        _resident((1, D_MODEL)),
        _resident_layer((D_MODEL, IN_W), li),
        _resident((8, GROUP_W)),
        _resident((GROUP_W, GROUP_W)),
        _resident((GROUP_W, GROUP_W)),
    ]
    args = [x2d, mod, g_row, w_bf, gains, g32, g64]
    if rope_tabs is not None:
        in_specs += [pl.BlockSpec((tm, GROUP_W), lambda i: (i % tpb, 0))] * 2
        args += list(rope_tabs)
    return pl.pallas_call(
        functools.partial(_inproj_kernel, rope=rope_tabs is not None),
        grid=(rows // tm,),
        in_specs=in_specs,
        out_specs=[pl.BlockSpec((tm, P_W), lambda i: (i, 0))]
        + [pl.BlockSpec((1, GROUP_W, tm), lambda i: (i // tpb, 0, i % tpb))] * 2,
        out_shape=[jax.ShapeDtypeStruct((rows, P_W), BF16)]
        + [jax.ShapeDtypeStruct((rows // n, GROUP_W, n), BF16)] * 2,
        compiler_params=_params(1),
        name="inproj_ctx" if ctx_stream else "inproj",
    )(*args)


def _fourier_kernel(p_ref, bdc_ref, bds_ref, cn_ref, sn_ref, rev_ref, o_ref):
    n = p_ref.shape[1]
    half = n // 2
    rb = rev_ref.shape[0]
    nb = n // rb
    mirrored = []
    for blk in range(half // rb):
        src = nb - blk - 1
        nxt = (src + 1) % nb
        pair = jnp.concatenate([p_ref[0, src * rb:(src + 1) * rb, :], p_ref[0, nxt * rb:(nxt + 1) * rb, :]], axis=0)
        mirrored.append(_dot(rev_ref[...], pair))
    mirrored = jnp.concatenate(mirrored, axis=0)
    x = p_ref[0, 0:half, :].astype(F32)
    first = lax.broadcasted_iota(jnp.int32, x.shape, 0) == 0
    fold_c = jnp.where(first, x, x + mirrored).astype(BF16)
    fold_s = (x - mirrored).astype(BF16)
    a = _dot(fold_c, bdc_ref[...]).astype(BF16)
    b = _dot(fold_s, bds_ref[...]).astype(BF16)
    mid = _dot(p_ref[0, half:half + 16, :], bdc_ref[...])[0:1]
    out = _dot(cn_ref[...], a) - _dot(sn_ref[...], b)
    odd = (lax.broadcasted_iota(jnp.int32, out.shape, 0) & 1) == 1
    o_ref[0] = (out + jnp.where(odd, -mid, mid)).astype(BF16)


def _fourier(p3, bdc, bds, cn, sn, rev, name):
    b, n, _ = p3.shape
    return pl.pallas_call(
        _fourier_kernel,
        grid=(b,),
        in_specs=[
            pl.BlockSpec((1, n, GROUP_W), lambda i: (i, 0, COL_F)),
            _resident((GROUP_W, GROUP_W)),
            _resident((GROUP_W, GROUP_W)),
            _resident((n, n // 2)),
            _resident((n, n // 2)),
            _resident(rev.shape),
        ],
        out_specs=pl.BlockSpec((1, n, GROUP_W), lambda i: (i, 0, 0)),
        out_shape=jax.ShapeDtypeStruct((b, n, GROUP_W), BF16),
        compiler_params=_params(1),
        name=name,
    )(p3, bdc, bds, cn, sn, rev)


def _column_max(parts):
    mx = parts[0].max(axis=0, keepdims=True)
    for s in parts[1:]:
        mx = jnp.maximum(mx, s.max(axis=0, keepdims=True))
    return mx


def _diff_kernel(*refs, has_lat):
    if has_lat:
        lam_ref, q_ref, kl_ref, vl_ref, kc_ref, vc_ref, sub_ref, sel_ref, o_ref, vt_ref, kmax_ref = refs
        key_refs, val_refs = [kl_ref, kc_ref], [vl_ref, vc_ref]
    else:
        lam_ref, q_ref, kc_ref, vc_ref, sub_ref, sel_ref, o_ref, vt_ref, kmax_ref = refs
        key_refs, val_refs = [kc_ref], [vc_ref]
    counts = [r.shape[1] for r in key_refs]
    starts = [sum(counts[:i]) for i in range(len(counts))]

    def pair_sq_norms(x):
        xf = x.astype(F32)
        return _dot_nt(sel_ref[...], (xf * xf).astype(BF16))

    @pl.when(pl.program_id(1) == 0)
    def _():
        for r, lo, cnt in zip(val_refs, starts, counts):
            for h in range(N_HEADS):
                vt_ref[h, 0:HEAD_DIM, lo:lo + cnt] = r[0, h * HEAD_DIM:(h + 1) * HEAD_DIM, :]
                vt_ref[h, HEAD_DIM:VT_ROWS, lo:lo + cnt] = jnp.ones((VT_ROWS - HEAD_DIM, cnt), BF16)
        kmax = None
        for r in key_refs:
            m = pair_sq_norms(r[0]).max(axis=1, keepdims=True)
            kmax = m if kmax is None else jnp.maximum(kmax, m)
        kmax_ref[...] = jnp.broadcast_to(kmax, kmax_ref.shape)

    q = q_ref[0]
    lam = lam_ref[0]
    lane = _lane()
    bound = jnp.sqrt(pair_sq_norms(q) * kmax_ref[:, 0:1])

    def attend(shift_fn):
        outs = []
        tq = q.shape[0]
        for h0 in range(0, N_HEADS, DIFF_HEADS_PER_DOT):
            pairs = range(2 * h0, 2 * (h0 + DIFF_HEADS_PER_DOT))
            qm = jnp.concatenate(
                [jnp.where((lane >= pair * DIFF_QK) & (lane < (pair + 1) * DIFF_QK), q, jnp.zeros_like(q))
                 for pair in pairs], axis=0)
            parts = [_dot_nt(r[0], qm) for r in key_refs]
            shift = shift_fn(parts, pairs)
            es = [jnp.exp2(s - shift).astype(BF16) for s in parts]
            for i in range(DIFF_HEADS_PER_DOT):
                o = None
                for e, klo, cnt in zip(es, starts, counts):
                    t = _dot(vt_ref[h0 + i, :, klo:klo + cnt], e[:, 2 * i * tq:2 * (i + 1) * tq])
                    o = t if o is None else o + t
                outs += [o[:, 0:tq], o[:, tq:]]
        return outs

    def finish(outs):
        heads = []
        for h in range(N_HEADS):
            o1, o2 = outs[2 * h], outs[2 * h + 1]
            o = (o1[0:HEAD_DIM] * (1.0 / o1[HEAD_DIM:HEAD_DIM + 1])
                 - o2[0:HEAD_DIM] * (lam / o2[HEAD_DIM:HEAD_DIM + 1]))
            heads.append(o * lax.rsqrt(jnp.mean(o * o, axis=0, keepdims=True) + EPS))
        o_ref[0] = (jnp.concatenate(heads, axis=0) * sub_ref[...]).T.astype(BF16)

    outs = attend(lambda parts, pairs: jnp.concatenate([bound[p:p + 1] for p in pairs], axis=1))
    finish(outs)
    smallest = outs[0][HEAD_DIM:HEAD_DIM + 1]
    for o in outs[1:]:
        smallest = jnp.minimum(smallest, o[HEAD_DIM:HEAD_DIM + 1])

    @pl.when(jnp.min(smallest) < SUM_FLOOR)
    def _():
        finish(attend(lambda parts, pairs: _column_max(parts)))


def _diff(lam, p3, vt3, pc3, vtc3, sub_row, tq, has_lat):
    q_src = p3 if has_lat else pc3
    sub_t = jnp.broadcast_to(sub_row.reshape(GROUP_W, 1), (GROUP_W, tq))
    b, n, _ = q_src.shape
    nc = pc3.shape[1]
    smem = pl.BlockSpec(memory_space=pltpu.SMEM)
    in_specs = [smem, pl.BlockSpec((1, tq, GROUP_W), lambda i, j: (i, j, COL_DQ))]
    args = [lam, q_src]
    if has_lat:
        in_specs += [pl.BlockSpec((1, n, GROUP_W), lambda i, j: (i, 0, COL_DK)),
                     pl.BlockSpec((1, GROUP_W, n), lambda i, j: (i, 0, 0))]
        args += [p3, vt3]
    in_specs += [pl.BlockSpec((1, nc, GROUP_W), lambda i, j: (i, 0, COL_DK)),
                 pl.BlockSpec((1, GROUP_W, nc), lambda i, j: (i, 0, 0)),
                 _resident((GROUP_W, tq)),
                 _resident((PAIR_ROWS, GROUP_W))]
    pair_sel = np.arange(PAIR_ROWS)[:, None] == np.arange(GROUP_W)[None, :] // DIFF_QK
    args += [pc3, vtc3, sub_t, jnp.asarray(pair_sel, F32).astype(BF16)]
    return pl.pallas_call(
        functools.partial(_diff_kernel, has_lat=has_lat),
        grid=(b, n // tq),
        in_specs=in_specs,
        out_specs=pl.BlockSpec((1, tq, GROUP_W), lambda i, j: (i, j, 0)),
        out_shape=jax.ShapeDtypeStruct((b, n, GROUP_W), BF16),
        scratch_shapes=[pltpu.VMEM((N_HEADS, VT_ROWS, (n if has_lat else 0) + nc), BF16),
                        pltpu.VMEM((PAIR_ROWS, 128), F32)],
        compiler_params=_params(2),
        name="diff" if has_lat else "diff_ctx",
    )(*args)


def _na_kernel(*refs, local):
    if local:
        q_ref, k0_ref, k1_ref, k2_ref, v0_ref, v1_ref, v2_ref, kc_ref, vc_ref, bias_ref, o_ref = refs
        key_refs = [k0_ref, k1_ref, k2_ref, kc_ref]
        vt_refs = [v0_ref, v1_ref, v2_ref, vc_ref]
    else:
        q_ref, kc_ref, vc_ref, o_ref = refs
        key_refs = [kc_ref]
        vt_refs = [vc_ref]
    n_elems, blk = q_ref.shape[0], q_ref.shape[1]
    lane = _lane()

    def scores(el):
        q = q_ref[el]
        q_heads = jnp.concatenate(
            [jnp.where((lane >= h * HEAD_DIM) & (lane < (h + 1) * HEAD_DIM), q, jnp.zeros_like(q))
             for h in range(N_HEADS)], axis=0)
        k_all = jnp.concatenate([r[el] for r in key_refs], axis=0)
        return _dot_nt(k_all, q_heads)

    def weights(s):
        if local:
            s = s + bias_ref[0]
        return jnp.exp2(s - s.max(axis=0, keepdims=True)).astype(BF16)

    def attend(el, e):
        vt_all = jnp.concatenate([r[el] for r in vt_refs], axis=1)
        ones = jnp.ones((VT_ROWS - HEAD_DIM, vt_all.shape[1]), BF16)
        heads = []
        for h in range(N_HEADS):
            vt_h = jnp.concatenate([vt_all[h * HEAD_DIM:(h + 1) * HEAD_DIM], ones], axis=0)
            o = _dot(vt_h, e[:, h * blk:(h + 1) * blk])
            heads.append(o[0:HEAD_DIM] * (1.0 / o[HEAD_DIM:HEAD_DIM + 1]))
        o_ref[el] = jnp.concatenate(heads, axis=0).T.astype(BF16)

    es = [weights(s) for s in [scores(el) for el in range(n_elems)]]
    for el in range(n_elems):
        attend(el, es[el])


def _na(p3, vt3, pc3, vtc3, bias):
    b, n, _ = p3.shape
    nc = pc3.shape[1]
    ne = NA_ELEMS
    blk = NA_Q_ROWS * GRID_W
    steps = n // blk
    last_win = n // blk - NA_WIN_ROWS // NA_Q_ROWS

    def first(g):
        return jnp.clip(g - 1, 0, last_win)

    def cls(g, i):
        return ((g > 0).astype(jnp.int32) + (g == steps - 1).astype(jnp.int32), 0, 0)

    in_specs = [pl.BlockSpec((ne, blk, GROUP_W), lambda g, i: (i, g, COL_NQ))]
    in_specs += [pl.BlockSpec((ne, blk, GROUP_W), lambda g, i, j=j: (i, first(g) + j, COL_NK)) for j in range(3)]
    in_specs += [pl.BlockSpec((ne, GROUP_W, blk), lambda g, i, j=j: (i, 0, first(g) + j)) for j in range(3)]
    in_specs += [pl.BlockSpec((ne, nc, GROUP_W), lambda g, i: (i, 0, COL_NK)),
                 pl.BlockSpec((ne, GROUP_W, nc), lambda g, i: (i, 0, 0)),
                 pl.BlockSpec((1, 3 * blk + nc, N_HEADS * blk), cls)]
    return pl.pallas_call(
        functools.partial(_na_kernel, local=True),
        grid=(steps, b // ne),
        in_specs=in_specs,
        out_specs=pl.BlockSpec((ne, blk, GROUP_W), lambda g, i: (i, g, 0)),
        out_shape=jax.ShapeDtypeStruct((b, n, GROUP_W), BF16),
        compiler_params=_params(2),
        name="na",
    )(p3, p3, p3, p3, vt3, vt3, vt3, pc3, vtc3, bias)


def _na_ctx(pc3, vtc3):
    b, nc, _ = pc3.shape
    ne = NA_ELEMS
    return pl.pallas_call(
        functools.partial(_na_kernel, local=False),
        grid=(b // ne,),
        in_specs=[pl.BlockSpec((ne, nc, GROUP_W), lambda i: (i, 0, COL_NQ)),
                  pl.BlockSpec((ne, nc, GROUP_W), lambda i: (i, 0, COL_NK)),
                  pl.BlockSpec((ne, GROUP_W, nc), lambda i: (i, 0, 0))],
        out_specs=pl.BlockSpec((ne, nc, GROUP_W), lambda i: (i, 0, 0)),
        out_shape=jax.ShapeDtypeStruct((b, nc, GROUP_W), BF16),
        compiler_params=_params(1),
        name="na_ctx",
    )(pc3, pc3, vtc3)


def _na_row_offsets():
    first_row = NA_WIN_ROWS - NA_KH
    rules = [lambda i, a: a - i + NA_KH - 1 if a < NA_KH else None,
             lambda i, a: a - i + NA_KH // 2 - 1 if i <= a < i + NA_KH else None,
             lambda i, a: a - i - 1 if a >= first_row else None]
    return [[[rule(i, a) for a in range(NA_WIN_ROWS)] for i in range(NA_Q_ROWS)] for rule in rules]


def _na_bias_kernel(band_ref, o_ref):
    masked = jnp.full((GRID_W, GRID_W), NEG, F32)
    for kind, per_query_row in enumerate(_na_row_offsets()):
        for a in range(NA_WIN_ROWS):
            blocks = [masked if offs[a] is None else band_ref[0, offs[a]] for offs in per_query_row]
            o_ref[kind, a * GRID_W:(a + 1) * GRID_W, :] = jnp.concatenate(blocks, axis=1)
        o_ref[kind, NA_WIN_ROWS * GRID_W:, :] = jnp.zeros((o_ref.shape[1] - NA_WIN_ROWS * GRID_W, o_ref.shape[2]), F32)


def _na_bias_tables(rpb, nc):
    kc = np.arange(GRID_W)[:, None]
    j = np.arange(GRID_W)[None, :]
    c0 = np.clip(j - NA_KW // 2, 0, GRID_W - NA_KW)
    col_ok = (kc >= c0) & (kc < c0 + NA_KW)
    onehot = (np.arange(2 * NA_KW - 1)[:, None, None] == (kc - j + NA_KW - 1)[None]) & col_ok[None]
    band = jnp.einsum("hdo,okj->hdkj", rpb.astype(F32), jnp.asarray(onehot, F32), precision=lax.Precision.HIGHEST)
    band = jnp.where(col_ok, band * LOG2E, NEG)
    n_off = 2 * NA_KH - 1
    q, k = NA_Q_ROWS * GRID_W, NA_WIN_ROWS * GRID_W
    return pl.pallas_call(
        _na_bias_kernel,
        grid=(N_HEADS,),
        in_specs=[pl.BlockSpec((1, n_off, GRID_W, GRID_W), lambda h: (h, 0, 0, 0))],
        out_specs=pl.BlockSpec((3, k + nc, q), lambda h: (0, 0, h)),
        out_shape=jax.ShapeDtypeStruct((3, k + nc, N_HEADS * q), F32),
        compiler_params=_params(1),
        name="na_bias",
    )(band)


def _gmlp_kernel(u_ref, v_ref, ws_ref, bias_ref, o_ref, *, chunks_per_step):
    n = u_ref.shape[1]
    lane = _lane()
    step_rows = chunks_per_step * CHUNK
    for s in range(n // step_rows):
        base = s * step_rows
        vcat = jnp.concatenate(
            [v_ref[0, base + c * CHUNK: base + (c + 1) * CHUNK, :] for c in range(chunks_per_step)], axis=1)
        r = _dot(ws_ref[...], vcat)
        for c in range(chunks_per_step):
            mix = r[0:CHUNK, c * GROUP_W:(c + 1) * GROUP_W]
            for g in range(1, N_HEADS):
                mix = jnp.where(lane >= g * HEAD_DIM, r[g * CHUNK:(g + 1) * CHUNK, c * GROUP_W:(c + 1) * GROUP_W], mix)
            rows = slice(base + c * CHUNK, base + (c + 1) * CHUNK)
            o_ref[0, rows, :] = (u_ref[0, rows, :].astype(F32) * (mix + bias_ref[...])).astype(BF16)


def _gmlp(p3, ws_stack, bias_full, name):
    b, n, _ = p3.shape
    cps = min(4, n // CHUNK)
    return pl.pallas_call(
        functools.partial(_gmlp_kernel, chunks_per_step=cps),
        grid=(b,),
        in_specs=[pl.BlockSpec((1, n, GROUP_W), lambda i: (i, 0, COL_GU)),
                  pl.BlockSpec((1, n, GROUP_W), lambda i: (i, 0, COL_GV)),
                  _resident((N_HEADS * CHUNK, CHUNK)),
                  _resident((CHUNK, GROUP_W))],
        out_specs=pl.BlockSpec((1, n, GROUP_W), lambda i: (i, 0, 0)),
        out_shape=jax.ShapeDtypeStruct((b, n, GROUP_W), BF16),
        compiler_params=_params(1),
        name=name,
    )(p3, p3, ws_stack, bias_full)


def _mix_ffn_kernel(*refs, tiles_per_seq):
    y_refs, (xp_ref, xm_ref, xn_ref) = refs[:12], refs[12:15]
    mod_ref, g_ref, wo_ref, wu_ref, cw_ref, cb_ref, wd_ref, o_ref, xs_ref, act_ref = refs[15:]
    tm = xm_ref.shape[0]
    pos = pl.program_id(0) % tiles_per_seq

    def with_halo(p_ref, m_ref, n_ref):
        return jnp.concatenate([p_ref[...], m_ref[...], n_ref[...]], axis=0)

    acc = None
    for j in range(4):
        t = _dot(with_halo(*y_refs[3 * j:3 * j + 3]), wo_ref[j * GROUP_W:(j + 1) * GROUP_W, :])
        acc = t if acc is None else acc + t
    x = with_halo(xp_ref, xm_ref, xn_ref) + mod_ref[0, 2:3, :] * acc
    xs_ref[...] = x[HALO:HALO + tm]
    ms = jnp.mean(x * x, axis=-1, keepdims=True)
    h = x * lax.rsqrt(ms + EPS) * g_ref[...]
    h = (h * (1.0 + mod_ref[0, 4:5, :]) + mod_ref[0, 3:4, :]).astype(BF16)
    hp, hn = h[0:HALO], h[HALO + tm:]
    hp = jnp.where(pos != 0, hp, jnp.zeros_like(hp))
    hn = jnp.where(pos != tiles_per_seq - 1, hn, jnp.zeros_like(hn))
    hcat = jnp.concatenate([hp, h[HALO:HALO + tm], hn], axis=0)

    def conv(z, lo):
        rows = z.shape[0]
        cols = slice(lo, lo + FF_CHUNK)
        return (pltpu.roll(z, 1, 0)[HALO:HALO + tm] * cw_ref[0:1, cols]
                + z[HALO:HALO + tm] * cw_ref[1:2, cols]
                + pltpu.roll(z, rows - 1, 0)[HALO:HALO + tm] * cw_ref[2:3, cols] + cb_ref[:, cols])

    for c in range(D_FF // FF_CHUNK):
        lo_g = c * FF_CHUNK
        lo_v = D_FF + c * FF_CHUNK
        g = conv(_dot(hcat, wu_ref[:, lo_g:lo_g + FF_CHUNK]), lo_g)
        v = conv(_dot(hcat, wu_ref[:, lo_v:lo_v + FF_CHUNK]), lo_v)
        act_ref[:, lo_g:lo_g + FF_CHUNK] = (g / (1.0 + jnp.exp(-g)) * v).astype(BF16)
    o_ref[...] = xs_ref[...] + mod_ref[0, 5:6, :] * _dot(act_ref[...], wd_ref[...])


def _mix_ffn(ys, x2d, mod, g_row, wo_bf, wu_bf, conv_w, conv_b, wd_bf, li, n, tm, ctx_stream):
    rows = x2d.shape[0]
    tpb = n // tm
    per_tile = tm // HALO
    last_halo = rows // HALO - 1
    mod_row = (lambda i: (ADA_ROWS // 2, 0, 0)) if ctx_stream else (lambda i: (i // tpb, 0, 0))

    def halo_specs(width):
        return [pl.BlockSpec((HALO, width), lambda i: (jnp.maximum(i * per_tile - 1, 0), 0)),
                pl.BlockSpec((tm, width), lambda i: (i, 0)),
                pl.BlockSpec((HALO, width), lambda i: (jnp.minimum((i + 1) * per_tile, last_halo), 0))]

    y2d = [y.reshape(rows, GROUP_W) for y in ys]
    return pl.pallas_call(
        functools.partial(_mix_ffn_kernel, tiles_per_seq=tpb),
        grid=(rows // tm,),
        in_specs=halo_specs(GROUP_W) * 4 + halo_specs(D_MODEL) + [
            pl.BlockSpec((1, 6, D_MODEL), mod_row),
            _resident((1, D_MODEL)),
            _resident_layer((D_MODEL, D_MODEL), li),
            _resident_layer((D_MODEL, 2 * D_FF), li),
            _resident((3, 2 * D_FF)),
            _resident((1, 2 * D_FF)),
            _resident_layer((D_FF, D_MODEL), li),
        ],
        out_specs=pl.BlockSpec((tm, D_MODEL), lambda i: (i, 0)),
        out_shape=jax.ShapeDtypeStruct((rows, D_MODEL), F32),
        scratch_shapes=[pltpu.VMEM((tm, D_MODEL), F32), pltpu.VMEM((tm, D_FF), BF16)],
        compiler_params=_params(1),
        name="mix_ffn_ctx" if ctx_stream else "mix_ffn",
    )(*[a for y in y2d for a in (y, y, y)], x2d, x2d, x2d, mod, g_row, wo_bf, wu_bf, conv_w,
      conv_b.reshape(1, 2 * D_FF), wd_bf)


def _dft_tables(n):
    lo_n = 64
    half = n // 2
    t = jnp.arange(half, dtype=jnp.int32)

    def table(k):
        ang = ((k[:, None] * t[None, :]) % n).astype(F32) * (2.0 * math.pi / n)
        return jnp.cos(ang), jnp.sin(ang)

    (ch, sh), (cl, sl) = table(jnp.arange(n // lo_n, dtype=jnp.int32) * lo_n), table(jnp.arange(lo_n, dtype=jnp.int32))
    cn = (ch[:, None] * cl[None] - sh[:, None] * sl[None]).reshape(n, half).astype(BF16)
    sn = (sh[:, None] * cl[None] + ch[:, None] * sl[None]).reshape(n, half).astype(BF16)
    rb = min(GROUP_W, half)
    r = np.arange(rb)
    rev = np.zeros((rb, 2 * rb), np.float32)
    rev[r[1:], rb - r[1:]] = 1.0
    rev[0, rb] = 1.0
    c = np.arange(HEAD_DIM)
    angc = 2.0 * np.pi * ((c[:, None] * c[None, :]) % HEAD_DIM) / HEAD_DIM
    norm = 1.0 / math.sqrt(n * HEAD_DIM)
    eye = np.eye(N_HEADS)
    bdc = jnp.asarray(np.kron(eye, np.cos(angc) * norm), F32).astype(BF16)
    bds = jnp.asarray(np.kron(eye, np.sin(angc) * norm), F32).astype(BF16)
    return bdc, bds, cn, sn, jnp.asarray(rev, F32).astype(BF16)


def _rope_tables(n):
    n_freq = DIFF_QK // 4
    freqs = ROPE_BASE ** (-jnp.arange(n_freq, dtype=F32) / n_freq)
    t = jnp.arange(n)
    row = (t // GRID_W).astype(F32)
    col = (t % GRID_W).astype(F32)
    ang = jnp.concatenate([row[:, None] * freqs, col[:, None] * freqs], axis=-1)
    cos, sin = jnp.cos(ang), jnp.sin(ang)
    reps = GROUP_W // DIFF_QK
    return jnp.tile(jnp.concatenate([cos, cos], axis=-1), (1, reps)), jnp.tile(jnp.concatenate([-sin, sin], axis=-1), (1, reps))


def _group_mean_matrix(group):
    return jnp.asarray(np.kron(np.eye(GROUP_W // group), np.full((group, group), 1.0 / group)), F32).astype(BF16)


def kernel(x, c, ctx, c_ctx, w_ada, b_ada, g_mix, g_ffn, w_in, w_out, diff_qn, diff_kn, diff_lam, diff_subln, na_qn,
           na_kn, na_rpb, gmlp_norm, gmlp_ws, gmlp_b, ffn_up, ffn_conv, ffn_conv_b, ffn_down):
    b, n, d = x.shape
    nc = ctx.shape[1]
    assert (d, n % (NA_Q_ROWS * GRID_W), nc % CHUNK, b < ADA_ROWS // 2 + 1) == (D_MODEL, 0, 0, True)
    tm = 512
    tq = 512

    c_all = jnp.zeros((ADA_ROWS, d), F32).at[:b].set(c).at[ADA_ROWS // 2].set(c_ctx)
    mods = _ada(c_all, w_ada, b_ada).reshape(DEPTH, ADA_ROWS, 6, d)

    g32 = _group_mean_matrix(DIFF_QK)
    g64 = _group_mean_matrix(HEAD_DIM)
    rope_tabs = _rope_tables(n)
    dft_lat = _dft_tables(n)
    dft_ctx = _dft_tables(nc)

    w_in_kernel_order = jnp.concatenate([w_in[..., r * GROUP_W:(r + 1) * GROUP_W] for r in W_ORDER], axis=-1)
    w_in_bf, w_out_bf, wu_bf, wd_bf = (w.astype(BF16) for w in (w_in_kernel_order, w_out, ffn_up, ffn_down))

    x2d = x.reshape(b * n, d)
    c2d = ctx.reshape(b * nc, d)
    for li in range(DEPTH):
        ctx_out = li < DEPTH - 1
        lam_init = 0.8 - 0.6 * math.exp(-0.3 * li)
        lf = diff_lam[li].astype(F32)
        lam = (jnp.exp(jnp.sum(lf[0] * lf[1])) - jnp.exp(jnp.sum(lf[2] * lf[3])) + lam_init).reshape(1)
        mod = mods[li]
        gains = jnp.zeros((8, GROUP_W), F32)
        gains = gains.at[0].set(jnp.tile(diff_qn[li], GROUP_W // DIFF_QK) * (DIFF_QK ** -0.5 * LOG2E))
        gains = gains.at[1].set(jnp.tile(diff_kn[li], GROUP_W // DIFF_QK))
        gains = gains.at[2].set(jnp.tile(na_qn[li], N_HEADS) * (HEAD_DIM ** -0.5 * LOG2E))
        gains = gains.at[3].set(jnp.tile(na_kn[li], N_HEADS))
        gains = gains.at[4].set(gmlp_norm[li])
        sub_row = (jnp.tile(diff_subln[li], N_HEADS) * (1.0 - lam_init)).reshape(1, GROUP_W)
        g_mix_row = g_mix[li].reshape(1, d)
        g_ffn_row = g_ffn[li].reshape(1, d)
        ws_stack = gmlp_ws[li].reshape(N_HEADS * CHUNK, CHUNK).astype(BF16)
        gbias = jnp.repeat(gmlp_b[li].T, HEAD_DIM, axis=1)
        bias_tab = _na_bias_tables(na_rpb[li], nc)

        p2d, dvt, nvt = _inproj(x2d, mod, g_mix_row, w_in_bf, li, gains, g32, g64, rope_tabs, n, 2 * tm, False)
        pc2d, dvtc, nvtc = _inproj(c2d, mod, g_mix_row, w_in_bf, li, gains, g32, g64, None, nc, nc, True)
        p3 = p2d.reshape(b, n, P_W)
        pc3 = pc2d.reshape(b, nc, P_W)

        ys = [_fourier(p3, *dft_lat, "fourier"),
              _diff(lam, p3, dvt, pc3, dvtc, sub_row, tq, True),
              _na(p3, nvt, pc3, nvtc, bias_tab),
              _gmlp(p3, ws_stack, gbias, "gmlp")]
        ffn_w = (g_ffn_row, w_out_bf, wu_bf, ffn_conv[li], ffn_conv_b[li], wd_bf, li)
        x2d = _mix_ffn(ys, x2d, mod, *ffn_w, n, tm, False)

        if ctx_out:
            ycs = [_fourier(pc3, *dft_ctx, "fourier_ctx"),
                   _diff(lam, None, None, pc3, dvtc, sub_row, nc, False),
                   _na_ctx(pc3, nvtc),
                   _gmlp(pc3, ws_stack, gbias, "gmlp_ctx")]
            c2d = _mix_ffn(ycs, c2d, mod, *ffn_w, nc, nc, True)
    return x2d.reshape(b, n, d)
```

```python
import functools
import math

import jax
import jax.numpy as jnp
import numpy as np
from jax import lax
from jax.experimental import pallas as pl
from jax.experimental.pallas import tpu as pltpu

F32 = jnp.float32
BF16 = jnp.bfloat16

D_MODEL = 1024
DEPTH = 4
GRID_W = 64
GROUP_W = 256
HEAD_DIM = 64
N_HEADS = 4
DIFF_QK = 32
NA_KH = 8
NA_KW = 16
CHUNK = 128
D_FF = 2816
ROPE_BASE = 10000.0
EPS = 1e-6
IN_W = 9 * GROUP_W

REF_F, REF_DQ, REF_DK, REF_DV, REF_NQ, REF_NK, REF_NV, REF_GU, REF_GV = range(9)
W_ORDER = (REF_GV, REF_DQ, REF_DK, REF_NQ, REF_NK, REF_GU, REF_DV, REF_NV, REF_F)
W_GV, W_DQ, W_DK, W_NQ, W_NK, W_GU, W_DV, W_NV, W_F = range(9)
COL_F, COL_DQ, COL_DK, COL_NQ, COL_NK, COL_GU, COL_GV = range(7)
P_W = 7 * GROUP_W

VT_ROWS = HEAD_DIM + 16
DIFF_HEADS_PER_DOT = 2
PAIR_ROWS = 16
SUM_FLOOR = 2.0 ** -80
NA_Q_ROWS = 4
NA_ELEMS = 4
NA_WIN_ROWS = 12
NEG = -1e30
LOG2E = math.log2(math.e)

ADA_ROWS = 16
VMEM_LIMIT = 56 * 1024 * 1024
HALO = 16
FF_CHUNK = 256


def _dot(a, b):
    return jnp.dot(a, b, preferred_element_type=F32)


def _dot_nt(a, b):
    return lax.dot_general(a, b, (((1,), (1,)), ((), ())), preferred_element_type=F32)


def _resident(shape):
    nd = len(shape)
    return pl.BlockSpec(shape, lambda *_: (0,) * nd, pipeline_mode=pl.Buffered(1))


def _resident_layer(shape, li):
    nd = len(shape)
    return pl.BlockSpec((None,) + tuple(shape), lambda *_: (li,) + (0,) * nd, pipeline_mode=pl.Buffered(1))


def _params(n_axes):
    return pltpu.CompilerParams(dimension_semantics=("arbitrary",) * n_axes, vmem_limit_bytes=VMEM_LIMIT)


def _lane(width=GROUP_W):
    return lax.broadcasted_iota(jnp.int32, (1, width), 1)


def _ada_kernel(c_ref, w_ref, b_ref, o_ref):
    c = c_ref[...]
    s = c / (1.0 + jnp.exp(-c))
    w = w_ref[0]
    s_hi = s.astype(BF16)
    s_lo = (s - s_hi.astype(F32)).astype(BF16)
    w_hi = w.astype(BF16)
    w_lo = (w - w_hi.astype(F32)).astype(BF16)
    o_ref[0] = _dot(s_hi, w_hi) + _dot(s_lo, w_hi) + _dot(s_hi, w_lo) + b_ref[0]


def _ada(c_all, w_ada, b_ada):
    tn = 1536
    out_w = 6 * D_MODEL
    return pl.pallas_call(
        _ada_kernel,
        grid=(DEPTH, out_w // tn),
        in_specs=[
            pl.BlockSpec((ADA_ROWS, D_MODEL), lambda l, j: (0, 0)),
            pl.BlockSpec((1, D_MODEL, tn), lambda l, j: (l, 0, j)),
            pl.BlockSpec((1, 1, tn), lambda l, j: (l, 0, j)),
        ],
        out_specs=pl.BlockSpec((1, ADA_ROWS, tn), lambda l, j: (l, 0, j)),
        out_shape=jax.ShapeDtypeStruct((DEPTH, ADA_ROWS, out_w), F32),
        compiler_params=_params(2),
        name="ada",
    )(c_all, w_ada, b_ada.reshape(DEPTH, 1, out_w))


def _gelu(x):
    return 0.5 * x * (1.0 + jnp.tanh(math.sqrt(2.0 / math.pi) * (x + 0.044715 * (x * x * x))))


def _inproj_kernel(*refs, rope):
    if rope:
        x_ref, mod_ref, g_ref, w_ref, gains_ref, g32_ref, g64_ref, cos_ref, sin_ref, o_ref, dvt_ref, nvt_ref = refs
    else:
        x_ref, mod_ref, g_ref, w_ref, gains_ref, g32_ref, g64_ref, o_ref, dvt_ref, nvt_ref = refs
    x = x_ref[...]
    ms = jnp.mean(x * x, axis=-1, keepdims=True)
    h = x * lax.rsqrt(ms + EPS) * g_ref[...]
    h = h * (1.0 + mod_ref[0, 1:2, :]) + mod_ref[0, 0:1, :]
    p = _dot(h.astype(BF16), w_ref[...])

    def col(j):
        return p[:, j * GROUP_W:(j + 1) * GROUP_W]

    def put(j, v):
        o_ref[:, j * GROUP_W:(j + 1) * GROUP_W] = v.astype(BF16)

    def group_norm(v, gmat_ref, gain_row):
        gms = _dot((v * v).astype(BF16), gmat_ref[...])
        return v * lax.rsqrt(gms + EPS) * gains_ref[gain_row:gain_row + 1, :]

    def rotary(v):
        if not rope:
            return v
        first_half = (_lane() % DIFF_QK) < (DIFF_QK // 2)
        partner = jnp.where(first_half, pltpu.roll(v, GROUP_W - DIFF_QK // 2, 1), pltpu.roll(v, DIFF_QK // 2, 1))
        return v * cos_ref[...] + partner * sin_ref[...]

    put(COL_GV, group_norm(_gelu(col(W_GV)), g64_ref, 4))
    put(COL_DQ, rotary(group_norm(col(W_DQ), g32_ref, 0)))
    put(COL_DK, rotary(group_norm(col(W_DK), g32_ref, 1)))
    put(COL_NQ, group_norm(col(W_NQ), g64_ref, 2))
    put(COL_NK, group_norm(col(W_NK), g64_ref, 3))
    put(COL_GU, _gelu(col(W_GU)))
    dvt_ref[0] = col(W_DV).T.astype(BF16)
    nvt_ref[0] = col(W_NV).T.astype(BF16)
    put(COL_F, col(W_F))


def _inproj(x2d, mod, g_row, w_bf, li, gains, g32, g64, rope_tabs, n, tm, ctx_stream):
    rows = x2d.shape[0]
    tpb = n // tm
    mod_row = (lambda i: (ADA_ROWS // 2, 0, 0)) if ctx_stream else (lambda i: (i // tpb, 0, 0))
    in_specs = [
        pl.BlockSpec((tm, D_MODEL), lambda i: (i, 0)),
        pl.BlockSpec((1, 6, D_MODEL), mod_row),
        _resident((1, D_MODEL)),
        _resident_layer((D_MODEL, IN_W), li),
        _resident((8, GROUP_W)),
        _resident((GROUP_W, GROUP_W)),
        _resident((GROUP_W, GROUP_W)),
    ]
    args = [x2d, mod, g_row, w_bf, gains, g32, g64]
    if rope_tabs is not None:
        in_specs += [pl.BlockSpec((tm, GROUP_W), lambda i: (i % tpb, 0))] * 2
        args += list(rope_tabs)
    return pl.pallas_call(
        functools.partial(_inproj_kernel, rope=rope_tabs is not None),
        grid=(rows // tm,),
        in_specs=in_specs,
        out_specs=[pl.BlockSpec((tm, P_W), lambda i: (i, 0))]
        + [pl.BlockSpec((1, GROUP_W, tm), lambda i: (i // tpb, 0, i % tpb))] * 2,
        out_shape=[jax.ShapeDtypeStruct((rows, P_W), BF16)]
        + [jax.ShapeDtypeStruct((rows // n, GROUP_W, n), BF16)] * 2,
        compiler_params=_params(1),
        name="inproj_ctx" if ctx_stream else "inproj",
    )(*args)


def _fourier_kernel(p_ref, bdc_ref, bds_ref, cn_ref, sn_ref, rev_ref, o_ref):
    n = p_ref.shape[1]
    half = n // 2
    rb = rev_ref.shape[0]
    nb = n // rb
    mirrored = []
    for blk in range(half // rb):
        src = nb - blk - 1
        nxt = (src + 1) % nb
        pair = jnp.concatenate([p_ref[0, src * rb:(src + 1) * rb, :], p_ref[0, nxt * rb:(nxt + 1) * rb, :]], axis=0)
        mirrored.append(_dot(rev_ref[...], pair))
    mirrored = jnp.concatenate(mirrored, axis=0)
    x = p_ref[0, 0:half, :].astype(F32)
    first = lax.broadcasted_iota(jnp.int32, x.shape, 0) == 0
    fold_c = jnp.where(first, x, x + mirrored).astype(BF16)
    fold_s = (x - mirrored).astype(BF16)
    a = _dot(fold_c, bdc_ref[...]).astype(BF16)
    b = _dot(fold_s, bds_ref[...]).astype(BF16)
    mid = _dot(p_ref[0, half:half + 16, :], bdc_ref[...])[0:1]
    out = _dot(cn_ref[...], a) - _dot(sn_ref[...], b)
    odd = (lax.broadcasted_iota(jnp.int32, out.shape, 0) & 1) == 1
    o_ref[0] = (out + jnp.where(odd, -mid, mid)).astype(BF16)


def _fourier(p3, bdc, bds, cn, sn, rev, name):
    b, n, _ = p3.shape
    return pl.pallas_call(
        _fourier_kernel,
        grid=(b,),
        in_specs=[
            pl.BlockSpec((1, n, GROUP_W), lambda i: (i, 0, COL_F)),
            _resident((GROUP_W, GROUP_W)),
            _resident((GROUP_W, GROUP_W)),
            _resident((n, n // 2)),
            _resident((n, n // 2)),
            _resident(rev.shape),
        ],
        out_specs=pl.BlockSpec((1, n, GROUP_W), lambda i: (i, 0, 0)),
        out_shape=jax.ShapeDtypeStruct((b, n, GROUP_W), BF16),
        compiler_params=_params(1),
        name=name,
    )(p3, bdc, bds, cn, sn, rev)


def _diff_kernel(*refs, has_lat):
    if has_lat:
        lam_ref, q_ref, kl_ref, vl_ref, kc_ref, vc_ref, sub_ref, sel_ref, o_ref, k_ref, vt_ref, kmax_ref = refs
        key_refs, val_refs = [kl_ref, kc_ref], [vl_ref, vc_ref]
    else:
        lam_ref, q_ref, kc_ref, vc_ref, sub_ref, sel_ref, o_ref, k_ref, vt_ref, kmax_ref = refs
        key_refs, val_refs = [kc_ref], [vc_ref]

    def pair_sq_norms(x):
        xf = x.astype(F32)
        return _dot_nt(sel_ref[...], (xf * xf).astype(BF16))

    @pl.when(pl.program_id(1) == 0)
    def _():
        lo = 0
        for kr, vr in zip(key_refs, val_refs):
            cnt = kr.shape[1]
            k_ref[lo:lo + cnt, :] = kr[0]
            for h in range(N_HEADS):
                vt_ref[h, 0:HEAD_DIM, lo:lo + cnt] = vr[0, h * HEAD_DIM:(h + 1) * HEAD_DIM, :]
                vt_ref[h, HEAD_DIM:VT_ROWS, lo:lo + cnt] = jnp.ones((VT_ROWS - HEAD_DIM, cnt), BF16)
            lo += cnt
        kmax_ref[...] = jnp.broadcast_to(pair_sq_norms(k_ref[...]).max(axis=1, keepdims=True), kmax_ref.shape)

    q = q_ref[0]
    lam = lam_ref[0]
    lane = _lane()
    bound = jnp.sqrt(pair_sq_norms(q) * kmax_ref[:, 0:1])

    def attend(shift_fn):
        outs = []
        tq = q.shape[0]
        for h0 in range(0, N_HEADS, DIFF_HEADS_PER_DOT):
            pairs = range(2 * h0, 2 * (h0 + DIFF_HEADS_PER_DOT))
            qm = jnp.concatenate(
                [jnp.where((lane >= pair * DIFF_QK) & (lane < (pair + 1) * DIFF_QK), q, jnp.zeros_like(q))
                 for pair in pairs], axis=0)
            s = _dot_nt(k_ref[...], qm)
            e = jnp.exp2(s - shift_fn(s, pairs)).astype(BF16)
            for i in range(DIFF_HEADS_PER_DOT):
                o = _dot(vt_ref[h0 + i], e[:, 2 * i * tq:2 * (i + 1) * tq])
                outs += [o[:, 0:tq], o[:, tq:]]
        return outs

    def finish(outs):
        heads = []
        for h in range(N_HEADS):
            o1, o2 = outs[2 * h], outs[2 * h + 1]
            o = (o1[0:HEAD_DIM] * (1.0 / o1[HEAD_DIM:HEAD_DIM + 1])
                 - o2[0:HEAD_DIM] * (lam / o2[HEAD_DIM:HEAD_DIM + 1]))
            heads.append(o * lax.rsqrt(jnp.mean(o * o, axis=0, keepdims=True) + EPS))
        o_ref[0] = (jnp.concatenate(heads, axis=0) * sub_ref[...]).T.astype(BF16)

    outs = attend(lambda s, pairs: jnp.concatenate([bound[p:p + 1] for p in pairs], axis=1))
    finish(outs)
    smallest = outs[0][HEAD_DIM:HEAD_DIM + 1]
    for o in outs[1:]:
        smallest = jnp.minimum(smallest, o[HEAD_DIM:HEAD_DIM + 1])

    @pl.when(jnp.min(smallest) < SUM_FLOOR)
    def _():
        finish(attend(lambda s, pairs: s.max(axis=0, keepdims=True)))


def _diff(lam, p3, vt3, pc3, vtc3, sub_row, tq, has_lat):
    q_src = p3 if has_lat else pc3
    sub_t = jnp.broadcast_to(sub_row.reshape(GROUP_W, 1), (GROUP_W, tq))
    b, n, _ = q_src.shape
    nc = pc3.shape[1]
    smem = pl.BlockSpec(memory_space=pltpu.SMEM)
    in_specs = [smem, pl.BlockSpec((1, tq, GROUP_W), lambda i, j: (i, j, COL_DQ))]
    args = [lam, q_src]
    if has_lat:
        in_specs += [pl.BlockSpec((1, n, GROUP_W), lambda i, j: (i, 0, COL_DK)),
                     pl.BlockSpec((1, GROUP_W, n), lambda i, j: (i, 0, 0))]
        args += [p3, vt3]
    in_specs += [pl.BlockSpec((1, nc, GROUP_W), lambda i, j: (i, 0, COL_DK)),
                 pl.BlockSpec((1, GROUP_W, nc), lambda i, j: (i, 0, 0)),
                 _resident((GROUP_W, tq)),
                 _resident((PAIR_ROWS, GROUP_W))]
    pair_sel = np.arange(PAIR_ROWS)[:, None] == np.arange(GROUP_W)[None, :] // DIFF_QK
    args += [pc3, vtc3, sub_t, jnp.asarray(pair_sel, F32).astype(BF16)]
    return pl.pallas_call(
        functools.partial(_diff_kernel, has_lat=has_lat),
        grid=(b, n // tq),
        in_specs=in_specs,
        out_specs=pl.BlockSpec((1, tq, GROUP_W), lambda i, j: (i, j, 0)),
        out_shape=jax.ShapeDtypeStruct((b, n, GROUP_W), BF16),
        scratch_shapes=[pltpu.VMEM(((n if has_lat else 0) + nc, GROUP_W), BF16),
                        pltpu.VMEM((N_HEADS, VT_ROWS, (n if has_lat else 0) + nc), BF16),
                        pltpu.VMEM((PAIR_ROWS, 128), F32)],
        compiler_params=_params(2),
        name="diff" if has_lat else "diff_ctx",
    )(*args)


def _na_kernel(*refs, local):
    if local:
        q_ref, k0_ref, k1_ref, k2_ref, v0_ref, v1_ref, v2_ref, kc_ref, vc_ref, bias_ref, o_ref = refs
        key_refs = [k0_ref, k1_ref, k2_ref, kc_ref]
        vt_refs = [v0_ref, v1_ref, v2_ref, vc_ref]
    else:
        q_ref, kc_ref, vc_ref, o_ref = refs
        key_refs = [kc_ref]
        vt_refs = [vc_ref]
    n_elems, blk = q_ref.shape[0], q_ref.shape[1]
    lane = _lane()

    def scores(el):
        q = q_ref[el]
        q_heads = jnp.concatenate(
            [jnp.where((lane >= h * HEAD_DIM) & (lane < (h + 1) * HEAD_DIM), q, jnp.zeros_like(q))
             for h in range(N_HEADS)], axis=0)
        k_all = jnp.concatenate([r[el] for r in key_refs], axis=0)
        return _dot_nt(k_all, q_heads)

    def weights(s):
        if local:
            s = s + bias_ref[0]
        return jnp.exp2(s - s.max(axis=0, keepdims=True)).astype(BF16)

    def attend(el, e):
        vt_all = jnp.concatenate([r[el] for r in vt_refs], axis=1)
        ones = jnp.ones((VT_ROWS - HEAD_DIM, vt_all.shape[1]), BF16)
        heads = []
        for h in range(N_HEADS):
            vt_h = jnp.concatenate([vt_all[h * HEAD_DIM:(h + 1) * HEAD_DIM], ones], axis=0)
            o = _dot(vt_h, e[:, h * blk:(h + 1) * blk])
            heads.append(o[0:HEAD_DIM] * (1.0 / o[HEAD_DIM:HEAD_DIM + 1]))
        o_ref[el] = jnp.concatenate(heads, axis=0).T.astype(BF16)

    es = [weights(s) for s in [scores(el) for el in range(n_elems)]]
    for el in range(n_elems):
        attend(el, es[el])


def _na(p3, vt3, pc3, vtc3, bias):
    b, n, _ = p3.shape
    nc = pc3.shape[1]
    ne = NA_ELEMS
    blk = NA_Q_ROWS * GRID_W
    steps = n // blk
    last_win = n // blk - NA_WIN_ROWS // NA_Q_ROWS

    def first(g):
        return jnp.clip(g - 1, 0, last_win)

    def cls(g, i):
        return ((g > 0).astype(jnp.int32) + (g == steps - 1).astype(jnp.int32), 0, 0)

    in_specs = [pl.BlockSpec((ne, blk, GROUP_W), lambda g, i: (i, g, COL_NQ))]
    in_specs += [pl.BlockSpec((ne, blk, GROUP_W), lambda g, i, j=j: (i, first(g) + j, COL_NK)) for j in range(3)]
    in_specs += [pl.BlockSpec((ne, GROUP_W, blk), lambda g, i, j=j: (i, 0, first(g) + j)) for j in range(3)]
    in_specs += [pl.BlockSpec((ne, nc, GROUP_W), lambda g, i: (i, 0, COL_NK)),
                 pl.BlockSpec((ne, GROUP_W, nc), lambda g, i: (i, 0, 0)),
                 pl.BlockSpec((1, 3 * blk + nc, N_HEADS * blk), cls)]
    return pl.pallas_call(
        functools.partial(_na_kernel, local=True),
        grid=(steps, b // ne),
        in_specs=in_specs,
        out_specs=pl.BlockSpec((ne, blk, GROUP_W), lambda g, i: (i, g, 0)),
        out_shape=jax.ShapeDtypeStruct((b, n, GROUP_W), BF16),
        compiler_params=_params(2),
        name="na",
    )(p3, p3, p3, p3, vt3, vt3, vt3, pc3, vtc3, bias)


def _na_ctx(pc3, vtc3):
    b, nc, _ = pc3.shape
    ne = NA_ELEMS
    return pl.pallas_call(
        functools.partial(_na_kernel, local=False),
        grid=(b // ne,),
        in_specs=[pl.BlockSpec((ne, nc, GROUP_W), lambda i: (i, 0, COL_NQ)),
                  pl.BlockSpec((ne, nc, GROUP_W), lambda i: (i, 0, COL_NK)),
                  pl.BlockSpec((ne, GROUP_W, nc), lambda i: (i, 0, 0))],
        out_specs=pl.BlockSpec((ne, nc, GROUP_W), lambda i: (i, 0, 0)),
        out_shape=jax.ShapeDtypeStruct((b, nc, GROUP_W), BF16),
        compiler_params=_params(1),
        name="na_ctx",
    )(pc3, pc3, vtc3)


def _na_row_offsets():
    first_row = NA_WIN_ROWS - NA_KH
    rules = [lambda i, a: a - i + NA_KH - 1 if a < NA_KH else None,
             lambda i, a: a - i + NA_KH // 2 - 1 if i <= a < i + NA_KH else None,
             lambda i, a: a - i - 1 if a >= first_row else None]
    return [[[rule(i, a) for a in range(NA_WIN_ROWS)] for i in range(NA_Q_ROWS)] for rule in rules]


def _na_bias_kernel(band_ref, o_ref):
    masked = jnp.full((GRID_W, GRID_W), NEG, F32)
    for kind, per_query_row in enumerate(_na_row_offsets()):
        for a in range(NA_WIN_ROWS):
            blocks = [masked if offs[a] is None else band_ref[0, offs[a]] for offs in per_query_row]
            o_ref[kind, a * GRID_W:(a + 1) * GRID_W, :] = jnp.concatenate(blocks, axis=1)
        o_ref[kind, NA_WIN_ROWS * GRID_W:, :] = jnp.zeros((o_ref.shape[1] - NA_WIN_ROWS * GRID_W, o_ref.shape[2]), F32)


def _na_bias_tables(rpb, nc):
    kc = np.arange(GRID_W)[:, None]
    j = np.arange(GRID_W)[None, :]
    c0 = np.clip(j - NA_KW // 2, 0, GRID_W - NA_KW)
    col_ok = (kc >= c0) & (kc < c0 + NA_KW)
    onehot = (np.arange(2 * NA_KW - 1)[:, None, None] == (kc - j + NA_KW - 1)[None]) & col_ok[None]
    band = jnp.einsum("hdo,okj->hdkj", rpb.astype(F32), jnp.asarray(onehot, F32), precision=lax.Precision.HIGHEST)
    band = jnp.where(col_ok, band * LOG2E, NEG)
    n_off = 2 * NA_KH - 1
    q, k = NA_Q_ROWS * GRID_W, NA_WIN_ROWS * GRID_W
    return pl.pallas_call(
        _na_bias_kernel,
        grid=(N_HEADS,),
        in_specs=[pl.BlockSpec((1, n_off, GRID_W, GRID_W), lambda h: (h, 0, 0, 0))],
        out_specs=pl.BlockSpec((3, k + nc, q), lambda h: (0, 0, h)),
        out_shape=jax.ShapeDtypeStruct((3, k + nc, N_HEADS * q), F32),
        compiler_params=_params(1),
        name="na_bias",
    )(band)


def _gmlp_kernel(u_ref, v_ref, ws_ref, bias_ref, o_ref, *, chunks_per_step):
    n = u_ref.shape[1]
    lane = _lane()
    step_rows = chunks_per_step * CHUNK
    for s in range(n // step_rows):
        base = s * step_rows
        vcat = jnp.concatenate(
            [v_ref[0, base + c * CHUNK: base + (c + 1) * CHUNK, :] for c in range(chunks_per_step)], axis=1)
        r = _dot(ws_ref[...], vcat)
        for c in range(chunks_per_step):
            mix = r[0:CHUNK, c * GROUP_W:(c + 1) * GROUP_W]
            for g in range(1, N_HEADS):
                mix = jnp.where(lane >= g * HEAD_DIM, r[g * CHUNK:(g + 1) * CHUNK, c * GROUP_W:(c + 1) * GROUP_W], mix)
            rows = slice(base + c * CHUNK, base + (c + 1) * CHUNK)
            o_ref[0, rows, :] = (u_ref[0, rows, :].astype(F32) * (mix + bias_ref[...])).astype(BF16)


def _gmlp(p3, ws_stack, bias_full, name):
    b, n, _ = p3.shape
    cps = min(4, n // CHUNK)
    return pl.pallas_call(
        functools.partial(_gmlp_kernel, chunks_per_step=cps),
        grid=(b,),
        in_specs=[pl.BlockSpec((1, n, GROUP_W), lambda i: (i, 0, COL_GU)),
                  pl.BlockSpec((1, n, GROUP_W), lambda i: (i, 0, COL_GV)),
                  _resident((N_HEADS * CHUNK, CHUNK)),
                  _resident((CHUNK, GROUP_W))],
        out_specs=pl.BlockSpec((1, n, GROUP_W), lambda i: (i, 0, 0)),
        out_shape=jax.ShapeDtypeStruct((b, n, GROUP_W), BF16),
        compiler_params=_params(1),
        name=name,
    )(p3, p3, ws_stack, bias_full)


def _mix_ffn_kernel(*refs, tiles_per_seq):
    y_refs, (xp_ref, xm_ref, xn_ref) = refs[:12], refs[12:15]
    mod_ref, g_ref, wo_ref, wu_ref, cw_ref, cb_ref, wd_ref, o_ref, xs_ref, act_ref = refs[15:]
    tm = xm_ref.shape[0]
    pos = pl.program_id(0) % tiles_per_seq

    def with_halo(p_ref, m_ref, n_ref):
        return jnp.concatenate([p_ref[...], m_ref[...], n_ref[...]], axis=0)

    y = jnp.concatenate([with_halo(*y_refs[3 * j:3 * j + 3]) for j in range(4)], axis=1)
    x = with_halo(xp_ref, xm_ref, xn_ref) + mod_ref[0, 2:3, :] * _dot(y, wo_ref[...])
    xs_ref[...] = x[HALO:HALO + tm]
    ms = jnp.mean(x * x, axis=-1, keepdims=True)
    h = x * lax.rsqrt(ms + EPS) * g_ref[...]
    h = (h * (1.0 + mod_ref[0, 4:5, :]) + mod_ref[0, 3:4, :]).astype(BF16)
    hp, hn = h[0:HALO], h[HALO + tm:]
    hp = jnp.where(pos != 0, hp, jnp.zeros_like(hp))
    hn = jnp.where(pos != tiles_per_seq - 1, hn, jnp.zeros_like(hn))
    hcat = jnp.concatenate([hp, h[HALO:HALO + tm], hn], axis=0)

    def conv(z, lo):
        rows = z.shape[0]
        cols = slice(lo, lo + FF_CHUNK)
        return (pltpu.roll(z, 1, 0)[HALO:HALO + tm] * cw_ref[0:1, cols]
                + z[HALO:HALO + tm] * cw_ref[1:2, cols]
                + pltpu.roll(z, rows - 1, 0)[HALO:HALO + tm] * cw_ref[2:3, cols] + cb_ref[:, cols])

    z_all = _dot(hcat, wu_ref[...])
    for c in range(D_FF // FF_CHUNK):
        lo_g = c * FF_CHUNK
        lo_v = D_FF + c * FF_CHUNK
        g = conv(z_all[:, lo_g:lo_g + FF_CHUNK], lo_g)
        v = conv(z_all[:, lo_v:lo_v + FF_CHUNK], lo_v)
        act_ref[:, lo_g:lo_g + FF_CHUNK] = (g / (1.0 + jnp.exp(-g)) * v).astype(BF16)
    o_ref[...] = xs_ref[...] + mod_ref[0, 5:6, :] * _dot(act_ref[...], wd_ref[...])


def _mix_ffn(ys, x2d, mod, g_row, wo_bf, wu_bf, conv_w, conv_b, wd_bf, li, n, tm, ctx_stream):
    rows = x2d.shape[0]
    tpb = n // tm
    per_tile = tm // HALO
    last_halo = rows // HALO - 1
    mod_row = (lambda i: (ADA_ROWS // 2, 0, 0)) if ctx_stream else (lambda i: (i // tpb, 0, 0))

    def halo_specs(width):
        return [pl.BlockSpec((HALO, width), lambda i: (jnp.maximum(i * per_tile - 1, 0), 0)),
                pl.BlockSpec((tm, width), lambda i: (i, 0)),
                pl.BlockSpec((HALO, width), lambda i: (jnp.minimum((i + 1) * per_tile, last_halo), 0))]

    y2d = [y.reshape(rows, GROUP_W) for y in ys]
    return pl.pallas_call(
        functools.partial(_mix_ffn_kernel, tiles_per_seq=tpb),
        grid=(rows // tm,),
        in_specs=halo_specs(GROUP_W) * 4 + halo_specs(D_MODEL) + [
            pl.BlockSpec((1, 6, D_MODEL), mod_row),
            _resident((1, D_MODEL)),
            _resident_layer((D_MODEL, D_MODEL), li),
            _resident_layer((D_MODEL, 2 * D_FF), li),
            _resident((3, 2 * D_FF)),
            _resident((1, 2 * D_FF)),
            _resident_layer((D_FF, D_MODEL), li),
        ],
        out_specs=pl.BlockSpec((tm, D_MODEL), lambda i: (i, 0)),
        out_shape=jax.ShapeDtypeStruct((rows, D_MODEL), F32),
        scratch_shapes=[pltpu.VMEM((tm, D_MODEL), F32), pltpu.VMEM((tm, D_FF), BF16)],
        compiler_params=_params(1),
        name="mix_ffn_ctx" if ctx_stream else "mix_ffn",
    )(*[a for y in y2d for a in (y, y, y)], x2d, x2d, x2d, mod, g_row, wo_bf, wu_bf, conv_w,
      conv_b.reshape(1, 2 * D_FF), wd_bf)


def _dft_tables(n):
    lo_n = 64
    half = n // 2
    t = jnp.arange(half, dtype=jnp.int32)

    def table(k):
        ang = ((k[:, None] * t[None, :]) % n).astype(F32) * (2.0 * math.pi / n)
        return jnp.cos(ang), jnp.sin(ang)

    (ch, sh), (cl, sl) = table(jnp.arange(n // lo_n, dtype=jnp.int32) * lo_n), table(jnp.arange(lo_n, dtype=jnp.int32))
    cn = (ch[:, None] * cl[None] - sh[:, None] * sl[None]).reshape(n, half).astype(BF16)
    sn = (sh[:, None] * cl[None] + ch[:, None] * sl[None]).reshape(n, half).astype(BF16)
    rb = min(GROUP_W, half)
    r = np.arange(rb)
    rev = np.zeros((rb, 2 * rb), np.float32)
    rev[r[1:], rb - r[1:]] = 1.0
    rev[0, rb] = 1.0
    c = np.arange(HEAD_DIM)
    angc = 2.0 * np.pi * ((c[:, None] * c[None, :]) % HEAD_DIM) / HEAD_DIM
    norm = 1.0 / math.sqrt(n * HEAD_DIM)
    eye = np.eye(N_HEADS)
    bdc = jnp.asarray(np.kron(eye, np.cos(angc) * norm), F32).astype(BF16)
    bds = jnp.asarray(np.kron(eye, np.sin(angc) * norm), F32).astype(BF16)
    return bdc, bds, cn, sn, jnp.asarray(rev, F32).astype(BF16)


def _rope_tables(n):
    n_freq = DIFF_QK // 4
    freqs = ROPE_BASE ** (-jnp.arange(n_freq, dtype=F32) / n_freq)
    t = jnp.arange(n)
    row = (t // GRID_W).astype(F32)
    col = (t % GRID_W).astype(F32)
    ang = jnp.concatenate([row[:, None] * freqs, col[:, None] * freqs], axis=-1)
    cos, sin = jnp.cos(ang), jnp.sin(ang)
    reps = GROUP_W // DIFF_QK
    return jnp.tile(jnp.concatenate([cos, cos], axis=-1), (1, reps)), jnp.tile(jnp.concatenate([-sin, sin], axis=-1), (1, reps))


def _group_mean_matrix(group):
    return jnp.asarray(np.kron(np.eye(GROUP_W // group), np.full((group, group), 1.0 / group)), F32).astype(BF16)


def kernel(x, c, ctx, c_ctx, w_ada, b_ada, g_mix, g_ffn, w_in, w_out, diff_qn, diff_kn, diff_lam, diff_subln, na_qn,
           na_kn, na_rpb, gmlp_norm, gmlp_ws, gmlp_b, ffn_up, ffn_conv, ffn_conv_b, ffn_down):
    b, n, d = x.shape
    nc = ctx.shape[1]
    assert (d, n % (NA_Q_ROWS * GRID_W), nc % CHUNK, b < ADA_ROWS // 2 + 1) == (D_MODEL, 0, 0, True)
    tm = 512
    tq = 512

    c_all = jnp.zeros((ADA_ROWS, d), F32).at[:b].set(c).at[ADA_ROWS // 2].set(c_ctx)
    mods = _ada(c_all, w_ada, b_ada).reshape(DEPTH, ADA_ROWS, 6, d)

    g32 = _group_mean_matrix(DIFF_QK)
    g64 = _group_mean_matrix(HEAD_DIM)
    rope_tabs = _rope_tables(n)
    dft_lat = _dft_tables(n)
    dft_ctx = _dft_tables(nc)

    w_in_kernel_order = jnp.concatenate([w_in[..., r * GROUP_W:(r + 1) * GROUP_W] for r in W_ORDER], axis=-1)
    w_in_bf, w_out_bf, wu_bf, wd_bf = (w.astype(BF16) for w in (w_in_kernel_order, w_out, ffn_up, ffn_down))

    x2d = x.reshape(b * n, d)
    c2d = ctx.reshape(b * nc, d)
    for li in range(DEPTH):
        ctx_out = li < DEPTH - 1
        lam_init = 0.8 - 0.6 * math.exp(-0.3 * li)
        lf = diff_lam[li].astype(F32)
        lam = (jnp.exp(jnp.sum(lf[0] * lf[1])) - jnp.exp(jnp.sum(lf[2] * lf[3])) + lam_init).reshape(1)
        mod = mods[li]
        gains = jnp.zeros((8, GROUP_W), F32)
        gains = gains.at[0].set(jnp.tile(diff_qn[li], GROUP_W // DIFF_QK) * (DIFF_QK ** -0.5 * LOG2E))
        gains = gains.at[1].set(jnp.tile(diff_kn[li], GROUP_W // DIFF_QK))
        gains = gains.at[2].set(jnp.tile(na_qn[li], N_HEADS) * (HEAD_DIM ** -0.5 * LOG2E))
        gains = gains.at[3].set(jnp.tile(na_kn[li], N_HEADS))
        gains = gains.at[4].set(gmlp_norm[li])
        sub_row = (jnp.tile(diff_subln[li], N_HEADS) * (1.0 - lam_init)).reshape(1, GROUP_W)
        g_mix_row = g_mix[li].reshape(1, d)
        g_ffn_row = g_ffn[li].reshape(1, d)
        ws_stack = gmlp_ws[li].reshape(N_HEADS * CHUNK, CHUNK).astype(BF16)
        gbias = jnp.repeat(gmlp_b[li].T, HEAD_DIM, axis=1)
        bias_tab = _na_bias_tables(na_rpb[li], nc)

        p2d, dvt, nvt = _inproj(x2d, mod, g_mix_row, w_in_bf, li, gains, g32, g64, rope_tabs, n, 2 * tm, False)
        pc2d, dvtc, nvtc = _inproj(c2d, mod, g_mix_row, w_in_bf, li, gains, g32, g64, None, nc, nc, True)
        p3 = p2d.reshape(b, n, P_W)
        pc3 = pc2d.reshape(b, nc, P_W)

        ys = [_fourier(p3, *dft_lat, "fourier"),
              _diff(lam, p3, dvt, pc3, dvtc, sub_row, tq, True),
              _na(p3, nvt, pc3, nvtc, bias_tab),
              _gmlp(p3, ws_stack, gbias, "gmlp")]
        ffn_w = (g_ffn_row, w_out_bf, wu_bf, ffn_conv[li], ffn_conv_b[li], wd_bf, li)
        x2d = _mix_ffn(ys, x2d, mod, *ffn_w, n, tm, False)

        if ctx_out:
            ycs = [_fourier(pc3, *dft_ctx, "fourier_ctx"),
                   _diff(lam, None, None, pc3, dvtc, sub_row, nc, False),
                   _na_ctx(pc3, nvtc),
                   _gmlp(pc3, ws_stack, gbias, "gmlp_ctx")]
            c2d = _mix_ffn(ycs, c2d, mod, *ffn_w, nc, nc, True)
    return x2d.reshape(b, n, d)
```

```python
import functools
import math

import jax
import jax.numpy as jnp
import numpy as np
from jax import lax
from jax.experimental import pallas as pl
from jax.experimental.pallas import tpu as pltpu

F32 = jnp.float32
BF16 = jnp.bfloat16

D_MODEL = 1024
DEPTH = 4
GRID_W = 64
GROUP_W = 256
HEAD_DIM = 64
N_HEADS = 4
DIFF_QK = 32
NA_KH = 8
NA_KW = 16
CHUNK = 128
D_FF = 2816
ROPE_BASE = 10000.0
EPS = 1e-6
IN_W = 9 * GROUP_W

REF_F, REF_DQ, REF_DK, REF_DV, REF_NQ, REF_NK, REF_NV, REF_GU, REF_GV = range(9)
W_ORDER = (REF_GV, REF_DQ, REF_DK, REF_NQ, REF_NK, REF_GU, REF_DV, REF_NV, REF_F)
W_GV, W_DQ, W_DK, W_NQ, W_NK, W_GU, W_DV, W_NV, W_F = range(9)
COL_F, COL_DQ, COL_DK, COL_NQ, COL_NK, COL_GU, COL_GV = range(7)
P_W = 7 * GROUP_W

VT_ROWS = HEAD_DIM + 16
DIFF_HEADS_PER_DOT = 2
PAIR_ROWS = 16
SUM_FLOOR = 2.0 ** -80
NA_Q_ROWS = 4
NA_ELEMS = 4
NA_WIN_ROWS = 12
NEG = -1e30
LOG2E = math.log2(math.e)

ADA_ROWS = 16
VMEM_LIMIT = 56 * 1024 * 1024
LANES = 128
HALO = 16
FF_CHUNK = 256


def _dot(a, b):
    return jnp.dot(a, b, preferred_element_type=F32)


def _dot_nt(a, b):
    return lax.dot_general(a, b, (((1,), (1,)), ((), ())), preferred_element_type=F32)


def _resident(shape):
    nd = len(shape)
    return pl.BlockSpec(shape, lambda *_: (0,) * nd, pipeline_mode=pl.Buffered(1))


def _resident_layer(shape, li):
    nd = len(shape)
    return pl.BlockSpec((None,) + tuple(shape), lambda *_: (li,) + (0,) * nd, pipeline_mode=pl.Buffered(1))


def _params(n_axes):
    return pltpu.CompilerParams(dimension_semantics=("arbitrary",) * n_axes, vmem_limit_bytes=VMEM_LIMIT)


def _lane(width=GROUP_W):
    return lax.broadcasted_iota(jnp.int32, (1, width), 1)


def _ada_kernel(c_ref, w_ref, b_ref, o_ref):
    c = c_ref[...]
    s = c / (1.0 + jnp.exp(-c))
    w = w_ref[0]
    s_hi = s.astype(BF16)
    s_lo = (s - s_hi.astype(F32)).astype(BF16)
    w_hi = w.astype(BF16)
    w_lo = (w - w_hi.astype(F32)).astype(BF16)
    o_ref[0] = _dot(s_hi, w_hi) + _dot(s_lo, w_hi) + _dot(s_hi, w_lo) + b_ref[0]


def _ada(c_all, w_ada, b_ada):
    tn = 1536
    out_w = 6 * D_MODEL
    return pl.pallas_call(
        _ada_kernel,
        grid=(DEPTH, out_w // tn),
        in_specs=[
            pl.BlockSpec((ADA_ROWS, D_MODEL), lambda l, j: (0, 0)),
            pl.BlockSpec((1, D_MODEL, tn), lambda l, j: (l, 0, j)),
            pl.BlockSpec((1, 1, tn), lambda l, j: (l, 0, j)),
        ],
        out_specs=pl.BlockSpec((1, ADA_ROWS, tn), lambda l, j: (l, 0, j)),
        out_shape=jax.ShapeDtypeStruct((DEPTH, ADA_ROWS, out_w), F32),
        compiler_params=_params(2),
        name="ada",
    )(c_all, w_ada, b_ada.reshape(DEPTH, 1, out_w))


def _gelu(x):
    return 0.5 * x * (1.0 + jnp.tanh(math.sqrt(2.0 / math.pi) * (x + 0.044715 * (x * x * x))))


def _inproj_kernel(*refs, rope):
    if rope:
        x_ref, mod_ref, g_ref, w_ref, gains_ref, g32_ref, g64_ref, cos_ref, sin_ref, o_ref, dvt_ref, nvt_ref = refs
    else:
        x_ref, mod_ref, g_ref, w_ref, gains_ref, g32_ref, g64_ref, o_ref, dvt_ref, nvt_ref = refs
    x = x_ref[...]
    ms = jnp.mean(x * x, axis=-1, keepdims=True)
    h = x * lax.rsqrt(ms + EPS) * g_ref[...]
    h = h * (1.0 + mod_ref[0, 1:2, :]) + mod_ref[0, 0:1, :]
    p = _dot(h.astype(BF16), w_ref[...])

    def col(j):
        return p[:, j * GROUP_W:(j + 1) * GROUP_W]

    def put(j, v):
        o_ref[:, j * GROUP_W:(j + 1) * GROUP_W] = v.astype(BF16)

    def group_norm(v, gmat_ref, gain_row):
        gms = _dot((v * v).astype(BF16), gmat_ref[...])
        return v * lax.rsqrt(gms + EPS) * gains_ref[gain_row:gain_row + 1, :]

    def rotary(v):
        if not rope:
            return v
        first_half = (_lane() % DIFF_QK) < (DIFF_QK // 2)
        partner = jnp.where(first_half, pltpu.roll(v, GROUP_W - DIFF_QK // 2, 1), pltpu.roll(v, DIFF_QK // 2, 1))
        return v * cos_ref[...] + partner * sin_ref[...]

    put(COL_GV, group_norm(_gelu(col(W_GV)), g64_ref, 4))
    put(COL_DQ, rotary(group_norm(col(W_DQ), g32_ref, 0)))
    put(COL_DK, rotary(group_norm(col(W_DK), g32_ref, 1)))
    put(COL_NQ, group_norm(col(W_NQ), g64_ref, 2))
    put(COL_NK, group_norm(col(W_NK), g64_ref, 3))
    put(COL_GU, _gelu(col(W_GU)))
    dvt_ref[0] = col(W_DV).T.astype(BF16)
    nvt_ref[0] = col(W_NV).T.astype(BF16)
    put(COL_F, col(W_F))


def _inproj(x2d, mod, g_row, w_bf, li, gains, g32, g64, rope_tabs, n, tm, ctx_stream):
    rows = x2d.shape[0]
    tpb = n // tm
    mod_row = (lambda i: (ADA_ROWS // 2, 0, 0)) if ctx_stream else (lambda i: (i // tpb, 0, 0))
    in_specs = [
        pl.BlockSpec((tm, D_MODEL), lambda i: (i, 0)),
        pl.BlockSpec((1, 6, D_MODEL), mod_row),
        _resident((1, D_MODEL)),
        _resident_layer((D_MODEL, IN_W), li),
        _resident((8, GROUP_W)),
        _resident((GROUP_W, GROUP_W)),
        _resident((GROUP_W, GROUP_W)),
    ]
    args = [x2d, mod, g_row, w_bf, gains, g32, g64]
    if rope_tabs is not None:
        in_specs += [pl.BlockSpec((tm, GROUP_W), lambda i: (i % tpb, 0))] * 2
        args += list(rope_tabs)
    return pl.pallas_call(
        functools.partial(_inproj_kernel, rope=rope_tabs is not None),
        grid=(rows // tm,),
        in_specs=in_specs,
        out_specs=[pl.BlockSpec((tm, P_W), lambda i: (i, 0))]
        + [pl.BlockSpec((1, GROUP_W, tm), lambda i: (i // tpb, 0, i % tpb))] * 2,
        out_shape=[jax.ShapeDtypeStruct((rows, P_W), BF16)]
        + [jax.ShapeDtypeStruct((rows // n, GROUP_W, n), BF16)] * 2,
        compiler_params=_params(1),
        name="inproj_ctx" if ctx_stream else "inproj",
    )(*args)


def _fourier_kernel(p_ref, bdc_ref, bds_ref, cn_ref, sn_ref, rev_ref, o_ref):
    n = p_ref.shape[1]
    half = n // 2
    rb = rev_ref.shape[0]
    nb = n // rb
    mirrored = []
    for blk in range(half // rb):
        src = nb - blk - 1
        nxt = (src + 1) % nb
        pair = jnp.concatenate([p_ref[0, src * rb:(src + 1) * rb, :], p_ref[0, nxt * rb:(nxt + 1) * rb, :]], axis=0)
        mirrored.append(_dot(rev_ref[...], pair))
    mirrored = jnp.concatenate(mirrored, axis=0)
    x = p_ref[0, 0:half, :].astype(F32)
    first = lax.broadcasted_iota(jnp.int32, x.shape, 0) == 0
    fold_c = jnp.where(first, x, x + mirrored).astype(BF16)
    fold_s = (x - mirrored).astype(BF16)
    a = _dot(fold_c, bdc_ref[...]).astype(BF16)
    b = _dot(fold_s, bds_ref[...]).astype(BF16)
    mid = _dot(p_ref[0, half:half + 16, :], bdc_ref[...])[0:1]
    out = _dot(cn_ref[...], a) - _dot(sn_ref[...], b)
    odd = (lax.broadcasted_iota(jnp.int32, out.shape, 0) & 1) == 1
    o_ref[0] = (out + jnp.where(odd, -mid, mid)).astype(BF16)


def _fourier(p3, bdc, bds, cn, sn, rev, name):
    b, n, _ = p3.shape
    return pl.pallas_call(
        _fourier_kernel,
        grid=(b,),
        in_specs=[
            pl.BlockSpec((1, n, GROUP_W), lambda i: (i, 0, COL_F)),
            _resident((GROUP_W, GROUP_W)),
            _resident((GROUP_W, GROUP_W)),
            _resident((n, n // 2)),
            _resident((n, n // 2)),
            _resident(rev.shape),
        ],
        out_specs=pl.BlockSpec((1, n, GROUP_W), lambda i: (i, 0, 0)),
        out_shape=jax.ShapeDtypeStruct((b, n, GROUP_W), BF16),
        compiler_params=_params(1),
        name=name,
    )(p3, bdc, bds, cn, sn, rev)


def _diff_kernel(*refs, has_lat):
    if has_lat:
        lam_ref, q_ref, kl_ref, vl_ref, kc_ref, vc_ref, sub_ref, sel_ref, o_ref, k_ref, vt_ref, kmax_ref = refs
        key_refs, val_refs = [kl_ref, kc_ref], [vl_ref, vc_ref]
    else:
        lam_ref, q_ref, kc_ref, vc_ref, sub_ref, sel_ref, o_ref, k_ref, vt_ref, kmax_ref = refs
        key_refs, val_refs = [kc_ref], [vc_ref]

    def pair_sq_norms(x):
        xf = x.astype(F32)
        return _dot_nt(sel_ref[...], (xf * xf).astype(BF16))

    @pl.when(pl.program_id(1) == 0)
    def _():
        lo = 0
        for kr, vr in zip(key_refs, val_refs):
            cnt = kr.shape[1]
            k_ref[lo:lo + cnt, :] = kr[0]
            for h in range(N_HEADS):
                vt_ref[h, 0:HEAD_DIM, lo:lo + cnt] = vr[0, h * HEAD_DIM:(h + 1) * HEAD_DIM, :]
                vt_ref[h, HEAD_DIM:VT_ROWS, lo:lo + cnt] = jnp.ones((VT_ROWS - HEAD_DIM, cnt), BF16)
            lo += cnt
        kmax_ref[...] = jnp.broadcast_to(pair_sq_norms(k_ref[...]).max(axis=1, keepdims=True), kmax_ref.shape)

    q = q_ref[0]
    lam = lam_ref[0]
    lane = _lane()
    bound = jnp.sqrt(pair_sq_norms(q) * kmax_ref[:, 0:1])

    def attend(shift_fn):
        outs = []
        tq = q.shape[0]
        for h0 in range(0, N_HEADS, DIFF_HEADS_PER_DOT):
            pairs = range(2 * h0, 2 * (h0 + DIFF_HEADS_PER_DOT))
            qm = jnp.concatenate(
                [jnp.where((lane >= pair * DIFF_QK) & (lane < (pair + 1) * DIFF_QK), q, jnp.zeros_like(q))
                 for pair in pairs], axis=0)
            s = _dot_nt(k_ref[...], qm)
            e = jnp.exp2(s - shift_fn(s, pairs)).astype(BF16)
            for i in range(DIFF_HEADS_PER_DOT):
                o = _dot(vt_ref[h0 + i], e[:, 2 * i * tq:2 * (i + 1) * tq])
                outs += [o[:, 0:tq], o[:, tq:]]
        return outs

    def finish(outs):
        heads = []
        for h in range(N_HEADS):
            o1, o2 = outs[2 * h], outs[2 * h + 1]
            o = (o1[0:HEAD_DIM] * (1.0 / o1[HEAD_DIM:HEAD_DIM + 1])
                 - o2[0:HEAD_DIM] * (lam / o2[HEAD_DIM:HEAD_DIM + 1]))
            heads.append(o * lax.rsqrt(jnp.mean(o * o, axis=0, keepdims=True) + EPS))
        o_ref[0] = (jnp.concatenate(heads, axis=0) * sub_ref[...]).T.astype(BF16)

    outs = attend(lambda s, pairs: jnp.concatenate([bound[p:p + 1] for p in pairs], axis=1))
    finish(outs)
    smallest = outs[0][HEAD_DIM:HEAD_DIM + 1]
    for o in outs[1:]:
        smallest = jnp.minimum(smallest, o[HEAD_DIM:HEAD_DIM + 1])

    @pl.when(jnp.min(smallest) < SUM_FLOOR)
    def _():
        finish(attend(lambda s, pairs: s.max(axis=0, keepdims=True)))


def _diff(lam, p3, vt3, pc3, vtc3, sub_row, tq, has_lat):
    q_src = p3 if has_lat else pc3
    sub_t = jnp.broadcast_to(sub_row.reshape(GROUP_W, 1), (GROUP_W, tq))
    b, n, _ = q_src.shape
    nc = pc3.shape[1]
    smem = pl.BlockSpec(memory_space=pltpu.SMEM)
    in_specs = [smem, pl.BlockSpec((1, tq, GROUP_W), lambda i, j: (i, j, COL_DQ))]
    args = [lam, q_src]
    if has_lat:
        in_specs += [pl.BlockSpec((1, n, GROUP_W), lambda i, j: (i, 0, COL_DK)),
                     pl.BlockSpec((1, GROUP_W, n), lambda i, j: (i, 0, 0))]
        args += [p3, vt3]
    in_specs += [pl.BlockSpec((1, nc, GROUP_W), lambda i, j: (i, 0, COL_DK)),
                 pl.BlockSpec((1, GROUP_W, nc), lambda i, j: (i, 0, 0)),
                 _resident((GROUP_W, tq)),
                 _resident((PAIR_ROWS, GROUP_W))]
    pair_sel = np.arange(PAIR_ROWS)[:, None] == np.arange(GROUP_W)[None, :] // DIFF_QK
    args += [pc3, vtc3, sub_t, jnp.asarray(pair_sel, F32).astype(BF16)]
    return pl.pallas_call(
        functools.partial(_diff_kernel, has_lat=has_lat),
        grid=(b, n // tq),
        in_specs=in_specs,
        out_specs=pl.BlockSpec((1, tq, GROUP_W), lambda i, j: (i, j, 0)),
        out_shape=jax.ShapeDtypeStruct((b, n, GROUP_W), BF16),
        scratch_shapes=[pltpu.VMEM(((n if has_lat else 0) + nc, GROUP_W), BF16),
                        pltpu.VMEM((N_HEADS, VT_ROWS, (n if has_lat else 0) + nc), BF16),
                        pltpu.VMEM((PAIR_ROWS, 128), F32)],
        compiler_params=_params(2),
        name="diff" if has_lat else "diff_ctx",
    )(*args)


def _na_kernel(*refs, local):
    if local:
        q_ref, k0_ref, k1_ref, k2_ref, v0_ref, v1_ref, v2_ref, kc_ref, vc_ref, bias_ref, o_ref = refs
        key_refs = [k0_ref, k1_ref, k2_ref, kc_ref]
        vt_refs = [v0_ref, v1_ref, v2_ref, vc_ref]
    else:
        q_ref, kc_ref, vc_ref, o_ref = refs
        key_refs = [kc_ref]
        vt_refs = [vc_ref]
    n_elems, blk = q_ref.shape[0], q_ref.shape[1]
    lane = _lane()

    def scores(el):
        q = q_ref[el]
        q_heads = jnp.concatenate(
            [jnp.where((lane >= h * HEAD_DIM) & (lane < (h + 1) * HEAD_DIM), q, jnp.zeros_like(q))
             for h in range(N_HEADS)], axis=0)
        k_all = jnp.concatenate([r[el] for r in key_refs], axis=0)
        return _dot_nt(k_all, q_heads)

    def weights(s):
        if local:
            s = s + bias_ref[0]
        return jnp.exp2(s - s.max(axis=0, keepdims=True)).astype(BF16)

    def attend(el, e):
        vt_all = jnp.concatenate([r[el] for r in vt_refs], axis=1)
        ones = jnp.ones((VT_ROWS - HEAD_DIM, vt_all.shape[1]), BF16)
        heads = []
        for h in range(N_HEADS):
            vt_h = jnp.concatenate([vt_all[h * HEAD_DIM:(h + 1) * HEAD_DIM], ones], axis=0)
            o = _dot(vt_h, e[:, h * blk:(h + 1) * blk])
            heads.append(o[0:HEAD_DIM] * (1.0 / o[HEAD_DIM:HEAD_DIM + 1]))
        o_ref[el] = jnp.concatenate(heads, axis=0).T.astype(BF16)

    es = [weights(s) for s in [scores(el) for el in range(n_elems)]]
    for el in range(n_elems):
        attend(el, es[el])


def _na(p3, vt3, pc3, vtc3, bias):
    b, n, _ = p3.shape
    nc = pc3.shape[1]
    ne = NA_ELEMS
    blk = NA_Q_ROWS * GRID_W
    steps = n // blk
    last_win = n // blk - NA_WIN_ROWS // NA_Q_ROWS

    def first(g):
        return jnp.clip(g - 1, 0, last_win)

    def cls(g, i):
        return ((g > 0).astype(jnp.int32) + (g == steps - 1).astype(jnp.int32), 0, 0)

    in_specs = [pl.BlockSpec((ne, blk, GROUP_W), lambda g, i: (i, g, COL_NQ))]
    in_specs += [pl.BlockSpec((ne, blk, GROUP_W), lambda g, i, j=j: (i, first(g) + j, COL_NK)) for j in range(3)]
    in_specs += [pl.BlockSpec((ne, GROUP_W, blk), lambda g, i, j=j: (i, 0, first(g) + j)) for j in range(3)]
    in_specs += [pl.BlockSpec((ne, nc, GROUP_W), lambda g, i: (i, 0, COL_NK)),
                 pl.BlockSpec((ne, GROUP_W, nc), lambda g, i: (i, 0, 0)),
                 pl.BlockSpec((1, 3 * blk + nc, N_HEADS * blk), cls)]
    return pl.pallas_call(
        functools.partial(_na_kernel, local=True),
        grid=(steps, b // ne),
        in_specs=in_specs,
        out_specs=pl.BlockSpec((ne, blk, GROUP_W), lambda g, i: (i, g, 0)),
        out_shape=jax.ShapeDtypeStruct((b, n, GROUP_W), BF16),
        compiler_params=_params(2),
        name="na",
    )(p3, p3, p3, p3, vt3, vt3, vt3, pc3, vtc3, bias)


def _na_ctx(pc3, vtc3):
    b, nc, _ = pc3.shape
    ne = NA_ELEMS
    return pl.pallas_call(
        functools.partial(_na_kernel, local=False),
        grid=(b // ne,),
        in_specs=[pl.BlockSpec((ne, nc, GROUP_W), lambda i: (i, 0, COL_NQ)),
                  pl.BlockSpec((ne, nc, GROUP_W), lambda i: (i, 0, COL_NK)),
                  pl.BlockSpec((ne, GROUP_W, nc), lambda i: (i, 0, 0))],
        out_specs=pl.BlockSpec((ne, nc, GROUP_W), lambda i: (i, 0, 0)),
        out_shape=jax.ShapeDtypeStruct((b, nc, GROUP_W), BF16),
        compiler_params=_params(1),
        name="na_ctx",
    )(pc3, pc3, vtc3)


def _na_row_offsets():
    first_row = NA_WIN_ROWS - NA_KH
    rules = [lambda i, a: a - i + NA_KH - 1 if a < NA_KH else None,
             lambda i, a: a - i + NA_KH // 2 - 1 if i <= a < i + NA_KH else None,
             lambda i, a: a - i - 1 if a >= first_row else None]
    return [[[rule(i, a) for a in range(NA_WIN_ROWS)] for i in range(NA_Q_ROWS)] for rule in rules]


def _na_bias_kernel(band_ref, o_ref):
    masked = jnp.full((GRID_W, GRID_W), NEG, F32)
    for kind, per_query_row in enumerate(_na_row_offsets()):
        for a in range(NA_WIN_ROWS):
            blocks = [masked if offs[a] is None else band_ref[0, offs[a]] for offs in per_query_row]
            o_ref[kind, a * GRID_W:(a + 1) * GRID_W, :] = jnp.concatenate(blocks, axis=1)
        o_ref[kind, NA_WIN_ROWS * GRID_W:, :] = jnp.zeros((o_ref.shape[1] - NA_WIN_ROWS * GRID_W, o_ref.shape[2]), F32)


def _na_bias_tables(rpb, nc):
    kc = np.arange(GRID_W)[:, None]
    j = np.arange(GRID_W)[None, :]
    c0 = np.clip(j - NA_KW // 2, 0, GRID_W - NA_KW)
    col_ok = (kc >= c0) & (kc < c0 + NA_KW)
    onehot = (np.arange(2 * NA_KW - 1)[:, None, None] == (kc - j + NA_KW - 1)[None]) & col_ok[None]
    band = jnp.einsum("hdo,okj->hdkj", rpb.astype(F32), jnp.asarray(onehot, F32), precision=lax.Precision.HIGHEST)
    band = jnp.where(col_ok, band * LOG2E, NEG)
    n_off = 2 * NA_KH - 1
    q, k = NA_Q_ROWS * GRID_W, NA_WIN_ROWS * GRID_W
    return pl.pallas_call(
        _na_bias_kernel,
        grid=(N_HEADS,),
        in_specs=[pl.BlockSpec((1, n_off, GRID_W, GRID_W), lambda h: (h, 0, 0, 0))],
        out_specs=pl.BlockSpec((3, k + nc, q), lambda h: (0, 0, h)),
        out_shape=jax.ShapeDtypeStruct((3, k + nc, N_HEADS * q), F32),
        compiler_params=_params(1),
        name="na_bias",
    )(band)


def _gmlp_kernel(u_ref, v_ref, ws_ref, bias_ref, o_ref, *, chunks_per_step):
    n = u_ref.shape[1]
    lane = _lane()
    step_rows = chunks_per_step * CHUNK
    for s in range(n // step_rows):
        base = s * step_rows
        vcat = jnp.concatenate(
            [v_ref[0, base + c * CHUNK: base + (c + 1) * CHUNK, :] for c in range(chunks_per_step)], axis=1)
        r = _dot(ws_ref[...], vcat)
        for c in range(chunks_per_step):
            mix = r[0:CHUNK, c * GROUP_W:(c + 1) * GROUP_W]
            for g in range(1, N_HEADS):
                mix = jnp.where(lane >= g * HEAD_DIM, r[g * CHUNK:(g + 1) * CHUNK, c * GROUP_W:(c + 1) * GROUP_W], mix)
            rows = slice(base + c * CHUNK, base + (c + 1) * CHUNK)
            o_ref[0, rows, :] = (u_ref[0, rows, :].astype(F32) * (mix + bias_ref[...])).astype(BF16)


def _gmlp(p3, ws_stack, bias_full, name):
    b, n, _ = p3.shape
    cps = min(4, n // CHUNK)
    return pl.pallas_call(
        functools.partial(_gmlp_kernel, chunks_per_step=cps),
        grid=(b,),
        in_specs=[pl.BlockSpec((1, n, GROUP_W), lambda i: (i, 0, COL_GU)),
                  pl.BlockSpec((1, n, GROUP_W), lambda i: (i, 0, COL_GV)),
                  _resident((N_HEADS * CHUNK, CHUNK)),
                  _resident((CHUNK, GROUP_W))],
        out_specs=pl.BlockSpec((1, n, GROUP_W), lambda i: (i, 0, 0)),
        out_shape=jax.ShapeDtypeStruct((b, n, GROUP_W), BF16),
        compiler_params=_params(1),
        name=name,
    )(p3, p3, ws_stack, bias_full)


def _mix_ffn_kernel(*refs, tiles_per_seq):
    y_refs, (xp_ref, xm_ref, xn_ref) = refs[:12], refs[12:15]
    mod_ref, g_ref, wo_ref, wu_ref, cw_ref, cb_ref, wd_ref, o_ref, xs_ref, hs_ref, ds_ref, act_ref = refs[15:]
    tm = xm_ref.shape[0]
    groups = tm // 8
    pos = pl.program_id(0) % tiles_per_seq

    def mixed(y_parts, x):
        return x + mod_ref[0, 2:3, :] * _dot(jnp.concatenate(y_parts, axis=1), wo_ref[...])

    def prenorm(x):
        ms = jnp.mean(x * x, axis=-1, keepdims=True)
        h = x * lax.rsqrt(ms + EPS) * g_ref[...]
        return h * (1.0 + mod_ref[0, 4:5, :]) + mod_ref[0, 3:4, :]

    x_main = mixed([y_refs[3 * j + 1][...] for j in range(4)], xm_ref[...])
    xs_ref[...] = x_main
    h_main = prenorm(x_main)
    slabs = hs_ref.shape[0]
    pitch = hs_ref.shape[1] // 8
    for c in range(slabs):
        for s in range(8):
            hs_ref[c, s * pitch:s * pitch + groups, :] = h_main[s * groups:(s + 1) * groups, c * LANES:(c + 1) * LANES]
    h_halo = prenorm(mixed([jnp.concatenate([y_refs[3 * j][...], y_refs[3 * j + 2][...]], axis=0) for j in range(4)],
                           jnp.concatenate([xp_ref[...], xn_ref[...]], axis=0)))
    before = jnp.where(pos != 0, h_halo[HALO - 1:HALO], 0.0)
    after = jnp.where(pos != tiles_per_seq - 1, h_halo[HALO:HALO + 1], 0.0)
    row = lax.broadcasted_iota(jnp.int32, (HALO, 1), 0)
    h_edge = jnp.where(row == 0, before, jnp.where(row == HALO - 1, after, 0.0))
    h_perm = jnp.concatenate(
        [jnp.concatenate([hs_ref[c, pl.ds(j, 8, stride=pitch), :] for c in range(slabs)], axis=1)
         for j in range(groups)], axis=0)
    z_all = _dot(jnp.concatenate([h_perm, h_edge], axis=0).astype(BF16), wu_ref[...])

    sub = lax.broadcasted_iota(jnp.int32, (8, FF_CHUNK), 0)

    def conv(lo):
        cols = slice(lo, lo + FF_CHUNK)
        z = z_all[0:tm, cols]
        edge = z_all[tm:tm + HALO, cols]
        first = jnp.where(sub == 0, edge[0:8], pltpu.roll(z[tm - 8:tm], 1, 0))
        last = jnp.where(sub == 7, edge[HALO - 8:HALO], pltpu.roll(z[0:8], 7, 0))
        return (jnp.concatenate([first, z[0:tm - 8]], axis=0) * cw_ref[0:1, cols] + z * cw_ref[1:2, cols]
                + jnp.concatenate([z[8:tm], last], axis=0) * cw_ref[2:3, cols] + cb_ref[:, cols])

    for c in range(D_FF // FF_CHUNK):
        g = conv(c * FF_CHUNK)
        v = conv(D_FF + c * FF_CHUNK)
        act_ref[:, c * FF_CHUNK:(c + 1) * FF_CHUNK] = (g / (1.0 + jnp.exp(-g)) * v).astype(BF16)
    down = _dot(act_ref[...], wd_ref[...])
    for j in range(groups):
        for c in range(slabs):
            ds_ref[c, pl.ds(j, 8, stride=pitch), :] = down[j * 8:(j + 1) * 8, c * LANES:(c + 1) * LANES]
    down = jnp.concatenate(
        [jnp.concatenate([ds_ref[c, s * pitch:s * pitch + groups, :] for s in range(8)], axis=0)
         for c in range(slabs)], axis=1)
    o_ref[...] = xs_ref[...] + mod_ref[0, 5:6, :] * down


def _mix_ffn(ys, x2d, mod, g_row, wo_bf, wu_bf, conv_w, conv_b, wd_bf, li, n, tm, ctx_stream):
    rows = x2d.shape[0]
    tpb = n // tm
    per_tile = tm // HALO
    last_halo = rows // HALO - 1
    mod_row = (lambda i: (ADA_ROWS // 2, 0, 0)) if ctx_stream else (lambda i: (i // tpb, 0, 0))

    def halo_specs(width):
        return [pl.BlockSpec((HALO, width), lambda i: (jnp.maximum(i * per_tile - 1, 0), 0)),
                pl.BlockSpec((tm, width), lambda i: (i, 0)),
                pl.BlockSpec((HALO, width), lambda i: (jnp.minimum((i + 1) * per_tile, last_halo), 0))]

    y2d = [y.reshape(rows, GROUP_W) for y in ys]
    return pl.pallas_call(
        functools.partial(_mix_ffn_kernel, tiles_per_seq=tpb),
        grid=(rows // tm,),
        in_specs=halo_specs(GROUP_W) * 4 + halo_specs(D_MODEL) + [
            pl.BlockSpec((1, 6, D_MODEL), mod_row),
            _resident((1, D_MODEL)),
            _resident_layer((D_MODEL, D_MODEL), li),
            _resident_layer((D_MODEL, 2 * D_FF), li),
            _resident((3, 2 * D_FF)),
            _resident((1, 2 * D_FF)),
            _resident_layer((D_FF, D_MODEL), li),
        ],
        out_specs=pl.BlockSpec((tm, D_MODEL), lambda i: (i, 0)),
        out_shape=jax.ShapeDtypeStruct((rows, D_MODEL), F32),
        scratch_shapes=[pltpu.VMEM((tm, D_MODEL), F32)]
        + [pltpu.VMEM((D_MODEL // LANES, 8 * (tm // 8 + 8), LANES), F32)] * 2
        + [pltpu.VMEM((tm, D_FF), BF16)],
        compiler_params=_params(1),
        name="mix_ffn_ctx" if ctx_stream else "mix_ffn",
    )(*[a for y in y2d for a in (y, y, y)], x2d, x2d, x2d, mod, g_row, wo_bf, wu_bf, conv_w,
      conv_b.reshape(1, 2 * D_FF), wd_bf)


def _dft_tables(n):
    lo_n = 64
    half = n // 2
    t = jnp.arange(half, dtype=jnp.int32)

    def table(k):
        ang = ((k[:, None] * t[None, :]) % n).astype(F32) * (2.0 * math.pi / n)
        return jnp.cos(ang), jnp.sin(ang)

    (ch, sh), (cl, sl) = table(jnp.arange(n // lo_n, dtype=jnp.int32) * lo_n), table(jnp.arange(lo_n, dtype=jnp.int32))
    cn = (ch[:, None] * cl[None] - sh[:, None] * sl[None]).reshape(n, half).astype(BF16)
    sn = (sh[:, None] * cl[None] + ch[:, None] * sl[None]).reshape(n, half).astype(BF16)
    rb = min(GROUP_W, half)
    r = np.arange(rb)
    rev = np.zeros((rb, 2 * rb), np.float32)
    rev[r[1:], rb - r[1:]] = 1.0
    rev[0, rb] = 1.0
    c = np.arange(HEAD_DIM)
    angc = 2.0 * np.pi * ((c[:, None] * c[None, :]) % HEAD_DIM) / HEAD_DIM
    norm = 1.0 / math.sqrt(n * HEAD_DIM)
    eye = np.eye(N_HEADS)
    bdc = jnp.asarray(np.kron(eye, np.cos(angc) * norm), F32).astype(BF16)
    bds = jnp.asarray(np.kron(eye, np.sin(angc) * norm), F32).astype(BF16)
    return bdc, bds, cn, sn, jnp.asarray(rev, F32).astype(BF16)


def _rope_tables(n):
    n_freq = DIFF_QK // 4
    freqs = ROPE_BASE ** (-jnp.arange(n_freq, dtype=F32) / n_freq)
    t = jnp.arange(n)
    row = (t // GRID_W).astype(F32)
    col = (t % GRID_W).astype(F32)
    ang = jnp.concatenate([row[:, None] * freqs, col[:, None] * freqs], axis=-1)
    cos, sin = jnp.cos(ang), jnp.sin(ang)
    reps = GROUP_W // DIFF_QK
    return jnp.tile(jnp.concatenate([cos, cos], axis=-1), (1, reps)), jnp.tile(jnp.concatenate([-sin, sin], axis=-1), (1, reps))


def _group_mean_matrix(group):
    return jnp.asarray(np.kron(np.eye(GROUP_W // group), np.full((group, group), 1.0 / group)), F32).astype(BF16)


def kernel(x, c, ctx, c_ctx, w_ada, b_ada, g_mix, g_ffn, w_in, w_out, diff_qn, diff_kn, diff_lam, diff_subln, na_qn,
           na_kn, na_rpb, gmlp_norm, gmlp_ws, gmlp_b, ffn_up, ffn_conv, ffn_conv_b, ffn_down):
    b, n, d = x.shape
    nc = ctx.shape[1]
    assert (d, n % (NA_Q_ROWS * GRID_W), nc % CHUNK, b < ADA_ROWS // 2 + 1) == (D_MODEL, 0, 0, True)
    tm = 512
    tq = 512

    c_all = jnp.zeros((ADA_ROWS, d), F32).at[:b].set(c).at[ADA_ROWS // 2].set(c_ctx)
    mods = _ada(c_all, w_ada, b_ada).reshape(DEPTH, ADA_ROWS, 6, d)

    g32 = _group_mean_matrix(DIFF_QK)
    g64 = _group_mean_matrix(HEAD_DIM)
    rope_tabs = _rope_tables(n)
    dft_lat = _dft_tables(n)
    dft_ctx = _dft_tables(nc)

    w_in_kernel_order = jnp.concatenate([w_in[..., r * GROUP_W:(r + 1) * GROUP_W] for r in W_ORDER], axis=-1)
    w_in_bf, w_out_bf, wu_bf, wd_bf = (w.astype(BF16) for w in (w_in_kernel_order, w_out, ffn_up, ffn_down))

    x2d = x.reshape(b * n, d)
    c2d = ctx.reshape(b * nc, d)
    for li in range(DEPTH):
        ctx_out = li < DEPTH - 1
        lam_init = 0.8 - 0.6 * math.exp(-0.3 * li)
        lf = diff_lam[li].astype(F32)
        lam = (jnp.exp(jnp.sum(lf[0] * lf[1])) - jnp.exp(jnp.sum(lf[2] * lf[3])) + lam_init).reshape(1)
        mod = mods[li]
        gains = jnp.zeros((8, GROUP_W), F32)
        gains = gains.at[0].set(jnp.tile(diff_qn[li], GROUP_W // DIFF_QK) * (DIFF_QK ** -0.5 * LOG2E))
        gains = gains.at[1].set(jnp.tile(diff_kn[li], GROUP_W // DIFF_QK))
        gains = gains.at[2].set(jnp.tile(na_qn[li], N_HEADS) * (HEAD_DIM ** -0.5 * LOG2E))
        gains = gains.at[3].set(jnp.tile(na_kn[li], N_HEADS))
        gains = gains.at[4].set(gmlp_norm[li])
        sub_row = (jnp.tile(diff_subln[li], N_HEADS) * (1.0 - lam_init)).reshape(1, GROUP_W)
        g_mix_row = g_mix[li].reshape(1, d)
        g_ffn_row = g_ffn[li].reshape(1, d)
        ws_stack = gmlp_ws[li].reshape(N_HEADS * CHUNK, CHUNK).astype(BF16)
        gbias = jnp.repeat(gmlp_b[li].T, HEAD_DIM, axis=1)
        bias_tab = _na_bias_tables(na_rpb[li], nc)

        p2d, dvt, nvt = _inproj(x2d, mod, g_mix_row, w_in_bf, li, gains, g32, g64, rope_tabs, n, 2 * tm, False)
        pc2d, dvtc, nvtc = _inproj(c2d, mod, g_mix_row, w_in_bf, li, gains, g32, g64, None, nc, nc, True)
        p3 = p2d.reshape(b, n, P_W)
        pc3 = pc2d.reshape(b, nc, P_W)

        ys = [_fourier(p3, *dft_lat, "fourier"),
              _diff(lam, p3, dvt, pc3, dvtc, sub_row, tq, True),
              _na(p3, nvt, pc3, nvtc, bias_tab),
              _gmlp(p3, ws_stack, gbias, "gmlp")]
        ffn_w = (g_ffn_row, w_out_bf, wu_bf, ffn_conv[li], ffn_conv_b[li], wd_bf, li)
        x2d = _mix_ffn(ys, x2d, mod, *ffn_w, n, tm, False)

        if ctx_out:
            ycs = [_fourier(pc3, *dft_ctx, "fourier_ctx"),
                   _diff(lam, None, None, pc3, dvtc, sub_row, nc, False),
                   _na_ctx(pc3, nvtc),
                   _gmlp(pc3, ws_stack, gbias, "gmlp_ctx")]
            c2d = _mix_ffn(ycs, c2d, mod, *ffn_w, nc, nc, True)
    return x2d.reshape(b, n, d)
```

```python
import functools
import math

import jax
import jax.numpy as jnp
import numpy as np
from jax import lax
from jax.experimental import pallas as pl
from jax.experimental.pallas import tpu as pltpu

F32 = jnp.float32
BF16 = jnp.bfloat16

D_MODEL = 1024
DEPTH = 4
GRID_W = 64
GROUP_W = 256
HEAD_DIM = 64
N_HEADS = 4
DIFF_QK = 32
NA_KH = 8
NA_KW = 16
CHUNK = 128
D_FF = 2816
ROPE_BASE = 10000.0
EPS = 1e-6
IN_W = 9 * GROUP_W

REF_F, REF_DQ, REF_DK, REF_DV, REF_NQ, REF_NK, REF_NV, REF_GU, REF_GV = range(9)
W_ORDER = (REF_GV, REF_DQ, REF_DK, REF_NQ, REF_NK, REF_GU, REF_DV, REF_NV, REF_F)
W_GV, W_DQ, W_DK, W_NQ, W_NK, W_GU, W_DV, W_NV, W_F = range(9)
COL_F, COL_DQ, COL_DK, COL_NQ, COL_NK, COL_GU, COL_GV = range(7)
P_W = 7 * GROUP_W

VT_ROWS = HEAD_DIM + 16
DIFF_SCORE_COLS = 2048
PAIR_ROWS = 16
SUM_FLOOR = 2.0 ** -80
NA_Q_ROWS = 4
NA_ELEMS = 4
NA_WIN_ROWS = 12
NEG = -1e30
LOG2E = math.log2(math.e)

ADA_ROWS = 16
VMEM_LIMIT = 56 * 1024 * 1024
LANES = 128
HALO = 16
FF_CHUNK = 256


def _dot(a, b):
    return jnp.dot(a, b, preferred_element_type=F32)


def _dot_nt(a, b):
    return lax.dot_general(a, b, (((1,), (1,)), ((), ())), preferred_element_type=F32)


def _resident(shape):
    nd = len(shape)
    return pl.BlockSpec(shape, lambda *_: (0,) * nd, pipeline_mode=pl.Buffered(1))


def _resident_layer(shape, li):
    nd = len(shape)
    return pl.BlockSpec((None,) + tuple(shape), lambda *_: (li,) + (0,) * nd, pipeline_mode=pl.Buffered(1))


def _params(n_axes):
    return pltpu.CompilerParams(dimension_semantics=("arbitrary",) * n_axes, vmem_limit_bytes=VMEM_LIMIT)


def _lane(width=GROUP_W):
    return lax.broadcasted_iota(jnp.int32, (1, width), 1)


def _ada_kernel(c_ref, w_ref, b_ref, o_ref):
    c = c_ref[...]
    s = c / (1.0 + jnp.exp(-c))
    w = w_ref[0]
    s_hi = s.astype(BF16)
    s_lo = (s - s_hi.astype(F32)).astype(BF16)
    w_hi = w.astype(BF16)
    w_lo = (w - w_hi.astype(F32)).astype(BF16)
    o_ref[0] = _dot(s_hi, w_hi) + _dot(s_lo, w_hi) + _dot(s_hi, w_lo) + b_ref[0]


def _ada(c_all, w_ada, b_ada):
    tn = 1536
    out_w = 6 * D_MODEL
    return pl.pallas_call(
        _ada_kernel,
        grid=(DEPTH, out_w // tn),
        in_specs=[
            pl.BlockSpec((ADA_ROWS, D_MODEL), lambda l, j: (0, 0)),
            pl.BlockSpec((1, D_MODEL, tn), lambda l, j: (l, 0, j)),
            pl.BlockSpec((1, 1, tn), lambda l, j: (l, 0, j)),
        ],
        out_specs=pl.BlockSpec((1, ADA_ROWS, tn), lambda l, j: (l, 0, j)),
        out_shape=jax.ShapeDtypeStruct((DEPTH, ADA_ROWS, out_w), F32),
        compiler_params=_params(2),
        name="ada",
    )(c_all, w_ada, b_ada.reshape(DEPTH, 1, out_w))


def _gelu(x):
    return 0.5 * x * (1.0 + jnp.tanh(math.sqrt(2.0 / math.pi) * (x + 0.044715 * (x * x * x))))


def _inproj_kernel(*refs, rope):
    if rope:
        x_ref, mod_ref, g_ref, w_ref, gains_ref, g32_ref, g64_ref, cos_ref, sin_ref, o_ref, dvt_ref, nvt_ref = refs
    else:
        x_ref, mod_ref, g_ref, w_ref, gains_ref, g32_ref, g64_ref, o_ref, dvt_ref, nvt_ref = refs
    x = x_ref[...]
    ms = jnp.mean(x * x, axis=-1, keepdims=True)
    h = x * lax.rsqrt(ms + EPS) * g_ref[...]
    h = h * (1.0 + mod_ref[0, 1:2, :]) + mod_ref[0, 0:1, :]
    p = _dot(h.astype(BF16), w_ref[...])

    def col(j):
        return p[:, j * GROUP_W:(j + 1) * GROUP_W]

    def put(j, v):
        o_ref[:, j * GROUP_W:(j + 1) * GROUP_W] = v.astype(BF16)

    def group_norm(v, gmat_ref, gain_row):
        gms = _dot((v * v).astype(BF16), gmat_ref[...])
        return v * lax.rsqrt(gms + EPS) * gains_ref[gain_row:gain_row + 1, :]

    def rotary(v):
        if not rope:
            return v
        first_half = (_lane() % DIFF_QK) < (DIFF_QK // 2)
        partner = jnp.where(first_half, pltpu.roll(v, GROUP_W - DIFF_QK // 2, 1), pltpu.roll(v, DIFF_QK // 2, 1))
        return v * cos_ref[...] + partner * sin_ref[...]

    put(COL_GV, group_norm(_gelu(col(W_GV)), g64_ref, 4))
    put(COL_DQ, rotary(group_norm(col(W_DQ), g32_ref, 0)))
    put(COL_DK, rotary(group_norm(col(W_DK), g32_ref, 1)))
    put(COL_NQ, group_norm(col(W_NQ), g64_ref, 2))
    put(COL_NK, group_norm(col(W_NK), g64_ref, 3))
    put(COL_GU, _gelu(col(W_GU)))
    dvt_ref[0] = col(W_DV).T.astype(BF16)
    nvt_ref[0] = col(W_NV).T.astype(BF16)
    put(COL_F, col(W_F))


def _inproj(x2d, mod, g_row, w_bf, li, gains, g32, g64, rope_tabs, n, tm, ctx_stream):
    rows = x2d.shape[0]
    tpb = n // tm
    mod_row = (lambda i: (ADA_ROWS // 2, 0, 0)) if ctx_stream else (lambda i: (i // tpb, 0, 0))
    in_specs = [
        pl.BlockSpec((tm, D_MODEL), lambda i: (i, 0)),
        pl.BlockSpec((1, 6, D_MODEL), mod_row),
        _resident((1, D_MODEL)),
        _resident_layer((D_MODEL, IN_W), li),
        _resident((8, GROUP_W)),
        _resident((GROUP_W, GROUP_W)),
        _resident((GROUP_W, GROUP_W)),
    ]
    args = [x2d, mod, g_row, w_bf, gains, g32, g64]
    if rope_tabs is not None:
        in_specs += [pl.BlockSpec((tm, GROUP_W), lambda i: (i % tpb, 0))] * 2
        args += list(rope_tabs)
    return pl.pallas_call(
        functools.partial(_inproj_kernel, rope=rope_tabs is not None),
        grid=(rows // tm,),
        in_specs=in_specs,
        out_specs=[pl.BlockSpec((tm, P_W), lambda i: (i, 0))]
        + [pl.BlockSpec((1, GROUP_W, tm), lambda i: (i // tpb, 0, i % tpb))] * 2,
        out_shape=[jax.ShapeDtypeStruct((rows, P_W), BF16)]
        + [jax.ShapeDtypeStruct((rows // n, GROUP_W, n), BF16)] * 2,
        compiler_params=_params(1),
        name="inproj_ctx" if ctx_stream else "inproj",
    )(*args)


def _fourier_kernel(p_ref, bdc_ref, bds_ref, cn_ref, sn_ref, rev_ref, o_ref):
    n = p_ref.shape[1]
    half = n // 2
    rb = rev_ref.shape[0]
    nb = n // rb
    mirrored = []
    for blk in range(half // rb):
        src = nb - blk - 1
        nxt = (src + 1) % nb
        pair = jnp.concatenate([p_ref[0, src * rb:(src + 1) * rb, :], p_ref[0, nxt * rb:(nxt + 1) * rb, :]], axis=0)
        mirrored.append(_dot(rev_ref[...], pair))
    mirrored = jnp.concatenate(mirrored, axis=0)
    x = p_ref[0, 0:half, :].astype(F32)
    first = lax.broadcasted_iota(jnp.int32, x.shape, 0) == 0
    fold_c = jnp.where(first, x, x + mirrored).astype(BF16)
    fold_s = (x - mirrored).astype(BF16)
    a = _dot(fold_c, bdc_ref[...]).astype(BF16)
    b = _dot(fold_s, bds_ref[...]).astype(BF16)
    mid = _dot(p_ref[0, half:half + 16, :], bdc_ref[...])[0:1]
    out = _dot(cn_ref[...], a) - _dot(sn_ref[...], b)
    odd = (lax.broadcasted_iota(jnp.int32, out.shape, 0) & 1) == 1
    o_ref[0] = (out + jnp.where(odd, -mid, mid)).astype(BF16)


def _fourier(p3, bdc, bds, cn, sn, rev, name):
    b, n, _ = p3.shape
    return pl.pallas_call(
        _fourier_kernel,
        grid=(b,),
        in_specs=[
            pl.BlockSpec((1, n, GROUP_W), lambda i: (i, 0, COL_F)),
            _resident((GROUP_W, GROUP_W)),
            _resident((GROUP_W, GROUP_W)),
            _resident((n, n // 2)),
            _resident((n, n // 2)),
            _resident(rev.shape),
        ],
        out_specs=pl.BlockSpec((1, n, GROUP_W), lambda i: (i, 0, 0)),
        out_shape=jax.ShapeDtypeStruct((b, n, GROUP_W), BF16),
        compiler_params=_params(1),
        name=name,
    )(p3, bdc, bds, cn, sn, rev)


def _diff_kernel(*refs, has_lat):
    if has_lat:
        lam_ref, q_ref, kl_ref, vl_ref, kc_ref, vc_ref, sub_ref, sel_ref, o_ref, k_ref, vt_ref, kmax_ref = refs
        key_refs, val_refs = [kl_ref, kc_ref], [vl_ref, vc_ref]
    else:
        lam_ref, q_ref, kc_ref, vc_ref, sub_ref, sel_ref, o_ref, k_ref, vt_ref, kmax_ref = refs
        key_refs, val_refs = [kc_ref], [vc_ref]

    def pair_sq_norms(x):
        xf = x.astype(F32)
        return _dot_nt(sel_ref[...], (xf * xf).astype(BF16))

    @pl.when(pl.program_id(1) == 0)
    def _():
        lo = 0
        for kr, vr in zip(key_refs, val_refs):
            cnt = kr.shape[1]
            k_ref[lo:lo + cnt, :] = kr[0]
            for h in range(N_HEADS):
                vt_ref[h, 0:HEAD_DIM, lo:lo + cnt] = vr[0, h * HEAD_DIM:(h + 1) * HEAD_DIM, :]
                vt_ref[h, HEAD_DIM:VT_ROWS, lo:lo + cnt] = jnp.ones((VT_ROWS - HEAD_DIM, cnt), BF16)
            lo += cnt
        kmax_ref[...] = jnp.broadcast_to(pair_sq_norms(k_ref[...]).max(axis=1, keepdims=True), kmax_ref.shape)

    q = q_ref[0]
    lam = lam_ref[0]
    lane = _lane()
    bound = jnp.sqrt(pair_sq_norms(q) * kmax_ref[:, 0:1])

    def attend(shift_fn):
        outs = []
        tq = q.shape[0]
        heads_per_dot = min(N_HEADS, DIFF_SCORE_COLS // (2 * tq))
        for h0 in range(0, N_HEADS, heads_per_dot):
            pairs = range(2 * h0, 2 * (h0 + heads_per_dot))
            qm = jnp.concatenate(
                [jnp.where((lane >= pair * DIFF_QK) & (lane < (pair + 1) * DIFF_QK), q, jnp.zeros_like(q))
                 for pair in pairs], axis=0)
            s = _dot_nt(k_ref[...], qm)
            e = jnp.exp2(s - shift_fn(s, pairs)).astype(BF16)
            for i in range(heads_per_dot):
                o = _dot(vt_ref[h0 + i], e[:, 2 * i * tq:2 * (i + 1) * tq])
                outs += [o[:, 0:tq], o[:, tq:]]
        return outs

    def finish(outs):
        heads = []
        for h in range(N_HEADS):
            o1, o2 = outs[2 * h], outs[2 * h + 1]
            o = (o1[0:HEAD_DIM] * (1.0 / o1[HEAD_DIM:HEAD_DIM + 1])
                 - o2[0:HEAD_DIM] * (lam / o2[HEAD_DIM:HEAD_DIM + 1]))
            heads.append(o * lax.rsqrt(jnp.mean(o * o, axis=0, keepdims=True) + EPS))
        o_ref[0] = (jnp.concatenate(heads, axis=0) * sub_ref[...]).T.astype(BF16)

    outs = attend(lambda s, pairs: jnp.concatenate([bound[p:p + 1] for p in pairs], axis=1))
    finish(outs)
    smallest = outs[0][HEAD_DIM:HEAD_DIM + 1]
    for o in outs[1:]:
        smallest = jnp.minimum(smallest, o[HEAD_DIM:HEAD_DIM + 1])

    @pl.when(jnp.min(smallest) < SUM_FLOOR)
    def _():
        finish(attend(lambda s, pairs: s.max(axis=0, keepdims=True)))


def _diff(lam, p3, vt3, pc3, vtc3, sub_row, tq, has_lat):
    q_src = p3 if has_lat else pc3
    sub_t = jnp.broadcast_to(sub_row.reshape(GROUP_W, 1), (GROUP_W, tq))
    b, n, _ = q_src.shape
    nc = pc3.shape[1]
    smem = pl.BlockSpec(memory_space=pltpu.SMEM)
    in_specs = [smem, pl.BlockSpec((1, tq, GROUP_W), lambda i, j: (i, j, COL_DQ))]
    args = [lam, q_src]
    if has_lat:
        in_specs += [pl.BlockSpec((1, n, GROUP_W), lambda i, j: (i, 0, COL_DK)),
                     pl.BlockSpec((1, GROUP_W, n), lambda i, j: (i, 0, 0))]
        args += [p3, vt3]
    in_specs += [pl.BlockSpec((1, nc, GROUP_W), lambda i, j: (i, 0, COL_DK)),
                 pl.BlockSpec((1, GROUP_W, nc), lambda i, j: (i, 0, 0)),
                 _resident((GROUP_W, tq)),
                 _resident((PAIR_ROWS, GROUP_W))]
    pair_sel = np.arange(PAIR_ROWS)[:, None] == np.arange(GROUP_W)[None, :] // DIFF_QK
    args += [pc3, vtc3, sub_t, jnp.asarray(pair_sel, F32).astype(BF16)]
    return pl.pallas_call(
        functools.partial(_diff_kernel, has_lat=has_lat),
        grid=(b, n // tq),
        in_specs=in_specs,
        out_specs=pl.BlockSpec((1, tq, GROUP_W), lambda i, j: (i, j, 0)),
        out_shape=jax.ShapeDtypeStruct((b, n, GROUP_W), BF16),
        scratch_shapes=[pltpu.VMEM(((n if has_lat else 0) + nc, GROUP_W), BF16),
                        pltpu.VMEM((N_HEADS, VT_ROWS, (n if has_lat else 0) + nc), BF16),
                        pltpu.VMEM((PAIR_ROWS, 128), F32)],
        compiler_params=_params(2),
        name="diff" if has_lat else "diff_ctx",
    )(*args)


def _na_kernel(*refs, local):
    if local:
        q_ref, k0_ref, k1_ref, k2_ref, v0_ref, v1_ref, v2_ref, kc_ref, vc_ref, bias_ref, o_ref = refs
        key_refs = [k0_ref, k1_ref, k2_ref, kc_ref]
        vt_refs = [v0_ref, v1_ref, v2_ref, vc_ref]
    else:
        q_ref, kc_ref, vc_ref, o_ref = refs
        key_refs = [kc_ref]
        vt_refs = [vc_ref]
    n_elems, blk = q_ref.shape[0], q_ref.shape[1]
    lane = _lane()

    def scores(el):
        q = q_ref[el]
        q_heads = jnp.concatenate(
            [jnp.where((lane >= h * HEAD_DIM) & (lane < (h + 1) * HEAD_DIM), q, jnp.zeros_like(q))
             for h in range(N_HEADS)], axis=0)
        k_all = jnp.concatenate([r[el] for r in key_refs], axis=0)
        return _dot_nt(k_all, q_heads)

    def weights(s):
        if local:
            s = s + bias_ref[0]
        return jnp.exp2(s - s.max(axis=0, keepdims=True)).astype(BF16)

    def attend(el, e):
        vt_all = jnp.concatenate([r[el] for r in vt_refs], axis=1)
        ones = jnp.ones((VT_ROWS - HEAD_DIM, vt_all.shape[1]), BF16)
        heads = []
        for h in range(N_HEADS):
            vt_h = jnp.concatenate([vt_all[h * HEAD_DIM:(h + 1) * HEAD_DIM], ones], axis=0)
            o = _dot(vt_h, e[:, h * blk:(h + 1) * blk])
            heads.append(o[0:HEAD_DIM] * (1.0 / o[HEAD_DIM:HEAD_DIM + 1]))
        o_ref[el] = jnp.concatenate(heads, axis=0).T.astype(BF16)

    es = [weights(s) for s in [scores(el) for el in range(n_elems)]]
    for el in range(n_elems):
        attend(el, es[el])


def _na(p3, vt3, pc3, vtc3, bias):
    b, n, _ = p3.shape
    nc = pc3.shape[1]
    ne = NA_ELEMS
    blk = NA_Q_ROWS * GRID_W
    steps = n // blk
    last_win = n // blk - NA_WIN_ROWS // NA_Q_ROWS

    def first(g):
        return jnp.clip(g - 1, 0, last_win)

    def cls(g, i):
        return ((g > 0).astype(jnp.int32) + (g == steps - 1).astype(jnp.int32), 0, 0)

    in_specs = [pl.BlockSpec((ne, blk, GROUP_W), lambda g, i: (i, g, COL_NQ))]
    in_specs += [pl.BlockSpec((ne, blk, GROUP_W), lambda g, i, j=j: (i, first(g) + j, COL_NK)) for j in range(3)]
    in_specs += [pl.BlockSpec((ne, GROUP_W, blk), lambda g, i, j=j: (i, 0, first(g) + j)) for j in range(3)]
    in_specs += [pl.BlockSpec((ne, nc, GROUP_W), lambda g, i: (i, 0, COL_NK)),
                 pl.BlockSpec((ne, GROUP_W, nc), lambda g, i: (i, 0, 0)),
                 pl.BlockSpec((1, 3 * blk + nc, N_HEADS * blk), cls)]
    return pl.pallas_call(
        functools.partial(_na_kernel, local=True),
        grid=(steps, b // ne),
        in_specs=in_specs,
        out_specs=pl.BlockSpec((ne, blk, GROUP_W), lambda g, i: (i, g, 0)),
        out_shape=jax.ShapeDtypeStruct((b, n, GROUP_W), BF16),
        compiler_params=_params(2),
        name="na",
    )(p3, p3, p3, p3, vt3, vt3, vt3, pc3, vtc3, bias)


def _na_ctx(pc3, vtc3):
    b, nc, _ = pc3.shape
    ne = NA_ELEMS
    return pl.pallas_call(
        functools.partial(_na_kernel, local=False),
        grid=(b // ne,),
        in_specs=[pl.BlockSpec((ne, nc, GROUP_W), lambda i: (i, 0, COL_NQ)),
                  pl.BlockSpec((ne, nc, GROUP_W), lambda i: (i, 0, COL_NK)),
                  pl.BlockSpec((ne, GROUP_W, nc), lambda i: (i, 0, 0))],
        out_specs=pl.BlockSpec((ne, nc, GROUP_W), lambda i: (i, 0, 0)),
        out_shape=jax.ShapeDtypeStruct((b, nc, GROUP_W), BF16),
        compiler_params=_params(1),
        name="na_ctx",
    )(pc3, pc3, vtc3)


def _na_row_offsets():
    first_row = NA_WIN_ROWS - NA_KH
    rules = [lambda i, a: a - i + NA_KH - 1 if a < NA_KH else None,
             lambda i, a: a - i + NA_KH // 2 - 1 if i <= a < i + NA_KH else None,
             lambda i, a: a - i - 1 if a >= first_row else None]
    return [[[rule(i, a) for a in range(NA_WIN_ROWS)] for i in range(NA_Q_ROWS)] for rule in rules]


def _na_bias_kernel(band_ref, o_ref):
    masked = jnp.full((GRID_W, GRID_W), NEG, F32)
    for kind, per_query_row in enumerate(_na_row_offsets()):
        for a in range(NA_WIN_ROWS):
            blocks = [masked if offs[a] is None else band_ref[0, offs[a]] for offs in per_query_row]
            o_ref[kind, a * GRID_W:(a + 1) * GRID_W, :] = jnp.concatenate(blocks, axis=1)
        o_ref[kind, NA_WIN_ROWS * GRID_W:, :] = jnp.zeros((o_ref.shape[1] - NA_WIN_ROWS * GRID_W, o_ref.shape[2]), F32)


def _na_bias_tables(rpb, nc):
    kc = np.arange(GRID_W)[:, None]
    j = np.arange(GRID_W)[None, :]
    c0 = np.clip(j - NA_KW // 2, 0, GRID_W - NA_KW)
    col_ok = (kc >= c0) & (kc < c0 + NA_KW)
    onehot = (np.arange(2 * NA_KW - 1)[:, None, None] == (kc - j + NA_KW - 1)[None]) & col_ok[None]
    band = jnp.einsum("hdo,okj->hdkj", rpb.astype(F32), jnp.asarray(onehot, F32), precision=lax.Precision.HIGHEST)
    band = jnp.where(col_ok, band * LOG2E, NEG)
    n_off = 2 * NA_KH - 1
    q, k = NA_Q_ROWS * GRID_W, NA_WIN_ROWS * GRID_W
    return pl.pallas_call(
        _na_bias_kernel,
        grid=(N_HEADS,),
        in_specs=[pl.BlockSpec((1, n_off, GRID_W, GRID_W), lambda h: (h, 0, 0, 0))],
        out_specs=pl.BlockSpec((3, k + nc, q), lambda h: (0, 0, h)),
        out_shape=jax.ShapeDtypeStruct((3, k + nc, N_HEADS * q), F32),
        compiler_params=_params(1),
        name="na_bias",
    )(band)


def _gmlp_kernel(u_ref, v_ref, ws_ref, bias_ref, o_ref, *, chunks_per_step):
    n = u_ref.shape[1]
    lane = _lane()
    step_rows = chunks_per_step * CHUNK
    for s in range(n // step_rows):
        base = s * step_rows
        vcat = jnp.concatenate(
            [v_ref[0, base + c * CHUNK: base + (c + 1) * CHUNK, :] for c in range(chunks_per_step)], axis=1)
        r = _dot(ws_ref[...], vcat)
        for c in range(chunks_per_step):
            mix = r[0:CHUNK, c * GROUP_W:(c + 1) * GROUP_W]
            for g in range(1, N_HEADS):
                mix = jnp.where(lane >= g * HEAD_DIM, r[g * CHUNK:(g + 1) * CHUNK, c * GROUP_W:(c + 1) * GROUP_W], mix)
            rows = slice(base + c * CHUNK, base + (c + 1) * CHUNK)
            o_ref[0, rows, :] = (u_ref[0, rows, :].astype(F32) * (mix + bias_ref[...])).astype(BF16)


def _gmlp(p3, ws_stack, bias_full, name):
    b, n, _ = p3.shape
    cps = min(4, n // CHUNK)
    return pl.pallas_call(
        functools.partial(_gmlp_kernel, chunks_per_step=cps),
        grid=(b,),
        in_specs=[pl.BlockSpec((1, n, GROUP_W), lambda i: (i, 0, COL_GU)),
                  pl.BlockSpec((1, n, GROUP_W), lambda i: (i, 0, COL_GV)),
                  _resident((N_HEADS * CHUNK, CHUNK)),
                  _resident((CHUNK, GROUP_W))],
        out_specs=pl.BlockSpec((1, n, GROUP_W), lambda i: (i, 0, 0)),
        out_shape=jax.ShapeDtypeStruct((b, n, GROUP_W), BF16),
        compiler_params=_params(1),
        name=name,
    )(p3, p3, ws_stack, bias_full)


def _mix_ffn_kernel(*refs, tiles_per_seq):
    y_refs, (xp_ref, xm_ref, xn_ref) = refs[:12], refs[12:15]
    mod_ref, g_ref, wo_ref, wu_ref, cw_ref, cb_ref, wd_ref, o_ref, xs_ref, hs_ref, ds_ref, act_ref = refs[15:]
    tm = xm_ref.shape[0]
    groups = tm // 8
    pos = pl.program_id(0) % tiles_per_seq

    def mixed(y_parts, x):
        return x + mod_ref[0, 2:3, :] * _dot(jnp.concatenate(y_parts, axis=1), wo_ref[...])

    def prenorm(x):
        ms = jnp.mean(x * x, axis=-1, keepdims=True)
        h = x * lax.rsqrt(ms + EPS) * g_ref[...]
        return h * (1.0 + mod_ref[0, 4:5, :]) + mod_ref[0, 3:4, :]

    x_main = mixed([y_refs[3 * j + 1][...] for j in range(4)], xm_ref[...])
    xs_ref[...] = x_main
    h_main = prenorm(x_main)
    slabs = hs_ref.shape[0]
    pitch = hs_ref.shape[1] // 8
    for c in range(slabs):
        for s in range(8):
            hs_ref[c, s * pitch:s * pitch + groups, :] = h_main[s * groups:(s + 1) * groups, c * LANES:(c + 1) * LANES]
    h_halo = prenorm(mixed([jnp.concatenate([y_refs[3 * j][...], y_refs[3 * j + 2][...]], axis=0) for j in range(4)],
                           jnp.concatenate([xp_ref[...], xn_ref[...]], axis=0)))
    before = jnp.where(pos != 0, h_halo[HALO - 1:HALO], 0.0)
    after = jnp.where(pos != tiles_per_seq - 1, h_halo[HALO:HALO + 1], 0.0)
    row = lax.broadcasted_iota(jnp.int32, (HALO, 1), 0)
    h_edge = jnp.where(row == 0, before, jnp.where(row == HALO - 1, after, 0.0))
    h_perm = jnp.concatenate(
        [jnp.concatenate([hs_ref[c, pl.ds(j, 8, stride=pitch), :] for c in range(slabs)], axis=1)
         for j in range(groups)], axis=0)
    z_all = _dot(jnp.concatenate([h_perm, h_edge], axis=0).astype(BF16), wu_ref[...])

    sub = lax.broadcasted_iota(jnp.int32, (8, FF_CHUNK), 0)

    def conv(lo):
        cols = slice(lo, lo + FF_CHUNK)
        z = z_all[0:tm, cols]
        edge = z_all[tm:tm + HALO, cols]
        first = jnp.where(sub == 0, edge[0:8], pltpu.roll(z[tm - 8:tm], 1, 0))
        last = jnp.where(sub == 7, edge[HALO - 8:HALO], pltpu.roll(z[0:8], 7, 0))
        return (jnp.concatenate([first, z[0:tm - 8]], axis=0) * cw_ref[0:1, cols] + z * cw_ref[1:2, cols]
                + jnp.concatenate([z[8:tm], last], axis=0) * cw_ref[2:3, cols] + cb_ref[:, cols])

    for c in range(D_FF // FF_CHUNK):
        g = conv(c * FF_CHUNK)
        v = conv(D_FF + c * FF_CHUNK)
        act_ref[:, c * FF_CHUNK:(c + 1) * FF_CHUNK] = (g / (1.0 + jnp.exp(-g)) * v).astype(BF16)
    down = _dot(act_ref[...], wd_ref[...])
    for j in range(groups):
        for c in range(slabs):
            ds_ref[c, pl.ds(j, 8, stride=pitch), :] = down[j * 8:(j + 1) * 8, c * LANES:(c + 1) * LANES]
    down = jnp.concatenate(
        [jnp.concatenate([ds_ref[c, s * pitch:s * pitch + groups, :] for s in range(8)], axis=0)
         for c in range(slabs)], axis=1)
    o_ref[...] = xs_ref[...] + mod_ref[0, 5:6, :] * down


def _mix_ffn(ys, x2d, mod, g_row, wo_bf, wu_bf, conv_w, conv_b, wd_bf, li, n, tm, ctx_stream):
    rows = x2d.shape[0]
    tpb = n // tm
    per_tile = tm // HALO
    last_halo = rows // HALO - 1
    mod_row = (lambda i: (ADA_ROWS // 2, 0, 0)) if ctx_stream else (lambda i: (i // tpb, 0, 0))

    def halo_specs(width):
        return [pl.BlockSpec((HALO, width), lambda i: (jnp.maximum(i * per_tile - 1, 0), 0)),
                pl.BlockSpec((tm, width), lambda i: (i, 0)),
                pl.BlockSpec((HALO, width), lambda i: (jnp.minimum((i + 1) * per_tile, last_halo), 0))]

    y2d = [y.reshape(rows, GROUP_W) for y in ys]
    return pl.pallas_call(
        functools.partial(_mix_ffn_kernel, tiles_per_seq=tpb),
        grid=(rows // tm,),
        in_specs=halo_specs(GROUP_W) * 4 + halo_specs(D_MODEL) + [
            pl.BlockSpec((1, 6, D_MODEL), mod_row),
            _resident((1, D_MODEL)),
            _resident_layer((D_MODEL, D_MODEL), li),
            _resident_layer((D_MODEL, 2 * D_FF), li),
            _resident((3, 2 * D_FF)),
            _resident((1, 2 * D_FF)),
            _resident_layer((D_FF, D_MODEL), li),
        ],
        out_specs=pl.BlockSpec((tm, D_MODEL), lambda i: (i, 0)),
        out_shape=jax.ShapeDtypeStruct((rows, D_MODEL), F32),
        scratch_shapes=[pltpu.VMEM((tm, D_MODEL), F32)]
        + [pltpu.VMEM((D_MODEL // LANES, 8 * (tm // 8 + 8), LANES), F32)] * 2
        + [pltpu.VMEM((tm, D_FF), BF16)],
        compiler_params=_params(1),
        name="mix_ffn_ctx" if ctx_stream else "mix_ffn",
    )(*[a for y in y2d for a in (y, y, y)], x2d, x2d, x2d, mod, g_row, wo_bf, wu_bf, conv_w,
      conv_b.reshape(1, 2 * D_FF), wd_bf)


def _dft_tables(n):
    lo_n = 64
    half = n // 2
    t = jnp.arange(half, dtype=jnp.int32)

    def table(k):
        ang = ((k[:, None] * t[None, :]) % n).astype(F32) * (2.0 * math.pi / n)
        return jnp.cos(ang), jnp.sin(ang)

    (ch, sh), (cl, sl) = table(jnp.arange(n // lo_n, dtype=jnp.int32) * lo_n), table(jnp.arange(lo_n, dtype=jnp.int32))
    cn = (ch[:, None] * cl[None] - sh[:, None] * sl[None]).reshape(n, half).astype(BF16)
    sn = (sh[:, None] * cl[None] + ch[:, None] * sl[None]).reshape(n, half).astype(BF16)
    rb = min(GROUP_W, half)
    r = np.arange(rb)
    rev = np.zeros((rb, 2 * rb), np.float32)
    rev[r[1:], rb - r[1:]] = 1.0
    rev[0, rb] = 1.0
    c = np.arange(HEAD_DIM)
    angc = 2.0 * np.pi * ((c[:, None] * c[None, :]) % HEAD_DIM) / HEAD_DIM
    norm = 1.0 / math.sqrt(n * HEAD_DIM)
    eye = np.eye(N_HEADS)
    bdc = jnp.asarray(np.kron(eye, np.cos(angc) * norm), F32).astype(BF16)
    bds = jnp.asarray(np.kron(eye, np.sin(angc) * norm), F32).astype(BF16)
    return bdc, bds, cn, sn, jnp.asarray(rev, F32).astype(BF16)


def _rope_tables(n):
    n_freq = DIFF_QK // 4
    freqs = ROPE_BASE ** (-jnp.arange(n_freq, dtype=F32) / n_freq)
    t = jnp.arange(n)
    row = (t // GRID_W).astype(F32)
    col = (t % GRID_W).astype(F32)
    ang = jnp.concatenate([row[:, None] * freqs, col[:, None] * freqs], axis=-1)
    cos, sin = jnp.cos(ang), jnp.sin(ang)
    reps = GROUP_W // DIFF_QK
    return jnp.tile(jnp.concatenate([cos, cos], axis=-1), (1, reps)), jnp.tile(jnp.concatenate([-sin, sin], axis=-1), (1, reps))


def _group_mean_matrix(group):
    return jnp.asarray(np.kron(np.eye(GROUP_W // group), np.full((group, group), 1.0 / group)), F32).astype(BF16)


def kernel(x, c, ctx, c_ctx, w_ada, b_ada, g_mix, g_ffn, w_in, w_out, diff_qn, diff_kn, diff_lam, diff_subln, na_qn,
           na_kn, na_rpb, gmlp_norm, gmlp_ws, gmlp_b, ffn_up, ffn_conv, ffn_conv_b, ffn_down):
    b, n, d = x.shape
    nc = ctx.shape[1]
    assert (d, n % (NA_Q_ROWS * GRID_W), nc % CHUNK, b < ADA_ROWS // 2 + 1) == (D_MODEL, 0, 0, True)
    tm = 512
    tq = 1024

    c_all = jnp.zeros((ADA_ROWS, d), F32).at[:b].set(c).at[ADA_ROWS // 2].set(c_ctx)
    mods = _ada(c_all, w_ada, b_ada).reshape(DEPTH, ADA_ROWS, 6, d)

    g32 = _group_mean_matrix(DIFF_QK)
    g64 = _group_mean_matrix(HEAD_DIM)
    rope_tabs = _rope_tables(n)
    dft_lat = _dft_tables(n)
    dft_ctx = _dft_tables(nc)

    w_in_kernel_order = jnp.concatenate([w_in[..., r * GROUP_W:(r + 1) * GROUP_W] for r in W_ORDER], axis=-1)
    w_in_bf, w_out_bf, wu_bf, wd_bf = (w.astype(BF16) for w in (w_in_kernel_order, w_out, ffn_up, ffn_down))

    x2d = x.reshape(b * n, d)
    c2d = ctx.reshape(b * nc, d)
    for li in range(DEPTH):
        ctx_out = li < DEPTH - 1
        lam_init = 0.8 - 0.6 * math.exp(-0.3 * li)
        lf = diff_lam[li].astype(F32)
        lam = (jnp.exp(jnp.sum(lf[0] * lf[1])) - jnp.exp(jnp.sum(lf[2] * lf[3])) + lam_init).reshape(1)
        mod = mods[li]
        gains = jnp.zeros((8, GROUP_W), F32)
        gains = gains.at[0].set(jnp.tile(diff_qn[li], GROUP_W // DIFF_QK) * (DIFF_QK ** -0.5 * LOG2E))
        gains = gains.at[1].set(jnp.tile(diff_kn[li], GROUP_W // DIFF_QK))
        gains = gains.at[2].set(jnp.tile(na_qn[li], N_HEADS) * (HEAD_DIM ** -0.5 * LOG2E))
        gains = gains.at[3].set(jnp.tile(na_kn[li], N_HEADS))
        gains = gains.at[4].set(gmlp_norm[li])
        sub_row = (jnp.tile(diff_subln[li], N_HEADS) * (1.0 - lam_init)).reshape(1, GROUP_W)
        g_mix_row = g_mix[li].reshape(1, d)
        g_ffn_row = g_ffn[li].reshape(1, d)
        ws_stack = gmlp_ws[li].reshape(N_HEADS * CHUNK, CHUNK).astype(BF16)
        gbias = jnp.repeat(gmlp_b[li].T, HEAD_DIM, axis=1)
        bias_tab = _na_bias_tables(na_rpb[li], nc)

        p2d, dvt, nvt = _inproj(x2d, mod, g_mix_row, w_in_bf, li, gains, g32, g64, rope_tabs, n, 2 * tm, False)
        pc2d, dvtc, nvtc = _inproj(c2d, mod, g_mix_row, w_in_bf, li, gains, g32, g64, None, nc, nc, True)
        p3 = p2d.reshape(b, n, P_W)
        pc3 = pc2d.reshape(b, nc, P_W)

        ys = [_fourier(p3, *dft_lat, "fourier"),
              _diff(lam, p3, dvt, pc3, dvtc, sub_row, tq, True),
              _na(p3, nvt, pc3, nvtc, bias_tab),
              _gmlp(p3, ws_stack, gbias, "gmlp")]
        ffn_w = (g_ffn_row, w_out_bf, wu_bf, ffn_conv[li], ffn_conv_b[li], wd_bf, li)
        x2d = _mix_ffn(ys, x2d, mod, *ffn_w, n, tm, False)

        if ctx_out:
            ycs = [_fourier(pc3, *dft_ctx, "fourier_ctx"),
                   _diff(lam, None, None, pc3, dvtc, sub_row, nc, False),
                   _na_ctx(pc3, nvtc),
                   _gmlp(pc3, ws_stack, gbias, "gmlp_ctx")]
            c2d = _mix_ffn(ycs, c2d, mod, *ffn_w, nc, nc, True)
    return x2d.reshape(b, n, d)
```

```python
import functools
import math

import jax
import jax.numpy as jnp
import numpy as np
from jax import lax
from jax.experimental import pallas as pl
from jax.experimental.pallas import tpu as pltpu

F32 = jnp.float32
BF16 = jnp.bfloat16

D_MODEL = 1024
DEPTH = 4
GRID_W = 64
GROUP_W = 256
HEAD_DIM = 64
N_HEADS = 4
DIFF_QK = 32
NA_KH = 8
NA_KW = 16
CHUNK = 128
D_FF = 2816
ROPE_BASE = 10000.0
EPS = 1e-6
IN_W = 9 * GROUP_W

REF_F, REF_DQ, REF_DK, REF_DV, REF_NQ, REF_NK, REF_NV, REF_GU, REF_GV = range(9)
W_ORDER = (REF_GV, REF_DQ, REF_DK, REF_NQ, REF_NK, REF_GU, REF_DV, REF_NV, REF_F)
W_GV, W_DQ, W_DK, W_NQ, W_NK, W_GU, W_DV, W_NV, W_F = range(9)
COL_F, COL_DQ, COL_DK, COL_NQ, COL_NK, COL_GU, COL_GV = range(7)
P_W = 7 * GROUP_W

VT_ROWS = HEAD_DIM + 16
DIFF_SCORE_COLS = 2048
PAIR_ROWS = 16
SUM_FLOOR = 2.0 ** -80
NA_Q_ROWS = 4
NA_ELEMS = 4
NA_WIN_ROWS = 12
NEG = -1e30
LOG2E = math.log2(math.e)

ADA_ROWS = 16
VMEM_LIMIT = 56 * 1024 * 1024
LANES = 128
HALO = 16
FF_CHUNK = 256


def _dot(a, b):
    return jnp.dot(a, b, preferred_element_type=F32)


def _dot_nt(a, b):
    return lax.dot_general(a, b, (((1,), (1,)), ((), ())), preferred_element_type=F32)


def _resident(shape):
    nd = len(shape)
    return pl.BlockSpec(shape, lambda *_: (0,) * nd, pipeline_mode=pl.Buffered(1))


def _resident_layer(shape, li):
    nd = len(shape)
    return pl.BlockSpec((None,) + tuple(shape), lambda *_: (li,) + (0,) * nd, pipeline_mode=pl.Buffered(1))


def _params(n_axes):
    return pltpu.CompilerParams(dimension_semantics=("arbitrary",) * n_axes, vmem_limit_bytes=VMEM_LIMIT)


def _lane(width=GROUP_W):
    return lax.broadcasted_iota(jnp.int32, (1, width), 1)


def _ada_kernel(c_ref, w_ref, b_ref, o_ref):
    c = c_ref[...]
    s = c / (1.0 + jnp.exp(-c))
    w = w_ref[0]
    s_hi = s.astype(BF16)
    s_lo = (s - s_hi.astype(F32)).astype(BF16)
    w_hi = w.astype(BF16)
    w_lo = (w - w_hi.astype(F32)).astype(BF16)
    o_ref[0] = _dot(s_hi, w_hi) + _dot(s_lo, w_hi) + _dot(s_hi, w_lo) + b_ref[0]


def _ada(c_all, w_ada, b_ada):
    tn = 1536
    out_w = 6 * D_MODEL
    return pl.pallas_call(
        _ada_kernel,
        grid=(DEPTH, out_w // tn),
        in_specs=[
            pl.BlockSpec((ADA_ROWS, D_MODEL), lambda l, j: (0, 0)),
            pl.BlockSpec((1, D_MODEL, tn), lambda l, j: (l, 0, j)),
            pl.BlockSpec((1, 1, tn), lambda l, j: (l, 0, j)),
        ],
        out_specs=pl.BlockSpec((1, ADA_ROWS, tn), lambda l, j: (l, 0, j)),
        out_shape=jax.ShapeDtypeStruct((DEPTH, ADA_ROWS, out_w), F32),
        compiler_params=_params(2),
        name="ada",
    )(c_all, w_ada, b_ada.reshape(DEPTH, 1, out_w))


def _gelu(x):
    return 0.5 * x * (1.0 + jnp.tanh(math.sqrt(2.0 / math.pi) * (x + 0.044715 * (x * x * x))))


def _inproj_kernel(*refs, rope):
    if rope:
        x_ref, mod_ref, g_ref, w_ref, gains_ref, g32_ref, g64_ref, cos_ref, sin_ref, o_ref, dvt_ref, nvt_ref = refs
    else:
        x_ref, mod_ref, g_ref, w_ref, gains_ref, g32_ref, g64_ref, o_ref, dvt_ref, nvt_ref = refs
    x = x_ref[...]
    ms = jnp.mean(x * x, axis=-1, keepdims=True)
    h = x * lax.rsqrt(ms + EPS) * g_ref[...]
    h = h * (1.0 + mod_ref[0, 1:2, :]) + mod_ref[0, 0:1, :]
    p = _dot(h.astype(BF16), w_ref[...])

    def col(j):
        return p[:, j * GROUP_W:(j + 1) * GROUP_W]

    def put(j, v):
        o_ref[:, j * GROUP_W:(j + 1) * GROUP_W] = v.astype(BF16)

    def group_norm(v, gmat_ref, gain_row):
        gms = _dot((v * v).astype(BF16), gmat_ref[...])
        return v * lax.rsqrt(gms + EPS) * gains_ref[gain_row:gain_row + 1, :]

    def rotary(v):
        if not rope:
            return v
        first_half = (_lane() % DIFF_QK) < (DIFF_QK // 2)
        partner = jnp.where(first_half, pltpu.roll(v, GROUP_W - DIFF_QK // 2, 1), pltpu.roll(v, DIFF_QK // 2, 1))
        return v * cos_ref[...] + partner * sin_ref[...]

    put(COL_GV, group_norm(_gelu(col(W_GV)), g64_ref, 4))
    put(COL_DQ, rotary(group_norm(col(W_DQ), g32_ref, 0)))
    put(COL_DK, rotary(group_norm(col(W_DK), g32_ref, 1)))
    put(COL_NQ, group_norm(col(W_NQ), g64_ref, 2))
    put(COL_NK, group_norm(col(W_NK), g64_ref, 3))
    put(COL_GU, _gelu(col(W_GU)))
    dvt_ref[0] = col(W_DV).T.astype(BF16)
    nvt_ref[0] = col(W_NV).T.astype(BF16)
    put(COL_F, col(W_F))


def _inproj(x2d, mod, g_row, w_bf, li, gains, g32, g64, rope_tabs, n, tm, ctx_stream):
    rows = x2d.shape[0]
    tpb = n // tm
    mod_row = (lambda i: (ADA_ROWS // 2, 0, 0)) if ctx_stream else (lambda i: (i // tpb, 0, 0))
    in_specs = [
        pl.BlockSpec((tm, D_MODEL), lambda i: (i, 0)),
        pl.BlockSpec((1, 6, D_MODEL), mod_row),
        _resident((1, D_MODEL)),
        _resident_layer((D_MODEL, IN_W), li),
        _resident((8, GROUP_W)),
        _resident((GROUP_W, GROUP_W)),
        _resident((GROUP_W, GROUP_W)),
    ]
    args = [x2d, mod, g_row, w_bf, gains, g32, g64]
    if rope_tabs is not None:
        in_specs += [pl.BlockSpec((tm, GROUP_W), lambda i: (i % tpb, 0))] * 2
        args += list(rope_tabs)
    return pl.pallas_call(
        functools.partial(_inproj_kernel, rope=rope_tabs is not None),
        grid=(rows // tm,),
        in_specs=in_specs,
        out_specs=[pl.BlockSpec((tm, P_W), lambda i: (i, 0))]
        + [pl.BlockSpec((1, GROUP_W, tm), lambda i: (i // tpb, 0, i % tpb))] * 2,
        out_shape=[jax.ShapeDtypeStruct((rows, P_W), BF16)]
        + [jax.ShapeDtypeStruct((rows // n, GROUP_W, n), BF16)] * 2,
        compiler_params=_params(1),
        name="inproj_ctx" if ctx_stream else "inproj",
    )(*args)


def _fourier_kernel(p_ref, bdc_ref, bds_ref, cn_ref, sn_ref, rev_ref, o_ref):
    n = p_ref.shape[1]
    half = n // 2
    rb = rev_ref.shape[0]
    nb = n // rb
    mirrored = []
    for blk in range(half // rb):
        src = nb - blk - 1
        nxt = (src + 1) % nb
        pair = jnp.concatenate([p_ref[0, src * rb:(src + 1) * rb, :], p_ref[0, nxt * rb:(nxt + 1) * rb, :]], axis=0)
        mirrored.append(_dot(rev_ref[...], pair))
    mirrored = jnp.concatenate(mirrored, axis=0)
    x = p_ref[0, 0:half, :].astype(F32)
    first = lax.broadcasted_iota(jnp.int32, x.shape, 0) == 0
    fold_c = jnp.where(first, x, x + mirrored).astype(BF16)
    fold_s = (x - mirrored).astype(BF16)
    a = _dot(fold_c, bdc_ref[...]).astype(BF16)
    b = _dot(fold_s, bds_ref[...]).astype(BF16)
    mid = _dot(p_ref[0, half:half + 16, :], bdc_ref[...])[0:1]
    out = _dot(cn_ref[...], a) - _dot(sn_ref[...], b)
    odd = (lax.broadcasted_iota(jnp.int32, out.shape, 0) & 1) == 1
    o_ref[0] = (out + jnp.where(odd, -mid, mid)).astype(BF16)


def _fourier(p3, bdc, bds, cn, sn, rev, name):
    b, n, _ = p3.shape
    return pl.pallas_call(
        _fourier_kernel,
        grid=(b,),
        in_specs=[
            pl.BlockSpec((1, n, GROUP_W), lambda i: (i, 0, COL_F)),
            _resident((GROUP_W, GROUP_W)),
            _resident((GROUP_W, GROUP_W)),
            _resident((n, n // 2)),
            _resident((n, n // 2)),
            _resident(rev.shape),
        ],
        out_specs=pl.BlockSpec((1, n, GROUP_W), lambda i: (i, 0, 0)),
        out_shape=jax.ShapeDtypeStruct((b, n, GROUP_W), BF16),
        compiler_params=_params(1),
        name=name,
    )(p3, bdc, bds, cn, sn, rev)


def _diff_kernel(*refs, has_lat):
    if has_lat:
        lam_ref, q_ref, kl_ref, vl_ref, kc_ref, vc_ref, sub_ref, sel_ref, o_ref, k_ref, vt_ref, kmax_ref = refs
        key_refs, val_refs = [kl_ref, kc_ref], [vl_ref, vc_ref]
    else:
        lam_ref, q_ref, kc_ref, vc_ref, sub_ref, sel_ref, o_ref, k_ref, vt_ref, kmax_ref = refs
        key_refs, val_refs = [kc_ref], [vc_ref]

    def pair_sq_norms(x):
        xf = x.astype(F32)
        return _dot_nt(sel_ref[...], (xf * xf).astype(BF16))

    @pl.when(pl.program_id(1) == 0)
    def _():
        lo = 0
        for kr, vr in zip(key_refs, val_refs):
            cnt = kr.shape[1]
            k_ref[lo:lo + cnt, :] = kr[0]
            for h in range(N_HEADS):
                vt_ref[h, 0:HEAD_DIM, lo:lo + cnt] = vr[0, h * HEAD_DIM:(h + 1) * HEAD_DIM, :]
                vt_ref[h, HEAD_DIM:VT_ROWS, lo:lo + cnt] = jnp.ones((VT_ROWS - HEAD_DIM, cnt), BF16)
            lo += cnt
        kmax_ref[...] = jnp.broadcast_to(pair_sq_norms(k_ref[...]).max(axis=1, keepdims=True), kmax_ref.shape)

    q = q_ref[0]
    lam = lam_ref[0]
    lane = _lane()
    bound = jnp.sqrt(pair_sq_norms(q) * kmax_ref[:, 0:1])

    def attend(shift_fn):
        outs = []
        tq = q.shape[0]
        heads_per_dot = min(N_HEADS, DIFF_SCORE_COLS // (2 * tq))
        for h0 in range(0, N_HEADS, heads_per_dot):
            pairs = range(2 * h0, 2 * (h0 + heads_per_dot))
            qm = jnp.concatenate(
                [jnp.where((lane >= pair * DIFF_QK) & (lane < (pair + 1) * DIFF_QK), q, jnp.zeros_like(q))
                 for pair in pairs], axis=0)
            s = _dot_nt(k_ref[...], qm)
            e = jnp.exp2(s - shift_fn(s, pairs)).astype(BF16)
            for i in range(heads_per_dot):
                o = _dot(vt_ref[h0 + i], e[:, 2 * i * tq:2 * (i + 1) * tq])
                outs += [o[:, 0:tq], o[:, tq:]]
        return outs

    def finish(outs):
        heads = []
        for h in range(N_HEADS):
            o1, o2 = outs[2 * h], outs[2 * h + 1]
            o = (o1[0:HEAD_DIM] * (1.0 / o1[HEAD_DIM:HEAD_DIM + 1])
                 - o2[0:HEAD_DIM] * (lam / o2[HEAD_DIM:HEAD_DIM + 1]))
            heads.append(o * lax.rsqrt(jnp.mean(o * o, axis=0, keepdims=True) + EPS))
        o_ref[0] = (jnp.concatenate(heads, axis=0) * sub_ref[...]).T.astype(BF16)

    outs = attend(lambda s, pairs: jnp.concatenate([bound[p:p + 1] for p in pairs], axis=1))
    finish(outs)
    smallest = outs[0][HEAD_DIM:HEAD_DIM + 1]
    for o in outs[1:]:
        smallest = jnp.minimum(smallest, o[HEAD_DIM:HEAD_DIM + 1])

    @pl.when(jnp.min(smallest) < SUM_FLOOR)
    def _():
        finish(attend(lambda s, pairs: s.max(axis=0, keepdims=True)))


def _diff(lam, p3, vt3, pc3, vtc3, sub_row, tq, has_lat):
    q_src = p3 if has_lat else pc3
    sub_t = jnp.broadcast_to(sub_row.reshape(GROUP_W, 1), (GROUP_W, tq))
    b, n, _ = q_src.shape
    nc = pc3.shape[1]
    smem = pl.BlockSpec(memory_space=pltpu.SMEM)
    in_specs = [smem, pl.BlockSpec((1, tq, GROUP_W), lambda i, j: (i, j, COL_DQ))]
    args = [lam, q_src]
    if has_lat:
        in_specs += [pl.BlockSpec((1, n, GROUP_W), lambda i, j: (i, 0, COL_DK)),
                     pl.BlockSpec((1, GROUP_W, n), lambda i, j: (i, 0, 0))]
        args += [p3, vt3]
    in_specs += [pl.BlockSpec((1, nc, GROUP_W), lambda i, j: (i, 0, COL_DK)),
                 pl.BlockSpec((1, GROUP_W, nc), lambda i, j: (i, 0, 0)),
                 _resident((GROUP_W, tq)),
                 _resident((PAIR_ROWS, GROUP_W))]
    pair_sel = np.arange(PAIR_ROWS)[:, None] == np.arange(GROUP_W)[None, :] // DIFF_QK
    args += [pc3, vtc3, sub_t, jnp.asarray(pair_sel, F32).astype(BF16)]
    return pl.pallas_call(
        functools.partial(_diff_kernel, has_lat=has_lat),
        grid=(b, n // tq),
        in_specs=in_specs,
        out_specs=pl.BlockSpec((1, tq, GROUP_W), lambda i, j: (i, j, 0)),
        out_shape=jax.ShapeDtypeStruct((b, n, GROUP_W), BF16),
        scratch_shapes=[pltpu.VMEM(((n if has_lat else 0) + nc, GROUP_W), BF16),
                        pltpu.VMEM((N_HEADS, VT_ROWS, (n if has_lat else 0) + nc), BF16),
                        pltpu.VMEM((PAIR_ROWS, 128), F32)],
        compiler_params=_params(2),
        name="diff" if has_lat else "diff_ctx",
    )(*args)


def _na_kernel(*refs, local):
    if local:
        q_ref, k0_ref, k1_ref, k2_ref, v0_ref, v1_ref, v2_ref, kc_ref, vc_ref, bias_ref, o_ref = refs
        key_refs = [k0_ref, k1_ref, k2_ref, kc_ref]
        vt_refs = [v0_ref, v1_ref, v2_ref, vc_ref]
    else:
        q_ref, kc_ref, vc_ref, o_ref = refs
        key_refs = [kc_ref]
        vt_refs = [vc_ref]
    n_elems, blk = q_ref.shape[0], q_ref.shape[1]
    lane = _lane()

    def scores(el):
        q = q_ref[el]
        q_heads = jnp.concatenate(
            [jnp.where((lane >= h * HEAD_DIM) & (lane < (h + 1) * HEAD_DIM), q, jnp.zeros_like(q))
             for h in range(N_HEADS)], axis=0)
        k_all = jnp.concatenate([r[el] for r in key_refs], axis=0)
        return _dot_nt(k_all, q_heads)

    def weights(s):
        if local:
            s = s + bias_ref[0]
        return jnp.exp2(s - s.max(axis=0, keepdims=True)).astype(BF16)

    def attend(el, e):
        vt_all = jnp.concatenate([r[el] for r in vt_refs], axis=1)
        ones = jnp.ones((VT_ROWS - HEAD_DIM, vt_all.shape[1]), BF16)
        heads = []
        for h in range(N_HEADS):
            vt_h = jnp.concatenate([vt_all[h * HEAD_DIM:(h + 1) * HEAD_DIM], ones], axis=0)
            o = _dot(vt_h, e[:, h * blk:(h + 1) * blk])
            heads.append(o[0:HEAD_DIM] * (1.0 / o[HEAD_DIM:HEAD_DIM + 1]))
        o_ref[el] = jnp.concatenate(heads, axis=0).T.astype(BF16)

    es = [weights(s) for s in [scores(el) for el in range(n_elems)]]
    for el in range(n_elems):
        attend(el, es[el])


def _na(p3, vt3, pc3, vtc3, bias):
    b, n, _ = p3.shape
    nc = pc3.shape[1]
    ne = NA_ELEMS
    blk = NA_Q_ROWS * GRID_W
    steps = n // blk
    last_win = n // blk - NA_WIN_ROWS // NA_Q_ROWS

    def first(g):
        return jnp.clip(g - 1, 0, last_win)

    def cls(g, i):
        return ((g > 0).astype(jnp.int32) + (g == steps - 1).astype(jnp.int32), 0, 0)

    in_specs = [pl.BlockSpec((ne, blk, GROUP_W), lambda g, i: (i, g, COL_NQ))]
    in_specs += [pl.BlockSpec((ne, blk, GROUP_W), lambda g, i, j=j: (i, first(g) + j, COL_NK)) for j in range(3)]
    in_specs += [pl.BlockSpec((ne, GROUP_W, blk), lambda g, i, j=j: (i, 0, first(g) + j)) for j in range(3)]
    in_specs += [pl.BlockSpec((ne, nc, GROUP_W), lambda g, i: (i, 0, COL_NK)),
                 pl.BlockSpec((ne, GROUP_W, nc), lambda g, i: (i, 0, 0)),
                 pl.BlockSpec((1, 3 * blk + nc, N_HEADS * blk), cls)]
    return pl.pallas_call(
        functools.partial(_na_kernel, local=True),
        grid=(steps, b // ne),
        in_specs=in_specs,
        out_specs=pl.BlockSpec((ne, blk, GROUP_W), lambda g, i: (i, g, 0)),
        out_shape=jax.ShapeDtypeStruct((b, n, GROUP_W), BF16),
        compiler_params=_params(2),
        name="na",
    )(p3, p3, p3, p3, vt3, vt3, vt3, pc3, vtc3, bias)


def _na_ctx(pc3, vtc3):
    b, nc, _ = pc3.shape
    ne = NA_ELEMS
    return pl.pallas_call(
        functools.partial(_na_kernel, local=False),
        grid=(b // ne,),
        in_specs=[pl.BlockSpec((ne, nc, GROUP_W), lambda i: (i, 0, COL_NQ)),
                  pl.BlockSpec((ne, nc, GROUP_W), lambda i: (i, 0, COL_NK)),
                  pl.BlockSpec((ne, GROUP_W, nc), lambda i: (i, 0, 0))],
        out_specs=pl.BlockSpec((ne, nc, GROUP_W), lambda i: (i, 0, 0)),
        out_shape=jax.ShapeDtypeStruct((b, nc, GROUP_W), BF16),
        compiler_params=_params(1),
        name="na_ctx",
    )(pc3, pc3, vtc3)


def _na_row_offsets():
    first_row = NA_WIN_ROWS - NA_KH
    rules = [lambda i, a: a - i + NA_KH - 1 if a < NA_KH else None,
             lambda i, a: a - i + NA_KH // 2 - 1 if i <= a < i + NA_KH else None,
             lambda i, a: a - i - 1 if a >= first_row else None]
    return [[[rule(i, a) for a in range(NA_WIN_ROWS)] for i in range(NA_Q_ROWS)] for rule in rules]


def _na_bias_kernel(band_ref, o_ref):
    masked = jnp.full((GRID_W, GRID_W), NEG, F32)
    for kind, per_query_row in enumerate(_na_row_offsets()):
        for a in range(NA_WIN_ROWS):
            blocks = [masked if offs[a] is None else band_ref[0, offs[a]] for offs in per_query_row]
            o_ref[kind, a * GRID_W:(a + 1) * GRID_W, :] = jnp.concatenate(blocks, axis=1)
        o_ref[kind, NA_WIN_ROWS * GRID_W:, :] = jnp.zeros((o_ref.shape[1] - NA_WIN_ROWS * GRID_W, o_ref.shape[2]), F32)


def _na_bias_tables(rpb, nc):
    kc = np.arange(GRID_W)[:, None]
    j = np.arange(GRID_W)[None, :]
    c0 = np.clip(j - NA_KW // 2, 0, GRID_W - NA_KW)
    col_ok = (kc >= c0) & (kc < c0 + NA_KW)
    onehot = (np.arange(2 * NA_KW - 1)[:, None, None] == (kc - j + NA_KW - 1)[None]) & col_ok[None]
    band = jnp.einsum("hdo,okj->hdkj", rpb.astype(F32), jnp.asarray(onehot, F32), precision=lax.Precision.HIGHEST)
    band = jnp.where(col_ok, band * LOG2E, NEG)
    n_off = 2 * NA_KH - 1
    q, k = NA_Q_ROWS * GRID_W, NA_WIN_ROWS * GRID_W
    return pl.pallas_call(
        _na_bias_kernel,
        grid=(N_HEADS,),
        in_specs=[pl.BlockSpec((1, n_off, GRID_W, GRID_W), lambda h: (h, 0, 0, 0))],
        out_specs=pl.BlockSpec((3, k + nc, q), lambda h: (0, 0, h)),
        out_shape=jax.ShapeDtypeStruct((3, k + nc, N_HEADS * q), F32),
        compiler_params=_params(1),
        name="na_bias",
    )(band)


def _gmlp_kernel(u_ref, v_ref, ws_ref, bias_ref, o_ref, *, chunks_per_step):
    n = u_ref.shape[1]
    lane = _lane()
    step_rows = chunks_per_step * CHUNK
    for s in range(n // step_rows):
        base = s * step_rows
        vcat = jnp.concatenate(
            [v_ref[0, base + c * CHUNK: base + (c + 1) * CHUNK, :] for c in range(chunks_per_step)], axis=1)
        r = _dot(ws_ref[...], vcat)
        for c in range(chunks_per_step):
            mix = r[0:CHUNK, c * GROUP_W:(c + 1) * GROUP_W]
            for g in range(1, N_HEADS):
                mix = jnp.where(lane >= g * HEAD_DIM, r[g * CHUNK:(g + 1) * CHUNK, c * GROUP_W:(c + 1) * GROUP_W], mix)
            rows = slice(base + c * CHUNK, base + (c + 1) * CHUNK)
            o_ref[0, rows, :] = (u_ref[0, rows, :].astype(F32) * (mix + bias_ref[...])).astype(BF16)


def _gmlp(p3, ws_stack, bias_full, name):
    b, n, _ = p3.shape
    cps = min(4, n // CHUNK)
    return pl.pallas_call(
        functools.partial(_gmlp_kernel, chunks_per_step=cps),
        grid=(b,),
        in_specs=[pl.BlockSpec((1, n, GROUP_W), lambda i: (i, 0, COL_GU)),
                  pl.BlockSpec((1, n, GROUP_W), lambda i: (i, 0, COL_GV)),
                  _resident((N_HEADS * CHUNK, CHUNK)),
                  _resident((CHUNK, GROUP_W))],
        out_specs=pl.BlockSpec((1, n, GROUP_W), lambda i: (i, 0, 0)),
        out_shape=jax.ShapeDtypeStruct((b, n, GROUP_W), BF16),
        compiler_params=_params(1),
        name=name,
    )(p3, p3, ws_stack, bias_full)


def _mix_ffn_kernel(*refs, tiles_per_seq, seq_rows):
    y_refs, (xp_ref, xm_ref, xn_ref) = refs[:12], refs[12:15]
    mod_ref, g_ref, wo_ref, wu_ref, cw_ref, cb_ref, wd_ref, o_ref, xs_ref, hs_ref, ds_ref, act_ref = refs[15:]
    tm = xm_ref.shape[0]
    groups = tm // 8
    pos = pl.program_id(0) % tiles_per_seq

    def mixed(y_parts, x):
        return x + mod_ref[0, 2:3, :] * _dot(jnp.concatenate(y_parts, axis=1), wo_ref[...])

    def prenorm(x):
        ms = jnp.mean(x * x, axis=-1, keepdims=True)
        h = x * lax.rsqrt(ms + EPS) * g_ref[...]
        return h * (1.0 + mod_ref[0, 4:5, :]) + mod_ref[0, 3:4, :]

    x_main = mixed([y_refs[3 * j + 1][...] for j in range(4)], xm_ref[...])
    xs_ref[...] = x_main
    h_main = prenorm(x_main)
    slabs = hs_ref.shape[0]
    pitch = hs_ref.shape[1] // 8
    for c in range(slabs):
        for s in range(8):
            hs_ref[c, s * pitch:s * pitch + groups, :] = h_main[s * groups:(s + 1) * groups, c * LANES:(c + 1) * LANES]
    h_halo = prenorm(mixed([jnp.concatenate([y_refs[3 * j][...], y_refs[3 * j + 2][...]], axis=0) for j in range(4)],
                           jnp.concatenate([xp_ref[...], xn_ref[...]], axis=0)))
    before = jnp.where(pos != 0, h_halo[HALO - 1:HALO], 0.0)
    after = jnp.where(pos != tiles_per_seq - 1, h_halo[HALO:HALO + 1], 0.0)
    row = lax.broadcasted_iota(jnp.int32, (HALO, 1), 0)
    h_edge = jnp.where(row == 0, before, jnp.where(row == HALO - 1, after, 0.0))
    h_perm = jnp.concatenate(
        [jnp.concatenate([hs_ref[c, pl.ds(j, 8, stride=pitch), :] for c in range(slabs)], axis=1)
         for j in range(groups)], axis=0)
    z_all = _dot(jnp.concatenate([h_perm, h_edge], axis=0).astype(BF16), wu_ref[...])

    sub = lax.broadcasted_iota(jnp.int32, (8, FF_CHUNK), 0)
    inner_starts = [s for s in range(1, 8) if (s * groups) % seq_rows == 0]
    starts_seq = functools.reduce(jnp.logical_or, [sub == s for s in inner_starts], sub < 0)
    ends_seq = functools.reduce(jnp.logical_or, [sub == s - 1 for s in inner_starts], sub < 0)

    def conv(lo):
        cols = slice(lo, lo + FF_CHUNK)
        z = z_all[0:tm, cols]
        edge = z_all[tm:tm + HALO, cols]
        first = jnp.where(sub == 0, edge[0:8], jnp.where(starts_seq, 0.0, pltpu.roll(z[tm - 8:tm], 1, 0)))
        last = jnp.where(sub == 7, edge[HALO - 8:HALO], jnp.where(ends_seq, 0.0, pltpu.roll(z[0:8], 7, 0)))
        return (jnp.concatenate([first, z[0:tm - 8]], axis=0) * cw_ref[0:1, cols] + z * cw_ref[1:2, cols]
                + jnp.concatenate([z[8:tm], last], axis=0) * cw_ref[2:3, cols] + cb_ref[:, cols])

    for c in range(D_FF // FF_CHUNK):
        g = conv(c * FF_CHUNK)
        v = conv(D_FF + c * FF_CHUNK)
        act_ref[:, c * FF_CHUNK:(c + 1) * FF_CHUNK] = (g / (1.0 + jnp.exp(-g)) * v).astype(BF16)
    down = _dot(act_ref[...], wd_ref[...])
    for j in range(groups):
        for c in range(slabs):
            ds_ref[c, pl.ds(j, 8, stride=pitch), :] = down[j * 8:(j + 1) * 8, c * LANES:(c + 1) * LANES]
    down = jnp.concatenate(
        [jnp.concatenate([ds_ref[c, s * pitch:s * pitch + groups, :] for s in range(8)], axis=0)
         for c in range(slabs)], axis=1)
    o_ref[...] = xs_ref[...] + mod_ref[0, 5:6, :] * down


def _mix_ffn(ys, x2d, mod, g_row, wo_bf, wu_bf, conv_w, conv_b, wd_bf, li, n, tm, ctx_stream):
    rows = x2d.shape[0]
    assert n % tm == 0 or (tm % n == 0 and n % (tm // 8) == 0)
    tpb = max(1, n // tm)
    per_tile = tm // HALO
    last_halo = rows // HALO - 1
    mod_row = (lambda i: (ADA_ROWS // 2, 0, 0)) if ctx_stream else (lambda i: (i // tpb, 0, 0))

    def halo_specs(width):
        return [pl.BlockSpec((HALO, width), lambda i: (jnp.maximum(i * per_tile - 1, 0), 0)),
                pl.BlockSpec((tm, width), lambda i: (i, 0)),
                pl.BlockSpec((HALO, width), lambda i: (jnp.minimum((i + 1) * per_tile, last_halo), 0))]

    y2d = [y.reshape(rows, GROUP_W) for y in ys]
    return pl.pallas_call(
        functools.partial(_mix_ffn_kernel, tiles_per_seq=tpb, seq_rows=n),
        grid=(rows // tm,),
        in_specs=halo_specs(GROUP_W) * 4 + halo_specs(D_MODEL) + [
            pl.BlockSpec((1, 6, D_MODEL), mod_row),
            _resident((1, D_MODEL)),
            _resident_layer((D_MODEL, D_MODEL), li),
            _resident_layer((D_MODEL, 2 * D_FF), li),
            _resident((3, 2 * D_FF)),
            _resident((1, 2 * D_FF)),
            _resident_layer((D_FF, D_MODEL), li),
        ],
        out_specs=pl.BlockSpec((tm, D_MODEL), lambda i: (i, 0)),
        out_shape=jax.ShapeDtypeStruct((rows, D_MODEL), F32),
        scratch_shapes=[pltpu.VMEM((tm, D_MODEL), F32)]
        + [pltpu.VMEM((D_MODEL // LANES, 8 * (tm // 8 + 8), LANES), F32)] * 2
        + [pltpu.VMEM((tm, D_FF), BF16)],
        compiler_params=_params(1),
        name="mix_ffn_ctx" if ctx_stream else "mix_ffn",
    )(*[a for y in y2d for a in (y, y, y)], x2d, x2d, x2d, mod, g_row, wo_bf, wu_bf, conv_w,
      conv_b.reshape(1, 2 * D_FF), wd_bf)


def _dft_tables(n):
    lo_n = 64
    half = n // 2
    t = jnp.arange(half, dtype=jnp.int32)

    def table(k):
        ang = ((k[:, None] * t[None, :]) % n).astype(F32) * (2.0 * math.pi / n)
        return jnp.cos(ang), jnp.sin(ang)

    (ch, sh), (cl, sl) = table(jnp.arange(n // lo_n, dtype=jnp.int32) * lo_n), table(jnp.arange(lo_n, dtype=jnp.int32))
    cn = (ch[:, None] * cl[None] - sh[:, None] * sl[None]).reshape(n, half).astype(BF16)
    sn = (sh[:, None] * cl[None] + ch[:, None] * sl[None]).reshape(n, half).astype(BF16)
    rb = min(GROUP_W, half)
    r = np.arange(rb)
    rev = np.zeros((rb, 2 * rb), np.float32)
    rev[r[1:], rb - r[1:]] = 1.0
    rev[0, rb] = 1.0
    c = np.arange(HEAD_DIM)
    angc = 2.0 * np.pi * ((c[:, None] * c[None, :]) % HEAD_DIM) / HEAD_DIM
    norm = 1.0 / math.sqrt(n * HEAD_DIM)
    eye = np.eye(N_HEADS)
    bdc = jnp.asarray(np.kron(eye, np.cos(angc) * norm), F32).astype(BF16)
    bds = jnp.asarray(np.kron(eye, np.sin(angc) * norm), F32).astype(BF16)
    return bdc, bds, cn, sn, jnp.asarray(rev, F32).astype(BF16)


def _rope_tables(n):
    n_freq = DIFF_QK // 4
    freqs = ROPE_BASE ** (-jnp.arange(n_freq, dtype=F32) / n_freq)
    t = jnp.arange(n)
    row = (t // GRID_W).astype(F32)
    col = (t % GRID_W).astype(F32)
    ang = jnp.concatenate([row[:, None] * freqs, col[:, None] * freqs], axis=-1)
    cos, sin = jnp.cos(ang), jnp.sin(ang)
    reps = GROUP_W // DIFF_QK
    return jnp.tile(jnp.concatenate([cos, cos], axis=-1), (1, reps)), jnp.tile(jnp.concatenate([-sin, sin], axis=-1), (1, reps))


def _group_mean_matrix(group):
    return jnp.asarray(np.kron(np.eye(GROUP_W // group), np.full((group, group), 1.0 / group)), F32).astype(BF16)


def kernel(x, c, ctx, c_ctx, w_ada, b_ada, g_mix, g_ffn, w_in, w_out, diff_qn, diff_kn, diff_lam, diff_subln, na_qn,
           na_kn, na_rpb, gmlp_norm, gmlp_ws, gmlp_b, ffn_up, ffn_conv, ffn_conv_b, ffn_down):
    b, n, d = x.shape
    nc = ctx.shape[1]
    assert (d, n % (NA_Q_ROWS * GRID_W), nc % CHUNK, b < ADA_ROWS // 2 + 1) == (D_MODEL, 0, 0, True)
    tm = 512
    tq = 512

    c_all = jnp.zeros((ADA_ROWS, d), F32).at[:b].set(c).at[ADA_ROWS // 2].set(c_ctx)
    mods = _ada(c_all, w_ada, b_ada).reshape(DEPTH, ADA_ROWS, 6, d)

    g32 = _group_mean_matrix(DIFF_QK)
    g64 = _group_mean_matrix(HEAD_DIM)
    rope_tabs = _rope_tables(n)
    dft_lat = _dft_tables(n)
    dft_ctx = _dft_tables(nc)

    w_in_kernel_order = jnp.concatenate([w_in[..., r * GROUP_W:(r + 1) * GROUP_W] for r in W_ORDER], axis=-1)
    w_in_bf, w_out_bf, wu_bf, wd_bf = (w.astype(BF16) for w in (w_in_kernel_order, w_out, ffn_up, ffn_down))

    x2d = x.reshape(b * n, d)
    c2d = ctx.reshape(b * nc, d)
    for li in range(DEPTH):
        ctx_out = li < DEPTH - 1
        lam_init = 0.8 - 0.6 * math.exp(-0.3 * li)
        lf = diff_lam[li].astype(F32)
        lam = (jnp.exp(jnp.sum(lf[0] * lf[1])) - jnp.exp(jnp.sum(lf[2] * lf[3])) + lam_init).reshape(1)
        mod = mods[li]
        gains = jnp.zeros((8, GROUP_W), F32)
        gains = gains.at[0].set(jnp.tile(diff_qn[li], GROUP_W // DIFF_QK) * (DIFF_QK ** -0.5 * LOG2E))
        gains = gains.at[1].set(jnp.tile(diff_kn[li], GROUP_W // DIFF_QK))
        gains = gains.at[2].set(jnp.tile(na_qn[li], N_HEADS) * (HEAD_DIM ** -0.5 * LOG2E))
        gains = gains.at[3].set(jnp.tile(na_kn[li], N_HEADS))
        gains = gains.at[4].set(gmlp_norm[li])
        sub_row = (jnp.tile(diff_subln[li], N_HEADS) * (1.0 - lam_init)).reshape(1, GROUP_W)
        g_mix_row = g_mix[li].reshape(1, d)
        g_ffn_row = g_ffn[li].reshape(1, d)
        ws_stack = gmlp_ws[li].reshape(N_HEADS * CHUNK, CHUNK).astype(BF16)
        gbias = jnp.repeat(gmlp_b[li].T, HEAD_DIM, axis=1)
        bias_tab = _na_bias_tables(na_rpb[li], nc)

        p2d, dvt, nvt = _inproj(x2d, mod, g_mix_row, w_in_bf, li, gains, g32, g64, rope_tabs, n, 2 * tm, False)
        pc2d, dvtc, nvtc = _inproj(c2d, mod, g_mix_row, w_in_bf, li, gains, g32, g64, None, nc, nc, True)
        p3 = p2d.reshape(b, n, P_W)
        pc3 = pc2d.reshape(b, nc, P_W)

        ys = [_fourier(p3, *dft_lat, "fourier"),
              _diff(lam, p3, dvt, pc3, dvtc, sub_row, tq, True),
              _na(p3, nvt, pc3, nvtc, bias_tab),
              _gmlp(p3, ws_stack, gbias, "gmlp")]
        ffn_w = (g_ffn_row, w_out_bf, wu_bf, ffn_conv[li], ffn_conv_b[li], wd_bf, li)
        x2d = _mix_ffn(ys, x2d, mod, *ffn_w, n, tm, False)

        if ctx_out:
            ycs = [_fourier(pc3, *dft_ctx, "fourier_ctx"),
                   _diff(lam, None, None, pc3, dvtc, sub_row, nc, False),
                   _na_ctx(pc3, nvtc),
                   _gmlp(pc3, ws_stack, gbias, "gmlp_ctx")]
            c2d = _mix_ffn(ycs, c2d, mod, *ffn_w, nc, tm, True)
    return x2d.reshape(b, n, d)
```

```python
import functools
import math

import jax
import jax.numpy as jnp
import numpy as np
from jax import lax
from jax.experimental import pallas as pl
from jax.experimental.pallas import tpu as pltpu

F32 = jnp.float32
BF16 = jnp.bfloat16

D_MODEL = 1024
DEPTH = 4
GRID_W = 64
GROUP_W = 256
HEAD_DIM = 64
N_HEADS = 4
DIFF_QK = 32
NA_KH = 8
NA_KW = 16
CHUNK = 128
D_FF = 2816
ROPE_BASE = 10000.0
EPS = 1e-6
IN_W = 9 * GROUP_W

REF_F, REF_DQ, REF_DK, REF_DV, REF_NQ, REF_NK, REF_NV, REF_GU, REF_GV = range(9)
W_ORDER = (REF_GV, REF_DQ, REF_DK, REF_NQ, REF_NK, REF_GU, REF_DV, REF_NV, REF_F)
W_GV, W_DQ, W_DK, W_NQ, W_NK, W_GU, W_DV, W_NV, W_F = range(9)
COL_F, COL_DQ, COL_DK, COL_NQ, COL_NK, COL_GU, COL_GV = range(7)
P_W = 7 * GROUP_W

VT_ROWS = HEAD_DIM + 16
DIFF_SCORE_COLS = 2048
PAIR_ROWS = 16
SUM_FLOOR = 2.0 ** -80
NA_Q_ROWS = 4
NA_ELEMS = 4
NA_WIN_ROWS = 12
NEG = -1e30
LOG2E = math.log2(math.e)

ADA_ROWS = 16
VMEM_LIMIT = 56 * 1024 * 1024
TOKEN_TILE = 512
DIFF_QUERY_TILE = 512
LANES = 128
HALO = 16
FF_CHUNK = 256


def _dot(a, b):
    return jnp.dot(a, b, preferred_element_type=F32)


def _dot_nt(a, b):
    return lax.dot_general(a, b, (((1,), (1,)), ((), ())), preferred_element_type=F32)


def _resident(shape):
    nd = len(shape)
    return pl.BlockSpec(shape, lambda *_: (0,) * nd, pipeline_mode=pl.Buffered(1))


def _resident_layer(shape, li):
    nd = len(shape)
    return pl.BlockSpec((None,) + tuple(shape), lambda *_: (li,) + (0,) * nd, pipeline_mode=pl.Buffered(1))


def _params(n_axes):
    return pltpu.CompilerParams(dimension_semantics=("arbitrary",) * n_axes, vmem_limit_bytes=VMEM_LIMIT)


def _lane(width=GROUP_W):
    return lax.broadcasted_iota(jnp.int32, (1, width), 1)


def _ada_kernel(c_ref, w_ref, b_ref, o_ref):
    c = c_ref[...]
    s = c / (1.0 + jnp.exp(-c))
    w = w_ref[0]
    s_hi = s.astype(BF16)
    s_lo = (s - s_hi.astype(F32)).astype(BF16)
    w_hi = w.astype(BF16)
    w_lo = (w - w_hi.astype(F32)).astype(BF16)
    o_ref[0] = _dot(s_hi, w_hi) + _dot(s_lo, w_hi) + _dot(s_hi, w_lo) + b_ref[0]


def _ada(c_all, w_ada, b_ada):
    tn = 1536
    out_w = 6 * D_MODEL
    return pl.pallas_call(
        _ada_kernel,
        grid=(DEPTH, out_w // tn),
        in_specs=[
            pl.BlockSpec((ADA_ROWS, D_MODEL), lambda l, j: (0, 0)),
            pl.BlockSpec((1, D_MODEL, tn), lambda l, j: (l, 0, j)),
            pl.BlockSpec((1, 1, tn), lambda l, j: (l, 0, j)),
        ],
        out_specs=pl.BlockSpec((1, ADA_ROWS, tn), lambda l, j: (l, 0, j)),
        out_shape=jax.ShapeDtypeStruct((DEPTH, ADA_ROWS, out_w), F32),
        compiler_params=_params(2),
        name="ada",
    )(c_all, w_ada, b_ada.reshape(DEPTH, 1, out_w))


def _gelu(x):
    return 0.5 * x * (1.0 + jnp.tanh(math.sqrt(2.0 / math.pi) * (x + 0.044715 * (x * x * x))))


def _inproj_kernel(*refs, rope):
    if rope:
        x_ref, mod_ref, g_ref, w_ref, gains_ref, g32_ref, g64_ref, cos_ref, sin_ref, o_ref, dvt_ref, nvt_ref = refs
    else:
        x_ref, mod_ref, g_ref, w_ref, gains_ref, g32_ref, g64_ref, o_ref, dvt_ref, nvt_ref = refs
    x = x_ref[...]
    ms = jnp.mean(x * x, axis=-1, keepdims=True)
    h = x * lax.rsqrt(ms + EPS) * g_ref[...]
    h = h * (1.0 + mod_ref[0, 1:2, :]) + mod_ref[0, 0:1, :]
    p = _dot(h.astype(BF16), w_ref[...])

    def col(j):
        return p[:, j * GROUP_W:(j + 1) * GROUP_W]

    def put(j, v):
        o_ref[:, j * GROUP_W:(j + 1) * GROUP_W] = v.astype(BF16)

    def group_norm(v, gmat_ref, gain_row):
        gms = _dot((v * v).astype(BF16), gmat_ref[...])
        return v * lax.rsqrt(gms + EPS) * gains_ref[gain_row:gain_row + 1, :]

    def rotary(v):
        if not rope:
            return v
        first_half = (_lane() % DIFF_QK) < (DIFF_QK // 2)
        partner = jnp.where(first_half, pltpu.roll(v, GROUP_W - DIFF_QK // 2, 1), pltpu.roll(v, DIFF_QK // 2, 1))
        return v * cos_ref[...] + partner * sin_ref[...]

    put(COL_GV, group_norm(_gelu(col(W_GV)), g64_ref, 4))
    put(COL_DQ, rotary(group_norm(col(W_DQ), g32_ref, 0)))
    put(COL_DK, rotary(group_norm(col(W_DK), g32_ref, 1)))
    put(COL_NQ, group_norm(col(W_NQ), g64_ref, 2))
    put(COL_NK, group_norm(col(W_NK), g64_ref, 3))
    put(COL_GU, _gelu(col(W_GU)))
    dvt_ref[0] = col(W_DV).T.astype(BF16)
    nvt_ref[0] = col(W_NV).T.astype(BF16)
    put(COL_F, col(W_F))


def _inproj(x2d, mod, g_row, w_bf, li, gains, g32, g64, rope_tabs, n, tm, ctx_stream):
    rows = x2d.shape[0]
    tpb = n // tm
    mod_row = (lambda i: (ADA_ROWS // 2, 0, 0)) if ctx_stream else (lambda i: (i // tpb, 0, 0))
    in_specs = [
        pl.BlockSpec((tm, D_MODEL), lambda i: (i, 0)),
        pl.BlockSpec((1, 6, D_MODEL), mod_row),
        _resident((1, D_MODEL)),
        _resident_layer((D_MODEL, IN_W), li),
        _resident((8, GROUP_W)),
        _resident((GROUP_W, GROUP_W)),
        _resident((GROUP_W, GROUP_W)),
    ]
    args = [x2d, mod, g_row, w_bf, gains, g32, g64]
    if rope_tabs is not None:
        in_specs += [pl.BlockSpec((tm, GROUP_W), lambda i: (i % tpb, 0))] * 2
        args += list(rope_tabs)
    return pl.pallas_call(
        functools.partial(_inproj_kernel, rope=rope_tabs is not None),
        grid=(rows // tm,),
        in_specs=in_specs,
        out_specs=[pl.BlockSpec((tm, P_W), lambda i: (i, 0))]
        + [pl.BlockSpec((1, GROUP_W, tm), lambda i: (i // tpb, 0, i % tpb))] * 2,
        out_shape=[jax.ShapeDtypeStruct((rows, P_W), BF16)]
        + [jax.ShapeDtypeStruct((rows // n, GROUP_W, n), BF16)] * 2,
        compiler_params=_params(1),
        name="inproj_ctx" if ctx_stream else "inproj",
    )(*args)


def _fourier_kernel(p_ref, bdc_ref, bds_ref, cn_ref, sn_ref, rev_ref, o_ref):
    n = p_ref.shape[1]
    half = n // 2
    rb = rev_ref.shape[0]
    nb = n // rb
    mirrored = []
    for blk in range(half // rb):
        src = nb - blk - 1
        nxt = (src + 1) % nb
        pair = jnp.concatenate([p_ref[0, src * rb:(src + 1) * rb, :], p_ref[0, nxt * rb:(nxt + 1) * rb, :]], axis=0)
        mirrored.append(_dot(rev_ref[...], pair))
    mirrored = jnp.concatenate(mirrored, axis=0)
    x = p_ref[0, 0:half, :].astype(F32)
    first = lax.broadcasted_iota(jnp.int32, x.shape, 0) == 0
    fold_c = jnp.where(first, x, x + mirrored).astype(BF16)
    fold_s = (x - mirrored).astype(BF16)
    a = _dot(fold_c, bdc_ref[...]).astype(BF16)
    b = _dot(fold_s, bds_ref[...]).astype(BF16)
    mid = _dot(p_ref[0, half:half + 16, :], bdc_ref[...])[0:1]
    out = _dot(cn_ref[...], a) - _dot(sn_ref[...], b)
    odd = (lax.broadcasted_iota(jnp.int32, out.shape, 0) & 1) == 1
    o_ref[0] = (out + jnp.where(odd, -mid, mid)).astype(BF16)


def _fourier(p3, bdc, bds, cn, sn, rev, name):
    b, n, _ = p3.shape
    return pl.pallas_call(
        _fourier_kernel,
        grid=(b,),
        in_specs=[
            pl.BlockSpec((1, n, GROUP_W), lambda i: (i, 0, COL_F)),
            _resident((GROUP_W, GROUP_W)),
            _resident((GROUP_W, GROUP_W)),
            _resident((n, n // 2)),
            _resident((n, n // 2)),
            _resident(rev.shape),
        ],
        out_specs=pl.BlockSpec((1, n, GROUP_W), lambda i: (i, 0, 0)),
        out_shape=jax.ShapeDtypeStruct((b, n, GROUP_W), BF16),
        compiler_params=_params(1),
        name=name,
    )(p3, bdc, bds, cn, sn, rev)


def _diff_kernel(*refs, has_lat):
    if has_lat:
        lam_ref, q_ref, kl_ref, vl_ref, kc_ref, vc_ref, sub_ref, sel_ref, o_ref, k_ref, vt_ref, kmax_ref = refs
        key_refs, val_refs = [kl_ref, kc_ref], [vl_ref, vc_ref]
    else:
        lam_ref, q_ref, kc_ref, vc_ref, sub_ref, sel_ref, o_ref, k_ref, vt_ref, kmax_ref = refs
        key_refs, val_refs = [kc_ref], [vc_ref]

    def pair_sq_norms(x):
        xf = x.astype(F32)
        return _dot_nt(sel_ref[...], (xf * xf).astype(BF16))

    @pl.when(pl.program_id(1) == 0)
    def _():
        lo = 0
        for kr, vr in zip(key_refs, val_refs):
            cnt = kr.shape[1]
            k_ref[lo:lo + cnt, :] = kr[0]
            for h in range(N_HEADS):
                vt_ref[h, 0:HEAD_DIM, lo:lo + cnt] = vr[0, h * HEAD_DIM:(h + 1) * HEAD_DIM, :]
                vt_ref[h, HEAD_DIM:VT_ROWS, lo:lo + cnt] = jnp.ones((VT_ROWS - HEAD_DIM, cnt), BF16)
            lo += cnt
        kmax_ref[...] = jnp.broadcast_to(pair_sq_norms(k_ref[...]).max(axis=1, keepdims=True), kmax_ref.shape)

    q = q_ref[0]
    lam = lam_ref[0]
    lane = _lane()
    bound = jnp.sqrt(pair_sq_norms(q) * kmax_ref[:, 0:1])

    def attend(shift_fn):
        outs = []
        tq = q.shape[0]
        heads_per_dot = min(N_HEADS, DIFF_SCORE_COLS // (2 * tq))
        for h0 in range(0, N_HEADS, heads_per_dot):
            pairs = range(2 * h0, 2 * (h0 + heads_per_dot))
            qm = jnp.concatenate(
                [jnp.where((lane >= pair * DIFF_QK) & (lane < (pair + 1) * DIFF_QK), q, jnp.zeros_like(q))
                 for pair in pairs], axis=0)
            s = _dot_nt(k_ref[...], qm)
            e = jnp.exp2(s - shift_fn(s, pairs)).astype(BF16)
            for i in range(heads_per_dot):
                o = _dot(vt_ref[h0 + i], e[:, 2 * i * tq:2 * (i + 1) * tq])
                outs += [o[:, 0:tq], o[:, tq:]]
        return outs

    def finish(outs):
        heads = []
        for h in range(N_HEADS):
            o1, o2 = outs[2 * h], outs[2 * h + 1]
            o = (o1[0:HEAD_DIM] * (1.0 / o1[HEAD_DIM:HEAD_DIM + 1])
                 - o2[0:HEAD_DIM] * (lam / o2[HEAD_DIM:HEAD_DIM + 1]))
            heads.append(o * lax.rsqrt(jnp.mean(o * o, axis=0, keepdims=True) + EPS))
        o_ref[0] = (jnp.concatenate(heads, axis=0) * sub_ref[...]).T.astype(BF16)

    outs = attend(lambda s, pairs: jnp.concatenate([bound[p:p + 1] for p in pairs], axis=1))
    finish(outs)
    smallest = outs[0][HEAD_DIM:HEAD_DIM + 1]
    for o in outs[1:]:
        smallest = jnp.minimum(smallest, o[HEAD_DIM:HEAD_DIM + 1])

    @pl.when(jnp.min(smallest) < SUM_FLOOR)
    def _():
        finish(attend(lambda s, pairs: s.max(axis=0, keepdims=True)))


def _diff(lam, p3, vt3, pc3, vtc3, sub_row, tq, has_lat):
    q_src = p3 if has_lat else pc3
    sub_t = jnp.broadcast_to(sub_row.reshape(GROUP_W, 1), (GROUP_W, tq))
    b, n, _ = q_src.shape
    nc = pc3.shape[1]
    smem = pl.BlockSpec(memory_space=pltpu.SMEM)
    in_specs = [smem, pl.BlockSpec((1, tq, GROUP_W), lambda i, j: (i, j, COL_DQ))]
    args = [lam, q_src]
    if has_lat:
        in_specs += [pl.BlockSpec((1, n, GROUP_W), lambda i, j: (i, 0, COL_DK)),
                     pl.BlockSpec((1, GROUP_W, n), lambda i, j: (i, 0, 0))]
        args += [p3, vt3]
    in_specs += [pl.BlockSpec((1, nc, GROUP_W), lambda i, j: (i, 0, COL_DK)),
                 pl.BlockSpec((1, GROUP_W, nc), lambda i, j: (i, 0, 0)),
                 _resident((GROUP_W, tq)),
                 _resident((PAIR_ROWS, GROUP_W))]
    pair_sel = np.arange(PAIR_ROWS)[:, None] == np.arange(GROUP_W)[None, :] // DIFF_QK
    args += [pc3, vtc3, sub_t, jnp.asarray(pair_sel, F32).astype(BF16)]
    return pl.pallas_call(
        functools.partial(_diff_kernel, has_lat=has_lat),
        grid=(b, n // tq),
        in_specs=in_specs,
        out_specs=pl.BlockSpec((1, tq, GROUP_W), lambda i, j: (i, j, 0)),
        out_shape=jax.ShapeDtypeStruct((b, n, GROUP_W), BF16),
        scratch_shapes=[pltpu.VMEM(((n if has_lat else 0) + nc, GROUP_W), BF16),
                        pltpu.VMEM((N_HEADS, VT_ROWS, (n if has_lat else 0) + nc), BF16),
                        pltpu.VMEM((PAIR_ROWS, 128), F32)],
        compiler_params=_params(2),
        name="diff" if has_lat else "diff_ctx",
    )(*args)


def _na_kernel(*refs, local):
    if local:
        q_ref, k0_ref, k1_ref, k2_ref, v0_ref, v1_ref, v2_ref, kc_ref, vc_ref, bias_ref, o_ref = refs
        key_refs = [k0_ref, k1_ref, k2_ref, kc_ref]
        vt_refs = [v0_ref, v1_ref, v2_ref, vc_ref]
    else:
        q_ref, kc_ref, vc_ref, o_ref = refs
        key_refs = [kc_ref]
        vt_refs = [vc_ref]
    n_elems, blk = q_ref.shape[0], q_ref.shape[1]
    lane = _lane()

    def scores(el):
        q = q_ref[el]
        q_heads = jnp.concatenate(
            [jnp.where((lane >= h * HEAD_DIM) & (lane < (h + 1) * HEAD_DIM), q, jnp.zeros_like(q))
             for h in range(N_HEADS)], axis=0)
        k_all = jnp.concatenate([r[el] for r in key_refs], axis=0)
        return _dot_nt(k_all, q_heads)

    def weights(s):
        if local:
            s = s + bias_ref[0]
        return jnp.exp2(s - s.max(axis=0, keepdims=True)).astype(BF16)

    def attend(el, e):
        vt_all = jnp.concatenate([r[el] for r in vt_refs], axis=1)
        ones = jnp.ones((VT_ROWS - HEAD_DIM, vt_all.shape[1]), BF16)
        heads = []
        for h in range(N_HEADS):
            vt_h = jnp.concatenate([vt_all[h * HEAD_DIM:(h + 1) * HEAD_DIM], ones], axis=0)
            o = _dot(vt_h, e[:, h * blk:(h + 1) * blk])
            heads.append(o[0:HEAD_DIM] * (1.0 / o[HEAD_DIM:HEAD_DIM + 1]))
        o_ref[el] = jnp.concatenate(heads, axis=0).T.astype(BF16)

    es = [weights(s) for s in [scores(el) for el in range(n_elems)]]
    for el in range(n_elems):
        attend(el, es[el])


def _na(p3, vt3, pc3, vtc3, bias):
    b, n, _ = p3.shape
    nc = pc3.shape[1]
    ne = NA_ELEMS
    blk = NA_Q_ROWS * GRID_W
    steps = n // blk
    last_win = n // blk - NA_WIN_ROWS // NA_Q_ROWS

    def first(g):
        return jnp.clip(g - 1, 0, last_win)

    def cls(g, i):
        return ((g > 0).astype(jnp.int32) + (g == steps - 1).astype(jnp.int32), 0, 0)

    in_specs = [pl.BlockSpec((ne, blk, GROUP_W), lambda g, i: (i, g, COL_NQ))]
    in_specs += [pl.BlockSpec((ne, blk, GROUP_W), lambda g, i, j=j: (i, first(g) + j, COL_NK)) for j in range(3)]
    in_specs += [pl.BlockSpec((ne, GROUP_W, blk), lambda g, i, j=j: (i, 0, first(g) + j)) for j in range(3)]
    in_specs += [pl.BlockSpec((ne, nc, GROUP_W), lambda g, i: (i, 0, COL_NK)),
                 pl.BlockSpec((ne, GROUP_W, nc), lambda g, i: (i, 0, 0)),
                 pl.BlockSpec((1, 3 * blk + nc, N_HEADS * blk), cls)]
    return pl.pallas_call(
        functools.partial(_na_kernel, local=True),
        grid=(steps, b // ne),
        in_specs=in_specs,
        out_specs=pl.BlockSpec((ne, blk, GROUP_W), lambda g, i: (i, g, 0)),
        out_shape=jax.ShapeDtypeStruct((b, n, GROUP_W), BF16),
        compiler_params=_params(2),
        name="na",
    )(p3, p3, p3, p3, vt3, vt3, vt3, pc3, vtc3, bias)


def _na_ctx(pc3, vtc3):
    b, nc, _ = pc3.shape
    ne = NA_ELEMS
    return pl.pallas_call(
        functools.partial(_na_kernel, local=False),
        grid=(b // ne,),
        in_specs=[pl.BlockSpec((ne, nc, GROUP_W), lambda i: (i, 0, COL_NQ)),
                  pl.BlockSpec((ne, nc, GROUP_W), lambda i: (i, 0, COL_NK)),
                  pl.BlockSpec((ne, GROUP_W, nc), lambda i: (i, 0, 0))],
        out_specs=pl.BlockSpec((ne, nc, GROUP_W), lambda i: (i, 0, 0)),
        out_shape=jax.ShapeDtypeStruct((b, nc, GROUP_W), BF16),
        compiler_params=_params(1),
        name="na_ctx",
    )(pc3, pc3, vtc3)


def _na_row_offsets():
    first_row = NA_WIN_ROWS - NA_KH
    rules = [lambda i, a: a - i + NA_KH - 1 if a < NA_KH else None,
             lambda i, a: a - i + NA_KH // 2 - 1 if i <= a < i + NA_KH else None,
             lambda i, a: a - i - 1 if a >= first_row else None]
    return [[[rule(i, a) for a in range(NA_WIN_ROWS)] for i in range(NA_Q_ROWS)] for rule in rules]


def _na_bias_kernel(band_ref, o_ref):
    masked = jnp.full((GRID_W, GRID_W), NEG, F32)
    for kind, per_query_row in enumerate(_na_row_offsets()):
        for a in range(NA_WIN_ROWS):
            blocks = [masked if offs[a] is None else band_ref[0, offs[a]] for offs in per_query_row]
            o_ref[kind, a * GRID_W:(a + 1) * GRID_W, :] = jnp.concatenate(blocks, axis=1)
        o_ref[kind, NA_WIN_ROWS * GRID_W:, :] = jnp.zeros((o_ref.shape[1] - NA_WIN_ROWS * GRID_W, o_ref.shape[2]), F32)


def _na_bias_tables(rpb, nc):
    kc = np.arange(GRID_W)[:, None]
    j = np.arange(GRID_W)[None, :]
    c0 = np.clip(j - NA_KW // 2, 0, GRID_W - NA_KW)
    col_ok = (kc >= c0) & (kc < c0 + NA_KW)
    onehot = (np.arange(2 * NA_KW - 1)[:, None, None] == (kc - j + NA_KW - 1)[None]) & col_ok[None]
    band = jnp.einsum("hdo,okj->hdkj", rpb.astype(F32), jnp.asarray(onehot, F32), precision=lax.Precision.HIGHEST)
    band = jnp.where(col_ok, band * LOG2E, NEG)
    n_off = 2 * NA_KH - 1
    q, k = NA_Q_ROWS * GRID_W, NA_WIN_ROWS * GRID_W
    return pl.pallas_call(
        _na_bias_kernel,
        grid=(N_HEADS,),
        in_specs=[pl.BlockSpec((1, n_off, GRID_W, GRID_W), lambda h: (h, 0, 0, 0))],
        out_specs=pl.BlockSpec((3, k + nc, q), lambda h: (0, 0, h)),
        out_shape=jax.ShapeDtypeStruct((3, k + nc, N_HEADS * q), F32),
        compiler_params=_params(1),
        name="na_bias",
    )(band)


def _gmlp_kernel(u_ref, v_ref, ws_ref, bias_ref, o_ref, *, chunks_per_step):
    n = u_ref.shape[1]
    lane = _lane()
    step_rows = chunks_per_step * CHUNK
    for s in range(n // step_rows):
        base = s * step_rows
        vcat = jnp.concatenate(
            [v_ref[0, base + c * CHUNK: base + (c + 1) * CHUNK, :] for c in range(chunks_per_step)], axis=1)
        r = _dot(ws_ref[...], vcat)
        for c in range(chunks_per_step):
            mix = r[0:CHUNK, c * GROUP_W:(c + 1) * GROUP_W]
            for g in range(1, N_HEADS):
                mix = jnp.where(lane >= g * HEAD_DIM, r[g * CHUNK:(g + 1) * CHUNK, c * GROUP_W:(c + 1) * GROUP_W], mix)
            rows = slice(base + c * CHUNK, base + (c + 1) * CHUNK)
            o_ref[0, rows, :] = (u_ref[0, rows, :].astype(F32) * (mix + bias_ref[...])).astype(BF16)


def _gmlp(p3, ws_stack, bias_full, name):
    b, n, _ = p3.shape
    cps = min(4, n // CHUNK)
    return pl.pallas_call(
        functools.partial(_gmlp_kernel, chunks_per_step=cps),
        grid=(b,),
        in_specs=[pl.BlockSpec((1, n, GROUP_W), lambda i: (i, 0, COL_GU)),
                  pl.BlockSpec((1, n, GROUP_W), lambda i: (i, 0, COL_GV)),
                  _resident((N_HEADS * CHUNK, CHUNK)),
                  _resident((CHUNK, GROUP_W))],
        out_specs=pl.BlockSpec((1, n, GROUP_W), lambda i: (i, 0, 0)),
        out_shape=jax.ShapeDtypeStruct((b, n, GROUP_W), BF16),
        compiler_params=_params(1),
        name=name,
    )(p3, p3, ws_stack, bias_full)


def _mix_ffn_kernel(*refs, tiles_per_seq, seq_rows):
    y_refs, (xp_ref, xm_ref, xn_ref) = refs[:12], refs[12:15]
    mod_ref, g_ref, wo_ref, wu_ref, cw_ref, cb_ref, wd_ref, o_ref, xs_ref, hs_ref, ds_ref, act_ref = refs[15:]
    tm = xm_ref.shape[0]
    groups = tm // 8
    pos = pl.program_id(0) % tiles_per_seq

    def mixed(y_parts, x):
        return x + mod_ref[0, 2:3, :] * _dot(jnp.concatenate(y_parts, axis=1), wo_ref[...])

    def prenorm(x):
        ms = jnp.mean(x * x, axis=-1, keepdims=True)
        h = x * lax.rsqrt(ms + EPS) * g_ref[...]
        return h * (1.0 + mod_ref[0, 4:5, :]) + mod_ref[0, 3:4, :]

    x_main = mixed([y_refs[3 * j + 1][...] for j in range(4)], xm_ref[...])
    xs_ref[...] = x_main
    h_main = prenorm(x_main)
    slabs = hs_ref.shape[0]
    pitch = hs_ref.shape[1] // 8
    for c in range(slabs):
        for s in range(8):
            hs_ref[c, s * pitch:s * pitch + groups, :] = h_main[s * groups:(s + 1) * groups, c * LANES:(c + 1) * LANES]
    h_halo = prenorm(mixed([jnp.concatenate([y_refs[3 * j][...], y_refs[3 * j + 2][...]], axis=0) for j in range(4)],
                           jnp.concatenate([xp_ref[...], xn_ref[...]], axis=0)))
    before = jnp.where(pos != 0, h_halo[HALO - 1:HALO], 0.0)
    after = jnp.where(pos != tiles_per_seq - 1, h_halo[HALO:HALO + 1], 0.0)
    row = lax.broadcasted_iota(jnp.int32, (HALO, 1), 0)
    h_edge = jnp.where(row == 0, before, jnp.where(row == HALO - 1, after, 0.0))
    h_perm = jnp.concatenate(
        [jnp.concatenate([hs_ref[c, pl.ds(j, 8, stride=pitch), :] for c in range(slabs)], axis=1)
         for j in range(groups)], axis=0)
    z_all = _dot(jnp.concatenate([h_perm, h_edge], axis=0).astype(BF16), wu_ref[...])

    sub = lax.broadcasted_iota(jnp.int32, (8, FF_CHUNK), 0)
    inner_starts = [s for s in range(1, 8) if (s * groups) % seq_rows == 0]
    starts_seq = functools.reduce(jnp.logical_or, [sub == s for s in inner_starts], sub < 0)
    ends_seq = functools.reduce(jnp.logical_or, [sub == s - 1 for s in inner_starts], sub < 0)

    def conv(lo):
        cols = slice(lo, lo + FF_CHUNK)
        z = z_all[0:tm, cols]
        edge = z_all[tm:tm + HALO, cols]
        first = jnp.where(sub == 0, edge[0:8], jnp.where(starts_seq, 0.0, pltpu.roll(z[tm - 8:tm], 1, 0)))
        last = jnp.where(sub == 7, edge[HALO - 8:HALO], jnp.where(ends_seq, 0.0, pltpu.roll(z[0:8], 7, 0)))
        return (jnp.concatenate([first, z[0:tm - 8]], axis=0) * cw_ref[0:1, cols] + z * cw_ref[1:2, cols]
                + jnp.concatenate([z[8:tm], last], axis=0) * cw_ref[2:3, cols] + cb_ref[:, cols])

    for c in range(D_FF // FF_CHUNK):
        g = conv(c * FF_CHUNK)
        v = conv(D_FF + c * FF_CHUNK)
        act_ref[:, c * FF_CHUNK:(c + 1) * FF_CHUNK] = (g / (1.0 + jnp.exp(-g)) * v).astype(BF16)
    down = _dot(act_ref[...], wd_ref[...])
    for j in range(groups):
        for c in range(slabs):
            ds_ref[c, pl.ds(j, 8, stride=pitch), :] = down[j * 8:(j + 1) * 8, c * LANES:(c + 1) * LANES]
    down = jnp.concatenate(
        [jnp.concatenate([ds_ref[c, s * pitch:s * pitch + groups, :] for s in range(8)], axis=0)
         for c in range(slabs)], axis=1)
    o_ref[...] = xs_ref[...] + mod_ref[0, 5:6, :] * down


def _mix_ffn(ys, x2d, mod, g_row, wo_bf, wu_bf, conv_w, conv_b, wd_bf, li, n, tm, ctx_stream):
    rows = x2d.shape[0]
    assert rows % tm == 0 and (n % tm == 0 or (tm % n == 0 and n % (tm // 8) == 0))
    tpb = max(1, n // tm)
    per_tile = tm // HALO
    last_halo = rows // HALO - 1
    mod_row = (lambda i: (ADA_ROWS // 2, 0, 0)) if ctx_stream else (lambda i: (i // tpb, 0, 0))

    def halo_specs(width):
        return [pl.BlockSpec((HALO, width), lambda i: (jnp.maximum(i * per_tile - 1, 0), 0)),
                pl.BlockSpec((tm, width), lambda i: (i, 0)),
                pl.BlockSpec((HALO, width), lambda i: (jnp.minimum((i + 1) * per_tile, last_halo), 0))]

    y2d = [y.reshape(rows, GROUP_W) for y in ys]
    return pl.pallas_call(
        functools.partial(_mix_ffn_kernel, tiles_per_seq=tpb, seq_rows=n),
        grid=(rows // tm,),
        in_specs=halo_specs(GROUP_W) * 4 + halo_specs(D_MODEL) + [
            pl.BlockSpec((1, 6, D_MODEL), mod_row),
            _resident((1, D_MODEL)),
            _resident_layer((D_MODEL, D_MODEL), li),
            _resident_layer((D_MODEL, 2 * D_FF), li),
            _resident((3, 2 * D_FF)),
            _resident((1, 2 * D_FF)),
            _resident_layer((D_FF, D_MODEL), li),
        ],
        out_specs=pl.BlockSpec((tm, D_MODEL), lambda i: (i, 0)),
        out_shape=jax.ShapeDtypeStruct((rows, D_MODEL), F32),
        scratch_shapes=[pltpu.VMEM((tm, D_MODEL), F32)]
        + [pltpu.VMEM((D_MODEL // LANES, 8 * (tm // 8 + 8), LANES), F32)] * 2
        + [pltpu.VMEM((tm, D_FF), BF16)],
        compiler_params=_params(1),
        name="mix_ffn_ctx" if ctx_stream else "mix_ffn",
    )(*[a for y in y2d for a in (y, y, y)], x2d, x2d, x2d, mod, g_row, wo_bf, wu_bf, conv_w,
      conv_b.reshape(1, 2 * D_FF), wd_bf)


def _dft_tables(n):
    lo_n = 64
    half = n // 2
    t = jnp.arange(half, dtype=jnp.int32)

    def table(k):
        ang = ((k[:, None] * t[None, :]) % n).astype(F32) * (2.0 * math.pi / n)
        return jnp.cos(ang), jnp.sin(ang)

    (ch, sh), (cl, sl) = table(jnp.arange(n // lo_n, dtype=jnp.int32) * lo_n), table(jnp.arange(lo_n, dtype=jnp.int32))
    cn = (ch[:, None] * cl[None] - sh[:, None] * sl[None]).reshape(n, half).astype(BF16)
    sn = (sh[:, None] * cl[None] + ch[:, None] * sl[None]).reshape(n, half).astype(BF16)
    rb = min(GROUP_W, half)
    r = np.arange(rb)
    rev = np.zeros((rb, 2 * rb), np.float32)
    rev[r[1:], rb - r[1:]] = 1.0
    rev[0, rb] = 1.0
    c = np.arange(HEAD_DIM)
    angc = 2.0 * np.pi * ((c[:, None] * c[None, :]) % HEAD_DIM) / HEAD_DIM
    norm = 1.0 / math.sqrt(n * HEAD_DIM)
    eye = np.eye(N_HEADS)
    bdc = jnp.asarray(np.kron(eye, np.cos(angc) * norm), F32).astype(BF16)
    bds = jnp.asarray(np.kron(eye, np.sin(angc) * norm), F32).astype(BF16)
    return bdc, bds, cn, sn, jnp.asarray(rev, F32).astype(BF16)


def _rope_tables(n):
    n_freq = DIFF_QK // 4
    freqs = ROPE_BASE ** (-jnp.arange(n_freq, dtype=F32) / n_freq)
    t = jnp.arange(n)
    row = (t // GRID_W).astype(F32)
    col = (t % GRID_W).astype(F32)
    ang = jnp.concatenate([row[:, None] * freqs, col[:, None] * freqs], axis=-1)
    cos, sin = jnp.cos(ang), jnp.sin(ang)
    reps = GROUP_W // DIFF_QK
    return jnp.tile(jnp.concatenate([cos, cos], axis=-1), (1, reps)), jnp.tile(jnp.concatenate([-sin, sin], axis=-1), (1, reps))


def _group_mean_matrix(group):
    return jnp.asarray(np.kron(np.eye(GROUP_W // group), np.full((group, group), 1.0 / group)), F32).astype(BF16)


def kernel(x, c, ctx, c_ctx, w_ada, b_ada, g_mix, g_ffn, w_in, w_out, diff_qn, diff_kn, diff_lam, diff_subln, na_qn,
           na_kn, na_rpb, gmlp_norm, gmlp_ws, gmlp_b, ffn_up, ffn_conv, ffn_conv_b, ffn_down):
    b, n, d = x.shape
    nc = ctx.shape[1]
    assert (d, n % (2 * TOKEN_TILE), nc % CHUNK, b % NA_ELEMS, b < ADA_ROWS // 2 + 1) == (D_MODEL, 0, 0, 0, True)
    tm = TOKEN_TILE
    tq = DIFF_QUERY_TILE

    c_all = jnp.zeros((ADA_ROWS, d), F32).at[:b].set(c).at[ADA_ROWS // 2].set(c_ctx)
    mods = _ada(c_all, w_ada, b_ada).reshape(DEPTH, ADA_ROWS, 6, d)

    g32 = _group_mean_matrix(DIFF_QK)
    g64 = _group_mean_matrix(HEAD_DIM)
    rope_tabs = _rope_tables(n)
    dft_lat = _dft_tables(n)
    dft_ctx = _dft_tables(nc)

    w_in_kernel_order = jnp.concatenate([w_in[..., r * GROUP_W:(r + 1) * GROUP_W] for r in W_ORDER], axis=-1)
    w_in_bf, w_out_bf, wu_bf, wd_bf = (w.astype(BF16) for w in (w_in_kernel_order, w_out, ffn_up, ffn_down))

    x2d = x.reshape(b * n, d)
    c2d = ctx.reshape(b * nc, d)
    for li in range(DEPTH):
        ctx_out = li < DEPTH - 1
        lam_init = 0.8 - 0.6 * math.exp(-0.3 * li)
        lf = diff_lam[li].astype(F32)
        lam = (jnp.exp(jnp.sum(lf[0] * lf[1])) - jnp.exp(jnp.sum(lf[2] * lf[3])) + lam_init).reshape(1)
        mod = mods[li]
        gains = jnp.zeros((8, GROUP_W), F32)
        gains = gains.at[0].set(jnp.tile(diff_qn[li], GROUP_W // DIFF_QK) * (DIFF_QK ** -0.5 * LOG2E))
        gains = gains.at[1].set(jnp.tile(diff_kn[li], GROUP_W // DIFF_QK))
        gains = gains.at[2].set(jnp.tile(na_qn[li], N_HEADS) * (HEAD_DIM ** -0.5 * LOG2E))
        gains = gains.at[3].set(jnp.tile(na_kn[li], N_HEADS))
        gains = gains.at[4].set(gmlp_norm[li])
        sub_row = (jnp.tile(diff_subln[li], N_HEADS) * (1.0 - lam_init)).reshape(1, GROUP_W)
        g_mix_row = g_mix[li].reshape(1, d)
        g_ffn_row = g_ffn[li].reshape(1, d)
        ws_stack = gmlp_ws[li].reshape(N_HEADS * CHUNK, CHUNK).astype(BF16)
        gbias = jnp.repeat(gmlp_b[li].T, HEAD_DIM, axis=1)
        bias_tab = _na_bias_tables(na_rpb[li], nc)

        p2d, dvt, nvt = _inproj(x2d, mod, g_mix_row, w_in_bf, li, gains, g32, g64, rope_tabs, n, 2 * tm, False)
        pc2d, dvtc, nvtc = _inproj(c2d, mod, g_mix_row, w_in_bf, li, gains, g32, g64, None, nc, nc, True)
        p3 = p2d.reshape(b, n, P_W)
        pc3 = pc2d.reshape(b, nc, P_W)

        ys = [_fourier(p3, *dft_lat, "fourier"),
              _diff(lam, p3, dvt, pc3, dvtc, sub_row, tq, True),
              _na(p3, nvt, pc3, nvtc, bias_tab),
              _gmlp(p3, ws_stack, gbias, "gmlp")]
        ffn_w = (g_ffn_row, w_out_bf, wu_bf, ffn_conv[li], ffn_conv_b[li], wd_bf, li)
        x2d = _mix_ffn(ys, x2d, mod, *ffn_w, n, tm, False)

        if ctx_out:
            ycs = [_fourier(pc3, *dft_ctx, "fourier_ctx"),
                   _diff(lam, None, None, pc3, dvtc, sub_row, nc, False),
                   _na_ctx(pc3, nvtc),
                   _gmlp(pc3, ws_stack, gbias, "gmlp_ctx")]
            c2d = _mix_ffn(ycs, c2d, mod, *ffn_w, nc, tm, True)
    return x2d.reshape(b, n, d)
```

```python
import functools
import math

import jax
import jax.numpy as jnp
import numpy as np
from jax import lax
from jax.experimental import pallas as pl
from jax.experimental.pallas import tpu as pltpu

F32 = jnp.float32
BF16 = jnp.bfloat16

D_MODEL = 1024
DEPTH = 4
GRID_W = 64
GROUP_W = 256
HEAD_DIM = 64
N_HEADS = 4
DIFF_QK = 32
NA_KH = 8
NA_KW = 16
CHUNK = 128
D_FF = 2816
ROPE_BASE = 10000.0
EPS = 1e-6
IN_W = 9 * GROUP_W

REF_F, REF_DQ, REF_DK, REF_DV, REF_NQ, REF_NK, REF_NV, REF_GU, REF_GV = range(9)
W_ORDER = (REF_GV, REF_DQ, REF_DK, REF_NQ, REF_NK, REF_GU, REF_DV, REF_NV, REF_F)
W_GV, W_DQ, W_DK, W_NQ, W_NK, W_GU, W_DV, W_NV, W_F = range(9)
COL_F, COL_DQ, COL_DK, COL_NQ, COL_NK, COL_GU, COL_GV = range(7)
P_W = 7 * GROUP_W

VT_ROWS = HEAD_DIM + 16
DIFF_SCORE_COLS = 2048
PAIR_ROWS = 16
SUM_FLOOR = 2.0 ** -80
NA_Q_ROWS = 4
NA_ELEMS = 4
NA_WIN_ROWS = 12
NEG = -1e30
LOG2E = math.log2(math.e)

ADA_ROWS = 16
VMEM_LIMIT = 56 * 1024 * 1024
TOKEN_TILE = 512
DIFF_QUERY_TILE = 512
LANES = 128
HALO = 16
FF_CHUNK = 256


def _dot(a, b):
    return jnp.dot(a, b, preferred_element_type=F32)


def _dot_nt(a, b):
    return lax.dot_general(a, b, (((1,), (1,)), ((), ())), preferred_element_type=F32)


def _resident(shape):
    nd = len(shape)
    return pl.BlockSpec(shape, lambda *_: (0,) * nd, pipeline_mode=pl.Buffered(1))


def _resident_layer(shape, li):
    nd = len(shape)
    return pl.BlockSpec((None,) + tuple(shape), lambda *_: (li,) + (0,) * nd, pipeline_mode=pl.Buffered(1))


def _params(n_axes):
    return pltpu.CompilerParams(dimension_semantics=("arbitrary",) * n_axes, vmem_limit_bytes=VMEM_LIMIT)


def _lane(width=GROUP_W):
    return lax.broadcasted_iota(jnp.int32, (1, width), 1)


def _ada_kernel(c_ref, w_ref, b_ref, o_ref):
    c = c_ref[...]
    s = c / (1.0 + jnp.exp(-c))
    w = w_ref[0]
    s_hi = s.astype(BF16)
    s_lo = (s - s_hi.astype(F32)).astype(BF16)
    w_hi = w.astype(BF16)
    w_lo = (w - w_hi.astype(F32)).astype(BF16)
    o_ref[0] = _dot(s_hi, w_hi) + _dot(s_lo, w_hi) + _dot(s_hi, w_lo) + b_ref[0]


def _ada(c_all, w_ada, b_ada):
    tn = 1536
    out_w = 6 * D_MODEL
    return pl.pallas_call(
        _ada_kernel,
        grid=(DEPTH, out_w // tn),
        in_specs=[
            pl.BlockSpec((ADA_ROWS, D_MODEL), lambda l, j: (0, 0)),
            pl.BlockSpec((1, D_MODEL, tn), lambda l, j: (l, 0, j)),
            pl.BlockSpec((1, 1, tn), lambda l, j: (l, 0, j)),
        ],
        out_specs=pl.BlockSpec((1, ADA_ROWS, tn), lambda l, j: (l, 0, j)),
        out_shape=jax.ShapeDtypeStruct((DEPTH, ADA_ROWS, out_w), F32),
        compiler_params=_params(2),
        name="ada",
    )(c_all, w_ada, b_ada.reshape(DEPTH, 1, out_w))


def _gelu(x):
    return 0.5 * x * (1.0 + jnp.tanh(math.sqrt(2.0 / math.pi) * (x + 0.044715 * (x * x * x))))


def _inproj_kernel(*refs, rope):
    if rope:
        x_ref, mod_ref, g_ref, w_ref, gains_ref, g32_ref, g64_ref, cos_ref, sin_ref, o_ref, dvt_ref, nvt_ref = refs
    else:
        x_ref, mod_ref, g_ref, w_ref, gains_ref, g32_ref, g64_ref, o_ref, dvt_ref, nvt_ref = refs
    x = x_ref[...]
    ms = jnp.mean(x * x, axis=-1, keepdims=True)
    h = x * lax.rsqrt(ms + EPS) * g_ref[...]
    h = h * (1.0 + mod_ref[0, 1:2, :]) + mod_ref[0, 0:1, :]
    p = _dot(h.astype(BF16), w_ref[...])

    def col(j):
        return p[:, j * GROUP_W:(j + 1) * GROUP_W]

    def put(j, v):
        o_ref[:, j * GROUP_W:(j + 1) * GROUP_W] = v.astype(BF16)

    def group_norm(v, gmat_ref, gain_row):
        gms = _dot((v * v).astype(BF16), gmat_ref[...])
        return v * lax.rsqrt(gms + EPS) * gains_ref[gain_row:gain_row + 1, :]

    def rotary(v):
        if not rope:
            return v
        first_half = (_lane() % DIFF_QK) < (DIFF_QK // 2)
        partner = jnp.where(first_half, pltpu.roll(v, GROUP_W - DIFF_QK // 2, 1), pltpu.roll(v, DIFF_QK // 2, 1))
        return v * cos_ref[...] + partner * sin_ref[...]

    put(COL_GV, group_norm(_gelu(col(W_GV)), g64_ref, 4))
    put(COL_DQ, rotary(group_norm(col(W_DQ), g32_ref, 0)))
    put(COL_DK, rotary(group_norm(col(W_DK), g32_ref, 1)))
    put(COL_NQ, group_norm(col(W_NQ), g64_ref, 2))
    put(COL_NK, group_norm(col(W_NK), g64_ref, 3))
    put(COL_GU, _gelu(col(W_GU)))
    dvt_ref[0] = col(W_DV).T.astype(BF16)
    nvt_ref[0] = col(W_NV).T.astype(BF16)
    put(COL_F, col(W_F))


def _inproj(x2d, mod, g_row, w_bf, li, gains, g32, g64, rope_tabs, n, tm, ctx_stream):
    rows = x2d.shape[0]
    tpb = n // tm
    mod_row = (lambda i: (ADA_ROWS // 2, 0, 0)) if ctx_stream else (lambda i: (i // tpb, 0, 0))
    in_specs = [
        pl.BlockSpec((tm, D_MODEL), lambda i: (i, 0)),
        pl.BlockSpec((1, 6, D_MODEL), mod_row),
        _resident((1, D_MODEL)),
        _resident_layer((D_MODEL, IN_W), li),
        _resident((8, GROUP_W)),
        _resident((GROUP_W, GROUP_W)),
        _resident((GROUP_W, GROUP_W)),
    ]
    args = [x2d, mod, g_row, w_bf, gains, g32, g64]
    if rope_tabs is not None:
        in_specs += [pl.BlockSpec((tm, GROUP_W), lambda i: (i % tpb, 0))] * 2
        args += list(rope_tabs)
    return pl.pallas_call(
        functools.partial(_inproj_kernel, rope=rope_tabs is not None),
        grid=(rows // tm,),
        in_specs=in_specs,
        out_specs=[pl.BlockSpec((tm, P_W), lambda i: (i, 0))]
        + [pl.BlockSpec((1, GROUP_W, tm), lambda i: (i // tpb, 0, i % tpb))] * 2,
        out_shape=[jax.ShapeDtypeStruct((rows, P_W), BF16)]
        + [jax.ShapeDtypeStruct((rows // n, GROUP_W, n), BF16)] * 2,
        compiler_params=_params(1),
        name="inproj_ctx" if ctx_stream else "inproj",
    )(*args)


def _fourier_kernel(p_ref, bdc_ref, bds_ref, cn_ref, sn_ref, rev_ref, o_ref):
    n = p_ref.shape[1]
    half = n // 2
    rb = rev_ref.shape[0]
    nb = n // rb
    mirrored = []
    for blk in range(half // rb):
        src = nb - blk - 1
        nxt = (src + 1) % nb
        pair = jnp.concatenate([p_ref[0, src * rb:(src + 1) * rb, :], p_ref[0, nxt * rb:(nxt + 1) * rb, :]], axis=0)
        mirrored.append(_dot(rev_ref[...], pair))
    mirrored = jnp.concatenate(mirrored, axis=0)
    x = p_ref[0, 0:half, :].astype(F32)
    first = lax.broadcasted_iota(jnp.int32, x.shape, 0) == 0
    fold_c = jnp.where(first, x, x + mirrored).astype(BF16)
    fold_s = (x - mirrored).astype(BF16)
    a = _dot(fold_c, bdc_ref[...]).astype(BF16)
    b = _dot(fold_s, bds_ref[...]).astype(BF16)
    mid = _dot(p_ref[0, half:half + 16, :], bdc_ref[...])[0:1]
    out = _dot(cn_ref[...], a) - _dot(sn_ref[...], b)
    odd = (lax.broadcasted_iota(jnp.int32, out.shape, 0) & 1) == 1
    o_ref[0] = (out + jnp.where(odd, -mid, mid)).astype(BF16)


def _fourier(p3, bdc, bds, cn, sn, rev, name):
    b, n, _ = p3.shape
    return pl.pallas_call(
        _fourier_kernel,
        grid=(b,),
        in_specs=[
            pl.BlockSpec((1, n, GROUP_W), lambda i: (i, 0, COL_F)),
            _resident((GROUP_W, GROUP_W)),
            _resident((GROUP_W, GROUP_W)),
            _resident((n, n // 2)),
            _resident((n, n // 2)),
            _resident(rev.shape),
        ],
        out_specs=pl.BlockSpec((1, n, GROUP_W), lambda i: (i, 0, 0)),
        out_shape=jax.ShapeDtypeStruct((b, n, GROUP_W), BF16),
        compiler_params=_params(1),
        name=name,
    )(p3, bdc, bds, cn, sn, rev)


def _diff_kernel(*refs, has_lat):
    if has_lat:
        lam_ref, q_ref, kl_ref, vl_ref, kc_ref, vc_ref, sub_ref, sel_ref, o_ref, k_ref, vt_ref, kmax_ref = refs
        key_refs, val_refs = [kl_ref, kc_ref], [vl_ref, vc_ref]
    else:
        lam_ref, q_ref, kc_ref, vc_ref, sub_ref, sel_ref, o_ref, k_ref, vt_ref, kmax_ref = refs
        key_refs, val_refs = [kc_ref], [vc_ref]

    def pair_sq_norms(x):
        xf = x.astype(F32)
        return _dot_nt(sel_ref[...], (xf * xf).astype(BF16))

    @pl.when(pl.program_id(1) == 0)
    def _():
        lo = 0
        for kr, vr in zip(key_refs, val_refs):
            cnt = kr.shape[1]
            k_ref[lo:lo + cnt, :] = kr[0]
            for h in range(N_HEADS):
                vt_ref[h, 0:HEAD_DIM, lo:lo + cnt] = vr[0, h * HEAD_DIM:(h + 1) * HEAD_DIM, :]
                vt_ref[h, HEAD_DIM:VT_ROWS, lo:lo + cnt] = jnp.ones((VT_ROWS - HEAD_DIM, cnt), BF16)
            lo += cnt
        kmax_ref[...] = jnp.broadcast_to(pair_sq_norms(k_ref[...]).max(axis=1, keepdims=True), kmax_ref.shape)

    q = q_ref[0]
    lam = lam_ref[0]
    lane = _lane()
    bound = jnp.sqrt(pair_sq_norms(q) * kmax_ref[:, 0:1])

    def attend(shift_fn):
        outs = []
        tq = q.shape[0]
        heads_per_dot = min(N_HEADS, DIFF_SCORE_COLS // (2 * tq))
        for h0 in range(0, N_HEADS, heads_per_dot):
            pairs = range(2 * h0, 2 * (h0 + heads_per_dot))
            qm = jnp.concatenate(
                [jnp.where((lane >= pair * DIFF_QK) & (lane < (pair + 1) * DIFF_QK), q, jnp.zeros_like(q))
                 for pair in pairs], axis=0)
            s = _dot_nt(k_ref[...], qm)
            e = jnp.exp2(s - shift_fn(s, pairs)).astype(BF16)
            for i in range(heads_per_dot):
                o = _dot(vt_ref[h0 + i], e[:, 2 * i * tq:2 * (i + 1) * tq])
                outs += [o[:, 0:tq], o[:, tq:]]
        return outs

    def finish(outs):
        heads = []
        for h in range(N_HEADS):
            o1, o2 = outs[2 * h], outs[2 * h + 1]
            o = (o1[0:HEAD_DIM] * (1.0 / o1[HEAD_DIM:HEAD_DIM + 1])
                 - o2[0:HEAD_DIM] * (lam / o2[HEAD_DIM:HEAD_DIM + 1]))
            heads.append(o * lax.rsqrt(jnp.mean(o * o, axis=0, keepdims=True) + EPS))
        o_ref[0] = (jnp.concatenate(heads, axis=0) * sub_ref[...]).T.astype(BF16)

    outs = attend(lambda s, pairs: jnp.concatenate([bound[p:p + 1] for p in pairs], axis=1))
    finish(outs)
    smallest = outs[0][HEAD_DIM:HEAD_DIM + 1]
    for o in outs[1:]:
        smallest = jnp.minimum(smallest, o[HEAD_DIM:HEAD_DIM + 1])

    @pl.when(jnp.min(smallest) < SUM_FLOOR)
    def _():
        finish(attend(lambda s, pairs: s.max(axis=0, keepdims=True)))


def _diff(lam, p3, vt3, pc3, vtc3, sub_row, tq, has_lat):
    q_src = p3 if has_lat else pc3
    sub_t = jnp.broadcast_to(sub_row.reshape(GROUP_W, 1), (GROUP_W, tq))
    b, n, _ = q_src.shape
    nc = pc3.shape[1]
    smem = pl.BlockSpec(memory_space=pltpu.SMEM)
    in_specs = [smem, pl.BlockSpec((1, tq, GROUP_W), lambda i, j: (i, j, COL_DQ))]
    args = [lam, q_src]
    if has_lat:
        in_specs += [pl.BlockSpec((1, n, GROUP_W), lambda i, j: (i, 0, COL_DK)),
                     pl.BlockSpec((1, GROUP_W, n), lambda i, j: (i, 0, 0))]
        args += [p3, vt3]
    in_specs += [pl.BlockSpec((1, nc, GROUP_W), lambda i, j: (i, 0, COL_DK)),
                 pl.BlockSpec((1, GROUP_W, nc), lambda i, j: (i, 0, 0)),
                 _resident((GROUP_W, tq)),
                 _resident((PAIR_ROWS, GROUP_W))]
    pair_sel = np.arange(PAIR_ROWS)[:, None] == np.arange(GROUP_W)[None, :] // DIFF_QK
    args += [pc3, vtc3, sub_t, jnp.asarray(pair_sel, F32).astype(BF16)]
    return pl.pallas_call(
        functools.partial(_diff_kernel, has_lat=has_lat),
        grid=(b, n // tq),
        in_specs=in_specs,
        out_specs=pl.BlockSpec((1, tq, GROUP_W), lambda i, j: (i, j, 0)),
        out_shape=jax.ShapeDtypeStruct((b, n, GROUP_W), BF16),
        scratch_shapes=[pltpu.VMEM(((n if has_lat else 0) + nc, GROUP_W), BF16),
                        pltpu.VMEM((N_HEADS, VT_ROWS, (n if has_lat else 0) + nc), BF16),
                        pltpu.VMEM((PAIR_ROWS, 128), F32)],
        compiler_params=_params(2),
        name="diff" if has_lat else "diff_ctx",
    )(*args)


def _na_kernel(*refs, local):
    if local:
        q_ref, k0_ref, k1_ref, k2_ref, v0_ref, v1_ref, v2_ref, kc_ref, vc_ref, bias_ref, o_ref = refs
        key_refs = [k0_ref, k1_ref, k2_ref, kc_ref]
        vt_refs = [v0_ref, v1_ref, v2_ref, vc_ref]
    else:
        q_ref, kc_ref, vc_ref, o_ref = refs
        key_refs = [kc_ref]
        vt_refs = [vc_ref]
    n_elems, blk = q_ref.shape[0], q_ref.shape[1]
    lane = _lane()

    def scores(el):
        q = q_ref[el]
        q_heads = jnp.concatenate(
            [jnp.where((lane >= h * HEAD_DIM) & (lane < (h + 1) * HEAD_DIM), q, jnp.zeros_like(q))
             for h in range(N_HEADS)], axis=0)
        k_all = jnp.concatenate([r[el] for r in key_refs], axis=0)
        return _dot_nt(k_all, q_heads)

    def weights(s):
        if local:
            s = s + bias_ref[0]
        return jnp.exp2(s - s.max(axis=0, keepdims=True)).astype(BF16)

    def attend(el, e):
        vt_all = jnp.concatenate([r[el] for r in vt_refs], axis=1)
        ones = jnp.ones((VT_ROWS - HEAD_DIM, vt_all.shape[1]), BF16)
        heads = []
        for h in range(N_HEADS):
            vt_h = jnp.concatenate([vt_all[h * HEAD_DIM:(h + 1) * HEAD_DIM], ones], axis=0)
            o = _dot(vt_h, e[:, h * blk:(h + 1) * blk])
            heads.append(o[0:HEAD_DIM] * (1.0 / o[HEAD_DIM:HEAD_DIM + 1]))
        o_ref[el] = jnp.concatenate(heads, axis=0).T.astype(BF16)

    es = [weights(s) for s in [scores(el) for el in range(n_elems)]]
    for el in range(n_elems):
        attend(el, es[el])


def _na(p3, vt3, pc3, vtc3, bias):
    b, n, _ = p3.shape
    nc = pc3.shape[1]
    ne = NA_ELEMS
    blk = NA_Q_ROWS * GRID_W
    steps = n // blk
    last_win = n // blk - NA_WIN_ROWS // NA_Q_ROWS

    def first(g):
        return jnp.clip(g - 1, 0, last_win)

    def cls(g, i):
        return ((g > 0).astype(jnp.int32) + (g == steps - 1).astype(jnp.int32), 0, 0)

    in_specs = [pl.BlockSpec((ne, blk, GROUP_W), lambda g, i: (i, g, COL_NQ))]
    in_specs += [pl.BlockSpec((ne, blk, GROUP_W), lambda g, i, j=j: (i, first(g) + j, COL_NK)) for j in range(3)]
    in_specs += [pl.BlockSpec((ne, GROUP_W, blk), lambda g, i, j=j: (i, 0, first(g) + j)) for j in range(3)]
    in_specs += [pl.BlockSpec((ne, nc, GROUP_W), lambda g, i: (i, 0, COL_NK)),
                 pl.BlockSpec((ne, GROUP_W, nc), lambda g, i: (i, 0, 0)),
                 pl.BlockSpec((1, 3 * blk + nc, N_HEADS * blk), cls)]
    return pl.pallas_call(
        functools.partial(_na_kernel, local=True),
        grid=(steps, b // ne),
        in_specs=in_specs,
        out_specs=pl.BlockSpec((ne, blk, GROUP_W), lambda g, i: (i, g, 0)),
        out_shape=jax.ShapeDtypeStruct((b, n, GROUP_W), BF16),
        compiler_params=_params(2),
        name="na",
    )(p3, p3, p3, p3, vt3, vt3, vt3, pc3, vtc3, bias)


def _na_ctx(pc3, vtc3):
    b, nc, _ = pc3.shape
    ne = NA_ELEMS
    return pl.pallas_call(
        functools.partial(_na_kernel, local=False),
        grid=(b // ne,),
        in_specs=[pl.BlockSpec((ne, nc, GROUP_W), lambda i: (i, 0, COL_NQ)),
                  pl.BlockSpec((ne, nc, GROUP_W), lambda i: (i, 0, COL_NK)),
                  pl.BlockSpec((ne, GROUP_W, nc), lambda i: (i, 0, 0))],
        out_specs=pl.BlockSpec((ne, nc, GROUP_W), lambda i: (i, 0, 0)),
        out_shape=jax.ShapeDtypeStruct((b, nc, GROUP_W), BF16),
        compiler_params=_params(1),
        name="na_ctx",
    )(pc3, pc3, vtc3)


def _na_row_offsets():
    first_row = NA_WIN_ROWS - NA_KH
    rules = [lambda i, a: a - i + NA_KH - 1 if a < NA_KH else None,
             lambda i, a: a - i + NA_KH // 2 - 1 if i <= a < i + NA_KH else None,
             lambda i, a: a - i - 1 if a >= first_row else None]
    return [[[rule(i, a) for a in range(NA_WIN_ROWS)] for i in range(NA_Q_ROWS)] for rule in rules]


def _na_bias_kernel(band_ref, o_ref):
    masked = jnp.full((GRID_W, GRID_W), NEG, F32)
    for kind, per_query_row in enumerate(_na_row_offsets()):
        for a in range(NA_WIN_ROWS):
            blocks = [masked if offs[a] is None else band_ref[0, offs[a]] for offs in per_query_row]
            o_ref[kind, a * GRID_W:(a + 1) * GRID_W, :] = jnp.concatenate(blocks, axis=1)
        o_ref[kind, NA_WIN_ROWS * GRID_W:, :] = jnp.zeros((o_ref.shape[1] - NA_WIN_ROWS * GRID_W, o_ref.shape[2]), F32)


def _na_bias_tables(rpb, nc):
    kc = np.arange(GRID_W)[:, None]
    j = np.arange(GRID_W)[None, :]
    c0 = np.clip(j - NA_KW // 2, 0, GRID_W - NA_KW)
    col_ok = (kc >= c0) & (kc < c0 + NA_KW)
    onehot = (np.arange(2 * NA_KW - 1)[:, None, None] == (kc - j + NA_KW - 1)[None]) & col_ok[None]
    band = jnp.einsum("hdo,okj->hdkj", rpb.astype(F32), jnp.asarray(onehot, F32), precision=lax.Precision.HIGHEST)
    band = jnp.where(col_ok, band * LOG2E, NEG)
    n_off = 2 * NA_KH - 1
    q, k = NA_Q_ROWS * GRID_W, NA_WIN_ROWS * GRID_W
    return pl.pallas_call(
        _na_bias_kernel,
        grid=(N_HEADS,),
        in_specs=[pl.BlockSpec((1, n_off, GRID_W, GRID_W), lambda h: (h, 0, 0, 0))],
        out_specs=pl.BlockSpec((3, k + nc, q), lambda h: (0, 0, h)),
        out_shape=jax.ShapeDtypeStruct((3, k + nc, N_HEADS * q), F32),
        compiler_params=_params(1),
        name="na_bias",
    )(band)


def _gmlp_kernel(u_ref, v_ref, ws_ref, bias_ref, o_ref, *, chunks_per_step):
    n = u_ref.shape[1]
    lane = _lane()
    step_rows = chunks_per_step * CHUNK
    for s in range(n // step_rows):
        base = s * step_rows
        vcat = jnp.concatenate(
            [v_ref[0, base + c * CHUNK: base + (c + 1) * CHUNK, :] for c in range(chunks_per_step)], axis=1)
        r = _dot(ws_ref[...], vcat)
        for c in range(chunks_per_step):
            mix = r[0:CHUNK, c * GROUP_W:(c + 1) * GROUP_W]
            for g in range(1, N_HEADS):
                mix = jnp.where(lane >= g * HEAD_DIM, r[g * CHUNK:(g + 1) * CHUNK, c * GROUP_W:(c + 1) * GROUP_W], mix)
            rows = slice(base + c * CHUNK, base + (c + 1) * CHUNK)
            o_ref[0, rows, :] = (u_ref[0, rows, :].astype(F32) * (mix + bias_ref[...])).astype(BF16)


def _gmlp(p3, ws_stack, bias_full, name):
    b, n, _ = p3.shape
    cps = min(4, n // CHUNK)
    return pl.pallas_call(
        functools.partial(_gmlp_kernel, chunks_per_step=cps),
        grid=(b,),
        in_specs=[pl.BlockSpec((1, n, GROUP_W), lambda i: (i, 0, COL_GU)),
                  pl.BlockSpec((1, n, GROUP_W), lambda i: (i, 0, COL_GV)),
                  _resident((N_HEADS * CHUNK, CHUNK)),
                  _resident((CHUNK, GROUP_W))],
        out_specs=pl.BlockSpec((1, n, GROUP_W), lambda i: (i, 0, 0)),
        out_shape=jax.ShapeDtypeStruct((b, n, GROUP_W), BF16),
        compiler_params=_params(1),
        name=name,
    )(p3, p3, ws_stack, bias_full)


def _mix_ffn_kernel(*refs, tiles_per_seq, seq_rows):
    y_refs, (xp_ref, xm_ref, xn_ref) = refs[:12], refs[12:15]
    mod_ref, g_ref, wo_ref, wu_ref, cw_ref, cb_ref, wd_ref, o_ref, xs_ref, hs_ref, ds_ref, act_ref = refs[15:]
    tm = xm_ref.shape[0]
    groups = tm // 8
    pos = pl.program_id(0) % tiles_per_seq

    def mixed(y_parts, x):
        return x + mod_ref[0, 2:3, :] * _dot(jnp.concatenate(y_parts, axis=1), wo_ref[...])

    def prenorm(x):
        ms = jnp.mean(x * x, axis=-1, keepdims=True)
        h = x * lax.rsqrt(ms + EPS) * g_ref[...]
        return h * (1.0 + mod_ref[0, 4:5, :]) + mod_ref[0, 3:4, :]

    x_all = mixed([jnp.concatenate([y_refs[3 * j + 1][...], y_refs[3 * j][...], y_refs[3 * j + 2][...]], axis=0)
                   for j in range(4)],
                  jnp.concatenate([xm_ref[...], xp_ref[...], xn_ref[...]], axis=0))
    xs_ref[...] = x_all[0:tm]
    h_all = prenorm(x_all)
    h_main, h_halo = h_all[0:tm], h_all[tm:]
    slabs = hs_ref.shape[0]
    pitch = hs_ref.shape[1] // 8
    for c in range(slabs):
        for s in range(8):
            hs_ref[c, s * pitch:s * pitch + groups, :] = h_main[s * groups:(s + 1) * groups, c * LANES:(c + 1) * LANES]
    before = jnp.where(pos != 0, h_halo[HALO - 1:HALO], 0.0)
    after = jnp.where(pos != tiles_per_seq - 1, h_halo[HALO:HALO + 1], 0.0)
    row = lax.broadcasted_iota(jnp.int32, (HALO, 1), 0)
    h_edge = jnp.where(row == 0, before, jnp.where(row == HALO - 1, after, 0.0))
    h_perm = jnp.concatenate(
        [jnp.concatenate([hs_ref[c, pl.ds(j, 8, stride=pitch), :] for c in range(slabs)], axis=1)
         for j in range(groups)], axis=0)
    z_all = _dot(jnp.concatenate([h_perm, h_edge], axis=0).astype(BF16), wu_ref[...])

    sub = lax.broadcasted_iota(jnp.int32, (8, FF_CHUNK), 0)
    inner_starts = [s for s in range(1, 8) if (s * groups) % seq_rows == 0]
    starts_seq = functools.reduce(jnp.logical_or, [sub == s for s in inner_starts], sub < 0)
    ends_seq = functools.reduce(jnp.logical_or, [sub == s - 1 for s in inner_starts], sub < 0)

    def conv(lo):
        cols = slice(lo, lo + FF_CHUNK)
        z = z_all[0:tm, cols]
        edge = z_all[tm:tm + HALO, cols]
        first = jnp.where(sub == 0, edge[0:8], jnp.where(starts_seq, 0.0, pltpu.roll(z[tm - 8:tm], 1, 0)))
        last = jnp.where(sub == 7, edge[HALO - 8:HALO], jnp.where(ends_seq, 0.0, pltpu.roll(z[0:8], 7, 0)))
        return (jnp.concatenate([first, z[0:tm - 8]], axis=0) * cw_ref[0:1, cols] + z * cw_ref[1:2, cols]
                + jnp.concatenate([z[8:tm], last], axis=0) * cw_ref[2:3, cols] + cb_ref[:, cols])

    for c in range(D_FF // FF_CHUNK):
        g = conv(c * FF_CHUNK)
        v = conv(D_FF + c * FF_CHUNK)
        act_ref[:, c * FF_CHUNK:(c + 1) * FF_CHUNK] = (g / (1.0 + jnp.exp(-g)) * v).astype(BF16)
    down = _dot(act_ref[...], wd_ref[...])
    for j in range(groups):
        for c in range(slabs):
            ds_ref[c, pl.ds(j, 8, stride=pitch), :] = down[j * 8:(j + 1) * 8, c * LANES:(c + 1) * LANES]
    down = jnp.concatenate(
        [jnp.concatenate([ds_ref[c, s * pitch:s * pitch + groups, :] for s in range(8)], axis=0)
         for c in range(slabs)], axis=1)
    o_ref[...] = xs_ref[...] + mod_ref[0, 5:6, :] * down


def _mix_ffn(ys, x2d, mod, g_row, wo_bf, wu_bf, conv_w, conv_b, wd_bf, li, n, tm, ctx_stream):
    rows = x2d.shape[0]
    assert rows % tm == 0 and (n % tm == 0 or (tm % n == 0 and n % (tm // 8) == 0))
    tpb = max(1, n // tm)
    per_tile = tm // HALO
    last_halo = rows // HALO - 1
    mod_row = (lambda i: (ADA_ROWS // 2, 0, 0)) if ctx_stream else (lambda i: (i // tpb, 0, 0))

    def halo_specs(width):
        return [pl.BlockSpec((HALO, width), lambda i: (jnp.maximum(i * per_tile - 1, 0), 0)),
                pl.BlockSpec((tm, width), lambda i: (i, 0)),
                pl.BlockSpec((HALO, width), lambda i: (jnp.minimum((i + 1) * per_tile, last_halo), 0))]

    y2d = [y.reshape(rows, GROUP_W) for y in ys]
    return pl.pallas_call(
        functools.partial(_mix_ffn_kernel, tiles_per_seq=tpb, seq_rows=n),
        grid=(rows // tm,),
        in_specs=halo_specs(GROUP_W) * 4 + halo_specs(D_MODEL) + [
            pl.BlockSpec((1, 6, D_MODEL), mod_row),
            _resident((1, D_MODEL)),
            _resident_layer((D_MODEL, D_MODEL), li),
            _resident_layer((D_MODEL, 2 * D_FF), li),
            _resident((3, 2 * D_FF)),
            _resident((1, 2 * D_FF)),
            _resident_layer((D_FF, D_MODEL), li),
        ],
        out_specs=pl.BlockSpec((tm, D_MODEL), lambda i: (i, 0)),
        out_shape=jax.ShapeDtypeStruct((rows, D_MODEL), F32),
        scratch_shapes=[pltpu.VMEM((tm, D_MODEL), F32)]
        + [pltpu.VMEM((D_MODEL // LANES, 8 * (tm // 8 + 8), LANES), F32)] * 2
        + [pltpu.VMEM((tm, D_FF), BF16)],
        compiler_params=_params(1),
        name="mix_ffn_ctx" if ctx_stream else "mix_ffn",
    )(*[a for y in y2d for a in (y, y, y)], x2d, x2d, x2d, mod, g_row, wo_bf, wu_bf, conv_w,
      conv_b.reshape(1, 2 * D_FF), wd_bf)


def _dft_tables(n):
    lo_n = 64
    half = n // 2
    t = jnp.arange(half, dtype=jnp.int32)

    def table(k):
        ang = ((k[:, None] * t[None, :]) % n).astype(F32) * (2.0 * math.pi / n)
        return jnp.cos(ang), jnp.sin(ang)

    (ch, sh), (cl, sl) = table(jnp.arange(n // lo_n, dtype=jnp.int32) * lo_n), table(jnp.arange(lo_n, dtype=jnp.int32))
    cn = (ch[:, None] * cl[None] - sh[:, None] * sl[None]).reshape(n, half).astype(BF16)
    sn = (sh[:, None] * cl[None] + ch[:, None] * sl[None]).reshape(n, half).astype(BF16)
    rb = min(GROUP_W, half)
    r = np.arange(rb)
    rev = np.zeros((rb, 2 * rb), np.float32)
    rev[r[1:], rb - r[1:]] = 1.0
    rev[0, rb] = 1.0
    c = np.arange(HEAD_DIM)
    angc = 2.0 * np.pi * ((c[:, None] * c[None, :]) % HEAD_DIM) / HEAD_DIM
    norm = 1.0 / math.sqrt(n * HEAD_DIM)
    eye = np.eye(N_HEADS)
    bdc = jnp.asarray(np.kron(eye, np.cos(angc) * norm), F32).astype(BF16)
    bds = jnp.asarray(np.kron(eye, np.sin(angc) * norm), F32).astype(BF16)
    return bdc, bds, cn, sn, jnp.asarray(rev, F32).astype(BF16)


def _rope_tables(n):
    n_freq = DIFF_QK // 4
    freqs = ROPE_BASE ** (-jnp.arange(n_freq, dtype=F32) / n_freq)
    t = jnp.arange(n)
    row = (t // GRID_W).astype(F32)
    col = (t % GRID_W).astype(F32)
    ang = jnp.concatenate([row[:, None] * freqs, col[:, None] * freqs], axis=-1)
    cos, sin = jnp.cos(ang), jnp.sin(ang)
    reps = GROUP_W // DIFF_QK
    return jnp.tile(jnp.concatenate([cos, cos], axis=-1), (1, reps)), jnp.tile(jnp.concatenate([-sin, sin], axis=-1), (1, reps))


def _group_mean_matrix(group):
    return jnp.asarray(np.kron(np.eye(GROUP_W // group), np.full((group, group), 1.0 / group)), F32).astype(BF16)


def kernel(x, c, ctx, c_ctx, w_ada, b_ada, g_mix, g_ffn, w_in, w_out, diff_qn, diff_kn, diff_lam, diff_subln, na_qn,
           na_kn, na_rpb, gmlp_norm, gmlp_ws, gmlp_b, ffn_up, ffn_conv, ffn_conv_b, ffn_down):
    b, n, d = x.shape
    nc = ctx.shape[1]
    assert (d, n % (2 * TOKEN_TILE), nc % CHUNK, b % NA_ELEMS, b < ADA_ROWS // 2 + 1) == (D_MODEL, 0, 0, 0, True)
    tm = TOKEN_TILE
    tq = DIFF_QUERY_TILE

    c_all = jnp.zeros((ADA_ROWS, d), F32).at[:b].set(c).at[ADA_ROWS // 2].set(c_ctx)
    mods = _ada(c_all, w_ada, b_ada).reshape(DEPTH, ADA_ROWS, 6, d)

    g32 = _group_mean_matrix(DIFF_QK)
    g64 = _group_mean_matrix(HEAD_DIM)
    rope_tabs = _rope_tables(n)
    dft_lat = _dft_tables(n)
    dft_ctx = _dft_tables(nc)

    w_in_kernel_order = jnp.concatenate([w_in[..., r * GROUP_W:(r + 1) * GROUP_W] for r in W_ORDER], axis=-1)
    w_in_bf, w_out_bf, wu_bf, wd_bf = (w.astype(BF16) for w in (w_in_kernel_order, w_out, ffn_up, ffn_down))

    x2d = x.reshape(b * n, d)
    c2d = ctx.reshape(b * nc, d)
    for li in range(DEPTH):
        ctx_out = li < DEPTH - 1
        lam_init = 0.8 - 0.6 * math.exp(-0.3 * li)
        lf = diff_lam[li].astype(F32)
        lam = (jnp.exp(jnp.sum(lf[0] * lf[1])) - jnp.exp(jnp.sum(lf[2] * lf[3])) + lam_init).reshape(1)
        mod = mods[li]
        gains = jnp.zeros((8, GROUP_W), F32)
        gains = gains.at[0].set(jnp.tile(diff_qn[li], GROUP_W // DIFF_QK) * (DIFF_QK ** -0.5 * LOG2E))
        gains = gains.at[1].set(jnp.tile(diff_kn[li], GROUP_W // DIFF_QK))
        gains = gains.at[2].set(jnp.tile(na_qn[li], N_HEADS) * (HEAD_DIM ** -0.5 * LOG2E))
        gains = gains.at[3].set(jnp.tile(na_kn[li], N_HEADS))
        gains = gains.at[4].set(gmlp_norm[li])
        sub_row = (jnp.tile(diff_subln[li], N_HEADS) * (1.0 - lam_init)).reshape(1, GROUP_W)
        g_mix_row = g_mix[li].reshape(1, d)
        g_ffn_row = g_ffn[li].reshape(1, d)
        ws_stack = gmlp_ws[li].reshape(N_HEADS * CHUNK, CHUNK).astype(BF16)
        gbias = jnp.repeat(gmlp_b[li].T, HEAD_DIM, axis=1)
        bias_tab = _na_bias_tables(na_rpb[li], nc)

        p2d, dvt, nvt = _inproj(x2d, mod, g_mix_row, w_in_bf, li, gains, g32, g64, rope_tabs, n, 2 * tm, False)
        pc2d, dvtc, nvtc = _inproj(c2d, mod, g_mix_row, w_in_bf, li, gains, g32, g64, None, nc, nc, True)
        p3 = p2d.reshape(b, n, P_W)
        pc3 = pc2d.reshape(b, nc, P_W)

        ys = [_fourier(p3, *dft_lat, "fourier"),
              _diff(lam, p3, dvt, pc3, dvtc, sub_row, tq, True),
              _na(p3, nvt, pc3, nvtc, bias_tab),
              _gmlp(p3, ws_stack, gbias, "gmlp")]
        ffn_w = (g_ffn_row, w_out_bf, wu_bf, ffn_conv[li], ffn_conv_b[li], wd_bf, li)
        x2d = _mix_ffn(ys, x2d, mod, *ffn_w, n, tm, False)

        if ctx_out:
            ycs = [_fourier(pc3, *dft_ctx, "fourier_ctx"),
                   _diff(lam, None, None, pc3, dvtc, sub_row, nc, False),
                   _na_ctx(pc3, nvtc),
                   _gmlp(pc3, ws_stack, gbias, "gmlp_ctx")]
            c2d = _mix_ffn(ycs, c2d, mod, *ffn_w, nc, tm, True)
    return x2d.reshape(b, n, d)
```

```python
import functools
import math

import jax
import jax.numpy as jnp
import numpy as np
from jax import lax
from jax.experimental import pallas as pl
from jax.experimental.pallas import tpu as pltpu

F32 = jnp.float32
BF16 = jnp.bfloat16

D_MODEL = 1024
DEPTH = 4
GRID_W = 64
GROUP_W = 256
HEAD_DIM = 64
N_HEADS = 4
DIFF_QK = 32
NA_KH = 8
NA_KW = 16
CHUNK = 128
D_FF = 2816
ROPE_BASE = 10000.0
EPS = 1e-6
IN_W = 9 * GROUP_W

REF_F, REF_DQ, REF_DK, REF_DV, REF_NQ, REF_NK, REF_NV, REF_GU, REF_GV = range(9)
W_ORDER = (REF_GV, REF_DQ, REF_DK, REF_NQ, REF_NK, REF_GU, REF_DV, REF_NV, REF_F)
W_GV, W_DQ, W_DK, W_NQ, W_NK, W_GU, W_DV, W_NV, W_F = range(9)
COL_F, COL_DQ, COL_DK, COL_NQ, COL_NK, COL_GU, COL_GV = range(7)
P_W = 7 * GROUP_W

VT_ROWS = HEAD_DIM + 16
DIFF_SCORE_COLS = 1024
PAIR_ROWS = 16
SUM_FLOOR = 2.0 ** -80
NA_Q_ROWS = 4
NA_ELEMS = 4
NA_WIN_ROWS = 12
NEG = -1e30
LOG2E = math.log2(math.e)

ADA_ROWS = 16
VMEM_LIMIT = 56 * 1024 * 1024
TOKEN_TILE = 512
DIFF_QUERY_TILE = 512
LANES = 128
HALO = 16
FF_CHUNK = 256


def _dot(a, b):
    return jnp.dot(a, b, preferred_element_type=F32)


def _dot_nt(a, b):
    return lax.dot_general(a, b, (((1,), (1,)), ((), ())), preferred_element_type=F32)


def _resident(shape):
    nd = len(shape)
    return pl.BlockSpec(shape, lambda *_: (0,) * nd, pipeline_mode=pl.Buffered(1))


def _resident_layer(shape, li):
    nd = len(shape)
    return pl.BlockSpec((None,) + tuple(shape), lambda *_: (li,) + (0,) * nd, pipeline_mode=pl.Buffered(1))


def _params(n_axes):
    return pltpu.CompilerParams(dimension_semantics=("arbitrary",) * n_axes, vmem_limit_bytes=VMEM_LIMIT)


def _lane(width=GROUP_W):
    return lax.broadcasted_iota(jnp.int32, (1, width), 1)


def _ada_kernel(c_ref, w_ref, b_ref, o_ref):
    c = c_ref[...]
    s = c / (1.0 + jnp.exp(-c))
    w = w_ref[0]
    s_hi = s.astype(BF16)
    s_lo = (s - s_hi.astype(F32)).astype(BF16)
    w_hi = w.astype(BF16)
    w_lo = (w - w_hi.astype(F32)).astype(BF16)
    o_ref[0] = _dot(s_hi, w_hi) + _dot(s_lo, w_hi) + _dot(s_hi, w_lo) + b_ref[0]


def _ada(c_all, w_ada, b_ada):
    tn = 1536
    out_w = 6 * D_MODEL
    return pl.pallas_call(
        _ada_kernel,
        grid=(DEPTH, out_w // tn),
        in_specs=[
            pl.BlockSpec((ADA_ROWS, D_MODEL), lambda l, j: (0, 0)),
            pl.BlockSpec((1, D_MODEL, tn), lambda l, j: (l, 0, j)),
            pl.BlockSpec((1, 1, tn), lambda l, j: (l, 0, j)),
        ],
        out_specs=pl.BlockSpec((1, ADA_ROWS, tn), lambda l, j: (l, 0, j)),
        out_shape=jax.ShapeDtypeStruct((DEPTH, ADA_ROWS, out_w), F32),
        compiler_params=_params(2),
        name="ada",
    )(c_all, w_ada, b_ada.reshape(DEPTH, 1, out_w))


def _gelu(x):
    return 0.5 * x * (1.0 + jnp.tanh(math.sqrt(2.0 / math.pi) * (x + 0.044715 * (x * x * x))))


def _inproj_kernel(*refs, rope):
    if rope:
        x_ref, mod_ref, g_ref, w_ref, gains_ref, g32_ref, g64_ref, cos_ref, sin_ref, o_ref, dvt_ref, nvt_ref = refs
    else:
        x_ref, mod_ref, g_ref, w_ref, gains_ref, g32_ref, g64_ref, o_ref, dvt_ref, nvt_ref = refs
    x = x_ref[...]
    ms = jnp.mean(x * x, axis=-1, keepdims=True)
    h = x * lax.rsqrt(ms + EPS) * g_ref[...]
    h = h * (1.0 + mod_ref[0, 1:2, :]) + mod_ref[0, 0:1, :]
    p = _dot(h.astype(BF16), w_ref[...])

    def col(j):
        return p[:, j * GROUP_W:(j + 1) * GROUP_W]

    def put(j, v):
        o_ref[:, j * GROUP_W:(j + 1) * GROUP_W] = v.astype(BF16)

    def group_norm(v, gmat_ref, gain_row):
        gms = _dot((v * v).astype(BF16), gmat_ref[...])
        return v * lax.rsqrt(gms + EPS) * gains_ref[gain_row:gain_row + 1, :]

    def rotary(v):
        if not rope:
            return v
        first_half = (_lane() % DIFF_QK) < (DIFF_QK // 2)
        partner = jnp.where(first_half, pltpu.roll(v, GROUP_W - DIFF_QK // 2, 1), pltpu.roll(v, DIFF_QK // 2, 1))
        return v * cos_ref[...] + partner * sin_ref[...]

    put(COL_GV, group_norm(_gelu(col(W_GV)), g64_ref, 4))
    put(COL_DQ, rotary(group_norm(col(W_DQ), g32_ref, 0)))
    put(COL_DK, rotary(group_norm(col(W_DK), g32_ref, 1)))
    put(COL_NQ, group_norm(col(W_NQ), g64_ref, 2))
    put(COL_NK, group_norm(col(W_NK), g64_ref, 3))
    put(COL_GU, _gelu(col(W_GU)))
    dvt_ref[0] = col(W_DV).T.astype(BF16)
    nvt_ref[0] = col(W_NV).T.astype(BF16)
    put(COL_F, col(W_F))


def _inproj(x2d, mod, g_row, w_bf, li, gains, g32, g64, rope_tabs, n, tm, ctx_stream):
    rows = x2d.shape[0]
    tpb = n // tm
    mod_row = (lambda i: (ADA_ROWS // 2, 0, 0)) if ctx_stream else (lambda i: (i // tpb, 0, 0))
    in_specs = [
        pl.BlockSpec((tm, D_MODEL), lambda i: (i, 0)),
        pl.BlockSpec((1, 6, D_MODEL), mod_row),
        _resident((1, D_MODEL)),
        _resident_layer((D_MODEL, IN_W), li),
        _resident((8, GROUP_W)),
        _resident((GROUP_W, GROUP_W)),
        _resident((GROUP_W, GROUP_W)),
    ]
    args = [x2d, mod, g_row, w_bf, gains, g32, g64]
    if rope_tabs is not None:
        in_specs += [pl.BlockSpec((tm, GROUP_W), lambda i: (i % tpb, 0))] * 2
        args += list(rope_tabs)
    return pl.pallas_call(
        functools.partial(_inproj_kernel, rope=rope_tabs is not None),
        grid=(rows // tm,),
        in_specs=in_specs,
        out_specs=[pl.BlockSpec((tm, P_W), lambda i: (i, 0))]
        + [pl.BlockSpec((1, GROUP_W, tm), lambda i: (i // tpb, 0, i % tpb))] * 2,
        out_shape=[jax.ShapeDtypeStruct((rows, P_W), BF16)]
        + [jax.ShapeDtypeStruct((rows // n, GROUP_W, n), BF16)] * 2,
        compiler_params=_params(1),
        name="inproj_ctx" if ctx_stream else "inproj",
    )(*args)


def _fourier_kernel(p_ref, bdc_ref, bds_ref, cn_ref, sn_ref, rev_ref, o_ref):
    n = p_ref.shape[1]
    half = n // 2
    rb = rev_ref.shape[0]
    nb = n // rb
    mirrored = []
    for blk in range(half // rb):
        src = nb - blk - 1
        nxt = (src + 1) % nb
        pair = jnp.concatenate([p_ref[0, src * rb:(src + 1) * rb, :], p_ref[0, nxt * rb:(nxt + 1) * rb, :]], axis=0)
        mirrored.append(_dot(rev_ref[...], pair))
    mirrored = jnp.concatenate(mirrored, axis=0)
    x = p_ref[0, 0:half, :].astype(F32)
    first = lax.broadcasted_iota(jnp.int32, x.shape, 0) == 0
    fold_c = jnp.where(first, x, x + mirrored).astype(BF16)
    fold_s = (x - mirrored).astype(BF16)
    a = _dot(fold_c, bdc_ref[...]).astype(BF16)
    b = _dot(fold_s, bds_ref[...]).astype(BF16)
    mid = _dot(p_ref[0, half:half + 16, :], bdc_ref[...])[0:1]
    out = _dot(cn_ref[...], a) - _dot(sn_ref[...], b)
    odd = (lax.broadcasted_iota(jnp.int32, out.shape, 0) & 1) == 1
    o_ref[0] = (out + jnp.where(odd, -mid, mid)).astype(BF16)


def _fourier(p3, bdc, bds, cn, sn, rev, name):
    b, n, _ = p3.shape
    return pl.pallas_call(
        _fourier_kernel,
        grid=(b,),
        in_specs=[
            pl.BlockSpec((1, n, GROUP_W), lambda i: (i, 0, COL_F)),
            _resident((GROUP_W, GROUP_W)),
            _resident((GROUP_W, GROUP_W)),
            _resident((n, n // 2)),
            _resident((n, n // 2)),
            _resident(rev.shape),
        ],
        out_specs=pl.BlockSpec((1, n, GROUP_W), lambda i: (i, 0, 0)),
        out_shape=jax.ShapeDtypeStruct((b, n, GROUP_W), BF16),
        compiler_params=_params(1),
        name=name,
    )(p3, bdc, bds, cn, sn, rev)


def _diff_kernel(*refs, has_lat):
    if has_lat:
        lam_ref, q_ref, kl_ref, vl_ref, kc_ref, vc_ref, sub_ref, sel_ref, o_ref, k_ref, vt_ref, kmax_ref = refs
        key_refs, val_refs = [kl_ref, kc_ref], [vl_ref, vc_ref]
    else:
        lam_ref, q_ref, kc_ref, vc_ref, sub_ref, sel_ref, o_ref, k_ref, vt_ref, kmax_ref = refs
        key_refs, val_refs = [kc_ref], [vc_ref]

    def pair_sq_norms(x):
        xf = x.astype(F32)
        return _dot_nt(sel_ref[...], (xf * xf).astype(BF16))

    @pl.when(pl.program_id(1) == 0)
    def _():
        lo = 0
        for kr, vr in zip(key_refs, val_refs):
            cnt = kr.shape[1]
            k_ref[lo:lo + cnt, :] = kr[0]
            for h in range(N_HEADS):
                vt_ref[h, 0:HEAD_DIM, lo:lo + cnt] = vr[0, h * HEAD_DIM:(h + 1) * HEAD_DIM, :]
                vt_ref[h, HEAD_DIM:VT_ROWS, lo:lo + cnt] = jnp.ones((VT_ROWS - HEAD_DIM, cnt), BF16)
            lo += cnt
        kmax_ref[...] = jnp.broadcast_to(pair_sq_norms(k_ref[...]).max(axis=1, keepdims=True), kmax_ref.shape)

    q = q_ref[0]
    lam = lam_ref[0]
    lane = _lane()
    bound = jnp.sqrt(pair_sq_norms(q) * kmax_ref[:, 0:1])

    def attend(shift_fn):
        outs = []
        tq = q.shape[0]
        heads_per_dot = min(N_HEADS, DIFF_SCORE_COLS // (2 * tq))
        groups = range(0, N_HEADS, heads_per_dot)

        def scores(h0):
            qm = jnp.concatenate(
                [jnp.where((lane >= pair * DIFF_QK) & (lane < (pair + 1) * DIFF_QK), q, jnp.zeros_like(q))
                 for pair in range(2 * h0, 2 * (h0 + heads_per_dot))], axis=0)
            return _dot_nt(k_ref[...], qm)

        s_next = scores(groups[0])
        for g, h0 in enumerate(groups):
            s = s_next
            if g + 1 < len(groups):
                s_next = scores(groups[g + 1])
            e = jnp.exp2(s - shift_fn(s, range(2 * h0, 2 * (h0 + heads_per_dot)))).astype(BF16)
            for i in range(heads_per_dot):
                o = _dot(vt_ref[h0 + i], e[:, 2 * i * tq:2 * (i + 1) * tq])
                outs += [o[:, 0:tq], o[:, tq:]]
        return outs

    def finish(outs):
        heads = []
        for h in range(N_HEADS):
            o1, o2 = outs[2 * h], outs[2 * h + 1]
            o = (o1[0:HEAD_DIM] * (1.0 / o1[HEAD_DIM:HEAD_DIM + 1])
                 - o2[0:HEAD_DIM] * (lam / o2[HEAD_DIM:HEAD_DIM + 1]))
            heads.append(o * lax.rsqrt(jnp.mean(o * o, axis=0, keepdims=True) + EPS))
        o_ref[0] = (jnp.concatenate(heads, axis=0) * sub_ref[...]).T.astype(BF16)

    outs = attend(lambda s, pairs: jnp.concatenate([bound[p:p + 1] for p in pairs], axis=1))
    finish(outs)
    smallest = outs[0][HEAD_DIM:HEAD_DIM + 1]
    for o in outs[1:]:
        smallest = jnp.minimum(smallest, o[HEAD_DIM:HEAD_DIM + 1])

    @pl.when(jnp.min(smallest) < SUM_FLOOR)
    def _():
        finish(attend(lambda s, pairs: s.max(axis=0, keepdims=True)))


def _diff(lam, p3, vt3, pc3, vtc3, sub_row, tq, has_lat):
    q_src = p3 if has_lat else pc3
    sub_t = jnp.broadcast_to(sub_row.reshape(GROUP_W, 1), (GROUP_W, tq))
    b, n, _ = q_src.shape
    nc = pc3.shape[1]
    smem = pl.BlockSpec(memory_space=pltpu.SMEM)
    in_specs = [smem, pl.BlockSpec((1, tq, GROUP_W), lambda i, j: (i, j, COL_DQ))]
    args = [lam, q_src]
    if has_lat:
        in_specs += [pl.BlockSpec((1, n, GROUP_W), lambda i, j: (i, 0, COL_DK)),
                     pl.BlockSpec((1, GROUP_W, n), lambda i, j: (i, 0, 0))]
        args += [p3, vt3]
    in_specs += [pl.BlockSpec((1, nc, GROUP_W), lambda i, j: (i, 0, COL_DK)),
                 pl.BlockSpec((1, GROUP_W, nc), lambda i, j: (i, 0, 0)),
                 _resident((GROUP_W, tq)),
                 _resident((PAIR_ROWS, GROUP_W))]
    pair_sel = np.arange(PAIR_ROWS)[:, None] == np.arange(GROUP_W)[None, :] // DIFF_QK
    args += [pc3, vtc3, sub_t, jnp.asarray(pair_sel, F32).astype(BF16)]
    return pl.pallas_call(
        functools.partial(_diff_kernel, has_lat=has_lat),
        grid=(b, n // tq),
        in_specs=in_specs,
        out_specs=pl.BlockSpec((1, tq, GROUP_W), lambda i, j: (i, j, 0)),
        out_shape=jax.ShapeDtypeStruct((b, n, GROUP_W), BF16),
        scratch_shapes=[pltpu.VMEM(((n if has_lat else 0) + nc, GROUP_W), BF16),
                        pltpu.VMEM((N_HEADS, VT_ROWS, (n if has_lat else 0) + nc), BF16),
                        pltpu.VMEM((PAIR_ROWS, 128), F32)],
        compiler_params=_params(2),
        name="diff" if has_lat else "diff_ctx",
    )(*args)


def _na_kernel(*refs, local):
    if local:
        q_ref, k0_ref, k1_ref, k2_ref, v0_ref, v1_ref, v2_ref, kc_ref, vc_ref, bias_ref, o_ref = refs
        key_refs = [k0_ref, k1_ref, k2_ref, kc_ref]
        vt_refs = [v0_ref, v1_ref, v2_ref, vc_ref]
    else:
        q_ref, kc_ref, vc_ref, o_ref = refs
        key_refs = [kc_ref]
        vt_refs = [vc_ref]
    n_elems, blk = q_ref.shape[0], q_ref.shape[1]
    lane = _lane()

    def scores(el):
        q = q_ref[el]
        q_heads = jnp.concatenate(
            [jnp.where((lane >= h * HEAD_DIM) & (lane < (h + 1) * HEAD_DIM), q, jnp.zeros_like(q))
             for h in range(N_HEADS)], axis=0)
        k_all = jnp.concatenate([r[el] for r in key_refs], axis=0)
        return _dot_nt(k_all, q_heads)

    def weights(s):
        if local:
            s = s + bias_ref[0]
        return jnp.exp2(s - s.max(axis=0, keepdims=True)).astype(BF16)

    def attend(el, e):
        vt_all = jnp.concatenate([r[el] for r in vt_refs], axis=1)
        ones = jnp.ones((VT_ROWS - HEAD_DIM, vt_all.shape[1]), BF16)
        heads = []
        for h in range(N_HEADS):
            vt_h = jnp.concatenate([vt_all[h * HEAD_DIM:(h + 1) * HEAD_DIM], ones], axis=0)
            o = _dot(vt_h, e[:, h * blk:(h + 1) * blk])
            heads.append(o[0:HEAD_DIM] * (1.0 / o[HEAD_DIM:HEAD_DIM + 1]))
        o_ref[el] = jnp.concatenate(heads, axis=0).T.astype(BF16)

    es = [weights(s) for s in [scores(el) for el in range(n_elems)]]
    for el in range(n_elems):
        attend(el, es[el])


def _na(p3, vt3, pc3, vtc3, bias):
    b, n, _ = p3.shape
    nc = pc3.shape[1]
    ne = NA_ELEMS
    blk = NA_Q_ROWS * GRID_W
    steps = n // blk
    last_win = n // blk - NA_WIN_ROWS // NA_Q_ROWS

    def first(g):
        return jnp.clip(g - 1, 0, last_win)

    def cls(g, i):
        return ((g > 0).astype(jnp.int32) + (g == steps - 1).astype(jnp.int32), 0, 0)

    in_specs = [pl.BlockSpec((ne, blk, GROUP_W), lambda g, i: (i, g, COL_NQ))]
    in_specs += [pl.BlockSpec((ne, blk, GROUP_W), lambda g, i, j=j: (i, first(g) + j, COL_NK)) for j in range(3)]
    in_specs += [pl.BlockSpec((ne, GROUP_W, blk), lambda g, i, j=j: (i, 0, first(g) + j)) for j in range(3)]
    in_specs += [pl.BlockSpec((ne, nc, GROUP_W), lambda g, i: (i, 0, COL_NK)),
                 pl.BlockSpec((ne, GROUP_W, nc), lambda g, i: (i, 0, 0)),
                 pl.BlockSpec((1, 3 * blk + nc, N_HEADS * blk), cls)]
    return pl.pallas_call(
        functools.partial(_na_kernel, local=True),
        grid=(steps, b // ne),
        in_specs=in_specs,
        out_specs=pl.BlockSpec((ne, blk, GROUP_W), lambda g, i: (i, g, 0)),
        out_shape=jax.ShapeDtypeStruct((b, n, GROUP_W), BF16),
        compiler_params=_params(2),
        name="na",
    )(p3, p3, p3, p3, vt3, vt3, vt3, pc3, vtc3, bias)


def _na_ctx(pc3, vtc3):
    b, nc, _ = pc3.shape
    ne = NA_ELEMS
    return pl.pallas_call(
        functools.partial(_na_kernel, local=False),
        grid=(b // ne,),
        in_specs=[pl.BlockSpec((ne, nc, GROUP_W), lambda i: (i, 0, COL_NQ)),
                  pl.BlockSpec((ne, nc, GROUP_W), lambda i: (i, 0, COL_NK)),
                  pl.BlockSpec((ne, GROUP_W, nc), lambda i: (i, 0, 0))],
        out_specs=pl.BlockSpec((ne, nc, GROUP_W), lambda i: (i, 0, 0)),
        out_shape=jax.ShapeDtypeStruct((b, nc, GROUP_W), BF16),
        compiler_params=_params(1),
        name="na_ctx",
    )(pc3, pc3, vtc3)


def _na_row_offsets():
    first_row = NA_WIN_ROWS - NA_KH
    rules = [lambda i, a: a - i + NA_KH - 1 if a < NA_KH else None,
             lambda i, a: a - i + NA_KH // 2 - 1 if i <= a < i + NA_KH else None,
             lambda i, a: a - i - 1 if a >= first_row else None]
    return [[[rule(i, a) for a in range(NA_WIN_ROWS)] for i in range(NA_Q_ROWS)] for rule in rules]


def _na_bias_kernel(band_ref, o_ref):
    masked = jnp.full((GRID_W, GRID_W), NEG, F32)
    for kind, per_query_row in enumerate(_na_row_offsets()):
        for a in range(NA_WIN_ROWS):
            blocks = [masked if offs[a] is None else band_ref[0, offs[a]] for offs in per_query_row]
            o_ref[kind, a * GRID_W:(a + 1) * GRID_W, :] = jnp.concatenate(blocks, axis=1)
        o_ref[kind, NA_WIN_ROWS * GRID_W:, :] = jnp.zeros((o_ref.shape[1] - NA_WIN_ROWS * GRID_W, o_ref.shape[2]), F32)


def _na_bias_tables(rpb, nc):
    kc = np.arange(GRID_W)[:, None]
    j = np.arange(GRID_W)[None, :]
    c0 = np.clip(j - NA_KW // 2, 0, GRID_W - NA_KW)
    col_ok = (kc >= c0) & (kc < c0 + NA_KW)
    onehot = (np.arange(2 * NA_KW - 1)[:, None, None] == (kc - j + NA_KW - 1)[None]) & col_ok[None]
    band = jnp.einsum("hdo,okj->hdkj", rpb.astype(F32), jnp.asarray(onehot, F32), precision=lax.Precision.HIGHEST)
    band = jnp.where(col_ok, band * LOG2E, NEG)
    n_off = 2 * NA_KH - 1
    q, k = NA_Q_ROWS * GRID_W, NA_WIN_ROWS * GRID_W
    return pl.pallas_call(
        _na_bias_kernel,
        grid=(N_HEADS,),
        in_specs=[pl.BlockSpec((1, n_off, GRID_W, GRID_W), lambda h: (h, 0, 0, 0))],
        out_specs=pl.BlockSpec((3, k + nc, q), lambda h: (0, 0, h)),
        out_shape=jax.ShapeDtypeStruct((3, k + nc, N_HEADS * q), F32),
        compiler_params=_params(1),
        name="na_bias",
    )(band)


def _gmlp_kernel(u_ref, v_ref, ws_ref, bias_ref, o_ref, *, chunks_per_step):
    n = u_ref.shape[1]
    lane = _lane()
    step_rows = chunks_per_step * CHUNK
    for s in range(n // step_rows):
        base = s * step_rows
        vcat = jnp.concatenate(
            [v_ref[0, base + c * CHUNK: base + (c + 1) * CHUNK, :] for c in range(chunks_per_step)], axis=1)
        r = _dot(ws_ref[...], vcat)
        for c in range(chunks_per_step):
            mix = r[0:CHUNK, c * GROUP_W:(c + 1) * GROUP_W]
            for g in range(1, N_HEADS):
                mix = jnp.where(lane >= g * HEAD_DIM, r[g * CHUNK:(g + 1) * CHUNK, c * GROUP_W:(c + 1) * GROUP_W], mix)
            rows = slice(base + c * CHUNK, base + (c + 1) * CHUNK)
            o_ref[0, rows, :] = (u_ref[0, rows, :].astype(F32) * (mix + bias_ref[...])).astype(BF16)


def _gmlp(p3, ws_stack, bias_full, name):
    b, n, _ = p3.shape
    cps = min(4, n // CHUNK)
    return pl.pallas_call(
        functools.partial(_gmlp_kernel, chunks_per_step=cps),
        grid=(b,),
        in_specs=[pl.BlockSpec((1, n, GROUP_W), lambda i: (i, 0, COL_GU)),
                  pl.BlockSpec((1, n, GROUP_W), lambda i: (i, 0, COL_GV)),
                  _resident((N_HEADS * CHUNK, CHUNK)),
                  _resident((CHUNK, GROUP_W))],
        out_specs=pl.BlockSpec((1, n, GROUP_W), lambda i: (i, 0, 0)),
        out_shape=jax.ShapeDtypeStruct((b, n, GROUP_W), BF16),
        compiler_params=_params(1),
        name=name,
    )(p3, p3, ws_stack, bias_full)


def _mix_ffn_kernel(*refs, tiles_per_seq, seq_rows):
    y_refs, (xp_ref, xm_ref, xn_ref) = refs[:12], refs[12:15]
    mod_ref, g_ref, wo_ref, wu_ref, cw_ref, cb_ref, wd_ref, o_ref, xs_ref, hs_ref, ds_ref, act_ref = refs[15:]
    tm = xm_ref.shape[0]
    groups = tm // 8
    pos = pl.program_id(0) % tiles_per_seq

    def mixed(y_parts, x):
        return x + mod_ref[0, 2:3, :] * _dot(jnp.concatenate(y_parts, axis=1), wo_ref[...])

    def prenorm(x):
        ms = jnp.mean(x * x, axis=-1, keepdims=True)
        h = x * lax.rsqrt(ms + EPS) * g_ref[...]
        return h * (1.0 + mod_ref[0, 4:5, :]) + mod_ref[0, 3:4, :]

    x_all = mixed([jnp.concatenate([y_refs[3 * j + 1][...], y_refs[3 * j][...], y_refs[3 * j + 2][...]], axis=0)
                   for j in range(4)],
                  jnp.concatenate([xm_ref[...], xp_ref[...], xn_ref[...]], axis=0))
    xs_ref[...] = x_all[0:tm]
    h_all = prenorm(x_all)
    h_main, h_halo = h_all[0:tm], h_all[tm:]
    slabs = hs_ref.shape[0]
    pitch = hs_ref.shape[1] // 8
    for c in range(slabs):
        for s in range(8):
            hs_ref[c, s * pitch:s * pitch + groups, :] = h_main[s * groups:(s + 1) * groups, c * LANES:(c + 1) * LANES]
    before = jnp.where(pos != 0, h_halo[HALO - 1:HALO], 0.0)
    after = jnp.where(pos != tiles_per_seq - 1, h_halo[HALO:HALO + 1], 0.0)
    row = lax.broadcasted_iota(jnp.int32, (HALO, 1), 0)
    h_edge = jnp.where(row == 0, before, jnp.where(row == HALO - 1, after, 0.0))
    h_perm = jnp.concatenate(
        [jnp.concatenate([hs_ref[c, pl.ds(j, 8, stride=pitch), :] for c in range(slabs)], axis=1)
         for j in range(groups)], axis=0)
    z_all = _dot(jnp.concatenate([h_perm, h_edge], axis=0).astype(BF16), wu_ref[...])

    sub = lax.broadcasted_iota(jnp.int32, (8, FF_CHUNK), 0)
    inner_starts = [s for s in range(1, 8) if (s * groups) % seq_rows == 0]
    starts_seq = functools.reduce(jnp.logical_or, [sub == s for s in inner_starts], sub < 0)
    ends_seq = functools.reduce(jnp.logical_or, [sub == s - 1 for s in inner_starts], sub < 0)

    def conv(lo):
        cols = slice(lo, lo + FF_CHUNK)
        z = z_all[0:tm, cols]
        edge = z_all[tm:tm + HALO, cols]
        first = jnp.where(sub == 0, edge[0:8], jnp.where(starts_seq, 0.0, pltpu.roll(z[tm - 8:tm], 1, 0)))
        last = jnp.where(sub == 7, edge[HALO - 8:HALO], jnp.where(ends_seq, 0.0, pltpu.roll(z[0:8], 7, 0)))
        return (jnp.concatenate([first, z[0:tm - 8]], axis=0) * cw_ref[0:1, cols] + z * cw_ref[1:2, cols]
                + jnp.concatenate([z[8:tm], last], axis=0) * cw_ref[2:3, cols] + cb_ref[:, cols])

    for c in range(D_FF // FF_CHUNK):
        g = conv(c * FF_CHUNK)
        v = conv(D_FF + c * FF_CHUNK)
        act_ref[:, c * FF_CHUNK:(c + 1) * FF_CHUNK] = (g / (1.0 + jnp.exp(-g)) * v).astype(BF16)
    down = _dot(act_ref[...], wd_ref[...])
    for j in range(groups):
        for c in range(slabs):
            ds_ref[c, pl.ds(j, 8, stride=pitch), :] = down[j * 8:(j + 1) * 8, c * LANES:(c + 1) * LANES]
    down = jnp.concatenate(
        [jnp.concatenate([ds_ref[c, s * pitch:s * pitch + groups, :] for s in range(8)], axis=0)
         for c in range(slabs)], axis=1)
    o_ref[...] = xs_ref[...] + mod_ref[0, 5:6, :] * down


def _mix_ffn(ys, x2d, mod, g_row, wo_bf, wu_bf, conv_w, conv_b, wd_bf, li, n, tm, ctx_stream):
    rows = x2d.shape[0]
    assert rows % tm == 0 and (n % tm == 0 or (tm % n == 0 and n % (tm // 8) == 0))
    tpb = max(1, n // tm)
    per_tile = tm // HALO
    last_halo = rows // HALO - 1
    mod_row = (lambda i: (ADA_ROWS // 2, 0, 0)) if ctx_stream else (lambda i: (i // tpb, 0, 0))

    def halo_specs(width):
        return [pl.BlockSpec((HALO, width), lambda i: (jnp.maximum(i * per_tile - 1, 0), 0)),
                pl.BlockSpec((tm, width), lambda i: (i, 0)),
                pl.BlockSpec((HALO, width), lambda i: (jnp.minimum((i + 1) * per_tile, last_halo), 0))]

    y2d = [y.reshape(rows, GROUP_W) for y in ys]
    return pl.pallas_call(
        functools.partial(_mix_ffn_kernel, tiles_per_seq=tpb, seq_rows=n),
        grid=(rows // tm,),
        in_specs=halo_specs(GROUP_W) * 4 + halo_specs(D_MODEL) + [
            pl.BlockSpec((1, 6, D_MODEL), mod_row),
            _resident((1, D_MODEL)),
            _resident_layer((D_MODEL, D_MODEL), li),
            _resident_layer((D_MODEL, 2 * D_FF), li),
            _resident((3, 2 * D_FF)),
            _resident((1, 2 * D_FF)),
            _resident_layer((D_FF, D_MODEL), li),
        ],
        out_specs=pl.BlockSpec((tm, D_MODEL), lambda i: (i, 0)),
        out_shape=jax.ShapeDtypeStruct((rows, D_MODEL), F32),
        scratch_shapes=[pltpu.VMEM((tm, D_MODEL), F32)]
        + [pltpu.VMEM((D_MODEL // LANES, 8 * (tm // 8 + 8), LANES), F32)] * 2
        + [pltpu.VMEM((tm, D_FF), BF16)],
        compiler_params=_params(1),
        name="mix_ffn_ctx" if ctx_stream else "mix_ffn",
    )(*[a for y in y2d for a in (y, y, y)], x2d, x2d, x2d, mod, g_row, wo_bf, wu_bf, conv_w,
      conv_b.reshape(1, 2 * D_FF), wd_bf)


def _dft_tables(n):
    lo_n = 64
    half = n // 2
    t = jnp.arange(half, dtype=jnp.int32)

    def table(k):
        ang = ((k[:, None] * t[None, :]) % n).astype(F32) * (2.0 * math.pi / n)
        return jnp.cos(ang), jnp.sin(ang)

    (ch, sh), (cl, sl) = table(jnp.arange(n // lo_n, dtype=jnp.int32) * lo_n), table(jnp.arange(lo_n, dtype=jnp.int32))
    cn = (ch[:, None] * cl[None] - sh[:, None] * sl[None]).reshape(n, half).astype(BF16)
    sn = (sh[:, None] * cl[None] + ch[:, None] * sl[None]).reshape(n, half).astype(BF16)
    rb = min(GROUP_W, half)
    r = np.arange(rb)
    rev = np.zeros((rb, 2 * rb), np.float32)
    rev[r[1:], rb - r[1:]] = 1.0
    rev[0, rb] = 1.0
    c = np.arange(HEAD_DIM)
    angc = 2.0 * np.pi * ((c[:, None] * c[None, :]) % HEAD_DIM) / HEAD_DIM
    norm = 1.0 / math.sqrt(n * HEAD_DIM)
    eye = np.eye(N_HEADS)
    bdc = jnp.asarray(np.kron(eye, np.cos(angc) * norm), F32).astype(BF16)
    bds = jnp.asarray(np.kron(eye, np.sin(angc) * norm), F32).astype(BF16)
    return bdc, bds, cn, sn, jnp.asarray(rev, F32).astype(BF16)


def _rope_tables(n):
    n_freq = DIFF_QK // 4
    freqs = ROPE_BASE ** (-jnp.arange(n_freq, dtype=F32) / n_freq)
    t = jnp.arange(n)
    row = (t // GRID_W).astype(F32)
    col = (t % GRID_W).astype(F32)
    ang = jnp.concatenate([row[:, None] * freqs, col[:, None] * freqs], axis=-1)
    cos, sin = jnp.cos(ang), jnp.sin(ang)
    reps = GROUP_W // DIFF_QK
    return jnp.tile(jnp.concatenate([cos, cos], axis=-1), (1, reps)), jnp.tile(jnp.concatenate([-sin, sin], axis=-1), (1, reps))


def _group_mean_matrix(group):
    return jnp.asarray(np.kron(np.eye(GROUP_W // group), np.full((group, group), 1.0 / group)), F32).astype(BF16)


def kernel(x, c, ctx, c_ctx, w_ada, b_ada, g_mix, g_ffn, w_in, w_out, diff_qn, diff_kn, diff_lam, diff_subln, na_qn,
           na_kn, na_rpb, gmlp_norm, gmlp_ws, gmlp_b, ffn_up, ffn_conv, ffn_conv_b, ffn_down):
    b, n, d = x.shape
    nc = ctx.shape[1]
    assert (d, n % (2 * TOKEN_TILE), nc % CHUNK, b % NA_ELEMS, b < ADA_ROWS // 2 + 1) == (D_MODEL, 0, 0, 0, True)
    tm = TOKEN_TILE
    tq = DIFF_QUERY_TILE

    c_all = jnp.zeros((ADA_ROWS, d), F32).at[:b].set(c).at[ADA_ROWS // 2].set(c_ctx)
    mods = _ada(c_all, w_ada, b_ada).reshape(DEPTH, ADA_ROWS, 6, d)

    g32 = _group_mean_matrix(DIFF_QK)
    g64 = _group_mean_matrix(HEAD_DIM)
    rope_tabs = _rope_tables(n)
    dft_lat = _dft_tables(n)
    dft_ctx = _dft_tables(nc)

    w_in_kernel_order = jnp.concatenate([w_in[..., r * GROUP_W:(r + 1) * GROUP_W] for r in W_ORDER], axis=-1)
    w_in_bf, w_out_bf, wu_bf, wd_bf = (w.astype(BF16) for w in (w_in_kernel_order, w_out, ffn_up, ffn_down))

    x2d = x.reshape(b * n, d)
    c2d = ctx.reshape(b * nc, d)
    for li in range(DEPTH):
        ctx_out = li < DEPTH - 1
        lam_init = 0.8 - 0.6 * math.exp(-0.3 * li)
        lf = diff_lam[li].astype(F32)
        lam = (jnp.exp(jnp.sum(lf[0] * lf[1])) - jnp.exp(jnp.sum(lf[2] * lf[3])) + lam_init).reshape(1)
        mod = mods[li]
        gains = jnp.zeros((8, GROUP_W), F32)
        gains = gains.at[0].set(jnp.tile(diff_qn[li], GROUP_W // DIFF_QK) * (DIFF_QK ** -0.5 * LOG2E))
        gains = gains.at[1].set(jnp.tile(diff_kn[li], GROUP_W // DIFF_QK))
        gains = gains.at[2].set(jnp.tile(na_qn[li], N_HEADS) * (HEAD_DIM ** -0.5 * LOG2E))
        gains = gains.at[3].set(jnp.tile(na_kn[li], N_HEADS))
        gains = gains.at[4].set(gmlp_norm[li])
        sub_row = (jnp.tile(diff_subln[li], N_HEADS) * (1.0 - lam_init)).reshape(1, GROUP_W)
        g_mix_row = g_mix[li].reshape(1, d)
        g_ffn_row = g_ffn[li].reshape(1, d)
        ws_stack = gmlp_ws[li].reshape(N_HEADS * CHUNK, CHUNK).astype(BF16)
        gbias = jnp.repeat(gmlp_b[li].T, HEAD_DIM, axis=1)
        bias_tab = _na_bias_tables(na_rpb[li], nc)

        p2d, dvt, nvt = _inproj(x2d, mod, g_mix_row, w_in_bf, li, gains, g32, g64, rope_tabs, n, 2 * tm, False)
        pc2d, dvtc, nvtc = _inproj(c2d, mod, g_mix_row, w_in_bf, li, gains, g32, g64, None, nc, nc, True)
        p3 = p2d.reshape(b, n, P_W)
        pc3 = pc2d.reshape(b, nc, P_W)

        ys = [_fourier(p3, *dft_lat, "fourier"),
              _diff(lam, p3, dvt, pc3, dvtc, sub_row, tq, True),
              _na(p3, nvt, pc3, nvtc, bias_tab),
              _gmlp(p3, ws_stack, gbias, "gmlp")]
        ffn_w = (g_ffn_row, w_out_bf, wu_bf, ffn_conv[li], ffn_conv_b[li], wd_bf, li)
        x2d = _mix_ffn(ys, x2d, mod, *ffn_w, n, tm, False)

        if ctx_out:
            ycs = [_fourier(pc3, *dft_ctx, "fourier_ctx"),
                   _diff(lam, None, None, pc3, dvtc, sub_row, nc, False),
                   _na_ctx(pc3, nvtc),
                   _gmlp(pc3, ws_stack, gbias, "gmlp_ctx")]
            c2d = _mix_ffn(ycs, c2d, mod, *ffn_w, nc, tm, True)
    return x2d.reshape(b, n, d)
```

```python
import functools
import math

import jax
import jax.numpy as jnp
import numpy as np
from jax import lax
from jax.experimental import pallas as pl
from jax.experimental.pallas import tpu as pltpu

F32 = jnp.float32
BF16 = jnp.bfloat16

D_MODEL = 1024
DEPTH = 4
GRID_W = 64
GROUP_W = 256
HEAD_DIM = 64
N_HEADS = 4
DIFF_QK = 32
NA_KH = 8
NA_KW = 16
CHUNK = 128
D_FF = 2816
ROPE_BASE = 10000.0
EPS = 1e-6
IN_W = 9 * GROUP_W

REF_F, REF_DQ, REF_DK, REF_DV, REF_NQ, REF_NK, REF_NV, REF_GU, REF_GV = range(9)
W_ORDER = (REF_GV, REF_DQ, REF_DK, REF_NQ, REF_NK, REF_GU, REF_DV, REF_NV, REF_F)
W_GV, W_DQ, W_DK, W_NQ, W_NK, W_GU, W_DV, W_NV, W_F = range(9)
COL_F, COL_DQ, COL_DK, COL_NQ, COL_NK, COL_GU, COL_GV = range(7)
P_W = 7 * GROUP_W

VT_ROWS = HEAD_DIM + 16
DIFF_SCORE_COLS = 2048
PAIR_ROWS = 16
SUM_FLOOR = 2.0 ** -80
NA_Q_ROWS = 4
NA_ELEMS = 4
NA_WIN_ROWS = 12
NEG = -1e30
LOG2E = math.log2(math.e)

ADA_ROWS = 16
VMEM_LIMIT = 56 * 1024 * 1024
TOKEN_TILE = 512
DIFF_QUERY_TILE = 512
LANES = 128
HALO = 16
FF_CHUNK = 256


def _dot(a, b):
    return jnp.dot(a, b, preferred_element_type=F32)


def _dot_nt(a, b):
    return lax.dot_general(a, b, (((1,), (1,)), ((), ())), preferred_element_type=F32)


def _resident(shape):
    nd = len(shape)
    return pl.BlockSpec(shape, lambda *_: (0,) * nd, pipeline_mode=pl.Buffered(1))


def _resident_layer(shape, li):
    nd = len(shape)
    return pl.BlockSpec((None,) + tuple(shape), lambda *_: (li,) + (0,) * nd, pipeline_mode=pl.Buffered(1))


def _params(n_axes):
    return pltpu.CompilerParams(dimension_semantics=("arbitrary",) * n_axes, vmem_limit_bytes=VMEM_LIMIT)


def _lane(width=GROUP_W):
    return lax.broadcasted_iota(jnp.int32, (1, width), 1)


def _ada_kernel(c_ref, w_ref, b_ref, o_ref):
    c = c_ref[...]
    s = c / (1.0 + jnp.exp(-c))
    w = w_ref[0]
    s_hi = s.astype(BF16)
    s_lo = (s - s_hi.astype(F32)).astype(BF16)
    w_hi = w.astype(BF16)
    w_lo = (w - w_hi.astype(F32)).astype(BF16)
    o_ref[0] = _dot(s_hi, w_hi) + _dot(s_lo, w_hi) + _dot(s_hi, w_lo) + b_ref[0]


def _ada(c_all, w_ada, b_ada):
    tn = 1536
    out_w = 6 * D_MODEL
    return pl.pallas_call(
        _ada_kernel,
        grid=(DEPTH, out_w // tn),
        in_specs=[
            pl.BlockSpec((ADA_ROWS, D_MODEL), lambda l, j: (0, 0)),
            pl.BlockSpec((1, D_MODEL, tn), lambda l, j: (l, 0, j)),
            pl.BlockSpec((1, 1, tn), lambda l, j: (l, 0, j)),
        ],
        out_specs=pl.BlockSpec((1, ADA_ROWS, tn), lambda l, j: (l, 0, j)),
        out_shape=jax.ShapeDtypeStruct((DEPTH, ADA_ROWS, out_w), F32),
        compiler_params=_params(2),
        name="ada",
    )(c_all, w_ada, b_ada.reshape(DEPTH, 1, out_w))


def _gelu(x):
    return 0.5 * x * (1.0 + jnp.tanh(math.sqrt(2.0 / math.pi) * (x + 0.044715 * (x * x * x))))


def _inproj_kernel(*refs, rope):
    if rope:
        x_ref, mod_ref, g_ref, w_ref, gains_ref, g32_ref, g64_ref, cos_ref, sin_ref, o_ref, dvt_ref, nvt_ref = refs
    else:
        x_ref, mod_ref, g_ref, w_ref, gains_ref, g32_ref, g64_ref, o_ref, dvt_ref, nvt_ref = refs
    x = x_ref[...]
    ms = jnp.mean(x * x, axis=-1, keepdims=True)
    h = x * lax.rsqrt(ms + EPS) * g_ref[...]
    h = h * (1.0 + mod_ref[0, 1:2, :]) + mod_ref[0, 0:1, :]
    p = _dot(h.astype(BF16), w_ref[...])

    def col(j):
        return p[:, j * GROUP_W:(j + 1) * GROUP_W]

    def put(j, v):
        o_ref[:, j * GROUP_W:(j + 1) * GROUP_W] = v.astype(BF16)

    def group_norm(v, gmat_ref, gain_row):
        gms = _dot((v * v).astype(BF16), gmat_ref[...])
        return v * lax.rsqrt(gms + EPS) * gains_ref[gain_row:gain_row + 1, :]

    def rotary(v):
        if not rope:
            return v
        first_half = (_lane() % DIFF_QK) < (DIFF_QK // 2)
        partner = jnp.where(first_half, pltpu.roll(v, GROUP_W - DIFF_QK // 2, 1), pltpu.roll(v, DIFF_QK // 2, 1))
        return v * cos_ref[...] + partner * sin_ref[...]

    put(COL_GV, group_norm(_gelu(col(W_GV)), g64_ref, 4))
    put(COL_DQ, rotary(group_norm(col(W_DQ), g32_ref, 0)))
    put(COL_DK, rotary(group_norm(col(W_DK), g32_ref, 1)))
    put(COL_NQ, group_norm(col(W_NQ), g64_ref, 2))
    put(COL_NK, group_norm(col(W_NK), g64_ref, 3))
    put(COL_GU, _gelu(col(W_GU)))
    dvt_ref[0] = col(W_DV).T.astype(BF16)
    nvt_ref[0] = col(W_NV).T.astype(BF16)
    put(COL_F, col(W_F))


def _inproj(x2d, mod, g_row, w_bf, li, gains, g32, g64, rope_tabs, n, tm, ctx_stream):
    rows = x2d.shape[0]
    tpb = n // tm
    mod_row = (lambda i: (ADA_ROWS // 2, 0, 0)) if ctx_stream else (lambda i: (i // tpb, 0, 0))
    in_specs = [
        pl.BlockSpec((tm, D_MODEL), lambda i: (i, 0)),
        pl.BlockSpec((1, 6, D_MODEL), mod_row),
        _resident((1, D_MODEL)),
        _resident_layer((D_MODEL, IN_W), li),
        _resident((8, GROUP_W)),
        _resident((GROUP_W, GROUP_W)),
        _resident((GROUP_W, GROUP_W)),
    ]
    args = [x2d, mod, g_row, w_bf, gains, g32, g64]
    if rope_tabs is not None:
        in_specs += [pl.BlockSpec((tm, GROUP_W), lambda i: (i % tpb, 0))] * 2
        args += list(rope_tabs)
    return pl.pallas_call(
        functools.partial(_inproj_kernel, rope=rope_tabs is not None),
        grid=(rows // tm,),
        in_specs=in_specs,
        out_specs=[pl.BlockSpec((tm, P_W), lambda i: (i, 0))]
        + [pl.BlockSpec((1, GROUP_W, tm), lambda i: (i // tpb, 0, i % tpb))] * 2,
        out_shape=[jax.ShapeDtypeStruct((rows, P_W), BF16)]
        + [jax.ShapeDtypeStruct((rows // n, GROUP_W, n), BF16)] * 2,
        compiler_params=_params(1),
        name="inproj_ctx" if ctx_stream else "inproj",
    )(*args)


def _fourier_kernel(p_ref, bdc_ref, bds_ref, cn_ref, sn_ref, rev_ref, o_ref):
    n = p_ref.shape[1]
    half = n // 2
    rb = rev_ref.shape[0]
    nb = n // rb
    mirrored = []
    for blk in range(half // rb):
        src = nb - blk - 1
        nxt = (src + 1) % nb
        pair = jnp.concatenate([p_ref[0, src * rb:(src + 1) * rb, :], p_ref[0, nxt * rb:(nxt + 1) * rb, :]], axis=0)
        mirrored.append(_dot(rev_ref[...], pair))
    mirrored = jnp.concatenate(mirrored, axis=0)
    x = p_ref[0, 0:half, :].astype(F32)
    first = lax.broadcasted_iota(jnp.int32, x.shape, 0) == 0
    fold_c = jnp.where(first, x, x + mirrored).astype(BF16)
    fold_s = (x - mirrored).astype(BF16)
    a = _dot(fold_c, bdc_ref[...]).astype(BF16)
    b = _dot(fold_s, bds_ref[...]).astype(BF16)
    mid = _dot(p_ref[0, half:half + 16, :], bdc_ref[...])[0:1]
    out = _dot(cn_ref[...], a) - _dot(sn_ref[...], b)
    odd = (lax.broadcasted_iota(jnp.int32, out.shape, 0) & 1) == 1
    o_ref[0] = (out + jnp.where(odd, -mid, mid)).astype(BF16)


def _fourier(p3, bdc, bds, cn, sn, rev, name):
    b, n, _ = p3.shape
    return pl.pallas_call(
        _fourier_kernel,
        grid=(b,),
        in_specs=[
            pl.BlockSpec((1, n, GROUP_W), lambda i: (i, 0, COL_F)),
            _resident((GROUP_W, GROUP_W)),
            _resident((GROUP_W, GROUP_W)),
            _resident((n, n // 2)),
            _resident((n, n // 2)),
            _resident(rev.shape),
        ],
        out_specs=pl.BlockSpec((1, n, GROUP_W), lambda i: (i, 0, 0)),
        out_shape=jax.ShapeDtypeStruct((b, n, GROUP_W), BF16),
        compiler_params=_params(1),
        name=name,
    )(p3, bdc, bds, cn, sn, rev)


def _diff_kernel(*refs, has_lat):
    if has_lat:
        lam_ref, q_ref, kl_ref, vl_ref, kc_ref, vc_ref, sub_ref, sel_ref, o_ref, k_ref, vt_ref, kmax_ref = refs
        key_refs, val_refs = [kl_ref, kc_ref], [vl_ref, vc_ref]
    else:
        lam_ref, q_ref, kc_ref, vc_ref, sub_ref, sel_ref, o_ref, k_ref, vt_ref, kmax_ref = refs
        key_refs, val_refs = [kc_ref], [vc_ref]

    def pair_sq_norms(x):
        xf = x.astype(F32)
        return _dot_nt(sel_ref[...], (xf * xf).astype(BF16))

    @pl.when(pl.program_id(1) == 0)
    def _():
        lo = 0
        for kr, vr in zip(key_refs, val_refs):
            cnt = kr.shape[1]
            k_ref[lo:lo + cnt, :] = kr[0]
            for h in range(N_HEADS):
                vt_ref[h, 0:HEAD_DIM, lo:lo + cnt] = vr[0, h * HEAD_DIM:(h + 1) * HEAD_DIM, :]
                vt_ref[h, HEAD_DIM:VT_ROWS, lo:lo + cnt] = jnp.ones((VT_ROWS - HEAD_DIM, cnt), BF16)
            lo += cnt
        kmax_ref[...] = jnp.broadcast_to(pair_sq_norms(k_ref[...]).max(axis=1, keepdims=True), kmax_ref.shape)

    q = q_ref[0]
    lam = lam_ref[0]
    lane = _lane()
    bound = jnp.sqrt(pair_sq_norms(q) * kmax_ref[:, 0:1])

    def attend(shift_fn):
        outs = []
        tq = q.shape[0]
        heads_per_dot = min(N_HEADS, DIFF_SCORE_COLS // (2 * tq))
        for h0 in range(0, N_HEADS, heads_per_dot):
            pairs = range(2 * h0, 2 * (h0 + heads_per_dot))
            qm = jnp.concatenate(
                [jnp.where((lane >= pair * DIFF_QK) & (lane < (pair + 1) * DIFF_QK), q, jnp.zeros_like(q))
                 for pair in pairs], axis=0)
            s = _dot_nt(k_ref[...], qm)
            e = jnp.exp2(s - shift_fn(s, pairs)).astype(BF16)
            for i in range(heads_per_dot):
                o = _dot(vt_ref[h0 + i], e[:, 2 * i * tq:2 * (i + 1) * tq])
                outs += [o[:, 0:tq], o[:, tq:]]
        return outs

    def finish(outs):
        heads = []
        for h in range(N_HEADS):
            o1, o2 = outs[2 * h], outs[2 * h + 1]
            o = (o1[0:HEAD_DIM] * (1.0 / o1[HEAD_DIM:HEAD_DIM + 1])
                 - o2[0:HEAD_DIM] * (lam / o2[HEAD_DIM:HEAD_DIM + 1]))
            heads.append(o * lax.rsqrt(jnp.mean(o * o, axis=0, keepdims=True) + EPS))
        o_ref[0] = (jnp.concatenate(heads, axis=0) * sub_ref[...]).T.astype(BF16)

    outs = attend(lambda s, pairs: jnp.concatenate([bound[p:p + 1] for p in pairs], axis=1))
    finish(outs)
    smallest = outs[0][HEAD_DIM:HEAD_DIM + 1]
    for o in outs[1:]:
        smallest = jnp.minimum(smallest, o[HEAD_DIM:HEAD_DIM + 1])

    @pl.when(jnp.min(smallest) < SUM_FLOOR)
    def _():
        finish(attend(lambda s, pairs: s.max(axis=0, keepdims=True)))


def _diff(lam, p3, vt3, pc3, vtc3, sub_row, tq, has_lat):
    q_src = p3 if has_lat else pc3
    sub_t = jnp.broadcast_to(sub_row.reshape(GROUP_W, 1), (GROUP_W, tq))
    b, n, _ = q_src.shape
    nc = pc3.shape[1]
    smem = pl.BlockSpec(memory_space=pltpu.SMEM)
    in_specs = [smem, pl.BlockSpec((1, tq, GROUP_W), lambda i, j: (i, j, COL_DQ))]
    args = [lam, q_src]
    if has_lat:
        in_specs += [pl.BlockSpec((1, n, GROUP_W), lambda i, j: (i, 0, COL_DK)),
                     pl.BlockSpec((1, GROUP_W, n), lambda i, j: (i, 0, 0))]
        args += [p3, vt3]
    in_specs += [pl.BlockSpec((1, nc, GROUP_W), lambda i, j: (i, 0, COL_DK)),
                 pl.BlockSpec((1, GROUP_W, nc), lambda i, j: (i, 0, 0)),
                 _resident((GROUP_W, tq)),
                 _resident((PAIR_ROWS, GROUP_W))]
    pair_sel = np.arange(PAIR_ROWS)[:, None] == np.arange(GROUP_W)[None, :] // DIFF_QK
    args += [pc3, vtc3, sub_t, jnp.asarray(pair_sel, F32).astype(BF16)]
    return pl.pallas_call(
        functools.partial(_diff_kernel, has_lat=has_lat),
        grid=(b, n // tq),
        in_specs=in_specs,
        out_specs=pl.BlockSpec((1, tq, GROUP_W), lambda i, j: (i, j, 0)),
        out_shape=jax.ShapeDtypeStruct((b, n, GROUP_W), BF16),
        scratch_shapes=[pltpu.VMEM(((n if has_lat else 0) + nc, GROUP_W), BF16),
                        pltpu.VMEM((N_HEADS, VT_ROWS, (n if has_lat else 0) + nc), BF16),
                        pltpu.VMEM((PAIR_ROWS, 128), F32)],
        compiler_params=_params(2),
        name="diff" if has_lat else "diff_ctx",
    )(*args)


def _na_kernel(*refs, local):
    if local:
        q_ref, k0_ref, k1_ref, k2_ref, v0_ref, v1_ref, v2_ref, kc_ref, vc_ref, bias_ref, o_ref = refs
        key_refs = [k0_ref, k1_ref, k2_ref, kc_ref]
        vt_refs = [v0_ref, v1_ref, v2_ref, vc_ref]
    else:
        q_ref, kc_ref, vc_ref, o_ref = refs
        key_refs = [kc_ref]
        vt_refs = [vc_ref]
    n_elems, blk = q_ref.shape[0], q_ref.shape[1]
    lane = _lane()

    def scores(el):
        q = q_ref[el]
        q_heads = jnp.concatenate(
            [jnp.where((lane >= h * HEAD_DIM) & (lane < (h + 1) * HEAD_DIM), q, jnp.zeros_like(q))
             for h in range(N_HEADS)], axis=0)
        k_all = jnp.concatenate([r[el] for r in key_refs], axis=0)
        return _dot_nt(k_all, q_heads)

    def weights(s):
        if local:
            s = s + bias_ref[0]
        return jnp.exp2(s - s.max(axis=0, keepdims=True)).astype(BF16)

    def attend(el, e):
        vt_all = jnp.concatenate([r[el] for r in vt_refs], axis=1)
        ones = jnp.ones((VT_ROWS - HEAD_DIM, vt_all.shape[1]), BF16)
        heads = []
        for h in range(N_HEADS):
            vt_h = jnp.concatenate([vt_all[h * HEAD_DIM:(h + 1) * HEAD_DIM], ones], axis=0)
            o = _dot(vt_h, e[:, h * blk:(h + 1) * blk])
            heads.append(o[0:HEAD_DIM] * (1.0 / o[HEAD_DIM:HEAD_DIM + 1]))
        o_ref[el] = jnp.concatenate(heads, axis=0).T.astype(BF16)

    es = [weights(s) for s in [scores(el) for el in range(n_elems)]]
    for el in range(n_elems):
        attend(el, es[el])


def _na(p3, vt3, pc3, vtc3, bias):
    b, n, _ = p3.shape
    nc = pc3.shape[1]
    ne = NA_ELEMS
    blk = NA_Q_ROWS * GRID_W
    steps = n // blk
    last_win = n // blk - NA_WIN_ROWS // NA_Q_ROWS

    def first(g):
        return jnp.clip(g - 1, 0, last_win)

    def cls(g, i):
        return ((g > 0).astype(jnp.int32) + (g == steps - 1).astype(jnp.int32), 0, 0)

    in_specs = [pl.BlockSpec((ne, blk, GROUP_W), lambda g, i: (i, g, COL_NQ))]
    in_specs += [pl.BlockSpec((ne, blk, GROUP_W), lambda g, i, j=j: (i, first(g) + j, COL_NK)) for j in range(3)]
    in_specs += [pl.BlockSpec((ne, GROUP_W, blk), lambda g, i, j=j: (i, 0, first(g) + j)) for j in range(3)]
    in_specs += [pl.BlockSpec((ne, nc, GROUP_W), lambda g, i: (i, 0, COL_NK)),
                 pl.BlockSpec((ne, GROUP_W, nc), lambda g, i: (i, 0, 0)),
                 pl.BlockSpec((1, 3 * blk + nc, N_HEADS * blk), cls)]
    return pl.pallas_call(
        functools.partial(_na_kernel, local=True),
        grid=(steps, b // ne),
        in_specs=in_specs,
        out_specs=pl.BlockSpec((ne, blk, GROUP_W), lambda g, i: (i, g, 0)),
        out_shape=jax.ShapeDtypeStruct((b, n, GROUP_W), BF16),
        compiler_params=_params(2),
        name="na",
    )(p3, p3, p3, p3, vt3, vt3, vt3, pc3, vtc3, bias)


def _na_ctx(pc3, vtc3):
    b, nc, _ = pc3.shape
    ne = NA_ELEMS
    return pl.pallas_call(
        functools.partial(_na_kernel, local=False),
        grid=(b // ne,),
        in_specs=[pl.BlockSpec((ne, nc, GROUP_W), lambda i: (i, 0, COL_NQ)),
                  pl.BlockSpec((ne, nc, GROUP_W), lambda i: (i, 0, COL_NK)),
                  pl.BlockSpec((ne, GROUP_W, nc), lambda i: (i, 0, 0))],
        out_specs=pl.BlockSpec((ne, nc, GROUP_W), lambda i: (i, 0, 0)),
        out_shape=jax.ShapeDtypeStruct((b, nc, GROUP_W), BF16),
        compiler_params=_params(1),
        name="na_ctx",
    )(pc3, pc3, vtc3)


def _na_row_offsets():
    first_row = NA_WIN_ROWS - NA_KH
    rules = [lambda i, a: a - i + NA_KH - 1 if a < NA_KH else None,
             lambda i, a: a - i + NA_KH // 2 - 1 if i <= a < i + NA_KH else None,
             lambda i, a: a - i - 1 if a >= first_row else None]
    return [[[rule(i, a) for a in range(NA_WIN_ROWS)] for i in range(NA_Q_ROWS)] for rule in rules]


def _na_bias_kernel(band_ref, o_ref):
    masked = jnp.full((GRID_W, GRID_W), NEG, F32)
    for kind, per_query_row in enumerate(_na_row_offsets()):
        for a in range(NA_WIN_ROWS):
            blocks = [masked if offs[a] is None else band_ref[0, offs[a]] for offs in per_query_row]
            o_ref[kind, a * GRID_W:(a + 1) * GRID_W, :] = jnp.concatenate(blocks, axis=1)
        o_ref[kind, NA_WIN_ROWS * GRID_W:, :] = jnp.zeros((o_ref.shape[1] - NA_WIN_ROWS * GRID_W, o_ref.shape[2]), F32)


def _na_bias_tables(rpb, nc):
    kc = np.arange(GRID_W)[:, None]
    j = np.arange(GRID_W)[None, :]
    c0 = np.clip(j - NA_KW // 2, 0, GRID_W - NA_KW)
    col_ok = (kc >= c0) & (kc < c0 + NA_KW)
    onehot = (np.arange(2 * NA_KW - 1)[:, None, None] == (kc - j + NA_KW - 1)[None]) & col_ok[None]
    band = jnp.einsum("hdo,okj->hdkj", rpb.astype(F32), jnp.asarray(onehot, F32), precision=lax.Precision.HIGHEST)
    band = jnp.where(col_ok, band * LOG2E, NEG)
    n_off = 2 * NA_KH - 1
    q, k = NA_Q_ROWS * GRID_W, NA_WIN_ROWS * GRID_W
    return pl.pallas_call(
        _na_bias_kernel,
        grid=(N_HEADS,),
        in_specs=[pl.BlockSpec((1, n_off, GRID_W, GRID_W), lambda h: (h, 0, 0, 0))],
        out_specs=pl.BlockSpec((3, k + nc, q), lambda h: (0, 0, h)),
        out_shape=jax.ShapeDtypeStruct((3, k + nc, N_HEADS * q), F32),
        compiler_params=_params(1),
        name="na_bias",
    )(band)


def _gmlp_kernel(u_ref, v_ref, ws_ref, bias_ref, o_ref, *, chunks_per_step):
    n = u_ref.shape[1]
    lane = _lane()
    step_rows = chunks_per_step * CHUNK
    for s in range(n // step_rows):
        base = s * step_rows
        vcat = jnp.concatenate(
            [v_ref[0, base + c * CHUNK: base + (c + 1) * CHUNK, :] for c in range(chunks_per_step)], axis=1)
        r = _dot(ws_ref[...], vcat)
        for c in range(chunks_per_step):
            mix = r[0:CHUNK, c * GROUP_W:(c + 1) * GROUP_W]
            for g in range(1, N_HEADS):
                mix = jnp.where(lane >= g * HEAD_DIM, r[g * CHUNK:(g + 1) * CHUNK, c * GROUP_W:(c + 1) * GROUP_W], mix)
            rows = slice(base + c * CHUNK, base + (c + 1) * CHUNK)
            o_ref[0, rows, :] = (u_ref[0, rows, :].astype(F32) * (mix + bias_ref[...])).astype(BF16)


def _gmlp(p3, ws_stack, bias_full, name):
    b, n, _ = p3.shape
    cps = min(4, n // CHUNK)
    return pl.pallas_call(
        functools.partial(_gmlp_kernel, chunks_per_step=cps),
        grid=(b,),
        in_specs=[pl.BlockSpec((1, n, GROUP_W), lambda i: (i, 0, COL_GU)),
                  pl.BlockSpec((1, n, GROUP_W), lambda i: (i, 0, COL_GV)),
                  _resident((N_HEADS * CHUNK, CHUNK)),
                  _resident((CHUNK, GROUP_W))],
        out_specs=pl.BlockSpec((1, n, GROUP_W), lambda i: (i, 0, 0)),
        out_shape=jax.ShapeDtypeStruct((b, n, GROUP_W), BF16),
        compiler_params=_params(1),
        name=name,
    )(p3, p3, ws_stack, bias_full)


def _mix_ffn_kernel(*refs, tiles_per_seq, seq_rows, layer):
    y_refs, (xp_ref, xm_ref, xn_ref) = refs[:12], refs[12:15]
    (mod_ref, g_ref, wo_ref, wu_hbm, cw_ref, cb_ref, wd_hbm, o_ref,
     xs_ref, hs_ref, ds_ref, act_ref, wu_ref, wd_ref, sem) = refs[15:]
    tm = xm_ref.shape[0]
    groups = tm // 8
    pos = pl.program_id(0) % tiles_per_seq
    first_step = pl.program_id(0) == 0

    def wu_copy():
        return pltpu.make_async_copy(wu_hbm.at[layer], wu_ref, sem.at[0])

    def wd_copy():
        return pltpu.make_async_copy(wd_hbm.at[layer], wd_ref, sem.at[1])

    @pl.when(first_step)
    def _():
        wu_copy().start()
        wd_copy().start()

    def mixed(y_parts, x):
        return x + mod_ref[0, 2:3, :] * _dot(jnp.concatenate(y_parts, axis=1), wo_ref[...])

    def prenorm(x):
        ms = jnp.mean(x * x, axis=-1, keepdims=True)
        h = x * lax.rsqrt(ms + EPS) * g_ref[...]
        return h * (1.0 + mod_ref[0, 4:5, :]) + mod_ref[0, 3:4, :]

    x_all = mixed([jnp.concatenate([y_refs[3 * j + 1][...], y_refs[3 * j][...], y_refs[3 * j + 2][...]], axis=0)
                   for j in range(4)],
                  jnp.concatenate([xm_ref[...], xp_ref[...], xn_ref[...]], axis=0))
    xs_ref[...] = x_all[0:tm]
    h_all = prenorm(x_all)
    h_main, h_halo = h_all[0:tm], h_all[tm:]
    slabs = hs_ref.shape[0]
    pitch = hs_ref.shape[1] // 8
    for c in range(slabs):
        for s in range(8):
            hs_ref[c, s * pitch:s * pitch + groups, :] = h_main[s * groups:(s + 1) * groups, c * LANES:(c + 1) * LANES]
    before = jnp.where(pos != 0, h_halo[HALO - 1:HALO], 0.0)
    after = jnp.where(pos != tiles_per_seq - 1, h_halo[HALO:HALO + 1], 0.0)
    row = lax.broadcasted_iota(jnp.int32, (HALO, 1), 0)
    h_edge = jnp.where(row == 0, before, jnp.where(row == HALO - 1, after, 0.0))
    h_perm = jnp.concatenate(
        [jnp.concatenate([hs_ref[c, pl.ds(j, 8, stride=pitch), :] for c in range(slabs)], axis=1)
         for j in range(groups)], axis=0)
    lhs = jnp.concatenate([h_perm, h_edge], axis=0).astype(BF16)

    @pl.when(first_step)
    def _():
        wu_copy().wait()

    z_all = _dot(lhs, wu_ref[...])

    sub = lax.broadcasted_iota(jnp.int32, (8, FF_CHUNK), 0)
    inner_starts = [s for s in range(1, 8) if (s * groups) % seq_rows == 0]
    starts_seq = functools.reduce(jnp.logical_or, [sub == s for s in inner_starts], sub < 0)
    ends_seq = functools.reduce(jnp.logical_or, [sub == s - 1 for s in inner_starts], sub < 0)

    def conv(lo):
        cols = slice(lo, lo + FF_CHUNK)
        z = z_all[0:tm, cols]
        edge = z_all[tm:tm + HALO, cols]
        first = jnp.where(sub == 0, edge[0:8], jnp.where(starts_seq, 0.0, pltpu.roll(z[tm - 8:tm], 1, 0)))
        last = jnp.where(sub == 7, edge[HALO - 8:HALO], jnp.where(ends_seq, 0.0, pltpu.roll(z[0:8], 7, 0)))
        return (jnp.concatenate([first, z[0:tm - 8]], axis=0) * cw_ref[0:1, cols] + z * cw_ref[1:2, cols]
                + jnp.concatenate([z[8:tm], last], axis=0) * cw_ref[2:3, cols] + cb_ref[:, cols])

    for c in range(D_FF // FF_CHUNK):
        g = conv(c * FF_CHUNK)
        v = conv(D_FF + c * FF_CHUNK)
        act_ref[:, c * FF_CHUNK:(c + 1) * FF_CHUNK] = (g / (1.0 + jnp.exp(-g)) * v).astype(BF16)
    @pl.when(first_step)
    def _():
        wd_copy().wait()

    down = _dot(act_ref[...], wd_ref[...])
    for j in range(groups):
        for c in range(slabs):
            ds_ref[c, pl.ds(j, 8, stride=pitch), :] = down[j * 8:(j + 1) * 8, c * LANES:(c + 1) * LANES]
    down = jnp.concatenate(
        [jnp.concatenate([ds_ref[c, s * pitch:s * pitch + groups, :] for s in range(8)], axis=0)
         for c in range(slabs)], axis=1)
    o_ref[...] = xs_ref[...] + mod_ref[0, 5:6, :] * down


def _mix_ffn(ys, x2d, mod, g_row, wo_bf, wu_bf, conv_w, conv_b, wd_bf, li, n, tm, ctx_stream):
    rows = x2d.shape[0]
    assert rows % tm == 0 and (n % tm == 0 or (tm % n == 0 and n % (tm // 8) == 0))
    tpb = max(1, n // tm)
    per_tile = tm // HALO
    last_halo = rows // HALO - 1
    mod_row = (lambda i: (ADA_ROWS // 2, 0, 0)) if ctx_stream else (lambda i: (i // tpb, 0, 0))

    def halo_specs(width):
        return [pl.BlockSpec((HALO, width), lambda i: (jnp.maximum(i * per_tile - 1, 0), 0)),
                pl.BlockSpec((tm, width), lambda i: (i, 0)),
                pl.BlockSpec((HALO, width), lambda i: (jnp.minimum((i + 1) * per_tile, last_halo), 0))]

    y2d = [y.reshape(rows, GROUP_W) for y in ys]
    return pl.pallas_call(
        functools.partial(_mix_ffn_kernel, tiles_per_seq=tpb, seq_rows=n, layer=li),
        grid=(rows // tm,),
        in_specs=halo_specs(GROUP_W) * 4 + halo_specs(D_MODEL) + [
            pl.BlockSpec((1, 6, D_MODEL), mod_row),
            _resident((1, D_MODEL)),
            _resident_layer((D_MODEL, D_MODEL), li),
            pl.BlockSpec(memory_space=pl.ANY),
            _resident((3, 2 * D_FF)),
            _resident((1, 2 * D_FF)),
            pl.BlockSpec(memory_space=pl.ANY),
        ],
        out_specs=pl.BlockSpec((tm, D_MODEL), lambda i: (i, 0)),
        out_shape=jax.ShapeDtypeStruct((rows, D_MODEL), F32),
        scratch_shapes=[pltpu.VMEM((tm, D_MODEL), F32)]
        + [pltpu.VMEM((D_MODEL // LANES, 8 * (tm // 8 + 8), LANES), F32)] * 2
        + [pltpu.VMEM((tm, D_FF), BF16), pltpu.VMEM((D_MODEL, 2 * D_FF), BF16), pltpu.VMEM((D_FF, D_MODEL), BF16),
           pltpu.SemaphoreType.DMA((2,))],
        compiler_params=_params(1),
        name="mix_ffn_ctx" if ctx_stream else "mix_ffn",
    )(*[a for y in y2d for a in (y, y, y)], x2d, x2d, x2d, mod, g_row, wo_bf, wu_bf, conv_w,
      conv_b.reshape(1, 2 * D_FF), wd_bf)


def _dft_tables(n):
    lo_n = 64
    half = n // 2
    t = jnp.arange(half, dtype=jnp.int32)

    def table(k):
        ang = ((k[:, None] * t[None, :]) % n).astype(F32) * (2.0 * math.pi / n)
        return jnp.cos(ang), jnp.sin(ang)

    (ch, sh), (cl, sl) = table(jnp.arange(n // lo_n, dtype=jnp.int32) * lo_n), table(jnp.arange(lo_n, dtype=jnp.int32))
    cn = (ch[:, None] * cl[None] - sh[:, None] * sl[None]).reshape(n, half).astype(BF16)
    sn = (sh[:, None] * cl[None] + ch[:, None] * sl[None]).reshape(n, half).astype(BF16)
    rb = min(GROUP_W, half)
    r = np.arange(rb)
    rev = np.zeros((rb, 2 * rb), np.float32)
    rev[r[1:], rb - r[1:]] = 1.0
    rev[0, rb] = 1.0
    c = np.arange(HEAD_DIM)
    angc = 2.0 * np.pi * ((c[:, None] * c[None, :]) % HEAD_DIM) / HEAD_DIM
    norm = 1.0 / math.sqrt(n * HEAD_DIM)
    eye = np.eye(N_HEADS)
    bdc = jnp.asarray(np.kron(eye, np.cos(angc) * norm), F32).astype(BF16)
    bds = jnp.asarray(np.kron(eye, np.sin(angc) * norm), F32).astype(BF16)
    return bdc, bds, cn, sn, jnp.asarray(rev, F32).astype(BF16)


def _rope_tables(n):
    n_freq = DIFF_QK // 4
    freqs = ROPE_BASE ** (-jnp.arange(n_freq, dtype=F32) / n_freq)
    t = jnp.arange(n)
    row = (t // GRID_W).astype(F32)
    col = (t % GRID_W).astype(F32)
    ang = jnp.concatenate([row[:, None] * freqs, col[:, None] * freqs], axis=-1)
    cos, sin = jnp.cos(ang), jnp.sin(ang)
    reps = GROUP_W // DIFF_QK
    return jnp.tile(jnp.concatenate([cos, cos], axis=-1), (1, reps)), jnp.tile(jnp.concatenate([-sin, sin], axis=-1), (1, reps))


def _group_mean_matrix(group):
    return jnp.asarray(np.kron(np.eye(GROUP_W // group), np.full((group, group), 1.0 / group)), F32).astype(BF16)


def kernel(x, c, ctx, c_ctx, w_ada, b_ada, g_mix, g_ffn, w_in, w_out, diff_qn, diff_kn, diff_lam, diff_subln, na_qn,
           na_kn, na_rpb, gmlp_norm, gmlp_ws, gmlp_b, ffn_up, ffn_conv, ffn_conv_b, ffn_down):
    b, n, d = x.shape
    nc = ctx.shape[1]
    assert (d, n % (2 * TOKEN_TILE), nc % CHUNK, b % NA_ELEMS, b < ADA_ROWS // 2 + 1) == (D_MODEL, 0, 0, 0, True)
    tm = TOKEN_TILE
    tq = DIFF_QUERY_TILE

    c_all = jnp.zeros((ADA_ROWS, d), F32).at[:b].set(c).at[ADA_ROWS // 2].set(c_ctx)
    mods = _ada(c_all, w_ada, b_ada).reshape(DEPTH, ADA_ROWS, 6, d)

    g32 = _group_mean_matrix(DIFF_QK)
    g64 = _group_mean_matrix(HEAD_DIM)
    rope_tabs = _rope_tables(n)
    dft_lat = _dft_tables(n)
    dft_ctx = _dft_tables(nc)

    w_in_kernel_order = jnp.concatenate([w_in[..., r * GROUP_W:(r + 1) * GROUP_W] for r in W_ORDER], axis=-1)
    w_in_bf, w_out_bf, wu_bf, wd_bf = (w.astype(BF16) for w in (w_in_kernel_order, w_out, ffn_up, ffn_down))

    x2d = x.reshape(b * n, d)
    c2d = ctx.reshape(b * nc, d)
    for li in range(DEPTH):
        ctx_out = li < DEPTH - 1
        lam_init = 0.8 - 0.6 * math.exp(-0.3 * li)
        lf = diff_lam[li].astype(F32)
        lam = (jnp.exp(jnp.sum(lf[0] * lf[1])) - jnp.exp(jnp.sum(lf[2] * lf[3])) + lam_init).reshape(1)
        mod = mods[li]
        gains = jnp.zeros((8, GROUP_W), F32)
        gains = gains.at[0].set(jnp.tile(diff_qn[li], GROUP_W // DIFF_QK) * (DIFF_QK ** -0.5 * LOG2E))
        gains = gains.at[1].set(jnp.tile(diff_kn[li], GROUP_W // DIFF_QK))
        gains = gains.at[2].set(jnp.tile(na_qn[li], N_HEADS) * (HEAD_DIM ** -0.5 * LOG2E))
        gains = gains.at[3].set(jnp.tile(na_kn[li], N_HEADS))
        gains = gains.at[4].set(gmlp_norm[li])
        sub_row = (jnp.tile(diff_subln[li], N_HEADS) * (1.0 - lam_init)).reshape(1, GROUP_W)
        g_mix_row = g_mix[li].reshape(1, d)
        g_ffn_row = g_ffn[li].reshape(1, d)
        ws_stack = gmlp_ws[li].reshape(N_HEADS * CHUNK, CHUNK).astype(BF16)
        gbias = jnp.repeat(gmlp_b[li].T, HEAD_DIM, axis=1)
        bias_tab = _na_bias_tables(na_rpb[li], nc)

        p2d, dvt, nvt = _inproj(x2d, mod, g_mix_row, w_in_bf, li, gains, g32, g64, rope_tabs, n, 2 * tm, False)
        pc2d, dvtc, nvtc = _inproj(c2d, mod, g_mix_row, w_in_bf, li, gains, g32, g64, None, nc, nc, True)
        p3 = p2d.reshape(b, n, P_W)
        pc3 = pc2d.reshape(b, nc, P_W)

        ys = [_fourier(p3, *dft_lat, "fourier"),
              _diff(lam, p3, dvt, pc3, dvtc, sub_row, tq, True),
              _na(p3, nvt, pc3, nvtc, bias_tab),
              _gmlp(p3, ws_stack, gbias, "gmlp")]
        ffn_w = (g_ffn_row, w_out_bf, wu_bf, ffn_conv[li], ffn_conv_b[li], wd_bf, li)
        x2d = _mix_ffn(ys, x2d, mod, *ffn_w, n, tm, False)

        if ctx_out:
            ycs = [_fourier(pc3, *dft_ctx, "fourier_ctx"),
                   _diff(lam, None, None, pc3, dvtc, sub_row, nc, False),
                   _na_ctx(pc3, nvtc),
                   _gmlp(pc3, ws_stack, gbias, "gmlp_ctx")]
            c2d = _mix_ffn(ycs, c2d, mod, *ffn_w, nc, tm, True)
    return x2d.reshape(b, n, d)
```

```python
import functools
import math

import jax
import jax.numpy as jnp
import numpy as np
from jax import lax
from jax.experimental import pallas as pl
from jax.experimental.pallas import tpu as pltpu

F32 = jnp.float32
BF16 = jnp.bfloat16

D_MODEL = 1024
DEPTH = 4
GRID_W = 64
GROUP_W = 256
HEAD_DIM = 64
N_HEADS = 4
DIFF_QK = 32
NA_KH = 8
NA_KW = 16
CHUNK = 128
D_FF = 2816
ROPE_BASE = 10000.0
EPS = 1e-6
IN_W = 9 * GROUP_W

REF_F, REF_DQ, REF_DK, REF_DV, REF_NQ, REF_NK, REF_NV, REF_GU, REF_GV = range(9)
W_ORDER = (REF_GV, REF_DQ, REF_DK, REF_NQ, REF_NK, REF_GU, REF_DV, REF_NV, REF_F)
W_GV, W_DQ, W_DK, W_NQ, W_NK, W_GU, W_DV, W_NV, W_F = range(9)
COL_F, COL_DQ, COL_DK, COL_NQ, COL_NK, COL_GU, COL_GV = range(7)
P_W = 7 * GROUP_W

VT_ROWS = HEAD_DIM + 16
DIFF_SCORE_COLS = 2048
PAIR_ROWS = 16
SUM_FLOOR = 2.0 ** -80
NA_Q_ROWS = 4
NA_ELEMS = 4
NA_WIN_ROWS = 12
NEG = -1e30
LOG2E = math.log2(math.e)

ADA_ROWS = 16
VMEM_LIMIT = 56 * 1024 * 1024
TOKEN_TILE = 512
DIFF_QUERY_TILE = 512
LANES = 128
HALO = 16
FF_CHUNK = 256


def _dot(a, b):
    return jnp.dot(a, b, preferred_element_type=F32)


def _dot_nt(a, b):
    return lax.dot_general(a, b, (((1,), (1,)), ((), ())), preferred_element_type=F32)


def _resident(shape):
    nd = len(shape)
    return pl.BlockSpec(shape, lambda *_: (0,) * nd, pipeline_mode=pl.Buffered(1))


def _resident_layer(shape, li):
    nd = len(shape)
    return pl.BlockSpec((None,) + tuple(shape), lambda *_: (li,) + (0,) * nd, pipeline_mode=pl.Buffered(1))


def _params(n_axes):
    return pltpu.CompilerParams(dimension_semantics=("arbitrary",) * n_axes, vmem_limit_bytes=VMEM_LIMIT)


def _lane(width=GROUP_W):
    return lax.broadcasted_iota(jnp.int32, (1, width), 1)


def _ada_kernel(c_ref, w_ref, b_ref, o_ref):
    c = c_ref[...]
    s = c / (1.0 + jnp.exp(-c))
    w = w_ref[0]
    s_hi = s.astype(BF16)
    s_lo = (s - s_hi.astype(F32)).astype(BF16)
    w_hi = w.astype(BF16)
    w_lo = (w - w_hi.astype(F32)).astype(BF16)
    o_ref[0] = _dot(s_hi, w_hi) + _dot(s_lo, w_hi) + _dot(s_hi, w_lo) + b_ref[0]


def _ada(c_all, w_ada, b_ada):
    tn = 1536
    out_w = 6 * D_MODEL
    return pl.pallas_call(
        _ada_kernel,
        grid=(DEPTH, out_w // tn),
        in_specs=[
            pl.BlockSpec((ADA_ROWS, D_MODEL), lambda l, j: (0, 0)),
            pl.BlockSpec((1, D_MODEL, tn), lambda l, j: (l, 0, j)),
            pl.BlockSpec((1, 1, tn), lambda l, j: (l, 0, j)),
        ],
        out_specs=pl.BlockSpec((1, ADA_ROWS, tn), lambda l, j: (l, 0, j)),
        out_shape=jax.ShapeDtypeStruct((DEPTH, ADA_ROWS, out_w), F32),
        compiler_params=_params(2),
        name="ada",
    )(c_all, w_ada, b_ada.reshape(DEPTH, 1, out_w))


def _gelu(x):
    return 0.5 * x * (1.0 + jnp.tanh(math.sqrt(2.0 / math.pi) * (x + 0.044715 * (x * x * x))))


def _inproj_kernel(*refs, rope):
    if rope:
        x_ref, mod_ref, g_ref, w_ref, gains_ref, g32_ref, g64_ref, cos_ref, sin_ref, o_ref, dvt_ref, nvt_ref = refs
    else:
        x_ref, mod_ref, g_ref, w_ref, gains_ref, g32_ref, g64_ref, o_ref, dvt_ref, nvt_ref = refs
    x = x_ref[...]
    ms = jnp.mean(x * x, axis=-1, keepdims=True)
    h = x * lax.rsqrt(ms + EPS) * g_ref[...]
    h = h * (1.0 + mod_ref[0, 1:2, :]) + mod_ref[0, 0:1, :]
    p = _dot(h.astype(BF16), w_ref[...])

    def col(j):
        return p[:, j * GROUP_W:(j + 1) * GROUP_W]

    def put(j, v):
        o_ref[:, j * GROUP_W:(j + 1) * GROUP_W] = v.astype(BF16)

    def group_norm(v, gmat_ref, gain_row):
        gms = _dot((v * v).astype(BF16), gmat_ref[...])
        return v * lax.rsqrt(gms + EPS) * gains_ref[gain_row:gain_row + 1, :]

    def rotary(v):
        if not rope:
            return v
        first_half = (_lane() % DIFF_QK) < (DIFF_QK // 2)
        partner = jnp.where(first_half, pltpu.roll(v, GROUP_W - DIFF_QK // 2, 1), pltpu.roll(v, DIFF_QK // 2, 1))
        return v * cos_ref[...] + partner * sin_ref[...]

    put(COL_GV, group_norm(_gelu(col(W_GV)), g64_ref, 4))
    put(COL_DQ, rotary(group_norm(col(W_DQ), g32_ref, 0)))
    put(COL_DK, rotary(group_norm(col(W_DK), g32_ref, 1)))
    put(COL_NQ, group_norm(col(W_NQ), g64_ref, 2))
    put(COL_NK, group_norm(col(W_NK), g64_ref, 3))
    put(COL_GU, _gelu(col(W_GU)))
    dvt_ref[0] = col(W_DV).T.astype(BF16)
    nvt_ref[0] = col(W_NV).T.astype(BF16)
    put(COL_F, col(W_F))


def _inproj(x2d, mod, g_row, w_bf, li, gains, g32, g64, rope_tabs, n, tm, ctx_stream):
    rows = x2d.shape[0]
    tpb = n // tm
    mod_row = (lambda i: (ADA_ROWS // 2, 0, 0)) if ctx_stream else (lambda i: (i // tpb, 0, 0))
    in_specs = [
        pl.BlockSpec((tm, D_MODEL), lambda i: (i, 0)),
        pl.BlockSpec((1, 6, D_MODEL), mod_row),
        _resident((1, D_MODEL)),
        _resident_layer((D_MODEL, IN_W), li),
        _resident((8, GROUP_W)),
        _resident((GROUP_W, GROUP_W)),
        _resident((GROUP_W, GROUP_W)),
    ]
    args = [x2d, mod, g_row, w_bf, gains, g32, g64]
    if rope_tabs is not None:
        in_specs += [pl.BlockSpec((tm, GROUP_W), lambda i: (i % tpb, 0))] * 2
        args += list(rope_tabs)
    return pl.pallas_call(
        functools.partial(_inproj_kernel, rope=rope_tabs is not None),
        grid=(rows // tm,),
        in_specs=in_specs,
        out_specs=[pl.BlockSpec((tm, P_W), lambda i: (i, 0))]
        + [pl.BlockSpec((1, GROUP_W, tm), lambda i: (i // tpb, 0, i % tpb))] * 2,
        out_shape=[jax.ShapeDtypeStruct((rows, P_W), BF16)]
        + [jax.ShapeDtypeStruct((rows // n, GROUP_W, n), BF16)] * 2,
        compiler_params=_params(1),
        name="inproj_ctx" if ctx_stream else "inproj",
    )(*args)


def _fourier_kernel(p_ref, bdc_ref, bds_ref, cn_ref, sn_ref, rev_ref, o_ref):
    n = p_ref.shape[1]
    half = n // 2
    rb = rev_ref.shape[0]
    nb = n // rb
    mirrored = []
    for blk in range(half // rb):
        src = nb - blk - 1
        nxt = (src + 1) % nb
        pair = jnp.concatenate([p_ref[0, src * rb:(src + 1) * rb, :], p_ref[0, nxt * rb:(nxt + 1) * rb, :]], axis=0)
        mirrored.append(_dot(rev_ref[...], pair))
    mirrored = jnp.concatenate(mirrored, axis=0)
    x = p_ref[0, 0:half, :].astype(F32)
    first = lax.broadcasted_iota(jnp.int32, x.shape, 0) == 0
    fold_c = jnp.where(first, x, x + mirrored).astype(BF16)
    fold_s = (x - mirrored).astype(BF16)
    a = _dot(fold_c, bdc_ref[...]).astype(BF16)
    b = _dot(fold_s, bds_ref[...]).astype(BF16)
    mid = _dot(p_ref[0, half:half + 16, :], bdc_ref[...])[0:1]
    out = _dot(cn_ref[...], a) - _dot(sn_ref[...], b)
    odd = (lax.broadcasted_iota(jnp.int32, out.shape, 0) & 1) == 1
    o_ref[0] = (out + jnp.where(odd, -mid, mid)).astype(BF16)


def _fourier(p3, bdc, bds, cn, sn, rev, name):
    b, n, _ = p3.shape
    return pl.pallas_call(
        _fourier_kernel,
        grid=(b,),
        in_specs=[
            pl.BlockSpec((1, n, GROUP_W), lambda i: (i, 0, COL_F)),
            _resident((GROUP_W, GROUP_W)),
            _resident((GROUP_W, GROUP_W)),
            _resident((n, n // 2)),
            _resident((n, n // 2)),
            _resident(rev.shape),
        ],
        out_specs=pl.BlockSpec((1, n, GROUP_W), lambda i: (i, 0, 0)),
        out_shape=jax.ShapeDtypeStruct((b, n, GROUP_W), BF16),
        compiler_params=_params(1),
        name=name,
    )(p3, bdc, bds, cn, sn, rev)


def _diff_kernel(*refs, has_lat):
    if has_lat:
        lam_ref, q_ref, kl_ref, vl_ref, kc_ref, vc_ref, sub_ref, sel_ref, o_ref, k_ref, vt_ref, kmax_ref = refs
        key_refs, val_refs = [kl_ref, kc_ref], [vl_ref, vc_ref]
    else:
        lam_ref, q_ref, kc_ref, vc_ref, sub_ref, sel_ref, o_ref, k_ref, vt_ref, kmax_ref = refs
        key_refs, val_refs = [kc_ref], [vc_ref]

    def pair_sq_norms(x):
        xf = x.astype(F32)
        return _dot_nt(sel_ref[...], (xf * xf).astype(BF16))

    @pl.when(pl.program_id(1) == 0)
    def _():
        lo = 0
        for kr, vr in zip(key_refs, val_refs):
            cnt = kr.shape[1]
            k_ref[lo:lo + cnt, :] = kr[0]
            for h in range(N_HEADS):
                vt_ref[h, 0:HEAD_DIM, lo:lo + cnt] = vr[0, h * HEAD_DIM:(h + 1) * HEAD_DIM, :]
                vt_ref[h, HEAD_DIM:VT_ROWS, lo:lo + cnt] = jnp.ones((VT_ROWS - HEAD_DIM, cnt), BF16)
            lo += cnt
        kmax_ref[...] = jnp.broadcast_to(pair_sq_norms(k_ref[...]).max(axis=1, keepdims=True), kmax_ref.shape)

    q = q_ref[0]
    lam = lam_ref[0]
    lane = _lane()
    bound = jnp.sqrt(pair_sq_norms(q) * kmax_ref[:, 0:1])

    def attend(shift_fn):
        outs = []
        tq = q.shape[0]
        heads_per_dot = min(N_HEADS, DIFF_SCORE_COLS // (2 * tq))
        for h0 in range(0, N_HEADS, heads_per_dot):
            pairs = range(2 * h0, 2 * (h0 + heads_per_dot))
            qm = jnp.concatenate(
                [jnp.where((lane >= pair * DIFF_QK) & (lane < (pair + 1) * DIFF_QK), q, jnp.zeros_like(q))
                 for pair in pairs], axis=0)
            s = _dot_nt(k_ref[...], qm)
            e = jnp.exp2(s - shift_fn(s, pairs)).astype(BF16)
            for i in range(heads_per_dot):
                o = _dot(vt_ref[h0 + i], e[:, 2 * i * tq:2 * (i + 1) * tq])
                outs += [o[:, 0:tq], o[:, tq:]]
        return outs

    def finish(outs):
        heads = []
        for h in range(N_HEADS):
            o1, o2 = outs[2 * h], outs[2 * h + 1]
            o = (o1[0:HEAD_DIM] * (1.0 / o1[HEAD_DIM:HEAD_DIM + 1])
                 - o2[0:HEAD_DIM] * (lam / o2[HEAD_DIM:HEAD_DIM + 1]))
            heads.append(o * lax.rsqrt(jnp.mean(o * o, axis=0, keepdims=True) + EPS))
        o_ref[0] = (jnp.concatenate(heads, axis=0) * sub_ref[...]).T.astype(BF16)

    outs = attend(lambda s, pairs: jnp.concatenate([bound[p:p + 1] for p in pairs], axis=1))
    finish(outs)
    smallest = outs[0][HEAD_DIM:HEAD_DIM + 1]
    for o in outs[1:]:
        smallest = jnp.minimum(smallest, o[HEAD_DIM:HEAD_DIM + 1])

    @pl.when(jnp.min(smallest) < SUM_FLOOR)
    def _():
        finish(attend(lambda s, pairs: s.max(axis=0, keepdims=True)))


def _diff(lam, p3, vt3, pc3, vtc3, sub_row, tq, has_lat):
    q_src = p3 if has_lat else pc3
    sub_t = jnp.broadcast_to(sub_row.reshape(GROUP_W, 1), (GROUP_W, tq))
    b, n, _ = q_src.shape
    nc = pc3.shape[1]
    smem = pl.BlockSpec(memory_space=pltpu.SMEM)
    in_specs = [smem, pl.BlockSpec((1, tq, GROUP_W), lambda i, j: (i, j, COL_DQ))]
    args = [lam, q_src]
    if has_lat:
        in_specs += [pl.BlockSpec((1, n, GROUP_W), lambda i, j: (i, 0, COL_DK)),
                     pl.BlockSpec((1, GROUP_W, n), lambda i, j: (i, 0, 0))]
        args += [p3, vt3]
    in_specs += [pl.BlockSpec((1, nc, GROUP_W), lambda i, j: (i, 0, COL_DK)),
                 pl.BlockSpec((1, GROUP_W, nc), lambda i, j: (i, 0, 0)),
                 _resident((GROUP_W, tq)),
                 _resident((PAIR_ROWS, GROUP_W))]
    pair_sel = np.arange(PAIR_ROWS)[:, None] == np.arange(GROUP_W)[None, :] // DIFF_QK
    args += [pc3, vtc3, sub_t, jnp.asarray(pair_sel, F32).astype(BF16)]
    return pl.pallas_call(
        functools.partial(_diff_kernel, has_lat=has_lat),
        grid=(b, n // tq),
        in_specs=in_specs,
        out_specs=pl.BlockSpec((1, tq, GROUP_W), lambda i, j: (i, j, 0)),
        out_shape=jax.ShapeDtypeStruct((b, n, GROUP_W), BF16),
        scratch_shapes=[pltpu.VMEM(((n if has_lat else 0) + nc, GROUP_W), BF16),
                        pltpu.VMEM((N_HEADS, VT_ROWS, (n if has_lat else 0) + nc), BF16),
                        pltpu.VMEM((PAIR_ROWS, 128), F32)],
        compiler_params=_params(2),
        name="diff" if has_lat else "diff_ctx",
    )(*args)


def _na_kernel(*refs, local):
    if local:
        q_ref, k0_ref, k1_ref, k2_ref, v0_ref, v1_ref, v2_ref, kc_ref, vc_ref, bias_ref, o_ref = refs
        key_refs = [k0_ref, k1_ref, k2_ref, kc_ref]
        vt_refs = [v0_ref, v1_ref, v2_ref, vc_ref]
    else:
        q_ref, kc_ref, vc_ref, o_ref = refs
        key_refs = [kc_ref]
        vt_refs = [vc_ref]
    n_elems, blk = q_ref.shape[0], q_ref.shape[1]
    lane = _lane()

    def scores(el):
        q = q_ref[el]
        q_heads = jnp.concatenate(
            [jnp.where((lane >= h * HEAD_DIM) & (lane < (h + 1) * HEAD_DIM), q, jnp.zeros_like(q))
             for h in range(N_HEADS)], axis=0)
        k_all = jnp.concatenate([r[el] for r in key_refs], axis=0)
        return _dot_nt(k_all, q_heads)

    def weights(s):
        if local:
            s = s + bias_ref[0]
        return jnp.exp2(s - s.max(axis=0, keepdims=True)).astype(BF16)

    def attend(el, e):
        vt_all = jnp.concatenate([r[el] for r in vt_refs], axis=1)
        ones = jnp.ones((VT_ROWS - HEAD_DIM, vt_all.shape[1]), BF16)
        heads = []
        for h in range(N_HEADS):
            vt_h = jnp.concatenate([vt_all[h * HEAD_DIM:(h + 1) * HEAD_DIM], ones], axis=0)
            o = _dot(vt_h, e[:, h * blk:(h + 1) * blk])
            heads.append(o[0:HEAD_DIM] * (1.0 / o[HEAD_DIM:HEAD_DIM + 1]))
        o_ref[el] = jnp.concatenate(heads, axis=0).T.astype(BF16)

    es = [weights(s) for s in [scores(el) for el in range(n_elems)]]
    for el in range(n_elems):
        attend(el, es[el])


def _na(p3, vt3, pc3, vtc3, bias):
    b, n, _ = p3.shape
    nc = pc3.shape[1]
    ne = NA_ELEMS
    blk = NA_Q_ROWS * GRID_W
    steps = n // blk
    last_win = n // blk - NA_WIN_ROWS // NA_Q_ROWS

    def first(g):
        return jnp.clip(g - 1, 0, last_win)

    def cls(g, i):
        return ((g > 0).astype(jnp.int32) + (g == steps - 1).astype(jnp.int32), 0, 0)

    in_specs = [pl.BlockSpec((ne, blk, GROUP_W), lambda g, i: (i, g, COL_NQ))]
    in_specs += [pl.BlockSpec((ne, blk, GROUP_W), lambda g, i, j=j: (i, first(g) + j, COL_NK)) for j in range(3)]
    in_specs += [pl.BlockSpec((ne, GROUP_W, blk), lambda g, i, j=j: (i, 0, first(g) + j)) for j in range(3)]
    in_specs += [pl.BlockSpec((ne, nc, GROUP_W), lambda g, i: (i, 0, COL_NK)),
                 pl.BlockSpec((ne, GROUP_W, nc), lambda g, i: (i, 0, 0)),
                 pl.BlockSpec((1, 3 * blk + nc, N_HEADS * blk), cls)]
    return pl.pallas_call(
        functools.partial(_na_kernel, local=True),
        grid=(steps, b // ne),
        in_specs=in_specs,
        out_specs=pl.BlockSpec((ne, blk, GROUP_W), lambda g, i: (i, g, 0)),
        out_shape=jax.ShapeDtypeStruct((b, n, GROUP_W), BF16),
        compiler_params=_params(2),
        name="na",
    )(p3, p3, p3, p3, vt3, vt3, vt3, pc3, vtc3, bias)


def _na_ctx(pc3, vtc3):
    b, nc, _ = pc3.shape
    ne = NA_ELEMS
    return pl.pallas_call(
        functools.partial(_na_kernel, local=False),
        grid=(b // ne,),
        in_specs=[pl.BlockSpec((ne, nc, GROUP_W), lambda i: (i, 0, COL_NQ)),
                  pl.BlockSpec((ne, nc, GROUP_W), lambda i: (i, 0, COL_NK)),
                  pl.BlockSpec((ne, GROUP_W, nc), lambda i: (i, 0, 0))],
        out_specs=pl.BlockSpec((ne, nc, GROUP_W), lambda i: (i, 0, 0)),
        out_shape=jax.ShapeDtypeStruct((b, nc, GROUP_W), BF16),
        compiler_params=_params(1),
        name="na_ctx",
    )(pc3, pc3, vtc3)


def _na_row_offsets():
    first_row = NA_WIN_ROWS - NA_KH
    rules = [lambda i, a: a - i + NA_KH - 1 if a < NA_KH else None,
             lambda i, a: a - i + NA_KH // 2 - 1 if i <= a < i + NA_KH else None,
             lambda i, a: a - i - 1 if a >= first_row else None]
    return [[[rule(i, a) for a in range(NA_WIN_ROWS)] for i in range(NA_Q_ROWS)] for rule in rules]


def _na_bias_kernel(band_ref, o_ref):
    masked = jnp.full((GRID_W, GRID_W), NEG, F32)
    for kind, per_query_row in enumerate(_na_row_offsets()):
        for a in range(NA_WIN_ROWS):
            blocks = [masked if offs[a] is None else band_ref[0, offs[a]] for offs in per_query_row]
            o_ref[kind, a * GRID_W:(a + 1) * GRID_W, :] = jnp.concatenate(blocks, axis=1)
        o_ref[kind, NA_WIN_ROWS * GRID_W:, :] = jnp.zeros((o_ref.shape[1] - NA_WIN_ROWS * GRID_W, o_ref.shape[2]), F32)


def _na_bias_tables(rpb, nc):
    kc = np.arange(GRID_W)[:, None]
    j = np.arange(GRID_W)[None, :]
    c0 = np.clip(j - NA_KW // 2, 0, GRID_W - NA_KW)
    col_ok = (kc >= c0) & (kc < c0 + NA_KW)
    onehot = (np.arange(2 * NA_KW - 1)[:, None, None] == (kc - j + NA_KW - 1)[None]) & col_ok[None]
    band = jnp.einsum("hdo,okj->hdkj", rpb.astype(F32), jnp.asarray(onehot, F32), precision=lax.Precision.HIGHEST)
    band = jnp.where(col_ok, band * LOG2E, NEG)
    n_off = 2 * NA_KH - 1
    q, k = NA_Q_ROWS * GRID_W, NA_WIN_ROWS * GRID_W
    return pl.pallas_call(
        _na_bias_kernel,
        grid=(N_HEADS,),
        in_specs=[pl.BlockSpec((1, n_off, GRID_W, GRID_W), lambda h: (h, 0, 0, 0))],
        out_specs=pl.BlockSpec((3, k + nc, q), lambda h: (0, 0, h)),
        out_shape=jax.ShapeDtypeStruct((3, k + nc, N_HEADS * q), F32),
        compiler_params=_params(1),
        name="na_bias",
    )(band)


def _gmlp_kernel(u_ref, v_ref, ws_ref, bias_ref, o_ref, *, chunks_per_step):
    n = u_ref.shape[1]
    lane = _lane()
    step_rows = chunks_per_step * CHUNK
    for s in range(n // step_rows):
        base = s * step_rows
        vcat = jnp.concatenate(
            [v_ref[0, base + c * CHUNK: base + (c + 1) * CHUNK, :] for c in range(chunks_per_step)], axis=1)
        r = _dot(ws_ref[...], vcat)
        for c in range(chunks_per_step):
            mix = r[0:CHUNK, c * GROUP_W:(c + 1) * GROUP_W]
            for g in range(1, N_HEADS):
                mix = jnp.where(lane >= g * HEAD_DIM, r[g * CHUNK:(g + 1) * CHUNK, c * GROUP_W:(c + 1) * GROUP_W], mix)
            rows = slice(base + c * CHUNK, base + (c + 1) * CHUNK)
            o_ref[0, rows, :] = (u_ref[0, rows, :].astype(F32) * (mix + bias_ref[...])).astype(BF16)


def _fourier_gmlp_kernel(p_ref, bdc_ref, bds_ref, cn_ref, sn_ref, rev_ref, u_ref, v_ref, ws_ref, bias_ref,
                         of_ref, og_ref, *, chunks_per_step):
    _fourier_kernel(p_ref, bdc_ref, bds_ref, cn_ref, sn_ref, rev_ref, of_ref)
    _gmlp_kernel(u_ref, v_ref, ws_ref, bias_ref, og_ref, chunks_per_step=chunks_per_step)


def _fourier_gmlp(p3, bdc, bds, cn, sn, rev, ws_stack, bias_full, name):
    b, n, _ = p3.shape
    cps = min(4, n // CHUNK)

    def col(c):
        return pl.BlockSpec((1, n, GROUP_W), lambda i: (i, 0, c))

    return pl.pallas_call(
        functools.partial(_fourier_gmlp_kernel, chunks_per_step=cps),
        grid=(b,),
        in_specs=[col(COL_F), _resident((GROUP_W, GROUP_W)), _resident((GROUP_W, GROUP_W)),
                  _resident((n, n // 2)), _resident((n, n // 2)), _resident(rev.shape),
                  col(COL_GU), col(COL_GV), _resident((N_HEADS * CHUNK, CHUNK)), _resident((CHUNK, GROUP_W))],
        out_specs=[col(0), col(0)],
        out_shape=[jax.ShapeDtypeStruct((b, n, GROUP_W), BF16)] * 2,
        compiler_params=_params(1),
        name=name,
    )(p3, bdc, bds, cn, sn, rev, p3, p3, ws_stack, bias_full)


def _mix_ffn_kernel(*refs, tiles_per_seq, seq_rows):
    y_refs, (xp_ref, xm_ref, xn_ref) = refs[:12], refs[12:15]
    mod_ref, g_ref, wo_ref, wu_ref, cw_ref, cb_ref, wd_ref, o_ref, xs_ref, hs_ref, ds_ref, act_ref = refs[15:]
    tm = xm_ref.shape[0]
    groups = tm // 8
    pos = pl.program_id(0) % tiles_per_seq

    def mixed(y_parts, x):
        return x + mod_ref[0, 2:3, :] * _dot(jnp.concatenate(y_parts, axis=1), wo_ref[...])

    def prenorm(x):
        ms = jnp.mean(x * x, axis=-1, keepdims=True)
        h = x * lax.rsqrt(ms + EPS) * g_ref[...]
        return h * (1.0 + mod_ref[0, 4:5, :]) + mod_ref[0, 3:4, :]

    x_all = mixed([jnp.concatenate([y_refs[3 * j + 1][...], y_refs[3 * j][...], y_refs[3 * j + 2][...]], axis=0)
                   for j in range(4)],
                  jnp.concatenate([xm_ref[...], xp_ref[...], xn_ref[...]], axis=0))
    xs_ref[...] = x_all[0:tm]
    h_all = prenorm(x_all)
    h_main, h_halo = h_all[0:tm], h_all[tm:]
    slabs = hs_ref.shape[0]
    pitch = hs_ref.shape[1] // 8
    for c in range(slabs):
        for s in range(8):
            hs_ref[c, s * pitch:s * pitch + groups, :] = h_main[s * groups:(s + 1) * groups, c * LANES:(c + 1) * LANES]
    before = jnp.where(pos != 0, h_halo[HALO - 1:HALO], 0.0)
    after = jnp.where(pos != tiles_per_seq - 1, h_halo[HALO:HALO + 1], 0.0)
    row = lax.broadcasted_iota(jnp.int32, (HALO, 1), 0)
    h_edge = jnp.where(row == 0, before, jnp.where(row == HALO - 1, after, 0.0))
    h_perm = jnp.concatenate(
        [jnp.concatenate([hs_ref[c, pl.ds(j, 8, stride=pitch), :] for c in range(slabs)], axis=1)
         for j in range(groups)], axis=0)
    z_all = _dot(jnp.concatenate([h_perm, h_edge], axis=0).astype(BF16), wu_ref[...])

    sub = lax.broadcasted_iota(jnp.int32, (8, FF_CHUNK), 0)
    inner_starts = [s for s in range(1, 8) if (s * groups) % seq_rows == 0]
    starts_seq = functools.reduce(jnp.logical_or, [sub == s for s in inner_starts], sub < 0)
    ends_seq = functools.reduce(jnp.logical_or, [sub == s - 1 for s in inner_starts], sub < 0)

    def conv(lo):
        cols = slice(lo, lo + FF_CHUNK)
        z = z_all[0:tm, cols]
        edge = z_all[tm:tm + HALO, cols]
        first = jnp.where(sub == 0, edge[0:8], jnp.where(starts_seq, 0.0, pltpu.roll(z[tm - 8:tm], 1, 0)))
        last = jnp.where(sub == 7, edge[HALO - 8:HALO], jnp.where(ends_seq, 0.0, pltpu.roll(z[0:8], 7, 0)))
        return (jnp.concatenate([first, z[0:tm - 8]], axis=0) * cw_ref[0:1, cols] + z * cw_ref[1:2, cols]
                + jnp.concatenate([z[8:tm], last], axis=0) * cw_ref[2:3, cols] + cb_ref[:, cols])

    for c in range(D_FF // FF_CHUNK):
        g = conv(c * FF_CHUNK)
        v = conv(D_FF + c * FF_CHUNK)
        act_ref[:, c * FF_CHUNK:(c + 1) * FF_CHUNK] = (g / (1.0 + jnp.exp(-g)) * v).astype(BF16)
    down = _dot(act_ref[...], wd_ref[...])
    for j in range(groups):
        for c in range(slabs):
            ds_ref[c, pl.ds(j, 8, stride=pitch), :] = down[j * 8:(j + 1) * 8, c * LANES:(c + 1) * LANES]
    down = jnp.concatenate(
        [jnp.concatenate([ds_ref[c, s * pitch:s * pitch + groups, :] for s in range(8)], axis=0)
         for c in range(slabs)], axis=1)
    o_ref[...] = xs_ref[...] + mod_ref[0, 5:6, :] * down


def _mix_ffn(ys, x2d, mod, g_row, wo_bf, wu_bf, conv_w, conv_b, wd_bf, li, n, tm, ctx_stream):
    rows = x2d.shape[0]
    assert rows % tm == 0 and (n % tm == 0 or (tm % n == 0 and n % (tm // 8) == 0))
    tpb = max(1, n // tm)
    per_tile = tm // HALO
    last_halo = rows // HALO - 1
    mod_row = (lambda i: (ADA_ROWS // 2, 0, 0)) if ctx_stream else (lambda i: (i // tpb, 0, 0))

    def halo_specs(width):
        return [pl.BlockSpec((HALO, width), lambda i: (jnp.maximum(i * per_tile - 1, 0), 0)),
                pl.BlockSpec((tm, width), lambda i: (i, 0)),
                pl.BlockSpec((HALO, width), lambda i: (jnp.minimum((i + 1) * per_tile, last_halo), 0))]

    y2d = [y.reshape(rows, GROUP_W) for y in ys]
    return pl.pallas_call(
        functools.partial(_mix_ffn_kernel, tiles_per_seq=tpb, seq_rows=n),
        grid=(rows // tm,),
        in_specs=halo_specs(GROUP_W) * 4 + halo_specs(D_MODEL) + [
            pl.BlockSpec((1, 6, D_MODEL), mod_row),
            _resident((1, D_MODEL)),
            _resident_layer((D_MODEL, D_MODEL), li),
            _resident_layer((D_MODEL, 2 * D_FF), li),
            _resident((3, 2 * D_FF)),
            _resident((1, 2 * D_FF)),
            _resident_layer((D_FF, D_MODEL), li),
        ],
        out_specs=pl.BlockSpec((tm, D_MODEL), lambda i: (i, 0)),
        out_shape=jax.ShapeDtypeStruct((rows, D_MODEL), F32),
        scratch_shapes=[pltpu.VMEM((tm, D_MODEL), F32)]
        + [pltpu.VMEM((D_MODEL // LANES, 8 * (tm // 8 + 8), LANES), F32)] * 2
        + [pltpu.VMEM((tm, D_FF), BF16)],
        compiler_params=_params(1),
        name="mix_ffn_ctx" if ctx_stream else "mix_ffn",
    )(*[a for y in y2d for a in (y, y, y)], x2d, x2d, x2d, mod, g_row, wo_bf, wu_bf, conv_w,
      conv_b.reshape(1, 2 * D_FF), wd_bf)


def _dft_tables(n):
    lo_n = 64
    half = n // 2
    t = jnp.arange(half, dtype=jnp.int32)

    def table(k):
        ang = ((k[:, None] * t[None, :]) % n).astype(F32) * (2.0 * math.pi / n)
        return jnp.cos(ang), jnp.sin(ang)

    (ch, sh), (cl, sl) = table(jnp.arange(n // lo_n, dtype=jnp.int32) * lo_n), table(jnp.arange(lo_n, dtype=jnp.int32))
    cn = (ch[:, None] * cl[None] - sh[:, None] * sl[None]).reshape(n, half).astype(BF16)
    sn = (sh[:, None] * cl[None] + ch[:, None] * sl[None]).reshape(n, half).astype(BF16)
    rb = min(GROUP_W, half)
    r = np.arange(rb)
    rev = np.zeros((rb, 2 * rb), np.float32)
    rev[r[1:], rb - r[1:]] = 1.0
    rev[0, rb] = 1.0
    c = np.arange(HEAD_DIM)
    angc = 2.0 * np.pi * ((c[:, None] * c[None, :]) % HEAD_DIM) / HEAD_DIM
    norm = 1.0 / math.sqrt(n * HEAD_DIM)
    eye = np.eye(N_HEADS)
    bdc = jnp.asarray(np.kron(eye, np.cos(angc) * norm), F32).astype(BF16)
    bds = jnp.asarray(np.kron(eye, np.sin(angc) * norm), F32).astype(BF16)
    return bdc, bds, cn, sn, jnp.asarray(rev, F32).astype(BF16)


def _rope_tables(n):
    n_freq = DIFF_QK // 4
    freqs = ROPE_BASE ** (-jnp.arange(n_freq, dtype=F32) / n_freq)
    t = jnp.arange(n)
    row = (t // GRID_W).astype(F32)
    col = (t % GRID_W).astype(F32)
    ang = jnp.concatenate([row[:, None] * freqs, col[:, None] * freqs], axis=-1)
    cos, sin = jnp.cos(ang), jnp.sin(ang)
    reps = GROUP_W // DIFF_QK
    return jnp.tile(jnp.concatenate([cos, cos], axis=-1), (1, reps)), jnp.tile(jnp.concatenate([-sin, sin], axis=-1), (1, reps))


def _group_mean_matrix(group):
    return jnp.asarray(np.kron(np.eye(GROUP_W // group), np.full((group, group), 1.0 / group)), F32).astype(BF16)


def kernel(x, c, ctx, c_ctx, w_ada, b_ada, g_mix, g_ffn, w_in, w_out, diff_qn, diff_kn, diff_lam, diff_subln, na_qn,
           na_kn, na_rpb, gmlp_norm, gmlp_ws, gmlp_b, ffn_up, ffn_conv, ffn_conv_b, ffn_down):
    b, n, d = x.shape
    nc = ctx.shape[1]
    assert (d, n % (2 * TOKEN_TILE), nc % CHUNK, b % NA_ELEMS, b < ADA_ROWS // 2 + 1) == (D_MODEL, 0, 0, 0, True)
    tm = TOKEN_TILE
    tq = DIFF_QUERY_TILE

    c_all = jnp.zeros((ADA_ROWS, d), F32).at[:b].set(c).at[ADA_ROWS // 2].set(c_ctx)
    mods = _ada(c_all, w_ada, b_ada).reshape(DEPTH, ADA_ROWS, 6, d)

    g32 = _group_mean_matrix(DIFF_QK)
    g64 = _group_mean_matrix(HEAD_DIM)
    rope_tabs = _rope_tables(n)
    dft_lat = _dft_tables(n)
    dft_ctx = _dft_tables(nc)

    w_in_kernel_order = jnp.concatenate([w_in[..., r * GROUP_W:(r + 1) * GROUP_W] for r in W_ORDER], axis=-1)
    w_in_bf, w_out_bf, wu_bf, wd_bf = (w.astype(BF16) for w in (w_in_kernel_order, w_out, ffn_up, ffn_down))

    x2d = x.reshape(b * n, d)
    c2d = ctx.reshape(b * nc, d)
    for li in range(DEPTH):
        ctx_out = li < DEPTH - 1
        lam_init = 0.8 - 0.6 * math.exp(-0.3 * li)
        lf = diff_lam[li].astype(F32)
        lam = (jnp.exp(jnp.sum(lf[0] * lf[1])) - jnp.exp(jnp.sum(lf[2] * lf[3])) + lam_init).reshape(1)
        mod = mods[li]
        gains = jnp.zeros((8, GROUP_W), F32)
        gains = gains.at[0].set(jnp.tile(diff_qn[li], GROUP_W // DIFF_QK) * (DIFF_QK ** -0.5 * LOG2E))
        gains = gains.at[1].set(jnp.tile(diff_kn[li], GROUP_W // DIFF_QK))
        gains = gains.at[2].set(jnp.tile(na_qn[li], N_HEADS) * (HEAD_DIM ** -0.5 * LOG2E))
        gains = gains.at[3].set(jnp.tile(na_kn[li], N_HEADS))
        gains = gains.at[4].set(gmlp_norm[li])
        sub_row = (jnp.tile(diff_subln[li], N_HEADS) * (1.0 - lam_init)).reshape(1, GROUP_W)
        g_mix_row = g_mix[li].reshape(1, d)
        g_ffn_row = g_ffn[li].reshape(1, d)
        ws_stack = gmlp_ws[li].reshape(N_HEADS * CHUNK, CHUNK).astype(BF16)
        gbias = jnp.repeat(gmlp_b[li].T, HEAD_DIM, axis=1)
        bias_tab = _na_bias_tables(na_rpb[li], nc)

        p2d, dvt, nvt = _inproj(x2d, mod, g_mix_row, w_in_bf, li, gains, g32, g64, rope_tabs, n, 2 * tm, False)
        pc2d, dvtc, nvtc = _inproj(c2d, mod, g_mix_row, w_in_bf, li, gains, g32, g64, None, nc, nc, True)
        p3 = p2d.reshape(b, n, P_W)
        pc3 = pc2d.reshape(b, nc, P_W)

        y_f, y_g = _fourier_gmlp(p3, *dft_lat, ws_stack, gbias, "fourier_gmlp")
        ys = [y_f,
              _diff(lam, p3, dvt, pc3, dvtc, sub_row, tq, True),
              _na(p3, nvt, pc3, nvtc, bias_tab),
              y_g]
        ffn_w = (g_ffn_row, w_out_bf, wu_bf, ffn_conv[li], ffn_conv_b[li], wd_bf, li)
        x2d = _mix_ffn(ys, x2d, mod, *ffn_w, n, tm, False)

        if ctx_out:
            yc_f, yc_g = _fourier_gmlp(pc3, *dft_ctx, ws_stack, gbias, "fourier_gmlp_ctx")
            ycs = [yc_f,
                   _diff(lam, None, None, pc3, dvtc, sub_row, nc, False),
                   _na_ctx(pc3, nvtc),
                   yc_g]
            c2d = _mix_ffn(ycs, c2d, mod, *ffn_w, nc, tm, True)
    return x2d.reshape(b, n, d)
```

```python
import functools
import math

import jax
import jax.numpy as jnp
import numpy as np
from jax import lax
from jax.experimental import pallas as pl
from jax.experimental.pallas import tpu as pltpu

F32 = jnp.float32
BF16 = jnp.bfloat16

D_MODEL = 1024
DEPTH = 4
GRID_W = 64
GROUP_W = 256
HEAD_DIM = 64
N_HEADS = 4
DIFF_QK = 32
NA_KH = 8
NA_KW = 16
CHUNK = 128
D_FF = 2816
ROPE_BASE = 10000.0
EPS = 1e-6
IN_W = 9 * GROUP_W

REF_F, REF_DQ, REF_DK, REF_DV, REF_NQ, REF_NK, REF_NV, REF_GU, REF_GV = range(9)
W_ORDER = (REF_GV, REF_DQ, REF_DK, REF_NQ, REF_NK, REF_GU, REF_DV, REF_NV, REF_F)
W_GV, W_DQ, W_DK, W_NQ, W_NK, W_GU, W_DV, W_NV, W_F = range(9)
COL_F, COL_DQ, COL_DK, COL_NQ, COL_NK, COL_GU, COL_GV = range(7)
P_W = 7 * GROUP_W

VT_ROWS = HEAD_DIM + 16
DIFF_SCORE_COLS = 2048
PAIR_ROWS = 16
SUM_FLOOR = 2.0 ** -80
NA_Q_ROWS = 4
NA_ELEMS = 4
NA_WIN_ROWS = 12
NEG = -1e30
LOG2E = math.log2(math.e)

ADA_ROWS = 16
VMEM_LIMIT = 56 * 1024 * 1024
TOKEN_TILE = 512
DIFF_QUERY_TILE = 512
LANES = 128
HALO = 16
FF_CHUNK = 256


def _dot(a, b):
    return jnp.dot(a, b, preferred_element_type=F32)


def _dot_nt(a, b):
    return lax.dot_general(a, b, (((1,), (1,)), ((), ())), preferred_element_type=F32)


def _resident(shape):
    nd = len(shape)
    return pl.BlockSpec(shape, lambda *_: (0,) * nd, pipeline_mode=pl.Buffered(1))


def _resident_layer(shape, li):
    nd = len(shape)
    return pl.BlockSpec((None,) + tuple(shape), lambda *_: (li,) + (0,) * nd, pipeline_mode=pl.Buffered(1))


def _params(n_axes):
    return pltpu.CompilerParams(dimension_semantics=("arbitrary",) * n_axes, vmem_limit_bytes=VMEM_LIMIT)


def _lane(width=GROUP_W):
    return lax.broadcasted_iota(jnp.int32, (1, width), 1)


def _ada_kernel(c_ref, w_ref, b_ref, o_ref):
    c = c_ref[...]
    s = c / (1.0 + jnp.exp(-c))
    w = w_ref[0]
    s_hi = s.astype(BF16)
    s_lo = (s - s_hi.astype(F32)).astype(BF16)
    w_hi = w.astype(BF16)
    w_lo = (w - w_hi.astype(F32)).astype(BF16)
    o_ref[0] = _dot(s_hi, w_hi) + _dot(s_lo, w_hi) + _dot(s_hi, w_lo) + b_ref[0]


def _ada(c_all, w_ada, b_ada):
    tn = 1536
    out_w = 6 * D_MODEL
    return pl.pallas_call(
        _ada_kernel,
        grid=(DEPTH, out_w // tn),
        in_specs=[
            pl.BlockSpec((ADA_ROWS, D_MODEL), lambda l, j: (0, 0)),
            pl.BlockSpec((1, D_MODEL, tn), lambda l, j: (l, 0, j)),
            pl.BlockSpec((1, 1, tn), lambda l, j: (l, 0, j)),
        ],
        out_specs=pl.BlockSpec((1, ADA_ROWS, tn), lambda l, j: (l, 0, j)),
        out_shape=jax.ShapeDtypeStruct((DEPTH, ADA_ROWS, out_w), F32),
        compiler_params=_params(2),
        name="ada",
    )(c_all, w_ada, b_ada.reshape(DEPTH, 1, out_w))


def _gelu(x):
    return 0.5 * x * (1.0 + jnp.tanh(math.sqrt(2.0 / math.pi) * (x + 0.044715 * (x * x * x))))


def _inproj_kernel(*refs, rope):
    if rope:
        x_ref, mod_ref, g_ref, w_ref, gains_ref, g32_ref, g64_ref, cos_ref, sin_ref, o_ref, dvt_ref, nvt_ref = refs
    else:
        x_ref, mod_ref, g_ref, w_ref, gains_ref, g32_ref, g64_ref, o_ref, dvt_ref, nvt_ref = refs
    x = x_ref[...]
    ms = jnp.mean(x * x, axis=-1, keepdims=True)
    h = x * lax.rsqrt(ms + EPS) * g_ref[...]
    h = h * (1.0 + mod_ref[0, 1:2, :]) + mod_ref[0, 0:1, :]
    p = _dot(h.astype(BF16), w_ref[...])

    def col(j):
        return p[:, j * GROUP_W:(j + 1) * GROUP_W]

    def put(j, v):
        o_ref[:, j * GROUP_W:(j + 1) * GROUP_W] = v.astype(BF16)

    def group_norm(v, gmat_ref, gain_row):
        gms = _dot((v * v).astype(BF16), gmat_ref[...])
        return v * lax.rsqrt(gms + EPS) * gains_ref[gain_row:gain_row + 1, :]

    def rotary(v):
        if not rope:
            return v
        first_half = (_lane() % DIFF_QK) < (DIFF_QK // 2)
        partner = jnp.where(first_half, pltpu.roll(v, GROUP_W - DIFF_QK // 2, 1), pltpu.roll(v, DIFF_QK // 2, 1))
        return v * cos_ref[...] + partner * sin_ref[...]

    put(COL_GV, group_norm(_gelu(col(W_GV)), g64_ref, 4))
    put(COL_DQ, rotary(group_norm(col(W_DQ), g32_ref, 0)))
    put(COL_DK, rotary(group_norm(col(W_DK), g32_ref, 1)))
    put(COL_NQ, group_norm(col(W_NQ), g64_ref, 2))
    put(COL_NK, group_norm(col(W_NK), g64_ref, 3))
    put(COL_GU, _gelu(col(W_GU)))
    dvt_ref[0] = col(W_DV).T.astype(BF16)
    nvt_ref[0] = col(W_NV).T.astype(BF16)
    put(COL_F, col(W_F))


def _inproj(x2d, mod, g_row, w_bf, li, gains, g32, g64, rope_tabs, n, tm, ctx_stream):
    rows = x2d.shape[0]
    tpb = n // tm
    mod_row = (lambda i: (ADA_ROWS // 2, 0, 0)) if ctx_stream else (lambda i: (i // tpb, 0, 0))
    in_specs = [
        pl.BlockSpec((tm, D_MODEL), lambda i: (i, 0)),
        pl.BlockSpec((1, 6, D_MODEL), mod_row),
        _resident((1, D_MODEL)),
        _resident_layer((D_MODEL, IN_W), li),
        _resident((8, GROUP_W)),
        _resident((GROUP_W, GROUP_W)),
        _resident((GROUP_W, GROUP_W)),
    ]
    args = [x2d, mod, g_row, w_bf, gains, g32, g64]
    if rope_tabs is not None:
        in_specs += [pl.BlockSpec((tm, GROUP_W), lambda i: (i % tpb, 0))] * 2
        args += list(rope_tabs)
    return pl.pallas_call(
        functools.partial(_inproj_kernel, rope=rope_tabs is not None),
        grid=(rows // tm,),
        in_specs=in_specs,
        out_specs=[pl.BlockSpec((tm, P_W), lambda i: (i, 0))]
        + [pl.BlockSpec((1, GROUP_W, tm), lambda i: (i // tpb, 0, i % tpb))] * 2,
        out_shape=[jax.ShapeDtypeStruct((rows, P_W), BF16)]
        + [jax.ShapeDtypeStruct((rows // n, GROUP_W, n), BF16)] * 2,
        compiler_params=_params(1),
        name="inproj_ctx" if ctx_stream else "inproj",
    )(*args)


def _fourier_kernel(p_ref, bdc_ref, bds_ref, cn_ref, sn_ref, rev_ref, o_ref):
    n = p_ref.shape[1]
    half = n // 2
    rb = rev_ref.shape[0]
    nb = n // rb
    mirrored = []
    for blk in range(half // rb):
        src = nb - blk - 1
        nxt = (src + 1) % nb
        pair = jnp.concatenate([p_ref[0, src * rb:(src + 1) * rb, :], p_ref[0, nxt * rb:(nxt + 1) * rb, :]], axis=0)
        mirrored.append(_dot(rev_ref[...], pair))
    mirrored = jnp.concatenate(mirrored, axis=0)
    x = p_ref[0, 0:half, :].astype(F32)
    first = lax.broadcasted_iota(jnp.int32, x.shape, 0) == 0
    fold_c = jnp.where(first, x, x + mirrored).astype(BF16)
    fold_s = (x - mirrored).astype(BF16)
    a = _dot(fold_c, bdc_ref[...]).astype(BF16)
    b = _dot(fold_s, bds_ref[...]).astype(BF16)
    mid = _dot(p_ref[0, half:half + 16, :], bdc_ref[...])[0:1]
    out = _dot(cn_ref[...], a) - _dot(sn_ref[...], b)
    odd = (lax.broadcasted_iota(jnp.int32, out.shape, 0) & 1) == 1
    o_ref[0] = (out + jnp.where(odd, -mid, mid)).astype(BF16)


def _fourier(p3, bdc, bds, cn, sn, rev, name):
    b, n, _ = p3.shape
    return pl.pallas_call(
        _fourier_kernel,
        grid=(b,),
        in_specs=[
            pl.BlockSpec((1, n, GROUP_W), lambda i: (i, 0, COL_F)),
            _resident((GROUP_W, GROUP_W)),
            _resident((GROUP_W, GROUP_W)),
            _resident((n, n // 2)),
            _resident((n, n // 2)),
            _resident(rev.shape),
        ],
        out_specs=pl.BlockSpec((1, n, GROUP_W), lambda i: (i, 0, 0)),
        out_shape=jax.ShapeDtypeStruct((b, n, GROUP_W), BF16),
        compiler_params=_params(1),
        name=name,
    )(p3, bdc, bds, cn, sn, rev)


def _diff_kernel(*refs, has_lat):
    if has_lat:
        lam_ref, q_ref, kl_ref, vl_ref, kc_ref, vc_ref, sub_ref, sel_ref, o_ref, k_ref, vt_ref, kmax_ref = refs
        key_refs, val_refs = [kl_ref, kc_ref], [vl_ref, vc_ref]
    else:
        lam_ref, q_ref, kc_ref, vc_ref, sub_ref, sel_ref, o_ref, k_ref, vt_ref, kmax_ref = refs
        key_refs, val_refs = [kc_ref], [vc_ref]

    def pair_sq_norms(x):
        xf = x.astype(F32)
        return _dot_nt(sel_ref[...], (xf * xf).astype(BF16))

    @pl.when(pl.program_id(1) == 0)
    def _():
        lo = 0
        for kr, vr in zip(key_refs, val_refs):
            cnt = kr.shape[1]
            k_ref[lo:lo + cnt, :] = kr[0]
            for h in range(N_HEADS):
                vt_ref[h, 0:HEAD_DIM, lo:lo + cnt] = vr[0, h * HEAD_DIM:(h + 1) * HEAD_DIM, :]
                vt_ref[h, HEAD_DIM:VT_ROWS, lo:lo + cnt] = jnp.ones((VT_ROWS - HEAD_DIM, cnt), BF16)
            lo += cnt
        kmax_ref[...] = jnp.broadcast_to(pair_sq_norms(k_ref[...]).max(axis=1, keepdims=True), kmax_ref.shape)

    q = q_ref[0]
    lam = lam_ref[0]
    lane = _lane()
    bound = jnp.sqrt(pair_sq_norms(q) * kmax_ref[:, 0:1])

    def attend(shift_fn):
        outs = []
        tq = q.shape[0]
        heads_per_dot = min(N_HEADS, DIFF_SCORE_COLS // (2 * tq))
        for h0 in range(0, N_HEADS, heads_per_dot):
            pairs = range(2 * h0, 2 * (h0 + heads_per_dot))
            qm = jnp.concatenate(
                [jnp.where((lane >= pair * DIFF_QK) & (lane < (pair + 1) * DIFF_QK), q, jnp.zeros_like(q))
                 for pair in pairs], axis=0)
            s = _dot_nt(k_ref[...], qm)
            e = jnp.exp2(s - shift_fn(s, pairs)).astype(BF16)
            for i in range(heads_per_dot):
                o = _dot(vt_ref[h0 + i], e[:, 2 * i * tq:2 * (i + 1) * tq])
                outs += [o[:, 0:tq], o[:, tq:]]
        return outs

    def finish(outs):
        heads = []
        for h in range(N_HEADS):
            o1, o2 = outs[2 * h], outs[2 * h + 1]
            o = (o1[0:HEAD_DIM] * (1.0 / o1[HEAD_DIM:HEAD_DIM + 1])
                 - o2[0:HEAD_DIM] * (lam / o2[HEAD_DIM:HEAD_DIM + 1]))
            heads.append(o * lax.rsqrt(jnp.mean(o * o, axis=0, keepdims=True) + EPS))
        o_ref[0] = (jnp.concatenate(heads, axis=0) * sub_ref[...]).T.astype(BF16)

    outs = attend(lambda s, pairs: jnp.concatenate([bound[p:p + 1] for p in pairs], axis=1))
    finish(outs)
    smallest = outs[0][HEAD_DIM:HEAD_DIM + 1]
    for o in outs[1:]:
        smallest = jnp.minimum(smallest, o[HEAD_DIM:HEAD_DIM + 1])

    @pl.when(jnp.min(smallest) < SUM_FLOOR)
    def _():
        finish(attend(lambda s, pairs: s.max(axis=0, keepdims=True)))


def _diff(lam, p3, vt3, pc3, vtc3, sub_row, tq, has_lat):
    q_src = p3 if has_lat else pc3
    sub_t = jnp.broadcast_to(sub_row.reshape(GROUP_W, 1), (GROUP_W, tq))
    b, n, _ = q_src.shape
    nc = pc3.shape[1]
    smem = pl.BlockSpec(memory_space=pltpu.SMEM)
    in_specs = [smem, pl.BlockSpec((1, tq, GROUP_W), lambda i, j: (i, j, COL_DQ))]
    args = [lam, q_src]
    if has_lat:
        in_specs += [pl.BlockSpec((1, n, GROUP_W), lambda i, j: (i, 0, COL_DK)),
                     pl.BlockSpec((1, GROUP_W, n), lambda i, j: (i, 0, 0))]
        args += [p3, vt3]
    in_specs += [pl.BlockSpec((1, nc, GROUP_W), lambda i, j: (i, 0, COL_DK)),
                 pl.BlockSpec((1, GROUP_W, nc), lambda i, j: (i, 0, 0)),
                 _resident((GROUP_W, tq)),
                 _resident((PAIR_ROWS, GROUP_W))]
    pair_sel = np.arange(PAIR_ROWS)[:, None] == np.arange(GROUP_W)[None, :] // DIFF_QK
    args += [pc3, vtc3, sub_t, jnp.asarray(pair_sel, F32).astype(BF16)]
    return pl.pallas_call(
        functools.partial(_diff_kernel, has_lat=has_lat),
        grid=(b, n // tq),
        in_specs=in_specs,
        out_specs=pl.BlockSpec((1, tq, GROUP_W), lambda i, j: (i, j, 0)),
        out_shape=jax.ShapeDtypeStruct((b, n, GROUP_W), BF16),
        scratch_shapes=[pltpu.VMEM(((n if has_lat else 0) + nc, GROUP_W), BF16),
                        pltpu.VMEM((N_HEADS, VT_ROWS, (n if has_lat else 0) + nc), BF16),
                        pltpu.VMEM((PAIR_ROWS, 128), F32)],
        compiler_params=_params(2),
        name="diff" if has_lat else "diff_ctx",
    )(*args)


def _na_kernel(*refs, local):
    if local:
        q_ref, k0_ref, k1_ref, k2_ref, v0_ref, v1_ref, v2_ref, kc_ref, vc_ref, bias_ref, o_ref = refs
        key_refs = [k0_ref, k1_ref, k2_ref, kc_ref]
        vt_refs = [v0_ref, v1_ref, v2_ref, vc_ref]
    else:
        q_ref, kc_ref, vc_ref, o_ref = refs
        key_refs = [kc_ref]
        vt_refs = [vc_ref]
    n_elems, blk = q_ref.shape[0], q_ref.shape[1]
    lane = _lane()

    def scores(el):
        q = q_ref[el]
        q_heads = jnp.concatenate(
            [jnp.where((lane >= h * HEAD_DIM) & (lane < (h + 1) * HEAD_DIM), q, jnp.zeros_like(q))
             for h in range(N_HEADS)], axis=0)
        k_all = jnp.concatenate([r[el] for r in key_refs], axis=0)
        return _dot_nt(k_all, q_heads)

    def weights(s):
        if local:
            s = s + bias_ref[0]
        return jnp.exp2(s - s.max(axis=0, keepdims=True)).astype(BF16)

    def attend(el, e):
        vt_all = jnp.concatenate([r[el] for r in vt_refs], axis=1)
        ones = jnp.ones((VT_ROWS - HEAD_DIM, vt_all.shape[1]), BF16)
        heads = []
        for h in range(N_HEADS):
            vt_h = jnp.concatenate([vt_all[h * HEAD_DIM:(h + 1) * HEAD_DIM], ones], axis=0)
            o = _dot(vt_h, e[:, h * blk:(h + 1) * blk])
            heads.append(o[0:HEAD_DIM] * (1.0 / o[HEAD_DIM:HEAD_DIM + 1]))
        o_ref[el] = jnp.concatenate(heads, axis=0).T.astype(BF16)

    es = [weights(s) for s in [scores(el) for el in range(n_elems)]]
    for el in range(n_elems):
        attend(el, es[el])


def _na(p3, vt3, pc3, vtc3, bias):
    b, n, _ = p3.shape
    nc = pc3.shape[1]
    ne = NA_ELEMS
    blk = NA_Q_ROWS * GRID_W
    steps = n // blk
    last_win = n // blk - NA_WIN_ROWS // NA_Q_ROWS

    def first(g):
        return jnp.clip(g - 1, 0, last_win)

    def cls(g, i):
        return ((g > 0).astype(jnp.int32) + (g == steps - 1).astype(jnp.int32), 0, 0)

    in_specs = [pl.BlockSpec((ne, blk, GROUP_W), lambda g, i: (i, g, COL_NQ))]
    in_specs += [pl.BlockSpec((ne, blk, GROUP_W), lambda g, i, j=j: (i, first(g) + j, COL_NK)) for j in range(3)]
    in_specs += [pl.BlockSpec((ne, GROUP_W, blk), lambda g, i, j=j: (i, 0, first(g) + j)) for j in range(3)]
    in_specs += [pl.BlockSpec((ne, nc, GROUP_W), lambda g, i: (i, 0, COL_NK)),
                 pl.BlockSpec((ne, GROUP_W, nc), lambda g, i: (i, 0, 0)),
                 pl.BlockSpec((1, 3 * blk + nc, N_HEADS * blk), cls)]
    return pl.pallas_call(
        functools.partial(_na_kernel, local=True),
        grid=(steps, b // ne),
        in_specs=in_specs,
        out_specs=pl.BlockSpec((ne, blk, GROUP_W), lambda g, i: (i, g, 0)),
        out_shape=jax.ShapeDtypeStruct((b, n, GROUP_W), BF16),
        compiler_params=_params(2),
        name="na",
    )(p3, p3, p3, p3, vt3, vt3, vt3, pc3, vtc3, bias)


def _na_ctx(pc3, vtc3):
    b, nc, _ = pc3.shape
    ne = NA_ELEMS
    return pl.pallas_call(
        functools.partial(_na_kernel, local=False),
        grid=(b // ne,),
        in_specs=[pl.BlockSpec((ne, nc, GROUP_W), lambda i: (i, 0, COL_NQ)),
                  pl.BlockSpec((ne, nc, GROUP_W), lambda i: (i, 0, COL_NK)),
                  pl.BlockSpec((ne, GROUP_W, nc), lambda i: (i, 0, 0))],
        out_specs=pl.BlockSpec((ne, nc, GROUP_W), lambda i: (i, 0, 0)),
        out_shape=jax.ShapeDtypeStruct((b, nc, GROUP_W), BF16),
        compiler_params=_params(1),
        name="na_ctx",
    )(pc3, pc3, vtc3)


def _ctx_mixers_kernel(p_ref, bdc_ref, bds_ref, cn_ref, sn_ref, rev_ref, u_ref, v_ref, ws_ref, bias_ref,
                       q_ref, k_ref, vt_ref, of_ref, og_ref, on_ref, *, chunks_per_step):
    _fourier_kernel(p_ref, bdc_ref, bds_ref, cn_ref, sn_ref, rev_ref, of_ref)
    _gmlp_kernel(u_ref, v_ref, ws_ref, bias_ref, og_ref, chunks_per_step=chunks_per_step)
    _na_kernel(q_ref, k_ref, vt_ref, on_ref, local=False)


def _ctx_mixers(pc3, vtc3, bdc, bds, cn, sn, rev, ws_stack, bias_full):
    b, nc, _ = pc3.shape

    def col(c):
        return pl.BlockSpec((1, nc, GROUP_W), lambda i: (i, 0, c))

    return pl.pallas_call(
        functools.partial(_ctx_mixers_kernel, chunks_per_step=min(4, nc // CHUNK)),
        grid=(b,),
        in_specs=[col(COL_F), _resident((GROUP_W, GROUP_W)), _resident((GROUP_W, GROUP_W)),
                  _resident((nc, nc // 2)), _resident((nc, nc // 2)), _resident(rev.shape),
                  col(COL_GU), col(COL_GV), _resident((N_HEADS * CHUNK, CHUNK)), _resident((CHUNK, GROUP_W)),
                  col(COL_NQ), col(COL_NK), pl.BlockSpec((1, GROUP_W, nc), lambda i: (i, 0, 0))],
        out_specs=[col(0)] * 3,
        out_shape=[jax.ShapeDtypeStruct((b, nc, GROUP_W), BF16)] * 3,
        compiler_params=_params(1),
        name="ctx_mixers",
    )(pc3, bdc, bds, cn, sn, rev, pc3, pc3, ws_stack, bias_full, pc3, pc3, vtc3)


def _na_row_offsets():
    first_row = NA_WIN_ROWS - NA_KH
    rules = [lambda i, a: a - i + NA_KH - 1 if a < NA_KH else None,
             lambda i, a: a - i + NA_KH // 2 - 1 if i <= a < i + NA_KH else None,
             lambda i, a: a - i - 1 if a >= first_row else None]
    return [[[rule(i, a) for a in range(NA_WIN_ROWS)] for i in range(NA_Q_ROWS)] for rule in rules]


def _na_bias_kernel(band_ref, o_ref):
    masked = jnp.full((GRID_W, GRID_W), NEG, F32)
    for kind, per_query_row in enumerate(_na_row_offsets()):
        for a in range(NA_WIN_ROWS):
            blocks = [masked if offs[a] is None else band_ref[0, offs[a]] for offs in per_query_row]
            o_ref[kind, a * GRID_W:(a + 1) * GRID_W, :] = jnp.concatenate(blocks, axis=1)
        o_ref[kind, NA_WIN_ROWS * GRID_W:, :] = jnp.zeros((o_ref.shape[1] - NA_WIN_ROWS * GRID_W, o_ref.shape[2]), F32)


def _na_bias_tables(rpb, nc):
    kc = np.arange(GRID_W)[:, None]
    j = np.arange(GRID_W)[None, :]
    c0 = np.clip(j - NA_KW // 2, 0, GRID_W - NA_KW)
    col_ok = (kc >= c0) & (kc < c0 + NA_KW)
    onehot = (np.arange(2 * NA_KW - 1)[:, None, None] == (kc - j + NA_KW - 1)[None]) & col_ok[None]
    band = jnp.einsum("hdo,okj->hdkj", rpb.astype(F32), jnp.asarray(onehot, F32), precision=lax.Precision.HIGHEST)
    band = jnp.where(col_ok, band * LOG2E, NEG)
    n_off = 2 * NA_KH - 1
    q, k = NA_Q_ROWS * GRID_W, NA_WIN_ROWS * GRID_W
    return pl.pallas_call(
        _na_bias_kernel,
        grid=(N_HEADS,),
        in_specs=[pl.BlockSpec((1, n_off, GRID_W, GRID_W), lambda h: (h, 0, 0, 0))],
        out_specs=pl.BlockSpec((3, k + nc, q), lambda h: (0, 0, h)),
        out_shape=jax.ShapeDtypeStruct((3, k + nc, N_HEADS * q), F32),
        compiler_params=_params(1),
        name="na_bias",
    )(band)


def _gmlp_kernel(u_ref, v_ref, ws_ref, bias_ref, o_ref, *, chunks_per_step):
    n = u_ref.shape[1]
    lane = _lane()
    step_rows = chunks_per_step * CHUNK
    for s in range(n // step_rows):
        base = s * step_rows
        vcat = jnp.concatenate(
            [v_ref[0, base + c * CHUNK: base + (c + 1) * CHUNK, :] for c in range(chunks_per_step)], axis=1)
        r = _dot(ws_ref[...], vcat)
        for c in range(chunks_per_step):
            mix = r[0:CHUNK, c * GROUP_W:(c + 1) * GROUP_W]
            for g in range(1, N_HEADS):
                mix = jnp.where(lane >= g * HEAD_DIM, r[g * CHUNK:(g + 1) * CHUNK, c * GROUP_W:(c + 1) * GROUP_W], mix)
            rows = slice(base + c * CHUNK, base + (c + 1) * CHUNK)
            o_ref[0, rows, :] = (u_ref[0, rows, :].astype(F32) * (mix + bias_ref[...])).astype(BF16)


def _fourier_gmlp_kernel(p_ref, bdc_ref, bds_ref, cn_ref, sn_ref, rev_ref, u_ref, v_ref, ws_ref, bias_ref,
                         of_ref, og_ref, *, chunks_per_step):
    _fourier_kernel(p_ref, bdc_ref, bds_ref, cn_ref, sn_ref, rev_ref, of_ref)
    _gmlp_kernel(u_ref, v_ref, ws_ref, bias_ref, og_ref, chunks_per_step=chunks_per_step)


def _fourier_gmlp(p3, bdc, bds, cn, sn, rev, ws_stack, bias_full, name):
    b, n, _ = p3.shape
    cps = min(4, n // CHUNK)

    def col(c):
        return pl.BlockSpec((1, n, GROUP_W), lambda i: (i, 0, c))

    return pl.pallas_call(
        functools.partial(_fourier_gmlp_kernel, chunks_per_step=cps),
        grid=(b,),
        in_specs=[col(COL_F), _resident((GROUP_W, GROUP_W)), _resident((GROUP_W, GROUP_W)),
                  _resident((n, n // 2)), _resident((n, n // 2)), _resident(rev.shape),
                  col(COL_GU), col(COL_GV), _resident((N_HEADS * CHUNK, CHUNK)), _resident((CHUNK, GROUP_W))],
        out_specs=[col(0), col(0)],
        out_shape=[jax.ShapeDtypeStruct((b, n, GROUP_W), BF16)] * 2,
        compiler_params=_params(1),
        name=name,
    )(p3, bdc, bds, cn, sn, rev, p3, p3, ws_stack, bias_full)


def _mix_ffn_kernel(*refs, tiles_per_seq, seq_rows):
    y_refs, (xp_ref, xm_ref, xn_ref) = refs[:12], refs[12:15]
    mod_ref, g_ref, wo_ref, wu_ref, cw_ref, cb_ref, wd_ref, o_ref, xs_ref, hs_ref, ds_ref, act_ref = refs[15:]
    tm = xm_ref.shape[0]
    groups = tm // 8
    pos = pl.program_id(0) % tiles_per_seq

    def mixed(y_parts, x):
        return x + mod_ref[0, 2:3, :] * _dot(jnp.concatenate(y_parts, axis=1), wo_ref[...])

    def prenorm(x):
        ms = jnp.mean(x * x, axis=-1, keepdims=True)
        h = x * lax.rsqrt(ms + EPS) * g_ref[...]
        return h * (1.0 + mod_ref[0, 4:5, :]) + mod_ref[0, 3:4, :]

    x_all = mixed([jnp.concatenate([y_refs[3 * j + 1][...], y_refs[3 * j][...], y_refs[3 * j + 2][...]], axis=0)
                   for j in range(4)],
                  jnp.concatenate([xm_ref[...], xp_ref[...], xn_ref[...]], axis=0))
    xs_ref[...] = x_all[0:tm]
    h_all = prenorm(x_all)
    h_main, h_halo = h_all[0:tm], h_all[tm:]
    slabs = hs_ref.shape[0]
    pitch = hs_ref.shape[1] // 8
    for c in range(slabs):
        for s in range(8):
            hs_ref[c, s * pitch:s * pitch + groups, :] = h_main[s * groups:(s + 1) * groups, c * LANES:(c + 1) * LANES]
    before = jnp.where(pos != 0, h_halo[HALO - 1:HALO], 0.0)
    after = jnp.where(pos != tiles_per_seq - 1, h_halo[HALO:HALO + 1], 0.0)
    row = lax.broadcasted_iota(jnp.int32, (HALO, 1), 0)
    h_edge = jnp.where(row == 0, before, jnp.where(row == HALO - 1, after, 0.0))
    h_perm = jnp.concatenate(
        [jnp.concatenate([hs_ref[c, pl.ds(j, 8, stride=pitch), :] for c in range(slabs)], axis=1)
         for j in range(groups)], axis=0)
    z_all = _dot(jnp.concatenate([h_perm, h_edge], axis=0).astype(BF16), wu_ref[...])

    sub = lax.broadcasted_iota(jnp.int32, (8, FF_CHUNK), 0)
    inner_starts = [s for s in range(1, 8) if (s * groups) % seq_rows == 0]
    starts_seq = functools.reduce(jnp.logical_or, [sub == s for s in inner_starts], sub < 0)
    ends_seq = functools.reduce(jnp.logical_or, [sub == s - 1 for s in inner_starts], sub < 0)

    def conv(lo):
        cols = slice(lo, lo + FF_CHUNK)
        z = z_all[0:tm, cols]
        edge = z_all[tm:tm + HALO, cols]
        first = jnp.where(sub == 0, edge[0:8], jnp.where(starts_seq, 0.0, pltpu.roll(z[tm - 8:tm], 1, 0)))
        last = jnp.where(sub == 7, edge[HALO - 8:HALO], jnp.where(ends_seq, 0.0, pltpu.roll(z[0:8], 7, 0)))
        return (jnp.concatenate([first, z[0:tm - 8]], axis=0) * cw_ref[0:1, cols] + z * cw_ref[1:2, cols]
                + jnp.concatenate([z[8:tm], last], axis=0) * cw_ref[2:3, cols] + cb_ref[:, cols])

    for c in range(D_FF // FF_CHUNK):
        g = conv(c * FF_CHUNK)
        v = conv(D_FF + c * FF_CHUNK)
        act_ref[:, c * FF_CHUNK:(c + 1) * FF_CHUNK] = (g / (1.0 + jnp.exp(-g)) * v).astype(BF16)
    down = _dot(act_ref[...], wd_ref[...])
    for j in range(groups):
        for c in range(slabs):
            ds_ref[c, pl.ds(j, 8, stride=pitch), :] = down[j * 8:(j + 1) * 8, c * LANES:(c + 1) * LANES]
    down = jnp.concatenate(
        [jnp.concatenate([ds_ref[c, s * pitch:s * pitch + groups, :] for s in range(8)], axis=0)
         for c in range(slabs)], axis=1)
    o_ref[...] = xs_ref[...] + mod_ref[0, 5:6, :] * down


def _mix_ffn(ys, x2d, mod, g_row, wo_bf, wu_bf, conv_w, conv_b, wd_bf, li, n, tm, ctx_stream):
    rows = x2d.shape[0]
    assert rows % tm == 0 and (n % tm == 0 or (tm % n == 0 and n % (tm // 8) == 0))
    tpb = max(1, n // tm)
    per_tile = tm // HALO
    last_halo = rows // HALO - 1
    mod_row = (lambda i: (ADA_ROWS // 2, 0, 0)) if ctx_stream else (lambda i: (i // tpb, 0, 0))

    def halo_specs(width):
        return [pl.BlockSpec((HALO, width), lambda i: (jnp.maximum(i * per_tile - 1, 0), 0)),
                pl.BlockSpec((tm, width), lambda i: (i, 0)),
                pl.BlockSpec((HALO, width), lambda i: (jnp.minimum((i + 1) * per_tile, last_halo), 0))]

    y2d = [y.reshape(rows, GROUP_W) for y in ys]
    return pl.pallas_call(
        functools.partial(_mix_ffn_kernel, tiles_per_seq=tpb, seq_rows=n),
        grid=(rows // tm,),
        in_specs=halo_specs(GROUP_W) * 4 + halo_specs(D_MODEL) + [
            pl.BlockSpec((1, 6, D_MODEL), mod_row),
            _resident((1, D_MODEL)),
            _resident_layer((D_MODEL, D_MODEL), li),
            _resident_layer((D_MODEL, 2 * D_FF), li),
            _resident((3, 2 * D_FF)),
            _resident((1, 2 * D_FF)),
            _resident_layer((D_FF, D_MODEL), li),
        ],
        out_specs=pl.BlockSpec((tm, D_MODEL), lambda i: (i, 0)),
        out_shape=jax.ShapeDtypeStruct((rows, D_MODEL), F32),
        scratch_shapes=[pltpu.VMEM((tm, D_MODEL), F32)]
        + [pltpu.VMEM((D_MODEL // LANES, 8 * (tm // 8 + 8), LANES), F32)] * 2
        + [pltpu.VMEM((tm, D_FF), BF16)],
        compiler_params=_params(1),
        name="mix_ffn_ctx" if ctx_stream else "mix_ffn",
    )(*[a for y in y2d for a in (y, y, y)], x2d, x2d, x2d, mod, g_row, wo_bf, wu_bf, conv_w,
      conv_b.reshape(1, 2 * D_FF), wd_bf)


def _dft_tables(n):
    lo_n = 64
    half = n // 2
    t = jnp.arange(half, dtype=jnp.int32)

    def table(k):
        ang = ((k[:, None] * t[None, :]) % n).astype(F32) * (2.0 * math.pi / n)
        return jnp.cos(ang), jnp.sin(ang)

    (ch, sh), (cl, sl) = table(jnp.arange(n // lo_n, dtype=jnp.int32) * lo_n), table(jnp.arange(lo_n, dtype=jnp.int32))
    cn = (ch[:, None] * cl[None] - sh[:, None] * sl[None]).reshape(n, half).astype(BF16)
    sn = (sh[:, None] * cl[None] + ch[:, None] * sl[None]).reshape(n, half).astype(BF16)
    rb = min(GROUP_W, half)
    r = np.arange(rb)
    rev = np.zeros((rb, 2 * rb), np.float32)
    rev[r[1:], rb - r[1:]] = 1.0
    rev[0, rb] = 1.0
    c = np.arange(HEAD_DIM)
    angc = 2.0 * np.pi * ((c[:, None] * c[None, :]) % HEAD_DIM) / HEAD_DIM
    norm = 1.0 / math.sqrt(n * HEAD_DIM)
    eye = np.eye(N_HEADS)
    bdc = jnp.asarray(np.kron(eye, np.cos(angc) * norm), F32).astype(BF16)
    bds = jnp.asarray(np.kron(eye, np.sin(angc) * norm), F32).astype(BF16)
    return bdc, bds, cn, sn, jnp.asarray(rev, F32).astype(BF16)


def _rope_tables(n):
    n_freq = DIFF_QK // 4
    freqs = ROPE_BASE ** (-jnp.arange(n_freq, dtype=F32) / n_freq)
    t = jnp.arange(n)
    row = (t // GRID_W).astype(F32)
    col = (t % GRID_W).astype(F32)
    ang = jnp.concatenate([row[:, None] * freqs, col[:, None] * freqs], axis=-1)
    cos, sin = jnp.cos(ang), jnp.sin(ang)
    reps = GROUP_W // DIFF_QK
    return jnp.tile(jnp.concatenate([cos, cos], axis=-1), (1, reps)), jnp.tile(jnp.concatenate([-sin, sin], axis=-1), (1, reps))


def _group_mean_matrix(group):
    return jnp.asarray(np.kron(np.eye(GROUP_W // group), np.full((group, group), 1.0 / group)), F32).astype(BF16)


def kernel(x, c, ctx, c_ctx, w_ada, b_ada, g_mix, g_ffn, w_in, w_out, diff_qn, diff_kn, diff_lam, diff_subln, na_qn,
           na_kn, na_rpb, gmlp_norm, gmlp_ws, gmlp_b, ffn_up, ffn_conv, ffn_conv_b, ffn_down):
    b, n, d = x.shape
    nc = ctx.shape[1]
    assert (d, n % (2 * TOKEN_TILE), nc % CHUNK, b % NA_ELEMS, b < ADA_ROWS // 2 + 1) == (D_MODEL, 0, 0, 0, True)
    tm = TOKEN_TILE
    tq = DIFF_QUERY_TILE

    c_all = jnp.zeros((ADA_ROWS, d), F32).at[:b].set(c).at[ADA_ROWS // 2].set(c_ctx)
    mods = _ada(c_all, w_ada, b_ada).reshape(DEPTH, ADA_ROWS, 6, d)

    g32 = _group_mean_matrix(DIFF_QK)
    g64 = _group_mean_matrix(HEAD_DIM)
    rope_tabs = _rope_tables(n)
    dft_lat = _dft_tables(n)
    dft_ctx = _dft_tables(nc)

    w_in_kernel_order = jnp.concatenate([w_in[..., r * GROUP_W:(r + 1) * GROUP_W] for r in W_ORDER], axis=-1)
    w_in_bf, w_out_bf, wu_bf, wd_bf = (w.astype(BF16) for w in (w_in_kernel_order, w_out, ffn_up, ffn_down))

    x2d = x.reshape(b * n, d)
    c2d = ctx.reshape(b * nc, d)
    for li in range(DEPTH):
        ctx_out = li < DEPTH - 1
        lam_init = 0.8 - 0.6 * math.exp(-0.3 * li)
        lf = diff_lam[li].astype(F32)
        lam = (jnp.exp(jnp.sum(lf[0] * lf[1])) - jnp.exp(jnp.sum(lf[2] * lf[3])) + lam_init).reshape(1)
        mod = mods[li]
        gains = jnp.zeros((8, GROUP_W), F32)
        gains = gains.at[0].set(jnp.tile(diff_qn[li], GROUP_W // DIFF_QK) * (DIFF_QK ** -0.5 * LOG2E))
        gains = gains.at[1].set(jnp.tile(diff_kn[li], GROUP_W // DIFF_QK))
        gains = gains.at[2].set(jnp.tile(na_qn[li], N_HEADS) * (HEAD_DIM ** -0.5 * LOG2E))
        gains = gains.at[3].set(jnp.tile(na_kn[li], N_HEADS))
        gains = gains.at[4].set(gmlp_norm[li])
        sub_row = (jnp.tile(diff_subln[li], N_HEADS) * (1.0 - lam_init)).reshape(1, GROUP_W)
        g_mix_row = g_mix[li].reshape(1, d)
        g_ffn_row = g_ffn[li].reshape(1, d)
        ws_stack = gmlp_ws[li].reshape(N_HEADS * CHUNK, CHUNK).astype(BF16)
        gbias = jnp.repeat(gmlp_b[li].T, HEAD_DIM, axis=1)
        bias_tab = _na_bias_tables(na_rpb[li], nc)

        p2d, dvt, nvt = _inproj(x2d, mod, g_mix_row, w_in_bf, li, gains, g32, g64, rope_tabs, n, 2 * tm, False)
        pc2d, dvtc, nvtc = _inproj(c2d, mod, g_mix_row, w_in_bf, li, gains, g32, g64, None, nc, nc, True)
        p3 = p2d.reshape(b, n, P_W)
        pc3 = pc2d.reshape(b, nc, P_W)

        y_f, y_g = _fourier_gmlp(p3, *dft_lat, ws_stack, gbias, "fourier_gmlp")
        ys = [y_f,
              _diff(lam, p3, dvt, pc3, dvtc, sub_row, tq, True),
              _na(p3, nvt, pc3, nvtc, bias_tab),
              y_g]
        ffn_w = (g_ffn_row, w_out_bf, wu_bf, ffn_conv[li], ffn_conv_b[li], wd_bf, li)
        x2d = _mix_ffn(ys, x2d, mod, *ffn_w, n, tm, False)

        if ctx_out:
            yc_f, yc_g, yc_n = _ctx_mixers(pc3, nvtc, *dft_ctx, ws_stack, gbias)
            ycs = [yc_f,
                   _diff(lam, None, None, pc3, dvtc, sub_row, nc, False),
                   yc_n,
                   yc_g]
            c2d = _mix_ffn(ycs, c2d, mod, *ffn_w, nc, tm, True)
    return x2d.reshape(b, n, d)
```
